```python
import math
import jax, jax.numpy as jnp
from jax import lax
import numpy as np

D_MODEL = 1024
BATCH = 32
SEQ = 256
DEPTH = 1
DEC_BATCH = 8
DEC_SEQ = 1024
PAST_LEN = 512

GRID_W = 64
HEAD_DIM = 64
HEADS_A = 8
KV_HEADS_A = 2
GROUP_A = HEADS_A // KV_HEADS_A
HEADS_B = 4
VDIM_B = 2 * HEAD_DIM
WIDTH_A = HEADS_A * HEAD_DIM
WIDTH_B = HEADS_B * VDIM_B
N_EXPERTS = 16
EXPERT_DIM = 1024
CAPACITY_FACTOR = 2
Q_BLOCK = 128
ROPE_THETA = 10000.0
EPS = 1e-6
SPLIT_SIZES = (HEADS_A * HEAD_DIM, KV_HEADS_A * HEAD_DIM, KV_HEADS_A * HEAD_DIM,
               HEADS_B * 2 * HEAD_DIM, HEADS_B * 2 * HEAD_DIM, HEADS_B * VDIM_B,
               D_MODEL, D_MODEL)
IN_WIDTH = sum(SPLIT_SIZES)

kernel_name = 'hybrid_diffusion_gqa_diffattn_ecmoe_step'


def rms_norm(x, g):
    xf = x.astype(jnp.float32)
    y = xf * lax.rsqrt(jnp.mean(xf * xf, axis=-1, keepdims=True) + EPS)
    return (y * g.astype(jnp.float32)).astype(x.dtype)


def rope_tables(n_tokens):
    n_rows = n_tokens // GRID_W
    row = jnp.repeat(jnp.arange(n_rows), GRID_W).astype(jnp.float32)
    col = jnp.tile(jnp.arange(GRID_W), n_rows).astype(jnp.float32)
    quarter = HEAD_DIM // 4
    freqs = ROPE_THETA ** (-jnp.arange(quarter, dtype=jnp.float32) / quarter)
    ang = jnp.stack([row[:, None] * freqs, col[:, None] * freqs], axis=1)
    return jnp.cos(ang)[:, None], jnp.sin(ang)[:, None]


def apply_rope_2d(x, cos, sin):
    shp = x.shape
    xf = x.astype(jnp.float32).reshape(shp[0], shp[1], -1, 2, 2, HEAD_DIM // 4)
    xa, xb = xf[..., 0, :], xf[..., 1, :]
    out = jnp.stack([xa * cos - xb * sin, xa * sin + xb * cos], axis=-2)
    return out.reshape(shp).astype(x.dtype)


def sweep_query_blocks(fn, q):
    b, s = q.shape[:2]
    nb = s // Q_BLOCK
    qb = jnp.moveaxis(q.reshape((b, nb, Q_BLOCK) + q.shape[2:]), 1, 0)
    out = lax.map(fn, qb)
    return jnp.moveaxis(out, 0, 1).reshape((b, s) + out.shape[3:])


def gqa_block(qb, k, v):
    s = jnp.einsum('bqhgd,bshd->bhgqs', qb, k).astype(jnp.float32) * (HEAD_DIM ** -0.5)
    p = jax.nn.softmax(s, axis=-1).astype(v.dtype)
    return jnp.einsum('bhgqs,bshd->bqhgd', p, v)


def diff_block(qb, k, v, lam):
    s = jnp.einsum('bqhmd,bshmd->bhmqs', qb, k).astype(jnp.float32) * (HEAD_DIM ** -0.5)
    p = jax.nn.softmax(s, axis=-1)
    a = p[:, :, 0] - lam[None, :, None, None] * p[:, :, 1]
    return jnp.einsum('bhqs,bshe->bqhe', a.astype(v.dtype), v)


def token_mixing(h, mix_w, rope, ctx, lam_init):
    (w_in, g_q, g_k, lq1, lk1, lq2, lk2, g_diff, w_ba, w_bb, w_out) = mix_w
    b, s, _ = h.shape
    offsets = np.cumsum(SPLIT_SIZES)[:-1].tolist()
    qa, ka, va, qb, kb, vb, ga, gb = jnp.split(h @ w_in, offsets, axis=-1)
    qa = rms_norm(qa.reshape(b, s, KV_HEADS_A, GROUP_A, HEAD_DIM), g_q)
    ka = rms_norm(ka.reshape(b, s, KV_HEADS_A, HEAD_DIM), g_k)
    va = va.reshape(b, s, KV_HEADS_A, HEAD_DIM)
    qb = qb.reshape(b, s, HEADS_B, 2, HEAD_DIM)
    kb = kb.reshape(b, s, HEADS_B, 2, HEAD_DIM)
    vb = vb.reshape(b, s, HEADS_B, VDIM_B)
    own_ctx = (ka, va, kb, vb)
    if rope is not None:
        cos, sin = rope
        qa, ka = apply_rope_2d(qa, cos, sin), apply_rope_2d(ka, cos, sin)
        qb, kb = apply_rope_2d(qb, cos, sin), apply_rope_2d(kb, cos, sin)
    if ctx is not None:
        ka = jnp.concatenate([ctx[0].astype(ka.dtype), ka], axis=1)
        va = jnp.concatenate([ctx[1].astype(va.dtype), va], axis=1)
        kb = jnp.concatenate([ctx[2].astype(kb.dtype), kb], axis=1)
        vb = jnp.concatenate([ctx[3].astype(vb.dtype), vb], axis=1)
    ya = sweep_query_blocks(lambda q: gqa_block(q, ka, va), qa).reshape(b, s, WIDTH_A)
    lam = (jnp.exp(jnp.sum(lq1.astype(jnp.float32) * lk1.astype(jnp.float32), axis=-1))
           - jnp.exp(jnp.sum(lq2.astype(jnp.float32) * lk2.astype(jnp.float32), axis=-1)) + lam_init)
    yb = sweep_query_blocks(lambda q: diff_block(q, kb, vb, lam), qb)
    yb = (rms_norm(yb, g_diff) * (1.0 - lam_init)).reshape(b, s, WIDTH_B)
    merged = jax.nn.sigmoid(ga) * (ya @ w_ba) + jax.nn.sigmoid(gb) * (yb @ w_bb)
    return merged @ w_out, own_ctx


def expert_choice_ffn(h, w_router, w_gate, w_up, w_down):
    b, n, d = h.shape
    cap = CAPACITY_FACTOR * n // N_EXPERTS
    aff = jax.nn.softmax(jnp.einsum('bnd,de->bne', h, w_router).astype(jnp.float32), axis=-1)
    gates, idx = lax.top_k(jnp.swapaxes(aff, 1, 2), cap)
    xe = jax.vmap(lambda hb, ib: hb[ib])(h, idx)
    a = jnp.einsum('becd,edf->becf', xe, w_gate)
    u = jnp.einsum('becd,edf->becf', xe, w_up)
    ye = jnp.einsum('becf,efd->becd', jax.nn.silu(a) * u, w_down) * gates[..., None].astype(h.dtype)
    return jax.vmap(lambda yb, ib: jnp.zeros((n, d), yb.dtype).at[ib.reshape(-1)].add(yb.reshape(-1, d)))(ye, idx)


def trunk_layer(x, mod, g_attn, g_ffn, mix_w, moe_w, rope, ctx, lam_init):
    shift1, scale1, gate1, shift2, scale2, gate2 = jnp.split(mod, 6, axis=-1)
    h = rms_norm(x, g_attn) * (1.0 + scale1) + shift1
    m, own_ctx = token_mixing(h, mix_w, rope, ctx, lam_init)
    x = x + gate1 * m
    h = rms_norm(x, g_ffn) * (1.0 + scale2) + shift2
    x = x + gate2 * expert_choice_ffn(h, *moe_w)
    return x, own_ctx


def setup_inputs(seed: int = 0) -> dict:
    key = jax.random.key(seed)
    ks = jax.random.split(key, 32)
    nrm = lambda k, shape, scale: jax.random.normal(k, shape, jnp.float32) * scale
    gain = lambda k, shape: 1.0 + 0.02 * jax.random.normal(k, shape, jnp.float32)
    return {
        'x_prompt': nrm(ks[0], (BATCH, SEQ, D_MODEL), 1.0),
        'x_sample': nrm(ks[1], (DEC_BATCH, DEC_SEQ, D_MODEL), 1.0),
        'cache_attn_k': nrm(ks[2], (DEC_BATCH, DEPTH, PAST_LEN, KV_HEADS_A, HEAD_DIM), 1.0),
        'cache_attn_v': nrm(ks[3], (DEC_BATCH, DEPTH, PAST_LEN, KV_HEADS_A, HEAD_DIM), 1.0),
        'cache_diff_k': nrm(ks[4], (DEC_BATCH, DEPTH, PAST_LEN, HEADS_B, 2, HEAD_DIM), 1.0),
        'cache_diff_v': nrm(ks[5], (DEC_BATCH, DEPTH, PAST_LEN, HEADS_B, VDIM_B), 1.0),
        'c': nrm(ks[6], (DEC_BATCH, D_MODEL), 1.0),
        'c_ctx': nrm(ks[7], (D_MODEL,), 1.0),
        'w_mod': nrm(ks[8], (DEPTH, D_MODEL, 6 * D_MODEL), 0.5 * D_MODEL ** -0.5),
        'b_mod': nrm(ks[9], (DEPTH, 6 * D_MODEL), 0.01),
        'g_attn_norm': gain(ks[10], (DEPTH, D_MODEL)),
        'g_ffn_norm': gain(ks[11], (DEPTH, D_MODEL)),
        'w_in': nrm(ks[12], (DEPTH, D_MODEL, IN_WIDTH), D_MODEL ** -0.5),
        'g_q_norm': gain(ks[13], (DEPTH, HEAD_DIM)),
        'g_k_norm': gain(ks[14], (DEPTH, HEAD_DIM)),
        'lambda_q1': nrm(ks[15], (DEPTH, HEADS_B, HEAD_DIM), 0.1),
        'lambda_k1': nrm(ks[16], (DEPTH, HEADS_B, HEAD_DIM), 0.1),
        'lambda_q2': nrm(ks[17], (DEPTH, HEADS_B, HEAD_DIM), 0.1),
        'lambda_k2': nrm(ks[18], (DEPTH, HEADS_B, HEAD_DIM), 0.1),
        'g_diff_norm': gain(ks[19], (DEPTH, VDIM_B)),
        'w_branch_a': nrm(ks[20], (DEPTH, WIDTH_A, D_MODEL), WIDTH_A ** -0.5),
        'w_branch_b': nrm(ks[21], (DEPTH, WIDTH_B, D_MODEL), WIDTH_B ** -0.5),
        'w_out': nrm(ks[22], (DEPTH, D_MODEL, D_MODEL), D_MODEL ** -0.5),
        'w_router': nrm(ks[23], (DEPTH, D_MODEL, N_EXPERTS), D_MODEL ** -0.5),
        'w_exp_gate': nrm(ks[24], (DEPTH, N_EXPERTS, D_MODEL, EXPERT_DIM), D_MODEL ** -0.5),
        'w_exp_up': nrm(ks[25], (DEPTH, N_EXPERTS, D_MODEL, EXPERT_DIM), D_MODEL ** -0.5),
        'w_exp_down': nrm(ks[26], (DEPTH, N_EXPERTS, EXPERT_DIM, D_MODEL), EXPERT_DIM ** -0.5),
        'g_final': gain(ks[27], (D_MODEL,)),
    }


def reference(x_prompt, x_sample, cache_attn_k, cache_attn_v, cache_diff_k, cache_diff_v, c,
              c_ctx, w_mod, b_mod, g_attn_norm, g_ffn_norm, w_in, g_q_norm, g_k_norm,
              lambda_q1, lambda_k1, lambda_q2, lambda_k2, g_diff_norm, w_branch_a, w_branch_b,
              w_out, w_router, w_exp_gate, w_exp_up, w_exp_down, g_final):
    rope = rope_tables(x_sample.shape[1])
    xp, xs = x_prompt, x_sample
    ak, av, dk, dv = [], [], [], []
    for l in range(DEPTH):
        lam_init = 0.8 - 0.6 * math.exp(-0.3 * l)
        mix_w = (w_in[l], g_q_norm[l], g_k_norm[l], lambda_q1[l], lambda_k1[l], lambda_q2[l],
                 lambda_k2[l], g_diff_norm[l], w_branch_a[l], w_branch_b[l], w_out[l])
        moe_w = (w_router[l], w_exp_gate[l], w_exp_up[l], w_exp_down[l])
        mod_ctx = (jax.nn.silu(c_ctx) @ w_mod[l] + b_mod[l])[None, None, :]
        mod_lat = (jax.nn.silu(c) @ w_mod[l] + b_mod[l])[:, None, :]
        xp, (ka, va, kb, vb) = trunk_layer(xp, mod_ctx, g_attn_norm[l], g_ffn_norm[l], mix_w, moe_w,
                                           None, None, lam_init)
        ak.append(ka); av.append(va); dk.append(kb); dv.append(vb)
        ctx = (cache_attn_k[:, l], cache_attn_v[:, l], cache_diff_k[:, l], cache_diff_v[:, l])
        xs, _ = trunk_layer(xs, mod_lat, g_attn_norm[l], g_ffn_norm[l], mix_w, moe_w,
                            rope, ctx, lam_init)
    y_prompt = rms_norm(xp, g_final)
    y_sample = rms_norm(xs, g_final)
    new_attn_k = jnp.stack(ak, axis=1)
    new_attn_v = jnp.stack(av, axis=1)
    new_diff_k = jnp.stack(dk, axis=1)
    new_diff_v = jnp.stack(dv, axis=1)
    return (y_prompt, y_sample, new_attn_k, new_attn_v, new_diff_k, new_diff_v)
```

```python
import functools
import math

import numpy as np
import jax
import jax.numpy as jnp
from jax import lax
from jax.experimental import pallas as pl
from jax.experimental.pallas import tpu as pltpu

F32 = jnp.float32
BF16 = jnp.bfloat16

HEAD_DIM = 64
HEADS_A = 8
KV_HEADS_A = 2
HEADS_B = 4
N_EXPERTS = 16
CAPACITY_FACTOR = 2
GRID_W = 64
ROPE_THETA = 10000.0
EPS = 1e-6
LANES = 128
ROW_TILE = 256
NEG_BIG = -1e30
NOT_SELECTED = -1e6
VMEM_LIMIT = 56 * 1024 * 1024


def _cparams(n_axes):
    return pltpu.CompilerParams(dimension_semantics=("arbitrary",) * n_axes,
                                vmem_limit_bytes=VMEM_LIMIT)


def _dot(a, b):
    return jnp.dot(a, b, preferred_element_type=F32)


def _dot_nt(a, b):
    return lax.dot_general(a, b, (((1,), (1,)), ((), ())), preferred_element_type=F32)


def _split(a):
    hi = a.astype(BF16)
    lo = (a - hi.astype(F32)).astype(BF16)
    return hi, lo


def _dot3(a, b):
    a_hi, a_lo = _split(a)
    b_hi, b_lo = _split(b)
    return _dot(a_hi, b_hi) + _dot(a_lo, b_hi) + _dot(a_hi, b_lo)


def _rms(x, g):
    return x * lax.rsqrt(jnp.mean(x * x, axis=-1, keepdims=True) + EPS) * g


def _mod_index(mod, tiles_per_set):
    if mod.shape[0] == 1:
        return lambda i: (0, 0, 0)
    return lambda i: (i // tiles_per_set, 0, 0)


def _mod_kernel(c_ref, w_ref, b_ref, o_ref):
    c = c_ref[...]
    a = c * jax.nn.sigmoid(c)
    o_ref[0] = _dot3(a, w_ref[...]) + b_ref[0]


def _modulation(c_rows, w_mod, b_mod):
    r, d = c_rows.shape
    return pl.pallas_call(
        _mod_kernel,
        grid=(6,),
        in_specs=[pl.BlockSpec((r, d), lambda j: (0, 0)),
                  pl.BlockSpec((d, d), lambda j: (0, j)),
                  pl.BlockSpec((1, 1, d), lambda j: (j, 0, 0))],
        out_specs=pl.BlockSpec((1, r, d), lambda j: (j, 0, 0)),
        out_shape=jax.ShapeDtypeStruct((6, r, d), F32),
        name="mod",
        compiler_params=_cparams(1),
    )(c_rows, w_mod, b_mod.reshape(6, 1, d))


def _seg_sumsq(x, ones_blockdiag):
    hi, lo = _split(x * x)
    return _dot(hi, ones_blockdiag) + _dot(lo, ones_blockdiag)


def _rope(x, c, s_up, s_dn):
    w = x.shape[1]
    reps = w // c.shape[1]
    if reps > 1:
        c, s_up, s_dn = (jnp.concatenate([t] * reps, axis=1) for t in (c, s_up, s_dn))
    return x * c + pltpu.roll(x, w - 16, 1) * s_up + pltpu.roll(x, 16, 1) * s_dn


def _inproj_kernel(*refs, rope, emit_cache):
    (x_ref, mod_ref, g_ref, w_ref, gq_ref, gk_ref, bd512_ref, bd128_ref), refs = refs[:8], refs[8:]
    if rope:
        (c_ref, su_ref, sd_ref), refs = refs[:3], refs[3:]
    qa_o, kpa_o, vpa_o, qb_o, kpb_o, vb_o = refs[:6]
    if emit_cache:
        ka_c, va_c, kb_c, vb_c = refs[6:]

    x = x_ref[...]
    mod = mod_ref[0]
    h = (_rms(x, g_ref[...]) * (1.0 + mod[1:2]) + mod[0:1]).astype(BF16)
    if rope:
        tabs = (c_ref[...], su_ref[...], sd_ref[...])
    lane = lax.broadcasted_iota(jnp.int32, (x.shape[0], LANES), 1)
    lo_half = lane < HEAD_DIM
    wa = HEADS_A * HEAD_DIM
    wkv = KV_HEADS_A * HEAD_DIM
    wb = HEADS_B * 2 * HEAD_DIM
    o_ka, o_va, o_qb = wa, wa + wkv, wa + 2 * wkv
    o_kb, o_vb = o_qb + wb, o_qb + 2 * wb
    scale = HEAD_DIM ** -0.5

    qa = _dot(h, w_ref[:, 0:wa])
    qa = qa * lax.rsqrt(_seg_sumsq(qa, bd512_ref[...]) * (1.0 / HEAD_DIM) + EPS) * gq_ref[...]
    if rope:
        qa = _rope(qa, *tabs)
    qa_o[...] = (qa * scale).astype(BF16)

    kv = _dot(h, w_ref[:, o_ka:o_qb])
    ka, va = kv[:, 0:wkv], kv[:, wkv:2 * wkv]
    ka = ka * lax.rsqrt(_seg_sumsq(ka, bd128_ref[...]) * (1.0 / HEAD_DIM) + EPS) * gk_ref[...]
    if emit_cache:
        ka_c[...] = ka
        va_c[...] = va
    if rope:
        ka = _rope(ka, *tabs)
    for val, out in ((ka, kpa_o), (va, vpa_o)):
        swapped = pltpu.roll(val, HEAD_DIM, 1)
        out[:, 0:128] = jnp.where(lo_half, val, 0.0).astype(BF16)
        out[:, 128:256] = jnp.where(lo_half, 0.0, swapped).astype(BF16)
        out[:, 256:384] = jnp.where(lo_half, swapped, 0.0).astype(BF16)
        out[:, 384:512] = jnp.where(lo_half, 0.0, val).astype(BF16)

    qb = _dot(h, w_ref[:, o_qb:o_kb])
    if rope:
        qb = _rope(qb, *tabs)
    qb_o[...] = (qb * scale).astype(BF16)

    kb = _dot(h, w_ref[:, o_kb:o_vb])
    if emit_cache:
        kb_c[...] = kb
    if rope:
        kb = _rope(kb, *tabs)
    for hd in range(HEADS_B):
        blk = kb[:, hd * 128:(hd + 1) * 128]
        kpb_o[:, hd * 256:hd * 256 + 128] = jnp.where(lo_half, blk, 0.0).astype(BF16)
        kpb_o[:, hd * 256 + 128:(hd + 1) * 256] = jnp.where(lo_half, 0.0, blk).astype(BF16)

    vb = _dot(h, w_ref[:, o_vb:o_vb + wb])
    if emit_cache:
        vb_c[...] = vb
    vb_o[...] = vb.astype(BF16)


def _blockdiag_ones(width):
    g = np.arange(width) // HEAD_DIM
    return jnp.asarray((g[:, None] == g[None, :]).astype(np.float32), dtype=BF16)


def _inproj(x2, mod, set_rows, g_attn, w_qkv, gq_t, gk_t, rope_tabs, emit_cache):
    t, d = x2.shape
    tm = ROW_TILE
    tiles_per_set = set_rows // tm
    rope = rope_tabs is not None
    nq = w_qkv.shape[1]
    row = lambda i: (i, 0)
    const = lambda i: (0, 0)
    in_specs = [pl.BlockSpec((tm, d), row),
                pl.BlockSpec((1, 8, d), _mod_index(mod, tiles_per_set)),
                pl.BlockSpec((1, d), const),
                pl.BlockSpec((d, nq), const),
                pl.BlockSpec((1, 512), const),
                pl.BlockSpec((1, 128), const),
                pl.BlockSpec((512, 512), const),
                pl.BlockSpec((128, 128), const)]
    args = [x2, mod, g_attn, w_qkv, gq_t, gk_t, _blockdiag_ones(512), _blockdiag_ones(128)]
    if rope:
        in_specs += [pl.BlockSpec((tm, LANES), lambda i: (i % tiles_per_set, 0))] * 3
        args += list(rope_tabs)
    widths = [(512, BF16), (512, BF16), (512, BF16), (512, BF16), (1024, BF16), (512, BF16)]
    if emit_cache:
        widths += [(128, F32), (128, F32), (512, F32), (512, F32)]
    return pl.pallas_call(
        functools.partial(_inproj_kernel, rope=rope, emit_cache=emit_cache),
        grid=(t // tm,),
        in_specs=in_specs,
        out_specs=[pl.BlockSpec((tm, w), row) for w, _ in widths],
        out_shape=[jax.ShapeDtypeStruct((t, w), dt) for w, dt in widths],
        name="inproj_rope" if rope else "inproj",
        compiler_params=_cparams(1),
    )(*args)


def _cache_prep_kernel(ka_ref, va_ref, kb_ref, vb_ref, kpa_o, vpa_o, kpb_o, vb_o):
    lane = lax.broadcasted_iota(jnp.int32, (ka_ref.shape[0], LANES), 1)
    lo_half = lane < HEAD_DIM
    for src, out in ((ka_ref, kpa_o), (va_ref, vpa_o)):
        val = src[...]
        swapped = pltpu.roll(val, HEAD_DIM, 1)
        out[:, 0:128] = jnp.where(lo_half, val, 0.0).astype(BF16)
        out[:, 128:256] = jnp.where(lo_half, 0.0, swapped).astype(BF16)
        out[:, 256:384] = jnp.where(lo_half, swapped, 0.0).astype(BF16)
        out[:, 384:512] = jnp.where(lo_half, 0.0, val).astype(BF16)
    for hd in range(HEADS_B):
        blk = kb_ref[:, hd * 128:(hd + 1) * 128]
        kpb_o[:, hd * 256:hd * 256 + 128] = jnp.where(lo_half, blk, 0.0).astype(BF16)
        kpb_o[:, hd * 256 + 128:(hd + 1) * 256] = jnp.where(lo_half, 0.0, blk).astype(BF16)
    vb_o[...] = vb_ref[...].astype(BF16)


def _cache_prep(ka, va, kb, vb):
    t = ka.shape[0]
    tm = ROW_TILE
    row = lambda i: (i, 0)
    widths = [512, 512, 1024, 512]
    return pl.pallas_call(
        _cache_prep_kernel,
        grid=(t // tm,),
        in_specs=[pl.BlockSpec((tm, a.shape[1]), row) for a in (ka, va, kb, vb)],
        out_specs=[pl.BlockSpec((tm, w), row) for w in widths],
        out_shape=[jax.ShapeDtypeStruct((t, w), BF16) for w in widths],
        name="cache_prep",
        compiler_params=_cparams(1),
    )(ka, va, kb, vb)


def _softmax_parts(scores):
    m = scores[0].max(axis=-1, keepdims=True)
    for s in scores[1:]:
        m = jnp.maximum(m, s.max(axis=-1, keepdims=True))
    es = [jnp.exp(s - m) for s in scores]
    l = es[0].sum(axis=-1, keepdims=True)
    for e in es[1:]:
        l = l + e.sum(axis=-1, keepdims=True)
    return es, l


def _gqa_kernel(*refs, n_src, groups):
    q_ref = refs[0]
    k_refs = refs[1:1 + n_src]
    v_refs = refs[1 + n_src:1 + 2 * n_src]
    o_ref = refs[1 + 2 * n_src]
    tq = q_ref.shape[0]
    lane = lax.broadcasted_iota(jnp.int32, (2 * tq, LANES), 1)
    lo_half = lane < HEAD_DIM
    for g in range(groups):
        c0 = g * 256
        q = jnp.concatenate([q_ref[:, c0:c0 + 128], q_ref[:, c0 + 128:c0 + 256]], axis=0)
        s_lo = [_dot_nt(q, k[:, c0:c0 + 128]) for k in k_refs]
        s_hi = [_dot_nt(q, k[:, c0 + 128:c0 + 256]) for k in k_refs]
        e_lo, l_lo = _softmax_parts(s_lo)
        e_hi, l_hi = _softmax_parts(s_hi)
        o = None
        for e_part, v in zip(e_lo, v_refs):
            t = _dot(e_part.astype(BF16), v[:, c0:c0 + 128])
            o = t if o is None else o + t
        for e_part, v in zip(e_hi, v_refs):
            o = o + _dot(e_part.astype(BF16), v[:, c0 + 128:c0 + 256])
        o = o * jnp.where(lo_half, 1.0 / l_lo, 1.0 / l_hi)
        o_ref[:, c0:c0 + 128] = o[0:tq].astype(BF16)
        o_ref[:, c0 + 128:c0 + 256] = o[tq:2 * tq].astype(BF16)


def _diff_kernel(*refs, n_src, heads, head_axis, lam_init):
    q_ref = refs[0]
    k_refs = refs[1:1 + n_src]
    v_refs = refs[1 + n_src:1 + 2 * n_src]
    lq1, lk1, lq2, lk2, gd_ref, o_ref = refs[1 + 2 * n_src:]
    lam_all = (jnp.exp(jnp.sum(lq1[...] * lk1[...], axis=-1, keepdims=True))
               - jnp.exp(jnp.sum(lq2[...] * lk2[...], axis=-1, keepdims=True)) + lam_init)
    for j in range(heads):
        if head_axis is None:
            lam = lam_all[j:j + 1, :]
        else:
            hsel = lax.broadcasted_iota(jnp.int32, lam_all.shape, 0) == pl.program_id(head_axis)
            lam = jnp.sum(jnp.where(hsel, lam_all, 0.0), axis=0, keepdims=True)
        q = q_ref[:, j * 128:(j + 1) * 128]
        s0 = [_dot_nt(q, k[:, j * 256:j * 256 + 128]) for k in k_refs]
        s1 = [_dot_nt(q, k[:, j * 256 + 128:(j + 1) * 256]) for k in k_refs]
        e0, l0 = _softmax_parts(s0)
        e1, l1 = _softmax_parts(s1)
        r0 = 1.0 / l0
        r1 = lam / l1
        o = None
        for a0, a1, v in zip(e0, e1, v_refs):
            t = _dot((a0 * r0 - a1 * r1).astype(BF16), v[:, j * 128:(j + 1) * 128])
            o = t if o is None else o + t
        o = _rms(o, gd_ref[...]) * (1.0 - lam_init)
        o_ref[:, j * 128:(j + 1) * 128] = o.astype(BF16)


def _attention(kind, q, ks, vs, n_sets, set_rows, tq, per_step, extra, extra_kw):
    t = q.shape[0]
    n_units = (2 if kind == "gqa" else HEADS_B) // per_step
    q_tiles = set_rows // tq
    qw = (256 if kind == "gqa" else 128) * per_step
    kw_ = 256 * per_step
    vw = qw
    in_specs = [pl.BlockSpec((tq, qw), lambda b, u, i: (b * q_tiles + i, u))]
    for k in ks:
        rows = k.shape[0] // n_sets
        in_specs.append(pl.BlockSpec((rows, kw_), lambda b, u, i: (b, u)))
    for v in vs:
        rows = v.shape[0] // n_sets
        in_specs.append(pl.BlockSpec((rows, vw), lambda b, u, i: (b, u)))
    for e in extra:
        in_specs.append(pl.BlockSpec(e.shape, lambda b, u, i: (0, 0)))
    if kind == "gqa":
        body = functools.partial(_gqa_kernel, n_src=len(ks), groups=per_step)
    else:
        body = functools.partial(_diff_kernel, n_src=len(ks), heads=per_step,
                                 head_axis=(1 if per_step == 1 else None), **extra_kw)
    return pl.pallas_call(
        body,
        grid=(n_sets, n_units, q_tiles),
        in_specs=in_specs,
        out_specs=pl.BlockSpec((tq, qw), lambda b, u, i: (b * q_tiles + i, u)),
        out_shape=jax.ShapeDtypeStruct((t, 512), BF16),
        name=f"{kind}_attn_{len(ks)}src",
        compiler_params=_cparams(3),
    )(q, *ks, *vs, *extra)


def _post_kernel(x_ref, ya_ref, yb_ref, mod_ref, g1_ref, g2_ref, wg_ref, wba_ref, wbb_ref, wo_ref, wr_ref,
                 x1_o, h2_o, aff_o):
    x = x_ref[...]
    mod = mod_ref[0]
    h = (_rms(x, g1_ref[...]) * (1.0 + mod[1:2]) + mod[0:1]).astype(BF16)
    d = x.shape[1]
    ga = jax.nn.sigmoid(_dot(h, wg_ref[:, 0:d]))
    merged = ga * _dot(ya_ref[...], wba_ref[...])
    gb = jax.nn.sigmoid(_dot(h, wg_ref[:, d:2 * d]))
    merged = merged + gb * _dot(yb_ref[...], wbb_ref[...])
    m = _dot(merged.astype(BF16), wo_ref[...])
    x1 = x + mod[2:3] * m
    x1_o[...] = x1
    h2 = _rms(x1, g2_ref[...]) * (1.0 + mod[4:5]) + mod[3:4]
    h2_o[...] = h2.astype(BF16)
    logits = _dot3(h2, wr_ref[...])
    lane = lax.broadcasted_iota(jnp.int32, logits.shape, 1)
    logits = jnp.where(lane < N_EXPERTS, logits, NEG_BIG)
    e = jnp.exp(logits - logits.max(axis=-1, keepdims=True))
    aff_o[...] = e / e.sum(axis=-1, keepdims=True)


def _post(x2, ya, yb, mod, set_rows, g1, g2, w_gate, w_ba, w_bb, w_out, w_router_p):
    t, d = x2.shape
    tm = ROW_TILE
    tiles_per_set = set_rows // tm
    row = lambda i: (i, 0)
    const = lambda i: (0, 0)
    return pl.pallas_call(
        _post_kernel,
        grid=(t // tm,),
        in_specs=[pl.BlockSpec((tm, d), row),
                  pl.BlockSpec((tm, 512), row),
                  pl.BlockSpec((tm, 512), row),
                  pl.BlockSpec((1, 8, d), _mod_index(mod, tiles_per_set)),
                  pl.BlockSpec((1, d), const),
                  pl.BlockSpec((1, d), const),
                  pl.BlockSpec(w_gate.shape, const),
                  pl.BlockSpec(w_ba.shape, const),
                  pl.BlockSpec(w_bb.shape, const),
                  pl.BlockSpec(w_out.shape, const),
                  pl.BlockSpec(w_router_p.shape, const)],
        out_specs=[pl.BlockSpec((tm, d), row), pl.BlockSpec((tm, d), row), pl.BlockSpec((tm, LANES), row)],
        out_shape=[jax.ShapeDtypeStruct((t, d), F32), jax.ShapeDtypeStruct((t, d), BF16),
                   jax.ShapeDtypeStruct((t, LANES), F32)],
        name="post_attn",
        compiler_params=_cparams(1),
    )(x2, ya, yb, mod, g1, g2, w_gate, w_ba, w_bb, w_out, w_router_p)


def _route_kernel(aff_ref, h2_ref, xe_o, gt_o, *, cap):
    aff = aff_ref[...]
    n = aff.shape[0]
    capf = float(cap)

    def enough(cand):
        return jnp.sum(jnp.where(aff >= cand, 1.0, 0.0), axis=0, keepdims=True) >= capf

    pw = jnp.ones((1, LANES), F32)
    for k in (64, 32, 16, 8, 4, 2, 1):
        pw = jnp.where(enough(pw * 2.0 ** -(k - 1)), pw, pw * 2.0 ** -k)

    def mantissa_step(_, carry):
        thr, step = carry
        cand = thr + step
        return jnp.where(enough(cand), cand, thr), step * 0.5

    thr, _ = lax.fori_loop(0, 23, mantissa_step, (pw, pw * 0.5))
    above = aff > thr
    tied = aff == thr
    need = capf - jnp.sum(jnp.where(above, 1.0, 0.0), axis=0, keepdims=True)
    r_i = lax.broadcasted_iota(jnp.int32, (n, n), 0)
    c_i = lax.broadcasted_iota(jnp.int32, (n, n), 1)
    before = jnp.where(c_i < r_i, 1.0, 0.0).astype(BF16)
    tie_rank = _dot(before, jnp.where(tied, 1.0, 0.0).astype(BF16))
    sel = above | (tied & (tie_rank < need))
    slot = _dot(before, jnp.where(sel, 1.0, 0.0).astype(BF16))
    slot = jnp.where(sel, slot, NOT_SELECTED)
    slot_t = slot.T
    h2 = h2_ref[...]

    per = max(1, LANES // cap)
    slot_iota = lax.broadcasted_iota(jnp.int32, (cap, n), 0).astype(F32)
    for e0 in range(0, N_EXPERTS, per):
        onehot = jnp.concatenate(
            [jnp.where(slot_t[e:e + 1, :] == slot_iota, 1.0, 0.0) for e in range(e0, e0 + per)],
            axis=0).astype(BF16)
        rows = _dot(onehot, h2).astype(BF16)
        for j in range(per):
            xe_o[e0 + j] = rows[j * cap:(j + 1) * cap]

    lane_f = lax.broadcasted_iota(jnp.int32, (n, LANES), 1).astype(F32)
    for blk in range(N_EXPERTS * cap // LANES):
        acc = jnp.zeros((n, LANES), F32)
        for j in range(per):
            e = blk * per + j
            hit = (slot[:, e:e + 1] + float(j * cap)) == lane_f
            acc = jnp.where(hit, aff[:, e:e + 1], acc)
        gt_o[:, blk * LANES:(blk + 1) * LANES] = acc.astype(BF16)


def _route(aff, h2, n_sets, set_rows):
    t, d = h2.shape
    n = set_rows
    cap = CAPACITY_FACTOR * n // N_EXPERTS
    assert cap <= LANES and LANES % cap == 0 and cap % 16 == 0
    return pl.pallas_call(
        functools.partial(_route_kernel, cap=cap),
        grid=(n_sets,),
        in_specs=[pl.BlockSpec((n, LANES), lambda s: (s, 0)),
                  pl.BlockSpec((n, d), lambda s: (s, 0))],
        out_specs=[pl.BlockSpec((N_EXPERTS, cap, d), lambda s: (0, s, 0)),
                   pl.BlockSpec((n, N_EXPERTS * cap), lambda s: (s, 0))],
        out_shape=[jax.ShapeDtypeStruct((N_EXPERTS, n_sets * cap, d), BF16),
                   jax.ShapeDtypeStruct((t, N_EXPERTS * cap), BF16)],
        name=f"route_cap{cap}",
        compiler_params=_cparams(1),
    )(aff, h2)


def _expert_kernel(x_ref, wg_ref, wu_ref, wd_ref, o_ref):
    rows = x_ref.shape[1]
    sub = ROW_TILE
    for r0 in range(0, rows, sub):
        x = x_ref[0, r0:r0 + sub, :]
        a = _dot(x, wg_ref[0])
        u = _dot(x, wu_ref[0])
        hmid = (a * jax.nn.sigmoid(a) * u).astype(BF16)
        o_ref[0, r0:r0 + sub, :] = _dot(hmid, wd_ref[0]).astype(BF16)


def _experts(xe, w_gate, w_up, w_down):
    e, rows, d = xe.shape
    f = w_gate.shape[2]
    return pl.pallas_call(
        _expert_kernel,
        grid=(e,),
        in_specs=[pl.BlockSpec((1, rows, d), lambda i: (i, 0, 0)),
                  pl.BlockSpec((1, d, f), lambda i: (i, 0, 0)),
                  pl.BlockSpec((1, d, f), lambda i: (i, 0, 0)),
                  pl.BlockSpec((1, f, d), lambda i: (i, 0, 0))],
        out_specs=pl.BlockSpec((1, rows, d), lambda i: (i, 0, 0)),
        out_shape=jax.ShapeDtypeStruct((e, rows, d), BF16),
        name="experts",
        compiler_params=_cparams(1),
    )(xe, w_gate, w_up, w_down)


def _combine_kernel(gt_ref, ye_ref, x1_ref, mod_ref, gf_ref, y_o):
    e, cap, d = ye_ref.shape
    ye = ye_ref[...].reshape(e * cap, d)
    moe = _dot(gt_ref[...], ye)
    x = x1_ref[...] + mod_ref[0][5:6] * moe
    y_o[...] = _rms(x, gf_ref[...])


def _combine(gt, ye, x1, mod, n_sets, set_rows, g_final):
    t, d = x1.shape
    tm = ROW_TILE
    tiles = set_rows // tm
    cap = ye.shape[1] // n_sets
    return pl.pallas_call(
        _combine_kernel,
        grid=(n_sets, tiles),
        in_specs=[pl.BlockSpec((tm, gt.shape[1]), lambda s, i: (s * tiles + i, 0)),
                  pl.BlockSpec((N_EXPERTS, cap, d), lambda s, i: (0, s, 0)),
                  pl.BlockSpec((tm, d), lambda s, i: (s * tiles + i, 0)),
                  pl.BlockSpec((1, 8, d), (lambda s, i: (s, 0, 0)) if mod.shape[0] > 1 else (lambda s, i: (0, 0, 0))),
                  pl.BlockSpec((1, d), lambda s, i: (0, 0))],
        out_specs=pl.BlockSpec((tm, d), lambda s, i: (s * tiles + i, 0)),
        out_shape=jax.ShapeDtypeStruct((t, d), F32),
        name=f"combine_cap{cap}",
        compiler_params=_cparams(2),
    )(gt, ye, x1, mod, g_final)


def _rope_tables(n_tokens):
    n_rows = n_tokens // GRID_W
    rowp = jnp.repeat(jnp.arange(n_rows), GRID_W).astype(F32)
    colp = jnp.tile(jnp.arange(GRID_W), n_rows).astype(F32)
    quarter = HEAD_DIM // 4
    freqs = ROPE_THETA ** (-jnp.arange(quarter, dtype=F32) / quarter)
    ang = jnp.stack([rowp[:, None] * freqs, colp[:, None] * freqs], axis=1)
    cos, sin = jnp.cos(ang), jnp.sin(ang)
    zero = jnp.zeros_like(sin)
    c = jnp.stack([cos, cos], axis=2).reshape(n_tokens, HEAD_DIM)
    s_up = jnp.stack([-sin, zero], axis=2).reshape(n_tokens, HEAD_DIM)
    s_dn = jnp.stack([zero, sin], axis=2).reshape(n_tokens, HEAD_DIM)
    return tuple(jnp.tile(t, (1, LANES // HEAD_DIM)) for t in (c, s_up, s_dn))


def _group_forward(x2, n_sets, set_rows, mod, lw, ew, rope_tabs, cache, emit_cache, lam_init, tq):
    (g_attn, g_ffn, w_qkv, w_gates, gq_t, gk_t, lq1, lk1, lq2, lk2, g_diff, w_ba, w_bb, w_out, w_router_p,
     g_final) = lw
    outs = _inproj(x2, mod, set_rows, g_attn, w_qkv, gq_t, gk_t, rope_tabs, emit_cache)
    qa, kpa, vpa, qb, kpb, vb16 = outs[:6]
    ks_a, vs_a, ks_b, vs_b = [kpa], [vpa], [kpb], [vb16]
    if cache is not None:
        ckpa, cvpa, ckpb, cvb = cache
        ks_a, vs_a, ks_b, vs_b = [ckpa, kpa], [cvpa, vpa], [ckpb, kpb], [cvb, vb16]
    whole = tq == set_rows
    ya = _attention("gqa", qa, ks_a, vs_a, n_sets, set_rows, tq, 2 if whole else 1, (), {})
    yb = _attention("diff", qb, ks_b, vs_b, n_sets, set_rows, tq, HEADS_B if whole else 1,
                    (lq1, lk1, lq2, lk2, g_diff), {"lam_init": lam_init})
    x1, h2, aff = _post(x2, ya, yb, mod, set_rows, g_attn, g_ffn, w_gates, w_ba, w_bb, w_out, w_router_p)
    xe, gt = _route(aff, h2, n_sets, set_rows)
    ye = _experts(xe, *ew)
    y = _combine(gt, ye, x1, mod, n_sets, set_rows, g_final)
    return y, x1, outs[6:]


def kernel(x_prompt, x_sample, cache_attn_k, cache_attn_v, cache_diff_k, cache_diff_v, c, c_ctx, w_mod, b_mod,
           g_attn_norm, g_ffn_norm, w_in, g_q_norm, g_k_norm, lambda_q1, lambda_k1, lambda_q2, lambda_k2,
           g_diff_norm, w_branch_a, w_branch_b, w_out, w_router, w_exp_gate, w_exp_up, w_exp_down, g_final):
    batch, seq, d = x_prompt.shape
    dec_batch, dec_seq, _ = x_sample.shape
    depth = w_in.shape[0]
    assert depth == 1, "the final norm is fused into the layer's combine step"
    past = cache_attn_k.shape[2]
    n_qkv = HEADS_A * HEAD_DIM + 2 * KV_HEADS_A * HEAD_DIM + 3 * HEADS_B * 2 * HEAD_DIM

    xp = x_prompt.reshape(batch * seq, d)
    xs = x_sample.reshape(dec_batch * dec_seq, d)
    rope_tabs = _rope_tables(dec_seq)
    c_rows = jnp.concatenate([c, c_ctx[None, :], jnp.zeros((16 - dec_batch - 1, d), F32)], axis=0)
    yp = ys = None
    caches = []
    for l in range(depth):
        lam_init = 0.8 - 0.6 * math.exp(-0.3 * l)
        mod6 = _modulation(c_rows, w_mod[l], b_mod[l])
        mod = jnp.pad(jnp.transpose(mod6, (1, 0, 2)), ((0, 0), (0, 2), (0, 0)))
        mod_lat, mod_ctx = mod[:dec_batch], mod[dec_batch:dec_batch + 1]
        w_l = w_in[l]
        lw = (g_attn_norm[l][None, :], g_ffn_norm[l][None, :],
              w_l[:, :n_qkv].astype(BF16), w_l[:, n_qkv:].astype(BF16),
              jnp.tile(g_q_norm[l], HEADS_A)[None, :], jnp.tile(g_k_norm[l], KV_HEADS_A)[None, :],
              lambda_q1[l], lambda_k1[l], lambda_q2[l], lambda_k2[l], g_diff_norm[l][None, :],
              w_branch_a[l].astype(BF16), w_branch_b[l].astype(BF16), w_out[l].astype(BF16),
              jnp.pad(w_router[l], ((0, 0), (0, LANES - N_EXPERTS))), g_final[None, :])
        ew = (w_exp_gate[l].astype(BF16), w_exp_up[l].astype(BF16), w_exp_down[l].astype(BF16))
        yp, xp, cache_out = _group_forward(xp, batch, seq, mod_ctx, lw, ew, None, None, True, lam_init, seq)
        caches.append(cache_out)
        cache_l = _cache_prep(cache_attn_k[:, l].reshape(dec_batch * past, -1),
                              cache_attn_v[:, l].reshape(dec_batch * past, -1),
                              cache_diff_k[:, l].reshape(dec_batch * past, -1),
                              cache_diff_v[:, l].reshape(dec_batch * past, -1))
        ys, xs, _ = _group_forward(xs, dec_batch, dec_seq, mod_lat, lw, ew, rope_tabs, cache_l, False, lam_init,
                                   128)
    y_prompt = yp.reshape(batch, seq, d)
    y_sample = ys.reshape(dec_batch, dec_seq, d)
    stack = lambda i, shape: jnp.stack([cc[i].reshape(shape) for cc in caches], axis=1)
    new_attn_k = stack(0, (batch, seq, KV_HEADS_A, HEAD_DIM))
    new_attn_v = stack(1, (batch, seq, KV_HEADS_A, HEAD_DIM))
    new_diff_k = stack(2, (batch, seq, HEADS_B, 2, HEAD_DIM))
    new_diff_v = stack(3, (batch, seq, HEADS_B, 2 * HEAD_DIM))
    return (y_prompt, y_sample, new_attn_k, new_attn_v, new_diff_k, new_diff_v)
```

```python
import functools
import math

import numpy as np
import jax
import jax.numpy as jnp
from jax import lax
from jax.experimental import pallas as pl
from jax.experimental.pallas import tpu as pltpu

F32 = jnp.float32
BF16 = jnp.bfloat16

HEAD_DIM = 64
HEADS_A = 8
KV_HEADS_A = 2
HEADS_B = 4
N_QKV = HEADS_A * HEAD_DIM + 2 * KV_HEADS_A * HEAD_DIM + 3 * HEADS_B * 2 * HEAD_DIM
N_EXPERTS = 16
CAPACITY_FACTOR = 2
GRID_W = 64
ROPE_THETA = 10000.0
EPS = 1e-6
LANES = 128
ROW_TILE = 256
NEG_BIG = -1e30
NOT_SELECTED = -1e6
VMEM_LIMIT = 56 * 1024 * 1024


def _cparams(n_axes):
    return pltpu.CompilerParams(dimension_semantics=("arbitrary",) * n_axes,
                                vmem_limit_bytes=VMEM_LIMIT)


def _dot(a, b):
    return jnp.dot(a, b, preferred_element_type=F32)


def _dot_nt(a, b):
    return lax.dot_general(a, b, (((1,), (1,)), ((), ())), preferred_element_type=F32)


def _split(a):
    hi = a.astype(BF16)
    lo = (a - hi.astype(F32)).astype(BF16)
    return hi, lo


def _dot3(a, b):
    a_hi, a_lo = _split(a)
    b_hi, b_lo = _split(b)
    return _dot(a_hi, b_hi) + _dot(a_lo, b_hi) + _dot(a_hi, b_lo)


def _rms(x, g):
    return x * lax.rsqrt(jnp.mean(x * x, axis=-1, keepdims=True) + EPS) * g


def _mod_index(mod, tiles_per_set):
    if mod.shape[0] == 1:
        return lambda i: (0, 0, 0)
    return lambda i: (i // tiles_per_set, 0, 0)


def _mod_kernel(c_ref, w_ref, b_ref, o_ref):
    c = c_ref[...]
    a = c * jax.nn.sigmoid(c)
    o_ref[0] = _dot3(a, w_ref[...]) + b_ref[0]


def _modulation(c_rows, w_mod, b_mod):
    r, d = c_rows.shape
    return pl.pallas_call(
        _mod_kernel,
        grid=(6,),
        in_specs=[pl.BlockSpec((r, d), lambda j: (0, 0)),
                  pl.BlockSpec((d, d), lambda j: (0, j)),
                  pl.BlockSpec((1, 1, d), lambda j: (j, 0, 0))],
        out_specs=pl.BlockSpec((1, r, d), lambda j: (j, 0, 0)),
        out_shape=jax.ShapeDtypeStruct((6, r, d), F32),
        name="mod",
        compiler_params=_cparams(1),
    )(c_rows, w_mod, b_mod.reshape(6, 1, d))


def _seg_sumsq(x, ones_blockdiag):
    hi, lo = _split(x * x)
    return _dot(hi, ones_blockdiag) + _dot(lo, ones_blockdiag)


def _rope(x, c, s_up, s_dn):
    w = x.shape[1]
    reps = w // c.shape[1]
    if reps > 1:
        c, s_up, s_dn = (jnp.concatenate([t] * reps, axis=1) for t in (c, s_up, s_dn))
    return x * c + pltpu.roll(x, w - 16, 1) * s_up + pltpu.roll(x, 16, 1) * s_dn


def _inproj_kernel(*refs, rope, emit_cache):
    (x_ref, mod_ref, g_ref, w32_ref, gq_ref, gk_ref, bd512_ref, bd128_ref), refs = refs[:8], refs[8:]
    if rope:
        (c_ref, su_ref, sd_ref), refs = refs[:3], refs[3:]
    qa_o, kpa_o, vpa_o, qb_o, kpb_o, vb_o = refs[:6]
    if emit_cache:
        ka_c, va_c, kb_c, vb_c = refs[6:10]
    w_ref = refs[-1]

    @pl.when(pl.program_id(0) == 0)
    def _():
        w_ref[...] = w32_ref[...].astype(BF16)

    x = x_ref[...]
    mod = mod_ref[0]
    h = (_rms(x, g_ref[...]) * (1.0 + mod[1:2]) + mod[0:1]).astype(BF16)
    if rope:
        tabs = (c_ref[...], su_ref[...], sd_ref[...])
    lane = lax.broadcasted_iota(jnp.int32, (x.shape[0], LANES), 1)
    lo_half = lane < HEAD_DIM
    wa = HEADS_A * HEAD_DIM
    wkv = KV_HEADS_A * HEAD_DIM
    wb = HEADS_B * 2 * HEAD_DIM
    o_ka, o_va, o_qb = wa, wa + wkv, wa + 2 * wkv
    o_kb, o_vb = o_qb + wb, o_qb + 2 * wb
    scale = HEAD_DIM ** -0.5

    qa = _dot(h, w_ref[:, 0:wa])
    qa = qa * lax.rsqrt(_seg_sumsq(qa, bd512_ref[...]) * (1.0 / HEAD_DIM) + EPS) * gq_ref[...]
    if rope:
        qa = _rope(qa, *tabs)
    qa_o[...] = (qa * scale).astype(BF16)

    kv = _dot(h, w_ref[:, o_ka:o_qb])
    ka, va = kv[:, 0:wkv], kv[:, wkv:2 * wkv]
    ka = ka * lax.rsqrt(_seg_sumsq(ka, bd128_ref[...]) * (1.0 / HEAD_DIM) + EPS) * gk_ref[...]
    if emit_cache:
        ka_c[...] = ka
        va_c[...] = va
    if rope:
        ka = _rope(ka, *tabs)
    for val, out in ((ka, kpa_o), (va, vpa_o)):
        swapped = pltpu.roll(val, HEAD_DIM, 1)
        out[:, 0:128] = jnp.where(lo_half, val, 0.0).astype(BF16)
        out[:, 128:256] = jnp.where(lo_half, 0.0, swapped).astype(BF16)
        out[:, 256:384] = jnp.where(lo_half, swapped, 0.0).astype(BF16)
        out[:, 384:512] = jnp.where(lo_half, 0.0, val).astype(BF16)

    qb = _dot(h, w_ref[:, o_qb:o_kb])
    if rope:
        qb = _rope(qb, *tabs)
    qb_o[...] = (qb * scale).astype(BF16)

    kb = _dot(h, w_ref[:, o_kb:o_vb])
    if emit_cache:
        kb_c[...] = kb
    if rope:
        kb = _rope(kb, *tabs)
    for hd in range(HEADS_B):
        blk = kb[:, hd * 128:(hd + 1) * 128]
        kpb_o[:, hd * 256:hd * 256 + 128] = jnp.where(lo_half, blk, 0.0).astype(BF16)
        kpb_o[:, hd * 256 + 128:(hd + 1) * 256] = jnp.where(lo_half, 0.0, blk).astype(BF16)

    vb = _dot(h, w_ref[:, o_vb:o_vb + wb])
    if emit_cache:
        vb_c[...] = vb
    vb_o[...] = vb.astype(BF16)


def _blockdiag_ones(width):
    g = np.arange(width) // HEAD_DIM
    return jnp.asarray((g[:, None] == g[None, :]).astype(np.float32), dtype=BF16)


def _inproj(x2, mod, set_rows, g_attn, w_in, gq_t, gk_t, rope_tabs, emit_cache):
    t, d = x2.shape
    tm = ROW_TILE
    tiles_per_set = set_rows // tm
    rope = rope_tabs is not None
    nq = N_QKV
    row = lambda i: (i, 0)
    const = lambda i: (0, 0)
    in_specs = [pl.BlockSpec((tm, d), row),
                pl.BlockSpec((1, 8, d), _mod_index(mod, tiles_per_set)),
                pl.BlockSpec((1, d), const),
                pl.BlockSpec((d, nq), const, pipeline_mode=pl.Buffered(1)),
                pl.BlockSpec((1, 512), const),
                pl.BlockSpec((1, 128), const),
                pl.BlockSpec((512, 512), const),
                pl.BlockSpec((128, 128), const)]
    args = [x2, mod, g_attn, w_in, gq_t, gk_t, _blockdiag_ones(512), _blockdiag_ones(128)]
    if rope:
        in_specs += [pl.BlockSpec((tm, LANES), lambda i: (i % tiles_per_set, 0))] * 3
        args += list(rope_tabs)
    widths = [(512, BF16), (512, BF16), (512, BF16), (512, BF16), (1024, BF16), (512, BF16)]
    if emit_cache:
        widths += [(128, F32), (128, F32), (512, F32), (512, F32)]
    return pl.pallas_call(
        functools.partial(_inproj_kernel, rope=rope, emit_cache=emit_cache),
        grid=(t // tm,),
        in_specs=in_specs,
        out_specs=[pl.BlockSpec((tm, w), row) for w, _ in widths],
        out_shape=[jax.ShapeDtypeStruct((t, w), dt) for w, dt in widths],
        scratch_shapes=[pltpu.VMEM((d, nq), BF16)],
        name="inproj_rope" if rope else "inproj",
        compiler_params=_cparams(1),
    )(*args)


def _cache_prep_kernel(ka_ref, va_ref, kb_ref, vb_ref, kpa_o, vpa_o, kpb_o, vb_o):
    lane = lax.broadcasted_iota(jnp.int32, (ka_ref.shape[0], LANES), 1)
    lo_half = lane < HEAD_DIM
    for src, out in ((ka_ref, kpa_o), (va_ref, vpa_o)):
        val = src[...]
        swapped = pltpu.roll(val, HEAD_DIM, 1)
        out[:, 0:128] = jnp.where(lo_half, val, 0.0).astype(BF16)
        out[:, 128:256] = jnp.where(lo_half, 0.0, swapped).astype(BF16)
        out[:, 256:384] = jnp.where(lo_half, swapped, 0.0).astype(BF16)
        out[:, 384:512] = jnp.where(lo_half, 0.0, val).astype(BF16)
    for hd in range(HEADS_B):
        blk = kb_ref[:, hd * 128:(hd + 1) * 128]
        kpb_o[:, hd * 256:hd * 256 + 128] = jnp.where(lo_half, blk, 0.0).astype(BF16)
        kpb_o[:, hd * 256 + 128:(hd + 1) * 256] = jnp.where(lo_half, 0.0, blk).astype(BF16)
    vb_o[...] = vb_ref[...].astype(BF16)


def _cache_prep(ka, va, kb, vb):
    t = ka.shape[0]
    tm = ROW_TILE
    row = lambda i: (i, 0)
    widths = [512, 512, 1024, 512]
    return pl.pallas_call(
        _cache_prep_kernel,
        grid=(t // tm,),
        in_specs=[pl.BlockSpec((tm, a.shape[1]), row) for a in (ka, va, kb, vb)],
        out_specs=[pl.BlockSpec((tm, w), row) for w in widths],
        out_shape=[jax.ShapeDtypeStruct((t, w), BF16) for w in widths],
        name="cache_prep",
        compiler_params=_cparams(1),
    )(ka, va, kb, vb)


def _softmax_parts(scores):
    m = scores[0].max(axis=-1, keepdims=True)
    for s in scores[1:]:
        m = jnp.maximum(m, s.max(axis=-1, keepdims=True))
    es = [jnp.exp(s - m) for s in scores]
    l = es[0].sum(axis=-1, keepdims=True)
    for e in es[1:]:
        l = l + e.sum(axis=-1, keepdims=True)
    return es, l


def _gqa_kernel(*refs, n_src):
    groups = KV_HEADS_A
    q_ref = refs[0]
    k_refs = refs[1:1 + n_src]
    v_refs = refs[1 + n_src:1 + 2 * n_src]
    o_ref = refs[1 + 2 * n_src]
    tq = q_ref.shape[0]
    lane = lax.broadcasted_iota(jnp.int32, (2 * tq, LANES), 1)
    lo_half = lane < HEAD_DIM
    for g in range(groups):
        c0 = g * 256
        q = jnp.concatenate([q_ref[:, c0:c0 + 128], q_ref[:, c0 + 128:c0 + 256]], axis=0)
        s_lo = [_dot_nt(q, k[:, c0:c0 + 128]) for k in k_refs]
        s_hi = [_dot_nt(q, k[:, c0 + 128:c0 + 256]) for k in k_refs]
        e_lo, l_lo = _softmax_parts(s_lo)
        e_hi, l_hi = _softmax_parts(s_hi)
        o = None
        for e_part, v in zip(e_lo, v_refs):
            t = _dot(e_part.astype(BF16), v[:, c0:c0 + 128])
            o = t if o is None else o + t
        for e_part, v in zip(e_hi, v_refs):
            o = o + _dot(e_part.astype(BF16), v[:, c0 + 128:c0 + 256])
        o = o * jnp.where(lo_half, 1.0 / l_lo, 1.0 / l_hi)
        o_ref[:, c0:c0 + 128] = o[0:tq].astype(BF16)
        o_ref[:, c0 + 128:c0 + 256] = o[tq:2 * tq].astype(BF16)


def _diff_kernel(*refs, n_src, lam_init):
    q_ref = refs[0]
    k_refs = refs[1:1 + n_src]
    v_refs = refs[1 + n_src:1 + 2 * n_src]
    lq1, lk1, lq2, lk2, gd_ref, o_ref = refs[1 + 2 * n_src:]
    lam_all = (jnp.exp(jnp.sum(lq1[...] * lk1[...], axis=-1, keepdims=True))
               - jnp.exp(jnp.sum(lq2[...] * lk2[...], axis=-1, keepdims=True)) + lam_init)
    for j in range(HEADS_B):
        lam = lam_all[j:j + 1, :]
        q = q_ref[:, j * 128:(j + 1) * 128]
        s0 = [_dot_nt(q, k[:, j * 256:j * 256 + 128]) for k in k_refs]
        s1 = [_dot_nt(q, k[:, j * 256 + 128:(j + 1) * 256]) for k in k_refs]
        e0, l0 = _softmax_parts(s0)
        e1, l1 = _softmax_parts(s1)
        r0 = 1.0 / l0
        r1 = lam / l1
        o = None
        for a0, a1, v in zip(e0, e1, v_refs):
            t = _dot((a0 * r0 - a1 * r1).astype(BF16), v[:, j * 128:(j + 1) * 128])
            o = t if o is None else o + t
        o = _rms(o, gd_ref[...]) * (1.0 - lam_init)
        o_ref[:, j * 128:(j + 1) * 128] = o.astype(BF16)


def _attention(kind, q, ks, vs, n_sets, set_rows, tq, extra, extra_kw):
    t, qw = q.shape
    q_tiles = set_rows // tq
    in_specs = [pl.BlockSpec((tq, qw), lambda b, i: (b * q_tiles + i, 0))]
    for a in list(ks) + list(vs):
        in_specs.append(pl.BlockSpec((a.shape[0] // n_sets, a.shape[1]), lambda b, i: (b, 0)))
    for e in extra:
        in_specs.append(pl.BlockSpec(e.shape, lambda b, i: (0, 0)))
    body = functools.partial(_gqa_kernel if kind == "gqa" else _diff_kernel, n_src=len(ks), **extra_kw)
    return pl.pallas_call(
        body,
        grid=(n_sets, q_tiles),
        in_specs=in_specs,
        out_specs=pl.BlockSpec((tq, qw), lambda b, i: (b * q_tiles + i, 0)),
        out_shape=jax.ShapeDtypeStruct((t, qw), BF16),
        name=f"{kind}_attn_{len(ks)}src",
        compiler_params=_cparams(2),
    )(q, *ks, *vs, *extra)


def _post_kernel(x_ref, ya_ref, yb_ref, mod_ref, g1_ref, g2_ref, win32_ref, wba32_ref, wbb32_ref, wo32_ref, wr_ref,
                 x1_o, h2_o, aff_o, wg_ref, wba_ref, wbb_ref, wo_ref):
    d = x_ref.shape[1]

    @pl.when(pl.program_id(0) == 0)
    def _():
        wg_ref[...] = win32_ref[:, win32_ref.shape[1] - 2 * d:].astype(BF16)
        wba_ref[...] = wba32_ref[...].astype(BF16)
        wbb_ref[...] = wbb32_ref[...].astype(BF16)
        wo_ref[...] = wo32_ref[...].astype(BF16)

    x = x_ref[...]
    mod = mod_ref[0]
    h = (_rms(x, g1_ref[...]) * (1.0 + mod[1:2]) + mod[0:1]).astype(BF16)
    ga = jax.nn.sigmoid(_dot(h, wg_ref[:, 0:d]))
    merged = ga * _dot(ya_ref[...], wba_ref[...])
    gb = jax.nn.sigmoid(_dot(h, wg_ref[:, d:2 * d]))
    merged = merged + gb * _dot(yb_ref[...], wbb_ref[...])
    m = _dot(merged.astype(BF16), wo_ref[...])
    x1 = x + mod[2:3] * m
    x1_o[...] = x1
    h2 = _rms(x1, g2_ref[...]) * (1.0 + mod[4:5]) + mod[3:4]
    h2_o[...] = h2.astype(BF16)
    logits = _dot3(h2, wr_ref[...])
    lane = lax.broadcasted_iota(jnp.int32, logits.shape, 1)
    logits = jnp.where(lane < N_EXPERTS, logits, NEG_BIG)
    e = jnp.exp(logits - logits.max(axis=-1, keepdims=True))
    aff_o[...] = e / e.sum(axis=-1, keepdims=True)


def _post(x2, ya, yb, mod, set_rows, g1, g2, w_in, w_ba, w_bb, w_out, w_router_p):
    t, d = x2.shape
    tm = ROW_TILE
    tiles_per_set = set_rows // tm
    row = lambda i: (i, 0)
    const = lambda i: (0, 0)
    once = pl.Buffered(1)
    half = w_in.shape[1] // 2
    assert w_in.shape[1] == 2 * half and half % LANES == 0 and half >= 2 * d
    return pl.pallas_call(
        _post_kernel,
        grid=(t // tm,),
        in_specs=[pl.BlockSpec((tm, d), row),
                  pl.BlockSpec((tm, 512), row),
                  pl.BlockSpec((tm, 512), row),
                  pl.BlockSpec((1, 8, d), _mod_index(mod, tiles_per_set)),
                  pl.BlockSpec((1, d), const),
                  pl.BlockSpec((1, d), const),
                  pl.BlockSpec((d, half), lambda i: (0, 1), pipeline_mode=once),
                  pl.BlockSpec(w_ba.shape, const, pipeline_mode=once),
                  pl.BlockSpec(w_bb.shape, const, pipeline_mode=once),
                  pl.BlockSpec(w_out.shape, const, pipeline_mode=once),
                  pl.BlockSpec(w_router_p.shape, const)],
        out_specs=[pl.BlockSpec((tm, d), row), pl.BlockSpec((tm, d), row), pl.BlockSpec((tm, LANES), row)],
        out_shape=[jax.ShapeDtypeStruct((t, d), F32), jax.ShapeDtypeStruct((t, d), BF16),
                   jax.ShapeDtypeStruct((t, LANES), F32)],
        scratch_shapes=[pltpu.VMEM((d, 2 * d), BF16), pltpu.VMEM(w_ba.shape, BF16),
                        pltpu.VMEM(w_bb.shape, BF16), pltpu.VMEM(w_out.shape, BF16)],
        name="post_attn",
        compiler_params=_cparams(1),
    )(x2, ya, yb, mod, g1, g2, w_in, w_ba, w_bb, w_out, w_router_p)


def _route_kernel(aff_ref, h2_ref, xe_o, gt_o, *, cap):
    aff = aff_ref[...]
    n = aff.shape[0]
    capf = float(cap)

    def enough(cand):
        return jnp.sum(jnp.where(aff >= cand, 1.0, 0.0), axis=0, keepdims=True) >= capf

    pw = jnp.ones((1, LANES), F32)
    for k in (64, 32, 16, 8, 4, 2, 1):
        pw = jnp.where(enough(pw * 2.0 ** -(k - 1)), pw, pw * 2.0 ** -k)

    def mantissa_step(_, carry):
        thr, step = carry
        cand = thr + step
        return jnp.where(enough(cand), cand, thr), step * 0.5

    thr, _ = lax.fori_loop(0, 23, mantissa_step, (pw, pw * 0.5))
    above = aff > thr
    tied = aff == thr
    need = capf - jnp.sum(jnp.where(above, 1.0, 0.0), axis=0, keepdims=True)
    r_i = lax.broadcasted_iota(jnp.int32, (n, n), 0)
    c_i = lax.broadcasted_iota(jnp.int32, (n, n), 1)
    before = jnp.where(c_i < r_i, 1.0, 0.0).astype(BF16)
    tie_rank = _dot(before, jnp.where(tied, 1.0, 0.0).astype(BF16))
    sel = above | (tied & (tie_rank < need))
    slot = _dot(before, jnp.where(sel, 1.0, 0.0).astype(BF16))
    slot = jnp.where(sel, slot, NOT_SELECTED)
    slot_t = slot.T
    h2 = h2_ref[...]

    per = max(1, LANES // cap)
    slot_iota = lax.broadcasted_iota(jnp.int32, (cap, n), 0).astype(F32)
    for e0 in range(0, N_EXPERTS, per):
        onehot = jnp.concatenate(
            [jnp.where(slot_t[e:e + 1, :] == slot_iota, 1.0, 0.0) for e in range(e0, e0 + per)],
            axis=0).astype(BF16)
        rows = _dot(onehot, h2).astype(BF16)
        for j in range(per):
            xe_o[e0 + j] = rows[j * cap:(j + 1) * cap]

    lane_f = lax.broadcasted_iota(jnp.int32, (n, LANES), 1).astype(F32)
    for blk in range(N_EXPERTS * cap // LANES):
        acc = jnp.zeros((n, LANES), F32)
        for j in range(per):
            e = blk * per + j
            hit = (slot[:, e:e + 1] + float(j * cap)) == lane_f
            acc = jnp.where(hit, aff[:, e:e + 1], acc)
        gt_o[:, blk * LANES:(blk + 1) * LANES] = acc.astype(BF16)


def _route(aff, h2, n_sets, set_rows):
    t, d = h2.shape
    n = set_rows
    cap = CAPACITY_FACTOR * n // N_EXPERTS
    assert cap <= LANES and LANES % cap == 0 and cap % 16 == 0
    return pl.pallas_call(
        functools.partial(_route_kernel, cap=cap),
        grid=(n_sets,),
        in_specs=[pl.BlockSpec((n, LANES), lambda s: (s, 0)),
                  pl.BlockSpec((n, d), lambda s: (s, 0))],
        out_specs=[pl.BlockSpec((N_EXPERTS, cap, d), lambda s: (0, s, 0)),
                   pl.BlockSpec((n, N_EXPERTS * cap), lambda s: (s, 0))],
        out_shape=[jax.ShapeDtypeStruct((N_EXPERTS, n_sets * cap, d), BF16),
                   jax.ShapeDtypeStruct((t, N_EXPERTS * cap), BF16)],
        name=f"route_cap{cap}",
        compiler_params=_cparams(1),
    )(aff, h2)


def _expert_kernel(*refs, n_groups):
    x_refs = refs[:n_groups]
    wg_ref, wu_ref, wd_ref = refs[n_groups:n_groups + 3]
    o_refs = refs[n_groups + 3:2 * n_groups + 3]
    wg_s, wu_s, wd_s = refs[2 * n_groups + 3:]
    wg_s[...] = wg_ref[0].astype(BF16)
    wu_s[...] = wu_ref[0].astype(BF16)
    wd_s[...] = wd_ref[0].astype(BF16)
    sub = ROW_TILE
    for x_ref, o_ref in zip(x_refs, o_refs):
        for r0 in range(0, x_ref.shape[1], sub):
            x = x_ref[0, r0:r0 + sub, :]
            a = _dot(x, wg_s[...])
            u = _dot(x, wu_s[...])
            hmid = (a * jax.nn.sigmoid(a) * u).astype(BF16)
            o_ref[0, r0:r0 + sub, :] = _dot(hmid, wd_s[...]).astype(BF16)


def _experts(xes, w_gate, w_up, w_down):
    e, d, f = w_gate.shape
    blk = lambda i: (i, 0, 0)
    return pl.pallas_call(
        functools.partial(_expert_kernel, n_groups=len(xes)),
        grid=(e,),
        in_specs=[pl.BlockSpec((1, x.shape[1], d), blk) for x in xes]
        + [pl.BlockSpec((1, d, f), blk), pl.BlockSpec((1, d, f), blk), pl.BlockSpec((1, f, d), blk)],
        out_specs=[pl.BlockSpec((1, x.shape[1], d), blk) for x in xes],
        out_shape=[jax.ShapeDtypeStruct(x.shape, BF16) for x in xes],
        scratch_shapes=[pltpu.VMEM((d, f), BF16), pltpu.VMEM((d, f), BF16), pltpu.VMEM((f, d), BF16)],
        name="experts",
        compiler_params=_cparams(1),
    )(*xes, w_gate, w_up, w_down)


def _combine_kernel(gt_ref, ye_ref, x1_ref, mod_ref, gf_ref, y_o):
    e, cap, d = ye_ref.shape
    ye = ye_ref[...].reshape(e * cap, d)
    moe = _dot(gt_ref[...], ye)
    x = x1_ref[...] + mod_ref[0][5:6] * moe
    y_o[...] = _rms(x, gf_ref[...])


def _combine(gt, ye, x1, mod, n_sets, set_rows, g_final):
    t, d = x1.shape
    tm = ROW_TILE
    tiles = set_rows // tm
    cap = ye.shape[1] // n_sets
    return pl.pallas_call(
        _combine_kernel,
        grid=(n_sets, tiles),
        in_specs=[pl.BlockSpec((tm, gt.shape[1]), lambda s, i: (s * tiles + i, 0)),
                  pl.BlockSpec((N_EXPERTS, cap, d), lambda s, i: (0, s, 0)),
                  pl.BlockSpec((tm, d), lambda s, i: (s * tiles + i, 0)),
                  pl.BlockSpec((1, 8, d), (lambda s, i: (s, 0, 0)) if mod.shape[0] > 1 else (lambda s, i: (0, 0, 0))),
                  pl.BlockSpec((1, d), lambda s, i: (0, 0))],
        out_specs=pl.BlockSpec((tm, d), lambda s, i: (s * tiles + i, 0)),
        out_shape=jax.ShapeDtypeStruct((t, d), F32),
        name=f"combine_cap{cap}",
        compiler_params=_cparams(2),
    )(gt, ye, x1, mod, g_final)


def _rope_tables(n_tokens):
    n_rows = n_tokens // GRID_W
    rowp = jnp.repeat(jnp.arange(n_rows), GRID_W).astype(F32)
    colp = jnp.tile(jnp.arange(GRID_W), n_rows).astype(F32)
    quarter = HEAD_DIM // 4
    freqs = ROPE_THETA ** (-jnp.arange(quarter, dtype=F32) / quarter)
    ang = jnp.stack([rowp[:, None] * freqs, colp[:, None] * freqs], axis=1)
    cos, sin = jnp.cos(ang), jnp.sin(ang)
    zero = jnp.zeros_like(sin)
    c = jnp.stack([cos, cos], axis=2).reshape(n_tokens, HEAD_DIM)
    s_up = jnp.stack([-sin, zero], axis=2).reshape(n_tokens, HEAD_DIM)
    s_dn = jnp.stack([zero, sin], axis=2).reshape(n_tokens, HEAD_DIM)
    return tuple(jnp.tile(t, (1, LANES // HEAD_DIM)) for t in (c, s_up, s_dn))


def _mix_and_route(x2, n_sets, set_rows, mod, lw, rope_tabs, cache, emit_cache, lam_init, tq):
    (g_attn, g_ffn, w_in, gq_t, gk_t, lq1, lk1, lq2, lk2, g_diff, w_ba, w_bb, w_out, w_router_p) = lw
    outs = _inproj(x2, mod, set_rows, g_attn, w_in, gq_t, gk_t, rope_tabs, emit_cache)
    qa, kpa, vpa, qb, kpb, vb16 = outs[:6]
    ks_a, vs_a, ks_b, vs_b = [kpa], [vpa], [kpb], [vb16]
    if cache is not None:
        ckpa, cvpa, ckpb, cvb = cache
        ks_a, vs_a, ks_b, vs_b = [ckpa, kpa], [cvpa, vpa], [ckpb, kpb], [cvb, vb16]
    ya = _attention("gqa", qa, ks_a, vs_a, n_sets, set_rows, tq, (), {})
    yb = _attention("diff", qb, ks_b, vs_b, n_sets, set_rows, tq,
                    (lq1, lk1, lq2, lk2, g_diff), {"lam_init": lam_init})
    x1, h2, aff = _post(x2, ya, yb, mod, set_rows, g_attn, g_ffn, w_in, w_ba, w_bb, w_out, w_router_p)
    xe, gt = _route(aff, h2, n_sets, set_rows)
    return x1, xe, gt, outs[6:]


def kernel(x_prompt, x_sample, cache_attn_k, cache_attn_v, cache_diff_k, cache_diff_v, c, c_ctx, w_mod, b_mod,
           g_attn_norm, g_ffn_norm, w_in, g_q_norm, g_k_norm, lambda_q1, lambda_k1, lambda_q2, lambda_k2,
           g_diff_norm, w_branch_a, w_branch_b, w_out, w_router, w_exp_gate, w_exp_up, w_exp_down, g_final):
    batch, seq, d = x_prompt.shape
    dec_batch, dec_seq, _ = x_sample.shape
    depth = w_in.shape[0]
    assert depth == 1, "the final norm is fused into the layer's combine step"
    past = cache_attn_k.shape[2]
    assert w_in.shape[2] == N_QKV + 2 * d

    xp = x_prompt.reshape(batch * seq, d)
    xs = x_sample.reshape(dec_batch * dec_seq, d)
    rope_tabs = _rope_tables(dec_seq)
    c_rows = jnp.concatenate([c, c_ctx[None, :], jnp.zeros((16 - dec_batch - 1, d), F32)], axis=0)
    yp = ys = None
    caches = []
    for l in range(depth):
        lam_init = 0.8 - 0.6 * math.exp(-0.3 * l)
        mod6 = _modulation(c_rows, w_mod[l], b_mod[l])
        mod = jnp.pad(jnp.transpose(mod6, (1, 0, 2)), ((0, 0), (0, 2), (0, 0)))
        mod_lat, mod_ctx = mod[:dec_batch], mod[dec_batch:dec_batch + 1]
        lw = (g_attn_norm[l][None, :], g_ffn_norm[l][None, :], w_in[l],
              jnp.tile(g_q_norm[l], HEADS_A)[None, :], jnp.tile(g_k_norm[l], KV_HEADS_A)[None, :],
              lambda_q1[l], lambda_k1[l], lambda_q2[l], lambda_k2[l], g_diff_norm[l][None, :],
              w_branch_a[l], w_branch_b[l], w_out[l],
              jnp.pad(w_router[l], ((0, 0), (0, LANES - N_EXPERTS))))
        x1p, xe_p, gt_p, cache_out = _mix_and_route(xp, batch, seq, mod_ctx, lw, None, None, True, lam_init, seq)
        caches.append(cache_out)
        cache_l = _cache_prep(cache_attn_k[:, l].reshape(dec_batch * past, -1),
                              cache_attn_v[:, l].reshape(dec_batch * past, -1),
                              cache_diff_k[:, l].reshape(dec_batch * past, -1),
                              cache_diff_v[:, l].reshape(dec_batch * past, -1))
        x1s, xe_s, gt_s, _ = _mix_and_route(xs, dec_batch, dec_seq, mod_lat, lw, rope_tabs, cache_l, False,
                                            lam_init, 128)
        ye_p, ye_s = _experts([xe_p, xe_s], w_exp_gate[l], w_exp_up[l], w_exp_down[l])
        yp = _combine(gt_p, ye_p, x1p, mod_ctx, batch, seq, g_final[None, :])
        ys = _combine(gt_s, ye_s, x1s, mod_lat, dec_batch, dec_seq, g_final[None, :])
    y_prompt = yp.reshape(batch, seq, d)
    y_sample = ys.reshape(dec_batch, dec_seq, d)
    stack = lambda i, shape: jnp.stack([cc[i].reshape(shape) for cc in caches], axis=1)
    new_attn_k = stack(0, (batch, seq, KV_HEADS_A, HEAD_DIM))
    new_attn_v = stack(1, (batch, seq, KV_HEADS_A, HEAD_DIM))
    new_diff_k = stack(2, (batch, seq, HEADS_B, 2, HEAD_DIM))
    new_diff_v = stack(3, (batch, seq, HEADS_B, 2 * HEAD_DIM))
    return (y_prompt, y_sample, new_attn_k, new_attn_v, new_diff_k, new_diff_v)
```

```python
import functools
import math

import numpy as np
import jax
import jax.numpy as jnp
from jax import lax
from jax.experimental import pallas as pl
from jax.experimental.pallas import tpu as pltpu

F32 = jnp.float32
BF16 = jnp.bfloat16

HEAD_DIM = 64
HEADS_A = 8
KV_HEADS_A = 2
HEADS_B = 4
N_QKV = HEADS_A * HEAD_DIM + 2 * KV_HEADS_A * HEAD_DIM + 3 * HEADS_B * 2 * HEAD_DIM
N_EXPERTS = 16
CAPACITY_FACTOR = 2
GRID_W = 64
ROPE_THETA = 10000.0
EPS = 1e-6
LANES = 128
ROW_TILE = 256
NEG_BIG = -1e30
NOT_SELECTED = -1e6
VMEM_LIMIT = 56 * 1024 * 1024


def _cparams(n_axes):
    return pltpu.CompilerParams(dimension_semantics=("arbitrary",) * n_axes,
                                vmem_limit_bytes=VMEM_LIMIT)


def _dot(a, b):
    return jnp.dot(a, b, preferred_element_type=F32)


def _dot_nt(a, b):
    return lax.dot_general(a, b, (((1,), (1,)), ((), ())), preferred_element_type=F32)


def _split(a):
    hi = a.astype(BF16)
    lo = (a - hi.astype(F32)).astype(BF16)
    return hi, lo


def _dot3(a, b):
    a_hi, a_lo = _split(a)
    b_hi, b_lo = _split(b)
    return _dot(a_hi, b_hi) + _dot(a_lo, b_hi) + _dot(a_hi, b_lo)


def _rms(x, g):
    return x * lax.rsqrt(jnp.mean(x * x, axis=-1, keepdims=True) + EPS) * g


def _mod_index(mod, tiles_per_set):
    if mod.shape[0] == 1:
        return lambda i: (0, 0, 0)
    return lambda i: (i // tiles_per_set, 0, 0)


def _mod_kernel(c_ref, w_ref, b_ref, o_ref):
    c = c_ref[...]
    a = c * jax.nn.sigmoid(c)
    o_ref[0] = _dot3(a, w_ref[...]) + b_ref[0]


def _modulation(c_rows, w_mod, b_mod):
    r, d = c_rows.shape
    return pl.pallas_call(
        _mod_kernel,
        grid=(6,),
        in_specs=[pl.BlockSpec((r, d), lambda j: (0, 0)),
                  pl.BlockSpec((d, d), lambda j: (0, j)),
                  pl.BlockSpec((1, 1, d), lambda j: (j, 0, 0))],
        out_specs=pl.BlockSpec((1, r, d), lambda j: (j, 0, 0)),
        out_shape=jax.ShapeDtypeStruct((6, r, d), F32),
        name="mod",
        compiler_params=_cparams(1),
    )(c_rows, w_mod, b_mod.reshape(6, 1, d))


def _seg_sumsq(x, ones_blockdiag):
    hi, lo = _split(x * x)
    return _dot(hi, ones_blockdiag) + _dot(lo, ones_blockdiag)


def _rope(x, c, s_up, s_dn):
    w = x.shape[1]
    reps = w // c.shape[1]
    if reps > 1:
        c, s_up, s_dn = (jnp.concatenate([t] * reps, axis=1) for t in (c, s_up, s_dn))
    return x * c + pltpu.roll(x, w - 16, 1) * s_up + pltpu.roll(x, 16, 1) * s_dn


def _inproj_kernel(*refs, rope, emit_cache):
    (x_ref, mod_ref, g_ref, w32_ref, gq_ref, gk_ref, bd512_ref, bd128_ref), refs = refs[:8], refs[8:]
    if rope:
        (c_ref, su_ref, sd_ref), refs = refs[:3], refs[3:]
    qa_o, kta_o, vta_o, qb_o, ktb_o, vb_o = refs[:6]
    if emit_cache:
        ka_c, va_c, kb_c, vb_c = refs[6:10]
    w_ref = refs[-1]

    @pl.when(pl.program_id(0) == 0)
    def _():
        w_ref[...] = w32_ref[...].astype(BF16)

    x = x_ref[...]
    mod = mod_ref[0]
    h = (_rms(x, g_ref[...]) * (1.0 + mod[1:2]) + mod[0:1]).astype(BF16)
    if rope:
        tabs = (c_ref[...], su_ref[...], sd_ref[...])
    wa = HEADS_A * HEAD_DIM
    wkv = KV_HEADS_A * HEAD_DIM
    wb = HEADS_B * 2 * HEAD_DIM
    o_ka, o_va, o_qb = wa, wa + wkv, wa + 2 * wkv
    o_kb, o_vb = o_qb + wb, o_qb + 2 * wb
    scale = HEAD_DIM ** -0.5

    qa = _dot(h, w_ref[:, 0:wa])
    qa = qa * lax.rsqrt(_seg_sumsq(qa, bd512_ref[...]) * (1.0 / HEAD_DIM) + EPS) * gq_ref[...]
    if rope:
        qa = _rope(qa, *tabs)
    qa_o[...] = (qa * scale).astype(BF16)

    kv = _dot(h, w_ref[:, o_ka:o_qb])
    ka, va = kv[:, 0:wkv], kv[:, wkv:2 * wkv]
    ka = ka * lax.rsqrt(_seg_sumsq(ka, bd128_ref[...]) * (1.0 / HEAD_DIM) + EPS) * gk_ref[...]
    va_t = va.T
    vta_o[0] = va_t.astype(BF16)
    if emit_cache:
        ka_c[0] = ka.T
        va_c[0] = va_t
    if rope:
        ka = _rope(ka, *tabs)
    kta_o[0] = ka.T.astype(BF16)

    qb = _dot(h, w_ref[:, o_qb:o_kb])
    if rope:
        qb = _rope(qb, *tabs)
    qb_o[...] = (qb * scale).astype(BF16)

    kb = _dot(h, w_ref[:, o_kb:o_vb])
    if emit_cache:
        kb_c[0] = kb.T
    if rope:
        kb = _rope(kb, *tabs)
    ktb_o[0] = kb.T.astype(BF16)

    vb = _dot(h, w_ref[:, o_vb:o_vb + wb])
    if emit_cache:
        vb_c[...] = vb
    vb_o[...] = vb.astype(BF16)


def _blockdiag_ones(width):
    g = np.arange(width) // HEAD_DIM
    return jnp.asarray((g[:, None] == g[None, :]).astype(np.float32), dtype=BF16)


def _inproj(x2, mod, set_rows, g_attn, w_in, gq_t, gk_t, rope_tabs, emit_cache):
    t, d = x2.shape
    tm = ROW_TILE
    tiles_per_set = set_rows // tm
    rope = rope_tabs is not None
    nq = N_QKV
    row = lambda i: (i, 0)
    const = lambda i: (0, 0)
    in_specs = [pl.BlockSpec((tm, d), row),
                pl.BlockSpec((1, 8, d), _mod_index(mod, tiles_per_set)),
                pl.BlockSpec((1, d), const),
                pl.BlockSpec((d, nq), const, pipeline_mode=pl.Buffered(1)),
                pl.BlockSpec((1, 512), const),
                pl.BlockSpec((1, 128), const),
                pl.BlockSpec((512, 512), const),
                pl.BlockSpec((128, 128), const)]
    args = [x2, mod, g_attn, w_in, gq_t, gk_t, _blockdiag_ones(512), _blockdiag_ones(128)]
    if rope:
        in_specs += [pl.BlockSpec((tm, LANES), lambda i: (i % tiles_per_set, 0))] * 3
        args += list(rope_tabs)
    n_sets = t // set_rows
    wkv, wb = KV_HEADS_A * HEAD_DIM, HEADS_B * 2 * HEAD_DIM
    outs = [("tok", 512, BF16), ("feat", wkv, BF16), ("feat", wkv, BF16),
            ("tok", 512, BF16), ("feat", wb, BF16), ("tok", wb, BF16)]
    if emit_cache:
        outs += [("feat", wkv, F32), ("feat", wkv, F32), ("feat", wb, F32), ("tok", wb, F32)]
    feat = lambda i: (i // tiles_per_set, 0, i % tiles_per_set)
    return pl.pallas_call(
        functools.partial(_inproj_kernel, rope=rope, emit_cache=emit_cache),
        grid=(t // tm,),
        in_specs=in_specs,
        out_specs=[pl.BlockSpec((tm, w), row) if kind == "tok" else pl.BlockSpec((1, w, tm), feat)
                   for kind, w, _ in outs],
        out_shape=[jax.ShapeDtypeStruct((t, w) if kind == "tok" else (n_sets, w, set_rows), dt)
                   for kind, w, dt in outs],
        scratch_shapes=[pltpu.VMEM((d, nq), BF16)],
        name="inproj_rope" if rope else "inproj",
        compiler_params=_cparams(1),
    )(*args)


def _cache_prep_kernel(*refs):
    half = len(refs) // 2
    for src, out in zip(refs[:half], refs[half:]):
        out[...] = src[...].astype(BF16)


def _cache_prep(arrays):
    blk = lambda b: (b, 0, 0)
    return pl.pallas_call(
        _cache_prep_kernel,
        grid=(arrays[0].shape[0],),
        in_specs=[pl.BlockSpec((1,) + a.shape[1:], blk) for a in arrays],
        out_specs=[pl.BlockSpec((1,) + a.shape[1:], blk) for a in arrays],
        out_shape=[jax.ShapeDtypeStruct(a.shape, BF16) for a in arrays],
        name="cache_prep",
        compiler_params=_cparams(1),
    )(*arrays)


def _softmax_parts(scores):
    m = scores[0].max(axis=-1, keepdims=True)
    for s in scores[1:]:
        m = jnp.maximum(m, s.max(axis=-1, keepdims=True))
    es = [jnp.exp(s - m) for s in scores]
    l = es[0].sum(axis=-1, keepdims=True)
    for e in es[1:]:
        l = l + e.sum(axis=-1, keepdims=True)
    return es, l


def _pad_rows(x, first):
    z = jnp.zeros_like(x)
    return jnp.concatenate([x, z] if first else [z, x], axis=0)


def _gqa_kernel(*refs, n_src):
    q_ref = refs[0]
    k_refs = refs[1:1 + n_src]
    v_refs = refs[1 + n_src:1 + 2 * n_src]
    o_ref = refs[1 + 2 * n_src]
    tq = q_ref.shape[0]
    lane = lax.broadcasted_iota(jnp.int32, (2 * tq, LANES), 1)
    lo_half = lane < HEAD_DIM
    for g in range(KV_HEADS_A):
        c0 = g * 256
        f0 = g * HEAD_DIM
        q = jnp.concatenate([q_ref[:, c0:c0 + 128], q_ref[:, c0 + 128:c0 + 256]], axis=0)
        kts = [k[0, f0:f0 + HEAD_DIM, :] for k in k_refs]
        vts = [v[0, f0:f0 + HEAD_DIM, :] for v in v_refs]
        s_lo = [_dot(q, _pad_rows(kt, True)) for kt in kts]
        s_hi = [_dot(q, _pad_rows(kt, False)) for kt in kts]
        e_lo, l_lo = _softmax_parts(s_lo)
        e_hi, l_hi = _softmax_parts(s_hi)
        o = None
        for e_part, vt in zip(e_lo, vts):
            t = _dot_nt(e_part.astype(BF16), _pad_rows(vt, True))
            o = t if o is None else o + t
        for e_part, vt in zip(e_hi, vts):
            o = o + _dot_nt(e_part.astype(BF16), _pad_rows(vt, False))
        o = o * jnp.where(lo_half, 1.0 / l_lo, 1.0 / l_hi)
        o_ref[:, c0:c0 + 128] = o[0:tq].astype(BF16)
        o_ref[:, c0 + 128:c0 + 256] = o[tq:2 * tq].astype(BF16)


def _diff_kernel(*refs, n_src, lam_init):
    q_ref = refs[0]
    k_refs = refs[1:1 + n_src]
    v_refs = refs[1 + n_src:1 + 2 * n_src]
    lq1, lk1, lq2, lk2, gd_ref, o_ref = refs[1 + 2 * n_src:]
    lam_all = (jnp.exp(jnp.sum(lq1[...] * lk1[...], axis=-1, keepdims=True))
               - jnp.exp(jnp.sum(lq2[...] * lk2[...], axis=-1, keepdims=True)) + lam_init)
    for j in range(HEADS_B):
        lam = lam_all[j:j + 1, :]
        q = q_ref[:, j * 128:(j + 1) * 128]
        s0 = [_dot(q, _pad_rows(k[0, j * 128:j * 128 + HEAD_DIM, :], True)) for k in k_refs]
        s1 = [_dot(q, _pad_rows(k[0, j * 128 + HEAD_DIM:(j + 1) * 128, :], False)) for k in k_refs]
        e0, l0 = _softmax_parts(s0)
        e1, l1 = _softmax_parts(s1)
        r0 = 1.0 / l0
        r1 = lam / l1
        o = None
        for a0, a1, v in zip(e0, e1, v_refs):
            t = _dot((a0 * r0 - a1 * r1).astype(BF16), v[0, :, j * 128:(j + 1) * 128])
            o = t if o is None else o + t
        o = _rms(o, gd_ref[...]) * (1.0 - lam_init)
        o_ref[:, j * 128:(j + 1) * 128] = o.astype(BF16)


def _attention(kind, q, ks, vs, set_rows, tq, extra, extra_kw):
    t, qw = q.shape
    n_sets = t // set_rows
    q_tiles = set_rows // tq
    in_specs = [pl.BlockSpec((tq, qw), lambda b, i: (b * q_tiles + i, 0))]
    for a in list(ks) + list(vs):
        in_specs.append(pl.BlockSpec((1,) + a.shape[1:], lambda b, i: (b, 0, 0)))
    for e in extra:
        in_specs.append(pl.BlockSpec(e.shape, lambda b, i: (0, 0)))
    body = functools.partial(_gqa_kernel if kind == "gqa" else _diff_kernel, n_src=len(ks), **extra_kw)
    return pl.pallas_call(
        body,
        grid=(n_sets, q_tiles),
        in_specs=in_specs,
        out_specs=pl.BlockSpec((tq, qw), lambda b, i: (b * q_tiles + i, 0)),
        out_shape=jax.ShapeDtypeStruct((t, qw), BF16),
        name=f"{kind}_attn_{len(ks)}src",
        compiler_params=_cparams(2),
    )(q, *ks, *vs, *extra)


def _post_kernel(x_ref, ya_ref, yb_ref, mod_ref, g1_ref, g2_ref, win32_ref, wba32_ref, wbb32_ref, wo32_ref, wr_ref,
                 x1_o, h2_o, aff_o, wg_ref, wba_ref, wbb_ref, wo_ref):
    d = x_ref.shape[1]

    @pl.when(pl.program_id(0) == 0)
    def _():
        wg_ref[...] = win32_ref[:, win32_ref.shape[1] - 2 * d:].astype(BF16)
        wba_ref[...] = wba32_ref[...].astype(BF16)
        wbb_ref[...] = wbb32_ref[...].astype(BF16)
        wo_ref[...] = wo32_ref[...].astype(BF16)

    x = x_ref[...]
    mod = mod_ref[0]
    h = (_rms(x, g1_ref[...]) * (1.0 + mod[1:2]) + mod[0:1]).astype(BF16)
    ga = jax.nn.sigmoid(_dot(h, wg_ref[:, 0:d]))
    merged = ga * _dot(ya_ref[...], wba_ref[...])
    gb = jax.nn.sigmoid(_dot(h, wg_ref[:, d:2 * d]))
    merged = merged + gb * _dot(yb_ref[...], wbb_ref[...])
    m = _dot(merged.astype(BF16), wo_ref[...])
    x1 = x + mod[2:3] * m
    x1_o[...] = x1
    h2 = _rms(x1, g2_ref[...]) * (1.0 + mod[4:5]) + mod[3:4]
    h2_o[...] = h2.astype(BF16)
    logits = _dot3(h2, wr_ref[...])
    lane = lax.broadcasted_iota(jnp.int32, logits.shape, 1)
    logits = jnp.where(lane < N_EXPERTS, logits, NEG_BIG)
    e = jnp.exp(logits - logits.max(axis=-1, keepdims=True))
    aff_o[...] = e / e.sum(axis=-1, keepdims=True)


def _post(x2, ya, yb, mod, set_rows, g1, g2, w_in, w_ba, w_bb, w_out, w_router_p):
    t, d = x2.shape
    tm = ROW_TILE
    tiles_per_set = set_rows // tm
    row = lambda i: (i, 0)
    const = lambda i: (0, 0)
    once = pl.Buffered(1)
    half = w_in.shape[1] // 2
    assert w_in.shape[1] == 2 * half and half % LANES == 0 and half >= 2 * d
    return pl.pallas_call(
        _post_kernel,
        grid=(t // tm,),
        in_specs=[pl.BlockSpec((tm, d), row),
                  pl.BlockSpec((tm, 512), row),
                  pl.BlockSpec((tm, 512), row),
                  pl.BlockSpec((1, 8, d), _mod_index(mod, tiles_per_set)),
                  pl.BlockSpec((1, d), const),
                  pl.BlockSpec((1, d), const),
                  pl.BlockSpec((d, half), lambda i: (0, 1), pipeline_mode=once),
                  pl.BlockSpec(w_ba.shape, const, pipeline_mode=once),
                  pl.BlockSpec(w_bb.shape, const, pipeline_mode=once),
                  pl.BlockSpec(w_out.shape, const, pipeline_mode=once),
                  pl.BlockSpec(w_router_p.shape, const)],
        out_specs=[pl.BlockSpec((tm, d), row), pl.BlockSpec((tm, d), row), pl.BlockSpec((tm, LANES), row)],
        out_shape=[jax.ShapeDtypeStruct((t, d), F32), jax.ShapeDtypeStruct((t, d), BF16),
                   jax.ShapeDtypeStruct((t, LANES), F32)],
        scratch_shapes=[pltpu.VMEM((d, 2 * d), BF16), pltpu.VMEM(w_ba.shape, BF16),
                        pltpu.VMEM(w_bb.shape, BF16), pltpu.VMEM(w_out.shape, BF16)],
        name="post_attn",
        compiler_params=_cparams(1),
    )(x2, ya, yb, mod, g1, g2, w_in, w_ba, w_bb, w_out, w_router_p)


def _route_kernel(aff_ref, h2_ref, xe_o, gt_o, *, cap):
    aff = aff_ref[...]
    n = aff.shape[0]
    capf = float(cap)

    def enough(cand):
        return jnp.sum(jnp.where(aff >= cand, 1.0, 0.0), axis=0, keepdims=True) >= capf

    pw = jnp.ones((1, LANES), F32)
    for k in (64, 32, 16, 8, 4, 2, 1):
        pw = jnp.where(enough(pw * 2.0 ** -(k - 1)), pw, pw * 2.0 ** -k)

    def mantissa_step(_, carry):
        thr, step = carry
        cand = thr + step
        return jnp.where(enough(cand), cand, thr), step * 0.5

    thr, _ = lax.fori_loop(0, 23, mantissa_step, (pw, pw * 0.5))
    above = aff > thr
    tied = aff == thr
    need = capf - jnp.sum(jnp.where(above, 1.0, 0.0), axis=0, keepdims=True)
    r_i = lax.broadcasted_iota(jnp.int32, (n, n), 0)
    c_i = lax.broadcasted_iota(jnp.int32, (n, n), 1)
    before = jnp.where(c_i < r_i, 1.0, 0.0).astype(BF16)
    tie_rank = _dot(before, jnp.where(tied, 1.0, 0.0).astype(BF16))
    sel = above | (tied & (tie_rank < need))
    slot = _dot(before, jnp.where(sel, 1.0, 0.0).astype(BF16))
    slot = jnp.where(sel, slot, NOT_SELECTED)
    slot_t = slot.T
    h2 = h2_ref[...]

    per = max(1, LANES // cap)
    slot_iota = lax.broadcasted_iota(jnp.int32, (cap, n), 0).astype(F32)
    for e0 in range(0, N_EXPERTS, per):
        onehot = jnp.concatenate(
            [jnp.where(slot_t[e:e + 1, :] == slot_iota, 1.0, 0.0) for e in range(e0, e0 + per)],
            axis=0).astype(BF16)
        rows = _dot(onehot, h2).astype(BF16)
        for j in range(per):
            xe_o[e0 + j] = rows[j * cap:(j + 1) * cap]

    lane_f = lax.broadcasted_iota(jnp.int32, (n, LANES), 1).astype(F32)
    for blk in range(N_EXPERTS * cap // LANES):
        acc = jnp.zeros((n, LANES), F32)
        for j in range(per):
            e = blk * per + j
            hit = (slot[:, e:e + 1] + float(j * cap)) == lane_f
            acc = jnp.where(hit, aff[:, e:e + 1], acc)
        gt_o[:, blk * LANES:(blk + 1) * LANES] = acc.astype(BF16)


def _route(aff, h2, n_sets, set_rows):
    t, d = h2.shape
    n = set_rows
    cap = CAPACITY_FACTOR * n // N_EXPERTS
    assert cap <= LANES and LANES % cap == 0 and cap % 16 == 0
    return pl.pallas_call(
        functools.partial(_route_kernel, cap=cap),
        grid=(n_sets,),
        in_specs=[pl.BlockSpec((n, LANES), lambda s: (s, 0)),
                  pl.BlockSpec((n, d), lambda s: (s, 0))],
        out_specs=[pl.BlockSpec((N_EXPERTS, cap, d), lambda s: (0, s, 0)),
                   pl.BlockSpec((n, N_EXPERTS * cap), lambda s: (s, 0))],
        out_shape=[jax.ShapeDtypeStruct((N_EXPERTS, n_sets * cap, d), BF16),
                   jax.ShapeDtypeStruct((t, N_EXPERTS * cap), BF16)],
        name=f"route_cap{cap}",
        compiler_params=_cparams(1),
    )(aff, h2)


def _expert_kernel(*refs, n_groups):
    x_refs = refs[:n_groups]
    wg_ref, wu_ref, wd_ref = refs[n_groups:n_groups + 3]
    o_refs = refs[n_groups + 3:2 * n_groups + 3]
    wg_s, wu_s, wd_s = refs[2 * n_groups + 3:]
    wg_s[...] = wg_ref[0].astype(BF16)
    wu_s[...] = wu_ref[0].astype(BF16)
    wd_s[...] = wd_ref[0].astype(BF16)
    sub = ROW_TILE
    for x_ref, o_ref in zip(x_refs, o_refs):
        for r0 in range(0, x_ref.shape[1], sub):
            x = x_ref[0, r0:r0 + sub, :]
            a = _dot(x, wg_s[...])
            u = _dot(x, wu_s[...])
            hmid = (a * jax.nn.sigmoid(a) * u).astype(BF16)
            o_ref[0, r0:r0 + sub, :] = _dot(hmid, wd_s[...]).astype(BF16)


def _experts(xes, w_gate, w_up, w_down):
    e, d, f = w_gate.shape
    blk = lambda i: (i, 0, 0)
    return pl.pallas_call(
        functools.partial(_expert_kernel, n_groups=len(xes)),
        grid=(e,),
        in_specs=[pl.BlockSpec((1, x.shape[1], d), blk) for x in xes]
        + [pl.BlockSpec((1, d, f), blk), pl.BlockSpec((1, d, f), blk), pl.BlockSpec((1, f, d), blk)],
        out_specs=[pl.BlockSpec((1, x.shape[1], d), blk) for x in xes],
        out_shape=[jax.ShapeDtypeStruct(x.shape, BF16) for x in xes],
        scratch_shapes=[pltpu.VMEM((d, f), BF16), pltpu.VMEM((d, f), BF16), pltpu.VMEM((f, d), BF16)],
        name="experts",
        compiler_params=_cparams(1),
    )(*xes, w_gate, w_up, w_down)


def _combine_kernel(gt_ref, ye_ref, x1_ref, mod_ref, gf_ref, y_o):
    e, cap, d = ye_ref.shape
    ye = ye_ref[...].reshape(e * cap, d)
    moe = _dot(gt_ref[...], ye)
    x = x1_ref[...] + mod_ref[0][5:6] * moe
    y_o[...] = _rms(x, gf_ref[...])


def _combine(gt, ye, x1, mod, n_sets, set_rows, g_final):
    t, d = x1.shape
    tm = ROW_TILE
    tiles = set_rows // tm
    cap = ye.shape[1] // n_sets
    return pl.pallas_call(
        _combine_kernel,
        grid=(n_sets, tiles),
        in_specs=[pl.BlockSpec((tm, gt.shape[1]), lambda s, i: (s * tiles + i, 0)),
                  pl.BlockSpec((N_EXPERTS, cap, d), lambda s, i: (0, s, 0)),
                  pl.BlockSpec((tm, d), lambda s, i: (s * tiles + i, 0)),
                  pl.BlockSpec((1, 8, d), (lambda s, i: (s, 0, 0)) if mod.shape[0] > 1 else (lambda s, i: (0, 0, 0))),
                  pl.BlockSpec((1, d), lambda s, i: (0, 0))],
        out_specs=pl.BlockSpec((tm, d), lambda s, i: (s * tiles + i, 0)),
        out_shape=jax.ShapeDtypeStruct((t, d), F32),
        name=f"combine_cap{cap}",
        compiler_params=_cparams(2),
    )(gt, ye, x1, mod, g_final)


def _rope_tables(n_tokens):
    n_rows = n_tokens // GRID_W
    rowp = jnp.repeat(jnp.arange(n_rows), GRID_W).astype(F32)
    colp = jnp.tile(jnp.arange(GRID_W), n_rows).astype(F32)
    quarter = HEAD_DIM // 4
    freqs = ROPE_THETA ** (-jnp.arange(quarter, dtype=F32) / quarter)
    ang = jnp.stack([rowp[:, None] * freqs, colp[:, None] * freqs], axis=1)
    cos, sin = jnp.cos(ang), jnp.sin(ang)
    zero = jnp.zeros_like(sin)
    c = jnp.stack([cos, cos], axis=2).reshape(n_tokens, HEAD_DIM)
    s_up = jnp.stack([-sin, zero], axis=2).reshape(n_tokens, HEAD_DIM)
    s_dn = jnp.stack([zero, sin], axis=2).reshape(n_tokens, HEAD_DIM)
    return tuple(jnp.tile(t, (1, LANES // HEAD_DIM)) for t in (c, s_up, s_dn))


def _mix_and_route(x2, n_sets, set_rows, mod, lw, rope_tabs, cache, emit_cache, lam_init, tq):
    (g_attn, g_ffn, w_in, gq_t, gk_t, lq1, lk1, lq2, lk2, g_diff, w_ba, w_bb, w_out, w_router_p) = lw
    outs = _inproj(x2, mod, set_rows, g_attn, w_in, gq_t, gk_t, rope_tabs, emit_cache)
    qa, kta, vta, qb, ktb, vb16 = outs[:6]
    vb16 = vb16.reshape(n_sets, set_rows, vb16.shape[1])
    ks_a, vs_a, ks_b, vs_b = [kta], [vta], [ktb], [vb16]
    if cache is not None:
        ckta, cvta, cktb, cvb = cache
        ks_a, vs_a, ks_b, vs_b = [ckta, kta], [cvta, vta], [cktb, ktb], [cvb, vb16]
    ya = _attention("gqa", qa, ks_a, vs_a, set_rows, tq, (), {})
    yb = _attention("diff", qb, ks_b, vs_b, set_rows, tq, (lq1, lk1, lq2, lk2, g_diff), {"lam_init": lam_init})
    x1, h2, aff = _post(x2, ya, yb, mod, set_rows, g_attn, g_ffn, w_in, w_ba, w_bb, w_out, w_router_p)
    xe, gt = _route(aff, h2, n_sets, set_rows)
    return x1, xe, gt, outs[6:]


def kernel(x_prompt, x_sample, cache_attn_k, cache_attn_v, cache_diff_k, cache_diff_v, c, c_ctx, w_mod, b_mod,
           g_attn_norm, g_ffn_norm, w_in, g_q_norm, g_k_norm, lambda_q1, lambda_k1, lambda_q2, lambda_k2,
           g_diff_norm, w_branch_a, w_branch_b, w_out, w_router, w_exp_gate, w_exp_up, w_exp_down, g_final):
    batch, seq, d = x_prompt.shape
    dec_batch, dec_seq, _ = x_sample.shape
    depth = w_in.shape[0]
    assert depth == 1, "the final norm is fused into the layer's combine step"
    past = cache_attn_k.shape[2]
    assert w_in.shape[2] == N_QKV + 2 * d

    xp = x_prompt.reshape(batch * seq, d)
    xs = x_sample.reshape(dec_batch * dec_seq, d)
    rope_tabs = _rope_tables(dec_seq)
    c_rows = jnp.concatenate([c, c_ctx[None, :], jnp.zeros((16 - dec_batch - 1, d), F32)], axis=0)
    yp = ys = None
    caches = []
    for l in range(depth):
        lam_init = 0.8 - 0.6 * math.exp(-0.3 * l)
        mod6 = _modulation(c_rows, w_mod[l], b_mod[l])
        mod = jnp.pad(jnp.transpose(mod6, (1, 0, 2)), ((0, 0), (0, 2), (0, 0)))
        mod_lat, mod_ctx = mod[:dec_batch], mod[dec_batch:dec_batch + 1]
        lw = (g_attn_norm[l][None, :], g_ffn_norm[l][None, :], w_in[l],
              jnp.tile(g_q_norm[l], HEADS_A)[None, :], jnp.tile(g_k_norm[l], KV_HEADS_A)[None, :],
              lambda_q1[l], lambda_k1[l], lambda_q2[l], lambda_k2[l], g_diff_norm[l][None, :],
              w_branch_a[l], w_branch_b[l], w_out[l],
              jnp.pad(w_router[l], ((0, 0), (0, LANES - N_EXPERTS))))
        x1p, xe_p, gt_p, cache_out = _mix_and_route(xp, batch, seq, mod_ctx, lw, None, None, True, lam_init, seq)
        caches.append(cache_out)
        feat_major = lambda a: jnp.moveaxis(a[:, l], 1, -1).reshape(dec_batch, -1, past)
        cache_l = _cache_prep([feat_major(cache_attn_k), feat_major(cache_attn_v), feat_major(cache_diff_k),
                               cache_diff_v[:, l].reshape(dec_batch, past, -1)])
        x1s, xe_s, gt_s, _ = _mix_and_route(xs, dec_batch, dec_seq, mod_lat, lw, rope_tabs, cache_l, False,
                                            lam_init, 128)
        ye_p, ye_s = _experts([xe_p, xe_s], w_exp_gate[l], w_exp_up[l], w_exp_down[l])
        yp = _combine(gt_p, ye_p, x1p, mod_ctx, batch, seq, g_final[None, :])
        ys = _combine(gt_s, ye_s, x1s, mod_lat, dec_batch, dec_seq, g_final[None, :])
    y_prompt = yp.reshape(batch, seq, d)
    y_sample = ys.reshape(dec_batch, dec_seq, d)
    tok_major = lambda a, dims: jnp.moveaxis(a.reshape((batch,) + dims + (seq,)), -1, 1)
    new_attn_k = jnp.stack([tok_major(cc[0], (KV_HEADS_A, HEAD_DIM)) for cc in caches], axis=1)
    new_attn_v = jnp.stack([tok_major(cc[1], (KV_HEADS_A, HEAD_DIM)) for cc in caches], axis=1)
    new_diff_k = jnp.stack([tok_major(cc[2], (HEADS_B, 2, HEAD_DIM)) for cc in caches], axis=1)
    new_diff_v = jnp.stack([cc[3].reshape(batch, seq, HEADS_B, 2 * HEAD_DIM) for cc in caches], axis=1)
    return (y_prompt, y_sample, new_attn_k, new_attn_v, new_diff_k, new_diff_v)
```

```python
import functools
import math

import numpy as np
import jax
import jax.numpy as jnp
from jax import lax
from jax.experimental import pallas as pl
from jax.experimental.pallas import tpu as pltpu

F32 = jnp.float32
BF16 = jnp.bfloat16

HEAD_DIM = 64
HEADS_A = 8
KV_HEADS_A = 2
HEADS_B = 4
N_QKV = HEADS_A * HEAD_DIM + 2 * KV_HEADS_A * HEAD_DIM + 3 * HEADS_B * 2 * HEAD_DIM
N_EXPERTS = 16
CAPACITY_FACTOR = 2
GRID_W = 64
ROPE_THETA = 10000.0
EPS = 1e-6
LANES = 128
ROW_TILE = 512
EXPERT_ROWS = 256
LATENT_Q_TILE = 256
SETS_PER_PACK = LANES // N_EXPERTS
DISPATCH_ROWS = 512
NEG_BIG = -1e30
NOT_SELECTED = -1.0
VMEM_LIMIT = 56 * 1024 * 1024


def _cparams(n_axes):
    return pltpu.CompilerParams(dimension_semantics=("arbitrary",) * n_axes,
                                vmem_limit_bytes=VMEM_LIMIT)


def _dot(a, b):
    return jnp.dot(a, b, preferred_element_type=F32)


def _dot_nt(a, b):
    return lax.dot_general(a, b, (((1,), (1,)), ((), ())), preferred_element_type=F32)


def _split(a):
    hi = a.astype(BF16)
    lo = (a - hi.astype(F32)).astype(BF16)
    return hi, lo


def _dot3(a, b):
    a_hi, a_lo = _split(a)
    b_hi, b_lo = _split(b)
    return _dot(a_hi, b_hi) + _dot(a_lo, b_hi) + _dot(a_hi, b_lo)


def _rms(x, g):
    return x * lax.rsqrt(jnp.mean(x * x, axis=-1, keepdims=True) + EPS) * g


def _mod_index(mod, tiles_per_set):
    if mod.shape[0] == 1:
        return lambda i: (0, 0, 0)
    return lambda i: (i // tiles_per_set, 0, 0)


def _mod_kernel(c_ref, w_ref, b_ref, o_ref):
    c = c_ref[...]
    a = c * jax.nn.sigmoid(c)
    o_ref[0] = _dot3(a, w_ref[...]) + b_ref[0]


def _modulation(c_rows, w_mod, b_mod):
    r, d = c_rows.shape
    return pl.pallas_call(
        _mod_kernel,
        grid=(6,),
        in_specs=[pl.BlockSpec((r, d), lambda j: (0, 0)),
                  pl.BlockSpec((d, d), lambda j: (0, j)),
                  pl.BlockSpec((1, 1, d), lambda j: (j, 0, 0))],
        out_specs=pl.BlockSpec((1, r, d), lambda j: (j, 0, 0)),
        out_shape=jax.ShapeDtypeStruct((6, r, d), F32),
        name="mod",
        compiler_params=_cparams(1),
    )(c_rows, w_mod, b_mod.reshape(6, 1, d))


def _seg_sumsq(x, ones_blockdiag):
    hi, lo = _split(x * x)
    return _dot(hi, ones_blockdiag) + _dot(lo, ones_blockdiag)


def _rope(x, c, s_up, s_dn):
    w = x.shape[1]
    reps = w // c.shape[1]
    if reps > 1:
        c, s_up, s_dn = (jnp.concatenate([t] * reps, axis=1) for t in (c, s_up, s_dn))
    return x * c + pltpu.roll(x, w - 16, 1) * s_up + pltpu.roll(x, 16, 1) * s_dn


def _inproj_kernel(*refs, rope, emit_cache):
    (x_ref, mod_ref, g_ref, w32_ref, gq_ref, gk_ref, bd512_ref, bd128_ref), refs = refs[:8], refs[8:]
    if rope:
        (c_ref, su_ref, sd_ref), refs = refs[:3], refs[3:]
    qa_o, kta_o, vta_o, qb_o, ktb_o, vb_o = refs[:6]
    if emit_cache:
        ka_c, va_c, kb_c, vb_c = refs[6:10]
    w_ref = refs[-1]

    @pl.when(pl.program_id(0) == 0)
    def _():
        w_ref[...] = w32_ref[...].astype(BF16)

    x = x_ref[...]
    mod = mod_ref[0]
    h = (_rms(x, g_ref[...]) * (1.0 + mod[1:2]) + mod[0:1]).astype(BF16)
    if rope:
        tabs = (c_ref[...], su_ref[...], sd_ref[...])
    wa = HEADS_A * HEAD_DIM
    wkv = KV_HEADS_A * HEAD_DIM
    wb = HEADS_B * 2 * HEAD_DIM
    o_ka, o_va, o_qb = wa, wa + wkv, wa + 2 * wkv
    o_kb, o_vb = o_qb + wb, o_qb + 2 * wb
    scale = HEAD_DIM ** -0.5

    qa = _dot(h, w_ref[:, 0:wa])
    qa = qa * lax.rsqrt(_seg_sumsq(qa, bd512_ref[...]) * (1.0 / HEAD_DIM) + EPS) * gq_ref[...]
    if rope:
        qa = _rope(qa, *tabs)
    qa_o[...] = (qa * scale).astype(BF16)

    kv = _dot(h, w_ref[:, o_ka:o_qb])
    ka, va = kv[:, 0:wkv], kv[:, wkv:2 * wkv]
    ka = ka * lax.rsqrt(_seg_sumsq(ka, bd128_ref[...]) * (1.0 / HEAD_DIM) + EPS) * gk_ref[...]
    def put_feat(val, out_bf16, out_f32):
        rows = out_bf16.shape[2]
        for s in range(out_bf16.shape[0]):
            t = val[s * rows:(s + 1) * rows].T
            if out_f32 is not None:
                out_f32[s] = t
            if out_bf16 is not None:
                out_bf16[s] = t.astype(BF16)

    if rope:
        put_feat(_rope(ka, *tabs), kta_o, None)
    else:
        put_feat(ka, kta_o, ka_c if emit_cache else None)
    put_feat(va, vta_o, va_c if emit_cache else None)

    qb = _dot(h, w_ref[:, o_qb:o_kb])
    if rope:
        qb = _rope(qb, *tabs)
    qb_o[...] = (qb * scale).astype(BF16)

    kb = _dot(h, w_ref[:, o_kb:o_vb])
    if rope:
        put_feat(_rope(kb, *tabs), ktb_o, None)
    else:
        put_feat(kb, ktb_o, kb_c if emit_cache else None)

    vb = _dot(h, w_ref[:, o_vb:o_vb + wb])
    if emit_cache:
        vb_c[...] = vb
    vb_o[...] = vb.astype(BF16)


def _blockdiag_ones(width):
    g = np.arange(width) // HEAD_DIM
    return jnp.asarray((g[:, None] == g[None, :]).astype(np.float32), dtype=BF16)


def _inproj(x2, mod, set_rows, g_attn, w_in, gq_t, gk_t, rope_tabs, emit_cache):
    t, d = x2.shape
    tm = ROW_TILE
    assert tm % set_rows == 0 or set_rows % tm == 0
    tiles_per_set = max(1, set_rows // tm)
    sets_per_tile = max(1, tm // set_rows)
    rope = rope_tabs is not None
    assert not (rope and emit_cache), "cached keys are the position-free ones"
    nq = N_QKV
    row = lambda i: (i, 0)
    const = lambda i: (0, 0)
    in_specs = [pl.BlockSpec((tm, d), row),
                pl.BlockSpec((1, 8, d), _mod_index(mod, tiles_per_set)),
                pl.BlockSpec((1, d), const),
                pl.BlockSpec((d, nq), const, pipeline_mode=pl.Buffered(1)),
                pl.BlockSpec((1, 512), const),
                pl.BlockSpec((1, 128), const),
                pl.BlockSpec((512, 512), const),
                pl.BlockSpec((128, 128), const)]
    args = [x2, mod, g_attn, w_in, gq_t, gk_t, _blockdiag_ones(512), _blockdiag_ones(128)]
    if rope:
        in_specs += [pl.BlockSpec((tm, LANES), lambda i: (i % tiles_per_set, 0))] * 3
        args += list(rope_tabs)
    n_sets = t // set_rows
    wkv, wb = KV_HEADS_A * HEAD_DIM, HEADS_B * 2 * HEAD_DIM
    outs = [("tok", 512, BF16), ("feat", wkv, BF16), ("feat", wkv, BF16),
            ("tok", 512, BF16), ("feat", wb, BF16), ("tok", wb, BF16)]
    if emit_cache:
        outs += [("feat", wkv, F32), ("feat", wkv, F32), ("feat", wb, F32), ("tok", wb, F32)]
    feat = lambda i: (i // tiles_per_set, 0, i % tiles_per_set)
    feat_rows = min(tm, set_rows)
    return pl.pallas_call(
        functools.partial(_inproj_kernel, rope=rope, emit_cache=emit_cache),
        grid=(t // tm,),
        in_specs=in_specs,
        out_specs=[pl.BlockSpec((tm, w), row) if kind == "tok"
                   else pl.BlockSpec((sets_per_tile, w, feat_rows), feat) for kind, w, _ in outs],
        out_shape=[jax.ShapeDtypeStruct((t, w) if kind == "tok" else (n_sets, w, set_rows), dt)
                   for kind, w, dt in outs],
        scratch_shapes=[pltpu.VMEM((d, nq), BF16)],
        name="inproj_rope" if rope else "inproj",
        compiler_params=_cparams(1),
    )(*args)


def _cache_prep_kernel(*refs):
    half = len(refs) // 2
    for src, out in zip(refs[:half], refs[half:]):
        out[...] = src[...].astype(BF16)


def _cache_prep(arrays):
    blk = lambda b: (b, 0, 0)
    return pl.pallas_call(
        _cache_prep_kernel,
        grid=(arrays[0].shape[0],),
        in_specs=[pl.BlockSpec((1,) + a.shape[1:], blk) for a in arrays],
        out_specs=[pl.BlockSpec((1,) + a.shape[1:], blk) for a in arrays],
        out_shape=[jax.ShapeDtypeStruct(a.shape, BF16) for a in arrays],
        name="cache_prep",
        compiler_params=_cparams(1),
    )(*arrays)


def _softmax_parts(scores):
    m = scores[0].max(axis=-1, keepdims=True)
    for s in scores[1:]:
        m = jnp.maximum(m, s.max(axis=-1, keepdims=True))
    es = [jnp.exp(s - m) for s in scores]
    l = es[0].sum(axis=-1, keepdims=True)
    for e in es[1:]:
        l = l + e.sum(axis=-1, keepdims=True)
    return es, l


def _pad_rows(x, first):
    z = jnp.zeros_like(x)
    return jnp.concatenate([x, z] if first else [z, x], axis=0)


def _gqa_kernel(*refs, n_src):
    q_ref = refs[0]
    k_refs = refs[1:1 + n_src]
    v_refs = refs[1 + n_src:1 + 2 * n_src]
    o_ref = refs[1 + 2 * n_src]
    tq = q_ref.shape[0]
    lane = lax.broadcasted_iota(jnp.int32, (2 * tq, LANES), 1)
    lo_half = lane < HEAD_DIM
    for g in range(KV_HEADS_A):
        c0 = g * 256
        f0 = g * HEAD_DIM
        q = jnp.concatenate([q_ref[:, c0:c0 + 128], q_ref[:, c0 + 128:c0 + 256]], axis=0)
        kts = [k[0, f0:f0 + HEAD_DIM, :] for k in k_refs]
        vts = [v[0, f0:f0 + HEAD_DIM, :] for v in v_refs]
        s_lo = [_dot(q, _pad_rows(kt, True)) for kt in kts]
        s_hi = [_dot(q, _pad_rows(kt, False)) for kt in kts]
        e_lo, l_lo = _softmax_parts(s_lo)
        e_hi, l_hi = _softmax_parts(s_hi)
        o = None
        for e_part, vt in zip(e_lo, vts):
            t = _dot_nt(e_part.astype(BF16), _pad_rows(vt, True))
            o = t if o is None else o + t
        for e_part, vt in zip(e_hi, vts):
            o = o + _dot_nt(e_part.astype(BF16), _pad_rows(vt, False))
        o = o * jnp.where(lo_half, 1.0 / l_lo, 1.0 / l_hi)
        o_ref[:, c0:c0 + 128] = o[0:tq].astype(BF16)
        o_ref[:, c0 + 128:c0 + 256] = o[tq:2 * tq].astype(BF16)


def _diff_kernel(*refs, n_src, lam_init):
    q_ref = refs[0]
    k_refs = refs[1:1 + n_src]
    v_refs = refs[1 + n_src:1 + 2 * n_src]
    lq1, lk1, lq2, lk2, gd_ref, o_ref = refs[1 + 2 * n_src:]
    lam_all = (jnp.exp(jnp.sum(lq1[...] * lk1[...], axis=-1, keepdims=True))
               - jnp.exp(jnp.sum(lq2[...] * lk2[...], axis=-1, keepdims=True)) + lam_init)
    for j in range(HEADS_B):
        lam = lam_all[j:j + 1, :]
        q = q_ref[:, j * 128:(j + 1) * 128]
        s0 = [_dot(q, _pad_rows(k[0, j * 128:j * 128 + HEAD_DIM, :], True)) for k in k_refs]
        s1 = [_dot(q, _pad_rows(k[0, j * 128 + HEAD_DIM:(j + 1) * 128, :], False)) for k in k_refs]
        e0, l0 = _softmax_parts(s0)
        e1, l1 = _softmax_parts(s1)
        r0 = 1.0 / l0
        r1 = lam / l1
        o = None
        for a0, a1, v in zip(e0, e1, v_refs):
            t = _dot((a0 * r0 - a1 * r1).astype(BF16), v[0, :, j * 128:(j + 1) * 128])
            o = t if o is None else o + t
        o = _rms(o, gd_ref[...]) * (1.0 - lam_init)
        o_ref[:, j * 128:(j + 1) * 128] = o.astype(BF16)


def _attention(kind, q, ks, vs, set_rows, tq, extra, extra_kw):
    t, qw = q.shape
    n_sets = t // set_rows
    q_tiles = set_rows // tq
    in_specs = [pl.BlockSpec((tq, qw), lambda b, i: (b * q_tiles + i, 0))]
    for a in list(ks) + list(vs):
        in_specs.append(pl.BlockSpec((1,) + a.shape[1:], lambda b, i: (b, 0, 0)))
    for e in extra:
        in_specs.append(pl.BlockSpec(e.shape, lambda b, i: (0, 0)))
    body = functools.partial(_gqa_kernel if kind == "gqa" else _diff_kernel, n_src=len(ks), **extra_kw)
    return pl.pallas_call(
        body,
        grid=(n_sets, q_tiles),
        in_specs=in_specs,
        out_specs=pl.BlockSpec((tq, qw), lambda b, i: (b * q_tiles + i, 0)),
        out_shape=jax.ShapeDtypeStruct((t, qw), BF16),
        name=f"{kind}_attn_{len(ks)}src",
        compiler_params=_cparams(2),
    )(q, *ks, *vs, *extra)


def _post_kernel(x_ref, ya_ref, yb_ref, mod_ref, g1_ref, g2_ref, win32_ref, wba32_ref, wbb32_ref, wo32_ref, wr_ref,
                 x1_o, h2_o, aff_o, wg_ref, wba_ref, wbb_ref, wo_ref):
    d = x_ref.shape[1]

    @pl.when(pl.program_id(0) == 0)
    def _():
        wg_ref[...] = win32_ref[:, win32_ref.shape[1] - 2 * d:].astype(BF16)
        wba_ref[...] = wba32_ref[...].astype(BF16)
        wbb_ref[...] = wbb32_ref[...].astype(BF16)
        wo_ref[...] = wo32_ref[...].astype(BF16)

    x = x_ref[...]
    mod = mod_ref[0]
    h = (_rms(x, g1_ref[...]) * (1.0 + mod[1:2]) + mod[0:1]).astype(BF16)
    ga = jax.nn.sigmoid(_dot(h, wg_ref[:, 0:d]))
    merged = ga * _dot(ya_ref[...], wba_ref[...])
    gb = jax.nn.sigmoid(_dot(h, wg_ref[:, d:2 * d]))
    merged = merged + gb * _dot(yb_ref[...], wbb_ref[...])
    m = _dot(merged.astype(BF16), wo_ref[...])
    x1 = x + mod[2:3] * m
    x1_o[...] = x1
    h2 = _rms(x1, g2_ref[...]) * (1.0 + mod[4:5]) + mod[3:4]
    h2_o[...] = h2.astype(BF16)
    logits = _dot3(h2, wr_ref[...])
    lane = lax.broadcasted_iota(jnp.int32, logits.shape, 1)
    logits = jnp.where(lane < N_EXPERTS, logits, NEG_BIG)
    e = jnp.exp(logits - logits.max(axis=-1, keepdims=True))
    aff_o[...] = e / e.sum(axis=-1, keepdims=True)


def _post(x2, ya, yb, mod, set_rows, g1, g2, w_in, w_ba, w_bb, w_out, w_router_p):
    t, d = x2.shape
    tm = ROW_TILE
    tiles_per_set = max(1, set_rows // tm)
    assert mod.shape[0] == 1 or set_rows % tm == 0
    row = lambda i: (i, 0)
    const = lambda i: (0, 0)
    once = pl.Buffered(1)
    half = w_in.shape[1] // 2
    assert w_in.shape[1] == 2 * half and half % LANES == 0 and half >= 2 * d
    return pl.pallas_call(
        _post_kernel,
        grid=(t // tm,),
        in_specs=[pl.BlockSpec((tm, d), row),
                  pl.BlockSpec((tm, 512), row),
                  pl.BlockSpec((tm, 512), row),
                  pl.BlockSpec((1, 8, d), _mod_index(mod, tiles_per_set)),
                  pl.BlockSpec((1, d), const),
                  pl.BlockSpec((1, d), const),
                  pl.BlockSpec((d, half), lambda i: (0, 1), pipeline_mode=once),
                  pl.BlockSpec(w_ba.shape, const, pipeline_mode=once),
                  pl.BlockSpec(w_bb.shape, const, pipeline_mode=once),
                  pl.BlockSpec(w_out.shape, const, pipeline_mode=once),
                  pl.BlockSpec(w_router_p.shape, const)],
        out_specs=[pl.BlockSpec((tm, d), row), pl.BlockSpec((tm, d), row), pl.BlockSpec((tm, LANES), row)],
        out_shape=[jax.ShapeDtypeStruct((t, d), F32), jax.ShapeDtypeStruct((t, d), BF16),
                   jax.ShapeDtypeStruct((t, LANES), F32)],
        scratch_shapes=[pltpu.VMEM((d, 2 * d), BF16), pltpu.VMEM(w_ba.shape, BF16),
                        pltpu.VMEM(w_bb.shape, BF16), pltpu.VMEM(w_out.shape, BF16)],
        name="post_attn",
        compiler_params=_cparams(1),
    )(x2, ya, yb, mod, g1, g2, w_in, w_ba, w_bb, w_out, w_router_p)


def _select_kernel(aff_ref, slot_o, slot_t_o, affb_o, *, cap):
    aff = aff_ref[0]
    n = aff.shape[0]
    capf = float(cap)

    def enough(cand):
        return jnp.sum(jnp.where(aff >= cand, 1.0, 0.0), axis=0, keepdims=True) >= capf

    pw = jnp.ones((1, LANES), F32)
    for k in (64, 32, 16, 8, 4, 2, 1):
        pw = jnp.where(enough(pw * 2.0 ** -(k - 1)), pw, pw * 2.0 ** -k)

    def mantissa_step(_, carry):
        thr, step = carry
        cand = thr + step
        return jnp.where(enough(cand), cand, thr), step * 0.5

    thr, _ = lax.fori_loop(0, 23, mantissa_step, (pw, pw * 0.5))
    above = aff > thr
    tied = aff == thr
    need = capf - jnp.sum(jnp.where(above, 1.0, 0.0), axis=0, keepdims=True)
    r_i = lax.broadcasted_iota(jnp.int32, (n, n), 0)
    c_i = lax.broadcasted_iota(jnp.int32, (n, n), 1)
    before = jnp.where(c_i < r_i, 1.0, 0.0).astype(BF16)
    tie_rank = _dot(before, jnp.where(tied, 1.0, 0.0).astype(BF16))
    sel = above | (tied & (tie_rank < need))
    slot = _dot(before, jnp.where(sel, 1.0, 0.0).astype(BF16))
    slot = jnp.where(sel, slot, NOT_SELECTED)
    slot_o[0] = slot.astype(BF16)
    slot_t_o[0] = slot.T
    affb_o[0] = aff.astype(BF16)


def _dispatch_kernel(slot_ref, affb_ref, slot_t_ref, h2_ref, xe_o, gt_o, *, cap):
    n = h2_ref.shape[0]
    h2 = h2_ref[...]
    slot_t = slot_t_ref[0]

    per = min(N_EXPERTS, DISPATCH_ROWS // cap)
    slot_iota = lax.broadcasted_iota(jnp.int32, (cap, n), 0).astype(F32)
    for e0 in range(0, N_EXPERTS, per):
        onehot = jnp.concatenate(
            [jnp.where(slot_t[e:e + 1, :] == slot_iota, 1.0, 0.0) for e in range(e0, e0 + per)],
            axis=0).astype(BF16)
        rows = _dot(onehot, h2).astype(BF16)
        for j in range(per):
            xe_o[e0 + j] = rows[j * cap:(j + 1) * cap]

    first_lane = (pl.program_id(0) % SETS_PER_PACK) * N_EXPERTS
    shift = cap.bit_length() - 1
    for c0 in range(0, N_EXPERTS * cap, DISPATCH_ROWS):
        src = lax.broadcasted_iota(jnp.int32, (LANES, DISPATCH_ROWS), 0)
        col = lax.broadcasted_iota(jnp.int32, (LANES, DISPATCH_ROWS), 1) + c0
        spread = jnp.where(src == first_lane + lax.shift_right_logical(col, shift), 1.0, 0.0).astype(BF16)
        slot_x = _dot(slot_ref[0], spread)
        aff_x = _dot(affb_ref[0], spread)
        want = (lax.broadcasted_iota(jnp.int32, (n, DISPATCH_ROWS), 1) & (cap - 1)).astype(F32)
        gt_o[:, c0:c0 + DISPATCH_ROWS] = jnp.where(slot_x == want, aff_x, 0.0).astype(BF16)


def _route(aff, h2, n_sets, set_rows):
    t, d = h2.shape
    n = set_rows
    cap = CAPACITY_FACTOR * n // N_EXPERTS
    assert cap & (cap - 1) == 0 and cap % 16 == 0 and DISPATCH_ROWS % cap == 0
    n_packs = -(-n_sets // SETS_PER_PACK)
    aff16 = aff[:, :N_EXPERTS].reshape(n_sets, n, N_EXPERTS)
    aff16 = jnp.pad(aff16, ((0, n_packs * SETS_PER_PACK - n_sets), (0, 0), (0, 0)))
    packed = aff16.reshape(n_packs, SETS_PER_PACK, n, N_EXPERTS).transpose(0, 2, 1, 3).reshape(n_packs, n, LANES)
    pack_blk = lambda p: (p, 0, 0)
    slot, slot_t, affb = pl.pallas_call(
        functools.partial(_select_kernel, cap=cap),
        grid=(n_packs,),
        in_specs=[pl.BlockSpec((1, n, LANES), pack_blk)],
        out_specs=[pl.BlockSpec((1, n, LANES), pack_blk), pl.BlockSpec((1, LANES, n), pack_blk),
                   pl.BlockSpec((1, n, LANES), pack_blk)],
        out_shape=[jax.ShapeDtypeStruct((n_packs, n, LANES), BF16),
                   jax.ShapeDtypeStruct((n_packs, LANES, n), F32),
                   jax.ShapeDtypeStruct((n_packs, n, LANES), BF16)],
        name=f"select_cap{cap}",
        compiler_params=_cparams(1),
    )(packed)
    of_set = lambda s: (s // SETS_PER_PACK, 0, 0)
    return pl.pallas_call(
        functools.partial(_dispatch_kernel, cap=cap),
        grid=(n_sets,),
        in_specs=[pl.BlockSpec((1, n, LANES), of_set),
                  pl.BlockSpec((1, n, LANES), of_set),
                  pl.BlockSpec((1, N_EXPERTS, n), lambda s: (s // SETS_PER_PACK, s % SETS_PER_PACK, 0)),
                  pl.BlockSpec((n, d), lambda s: (s, 0))],
        out_specs=[pl.BlockSpec((N_EXPERTS, cap, d), lambda s: (0, s, 0)),
                   pl.BlockSpec((n, N_EXPERTS * cap), lambda s: (s, 0))],
        out_shape=[jax.ShapeDtypeStruct((N_EXPERTS, n_sets * cap, d), BF16),
                   jax.ShapeDtypeStruct((t, N_EXPERTS * cap), BF16)],
        name=f"dispatch_cap{cap}",
        compiler_params=_cparams(1),
    )(slot, affb, slot_t, h2)


def _expert_kernel(*refs, n_groups):
    x_refs = refs[:n_groups]
    wg_ref, wu_ref, wd_ref = refs[n_groups:n_groups + 3]
    o_refs = refs[n_groups + 3:2 * n_groups + 3]
    wg_s, wu_s, wd_s = refs[2 * n_groups + 3:]
    wg_s[...] = wg_ref[0].astype(BF16)
    wu_s[...] = wu_ref[0].astype(BF16)
    wd_s[...] = wd_ref[0].astype(BF16)
    sub = EXPERT_ROWS
    for x_ref, o_ref in zip(x_refs, o_refs):
        for r0 in range(0, x_ref.shape[1], sub):
            x = x_ref[0, r0:r0 + sub, :]
            a = _dot(x, wg_s[...])
            u = _dot(x, wu_s[...])
            hmid = (a * jax.nn.sigmoid(a) * u).astype(BF16)
            o_ref[0, r0:r0 + sub, :] = _dot(hmid, wd_s[...]).astype(BF16)


def _experts(xes, w_gate, w_up, w_down):
    e, d, f = w_gate.shape
    blk = lambda i: (i, 0, 0)
    return pl.pallas_call(
        functools.partial(_expert_kernel, n_groups=len(xes)),
        grid=(e,),
        in_specs=[pl.BlockSpec((1, x.shape[1], d), blk) for x in xes]
        + [pl.BlockSpec((1, d, f), blk), pl.BlockSpec((1, d, f), blk), pl.BlockSpec((1, f, d), blk)],
        out_specs=[pl.BlockSpec((1, x.shape[1], d), blk) for x in xes],
        out_shape=[jax.ShapeDtypeStruct(x.shape, BF16) for x in xes],
        scratch_shapes=[pltpu.VMEM((d, f), BF16), pltpu.VMEM((d, f), BF16), pltpu.VMEM((f, d), BF16)],
        name="experts",
        compiler_params=_cparams(1),
    )(*xes, w_gate, w_up, w_down)


def _combine_kernel(gt_ref, ye_ref, x1_ref, mod_ref, gf_ref, y_o):
    e, cap, d = ye_ref.shape
    ye = ye_ref[...].reshape(e * cap, d)
    moe = _dot(gt_ref[...], ye)
    x = x1_ref[...] + mod_ref[0][5:6] * moe
    y_o[...] = _rms(x, gf_ref[...])


def _combine(gt, ye, x1, mod, n_sets, set_rows, g_final):
    t, d = x1.shape
    tm = min(ROW_TILE, set_rows)
    tiles = set_rows // tm
    cap = ye.shape[1] // n_sets
    return pl.pallas_call(
        _combine_kernel,
        grid=(n_sets, tiles),
        in_specs=[pl.BlockSpec((tm, gt.shape[1]), lambda s, i: (s * tiles + i, 0)),
                  pl.BlockSpec((N_EXPERTS, cap, d), lambda s, i: (0, s, 0)),
                  pl.BlockSpec((tm, d), lambda s, i: (s * tiles + i, 0)),
                  pl.BlockSpec((1, 8, d), (lambda s, i: (s, 0, 0)) if mod.shape[0] > 1 else (lambda s, i: (0, 0, 0))),
                  pl.BlockSpec((1, d), lambda s, i: (0, 0))],
        out_specs=pl.BlockSpec((tm, d), lambda s, i: (s * tiles + i, 0)),
        out_shape=jax.ShapeDtypeStruct((t, d), F32),
        name=f"combine_cap{cap}",
        compiler_params=_cparams(2),
    )(gt, ye, x1, mod, g_final)


def _rope_tables(n_tokens):
    n_rows = n_tokens // GRID_W
    rowp = jnp.repeat(jnp.arange(n_rows), GRID_W).astype(F32)
    colp = jnp.tile(jnp.arange(GRID_W), n_rows).astype(F32)
    quarter = HEAD_DIM // 4
    freqs = ROPE_THETA ** (-jnp.arange(quarter, dtype=F32) / quarter)
    ang = jnp.stack([rowp[:, None] * freqs, colp[:, None] * freqs], axis=1)
    cos, sin = jnp.cos(ang), jnp.sin(ang)
    zero = jnp.zeros_like(sin)
    c = jnp.stack([cos, cos], axis=2).reshape(n_tokens, HEAD_DIM)
    s_up = jnp.stack([-sin, zero], axis=2).reshape(n_tokens, HEAD_DIM)
    s_dn = jnp.stack([zero, sin], axis=2).reshape(n_tokens, HEAD_DIM)
    return tuple(jnp.tile(t, (1, LANES // HEAD_DIM)) for t in (c, s_up, s_dn))


def _mix_and_route(x2, n_sets, set_rows, mod, lw, rope_tabs, cache, emit_cache, lam_init, tq):
    (g_attn, g_ffn, w_in, gq_t, gk_t, lq1, lk1, lq2, lk2, g_diff, w_ba, w_bb, w_out, w_router_p) = lw
    outs = _inproj(x2, mod, set_rows, g_attn, w_in, gq_t, gk_t, rope_tabs, emit_cache)
    qa, kta, vta, qb, ktb, vb16 = outs[:6]
    vb16 = vb16.reshape(n_sets, set_rows, vb16.shape[1])
    ks_a, vs_a, ks_b, vs_b = [kta], [vta], [ktb], [vb16]
    if cache is not None:
        ckta, cvta, cktb, cvb = cache
        ks_a, vs_a, ks_b, vs_b = [ckta, kta], [cvta, vta], [cktb, ktb], [cvb, vb16]
    ya = _attention("gqa", qa, ks_a, vs_a, set_rows, tq, (), {})
    yb = _attention("diff", qb, ks_b, vs_b, set_rows, tq, (lq1, lk1, lq2, lk2, g_diff), {"lam_init": lam_init})
    x1, h2, aff = _post(x2, ya, yb, mod, set_rows, g_attn, g_ffn, w_in, w_ba, w_bb, w_out, w_router_p)
    xe, gt = _route(aff, h2, n_sets, set_rows)
    return x1, xe, gt, outs[6:]


def kernel(x_prompt, x_sample, cache_attn_k, cache_attn_v, cache_diff_k, cache_diff_v, c, c_ctx, w_mod, b_mod,
           g_attn_norm, g_ffn_norm, w_in, g_q_norm, g_k_norm, lambda_q1, lambda_k1, lambda_q2, lambda_k2,
           g_diff_norm, w_branch_a, w_branch_b, w_out, w_router, w_exp_gate, w_exp_up, w_exp_down, g_final):
    batch, seq, d = x_prompt.shape
    dec_batch, dec_seq, _ = x_sample.shape
    depth = w_in.shape[0]
    assert depth == 1, "the final norm is fused into the layer's combine step"
    past = cache_attn_k.shape[2]
    assert w_in.shape[2] == N_QKV + 2 * d

    xp = x_prompt.reshape(batch * seq, d)
    xs = x_sample.reshape(dec_batch * dec_seq, d)
    rope_tabs = _rope_tables(dec_seq)
    c_rows = jnp.concatenate([c, c_ctx[None, :], jnp.zeros((16 - dec_batch - 1, d), F32)], axis=0)
    yp = ys = None
    caches = []
    for l in range(depth):
        lam_init = 0.8 - 0.6 * math.exp(-0.3 * l)
        mod6 = _modulation(c_rows, w_mod[l], b_mod[l])
        mod = jnp.pad(jnp.transpose(mod6, (1, 0, 2)), ((0, 0), (0, 2), (0, 0)))
        mod_lat, mod_ctx = mod[:dec_batch], mod[dec_batch:dec_batch + 1]
        lw = (g_attn_norm[l][None, :], g_ffn_norm[l][None, :], w_in[l],
              jnp.tile(g_q_norm[l], HEADS_A)[None, :], jnp.tile(g_k_norm[l], KV_HEADS_A)[None, :],
              lambda_q1[l], lambda_k1[l], lambda_q2[l], lambda_k2[l], g_diff_norm[l][None, :],
              w_branch_a[l], w_branch_b[l], w_out[l],
              jnp.pad(w_router[l], ((0, 0), (0, LANES - N_EXPERTS))))
        x1p, xe_p, gt_p, cache_out = _mix_and_route(xp, batch, seq, mod_ctx, lw, None, None, True, lam_init, seq)
        caches.append(cache_out)
        feat_major = lambda a: jnp.moveaxis(a[:, l], 1, -1).reshape(dec_batch, -1, past)
        cache_l = _cache_prep([feat_major(cache_attn_k), feat_major(cache_attn_v), feat_major(cache_diff_k),
                               cache_diff_v[:, l].reshape(dec_batch, past, -1)])
        x1s, xe_s, gt_s, _ = _mix_and_route(xs, dec_batch, dec_seq, mod_lat, lw, rope_tabs, cache_l, False,
                                            lam_init, LATENT_Q_TILE)
        ye_p, ye_s = _experts([xe_p, xe_s], w_exp_gate[l], w_exp_up[l], w_exp_down[l])
        yp = _combine(gt_p, ye_p, x1p, mod_ctx, batch, seq, g_final[None, :])
        ys = _combine(gt_s, ye_s, x1s, mod_lat, dec_batch, dec_seq, g_final[None, :])
    y_prompt = yp.reshape(batch, seq, d)
    y_sample = ys.reshape(dec_batch, dec_seq, d)
    tok_major = lambda a, dims: jnp.moveaxis(a.reshape((batch,) + dims + (seq,)), -1, 1)
    new_attn_k = jnp.stack([tok_major(cc[0], (KV_HEADS_A, HEAD_DIM)) for cc in caches], axis=1)
    new_attn_v = jnp.stack([tok_major(cc[1], (KV_HEADS_A, HEAD_DIM)) for cc in caches], axis=1)
    new_diff_k = jnp.stack([tok_major(cc[2], (HEADS_B, 2, HEAD_DIM)) for cc in caches], axis=1)
    new_diff_v = jnp.stack([cc[3].reshape(batch, seq, HEADS_B, 2 * HEAD_DIM) for cc in caches], axis=1)
    return (y_prompt, y_sample, new_attn_k, new_attn_v, new_diff_k, new_diff_v)
```

```python
import functools
import math

import numpy as np
import jax
import jax.numpy as jnp
from jax import lax
from jax.experimental import pallas as pl
from jax.experimental.pallas import tpu as pltpu

F32 = jnp.float32
BF16 = jnp.bfloat16

HEAD_DIM = 64
HEADS_A = 8
KV_HEADS_A = 2
HEADS_B = 4
N_QKV = HEADS_A * HEAD_DIM + 2 * KV_HEADS_A * HEAD_DIM + 3 * HEADS_B * 2 * HEAD_DIM
N_EXPERTS = 16
CAPACITY_FACTOR = 2
GRID_W = 64
ROPE_THETA = 10000.0
EPS = 1e-6
LANES = 128
ROW_TILE = 512
CHAIN_ROWS = 256
EXPERT_ROWS = 256
LATENT_Q_TILE = 256
SETS_PER_PACK = LANES // N_EXPERTS
DISPATCH_ROWS = 512
NEG_BIG = -1e30
NOT_SELECTED = -1.0
VMEM_LIMIT = 56 * 1024 * 1024


def _cparams(n_axes):
    return pltpu.CompilerParams(dimension_semantics=("arbitrary",) * n_axes,
                                vmem_limit_bytes=VMEM_LIMIT)


def _dot(a, b):
    return jnp.dot(a, b, preferred_element_type=F32)


def _dot_nt(a, b):
    return lax.dot_general(a, b, (((1,), (1,)), ((), ())), preferred_element_type=F32)


def _split(a):
    hi = a.astype(BF16)
    lo = (a - hi.astype(F32)).astype(BF16)
    return hi, lo


def _dot3(a, b):
    a_hi, a_lo = _split(a)
    b_hi, b_lo = _split(b)
    return _dot(a_hi, b_hi) + _dot(a_lo, b_hi) + _dot(a_hi, b_lo)


def _rms(x, g):
    return x * lax.rsqrt(jnp.mean(x * x, axis=-1, keepdims=True) + EPS) * g


def _mod_index(mod, tiles_per_set):
    if mod.shape[0] == 1:
        return lambda i: (0, 0, 0)
    return lambda i: (i // tiles_per_set, 0, 0)


def _mod_kernel(c_ref, w_ref, b_ref, o_ref):
    c = c_ref[...]
    a = c * jax.nn.sigmoid(c)
    o_ref[0] = _dot3(a, w_ref[...]) + b_ref[0]


def _modulation(c_rows, w_mod, b_mod):
    r, d = c_rows.shape
    return pl.pallas_call(
        _mod_kernel,
        grid=(6,),
        in_specs=[pl.BlockSpec((r, d), lambda j: (0, 0)),
                  pl.BlockSpec((d, d), lambda j: (0, j)),
                  pl.BlockSpec((1, 1, d), lambda j: (j, 0, 0))],
        out_specs=pl.BlockSpec((1, r, d), lambda j: (j, 0, 0)),
        out_shape=jax.ShapeDtypeStruct((6, r, d), F32),
        name="mod",
        compiler_params=_cparams(1),
    )(c_rows, w_mod, b_mod.reshape(6, 1, d))


def _seg_sumsq(x, ones_blockdiag):
    hi, lo = _split(x * x)
    return _dot(hi, ones_blockdiag) + _dot(lo, ones_blockdiag)


def _rope(x, c, s_up, s_dn):
    w = x.shape[1]
    reps = w // c.shape[1]
    if reps > 1:
        c, s_up, s_dn = (jnp.concatenate([t] * reps, axis=1) for t in (c, s_up, s_dn))
    return x * c + pltpu.roll(x, w - 16, 1) * s_up + pltpu.roll(x, 16, 1) * s_dn


def _inproj_kernel(*refs, rope, emit_cache):
    (x_ref, mod_ref, g_ref, w32_ref, gq_ref, gk_ref, bd512_ref, bd128_ref), refs = refs[:8], refs[8:]
    if rope:
        (c_ref, su_ref, sd_ref), refs = refs[:3], refs[3:]
    qa_o, kta_o, vta_o, qb_o, ktb_o, vb_o = refs[:6]
    if emit_cache:
        ka_c, va_c, kb_c, vb_c = refs[6:10]
    w_ref = refs[-1]

    @pl.when(pl.program_id(0) == 0)
    def _():
        w_ref[...] = w32_ref[...].astype(BF16)

    x = x_ref[...]
    mod = mod_ref[0]
    h = (_rms(x, g_ref[...]) * (1.0 + mod[1:2]) + mod[0:1]).astype(BF16)
    if rope:
        tabs = (c_ref[...], su_ref[...], sd_ref[...])
    wa = HEADS_A * HEAD_DIM
    wkv = KV_HEADS_A * HEAD_DIM
    wb = HEADS_B * 2 * HEAD_DIM
    o_ka, o_va, o_qb = wa, wa + wkv, wa + 2 * wkv
    o_kb, o_vb = o_qb + wb, o_qb + 2 * wb
    scale = HEAD_DIM ** -0.5 * math.log2(math.e)

    qa = _dot(h, w_ref[:, 0:wa])
    qa = qa * lax.rsqrt(_seg_sumsq(qa, bd512_ref[...]) * (1.0 / HEAD_DIM) + EPS) * gq_ref[...]
    if rope:
        qa = _rope(qa, *tabs)
    qa_o[...] = (qa * scale).astype(BF16)

    kv = _dot(h, w_ref[:, o_ka:o_qb])
    ka, va = kv[:, 0:wkv], kv[:, wkv:2 * wkv]
    ka = ka * lax.rsqrt(_seg_sumsq(ka, bd128_ref[...]) * (1.0 / HEAD_DIM) + EPS) * gk_ref[...]
    def put_feat(val, out_bf16, out_f32):
        rows = out_bf16.shape[2]
        for s in range(out_bf16.shape[0]):
            t = val[s * rows:(s + 1) * rows].T
            if out_f32 is not None:
                out_f32[s] = t
            if out_bf16 is not None:
                out_bf16[s] = t.astype(BF16)

    if rope:
        put_feat(_rope(ka, *tabs), kta_o, None)
    else:
        put_feat(ka, kta_o, ka_c if emit_cache else None)
    put_feat(va, vta_o, va_c if emit_cache else None)

    qb = _dot(h, w_ref[:, o_qb:o_kb])
    if rope:
        qb = _rope(qb, *tabs)
    qb_o[...] = (qb * scale).astype(BF16)

    kb = _dot(h, w_ref[:, o_kb:o_vb])
    if rope:
        put_feat(_rope(kb, *tabs), ktb_o, None)
    else:
        put_feat(kb, ktb_o, kb_c if emit_cache else None)

    vb = _dot(h, w_ref[:, o_vb:o_vb + wb])
    if emit_cache:
        for hd in range(HEADS_B):
            vb_c[pl.ds(hd, vb.shape[0], stride=HEADS_B), :] = vb[:, hd * 128:(hd + 1) * 128]
    vb_o[...] = vb.astype(BF16)


def _blockdiag_ones(width):
    g = np.arange(width) // HEAD_DIM
    return jnp.asarray((g[:, None] == g[None, :]).astype(np.float32), dtype=BF16)


def _inproj(x2, mod, set_rows, g_attn, w_in, gq_t, gk_t, rope_tabs, emit_cache):
    t, d = x2.shape
    tm = ROW_TILE
    assert tm % set_rows == 0 or set_rows % tm == 0
    tiles_per_set = max(1, set_rows // tm)
    sets_per_tile = max(1, tm // set_rows)
    rope = rope_tabs is not None
    assert not (rope and emit_cache), "cached keys are the position-free ones"
    nq = N_QKV
    row = lambda i: (i, 0)
    const = lambda i: (0, 0)
    in_specs = [pl.BlockSpec((tm, d), row),
                pl.BlockSpec((1, 8, d), _mod_index(mod, tiles_per_set)),
                pl.BlockSpec((1, d), const),
                pl.BlockSpec((d, nq), const, pipeline_mode=pl.Buffered(1)),
                pl.BlockSpec((1, 512), const),
                pl.BlockSpec((1, 128), const),
                pl.BlockSpec((512, 512), const),
                pl.BlockSpec((128, 128), const)]
    args = [x2, mod, g_attn, w_in, gq_t, gk_t, _blockdiag_ones(512), _blockdiag_ones(128)]
    if rope:
        in_specs += [pl.BlockSpec((tm, LANES), lambda i: (i % tiles_per_set, 0))] * 3
        args += list(rope_tabs)
    n_sets = t // set_rows
    wkv, wb = KV_HEADS_A * HEAD_DIM, HEADS_B * 2 * HEAD_DIM
    outs = [("tok", 512, BF16), ("feat", wkv, BF16), ("feat", wkv, BF16),
            ("tok", 512, BF16), ("feat", wb, BF16), ("tok", wb, BF16)]
    if emit_cache:
        outs += [("feat", wkv, F32), ("feat", wkv, F32), ("feat", wb, F32), ("tokhead", wb, F32)]
    feat = lambda i: (i // tiles_per_set, 0, i % tiles_per_set)
    feat_rows = min(tm, set_rows)

    def out_block(kind, w):
        if kind == "tok":
            return pl.BlockSpec((tm, w), row)
        if kind == "tokhead":
            return pl.BlockSpec((tm * HEADS_B, w // HEADS_B), row)
        return pl.BlockSpec((sets_per_tile, w, feat_rows), feat)

    def out_array(kind, w, dt):
        shape = {"tok": (t, w), "tokhead": (t * HEADS_B, w // HEADS_B), "feat": (n_sets, w, set_rows)}[kind]
        return jax.ShapeDtypeStruct(shape, dt)

    return pl.pallas_call(
        functools.partial(_inproj_kernel, rope=rope, emit_cache=emit_cache),
        grid=(t // tm,),
        in_specs=in_specs,
        out_specs=[out_block(kind, w) for kind, w, _ in outs],
        out_shape=[out_array(kind, w, dt) for kind, w, dt in outs],
        scratch_shapes=[pltpu.VMEM((d, nq), BF16)],
        name="inproj_rope" if rope else "inproj",
        compiler_params=_cparams(1),
    )(*args)


def _cache_prep_kernel(*refs):
    half = len(refs) // 2
    for src, out in zip(refs[:half - 1], refs[half:-1]):
        out[...] = src[...].astype(BF16)
    v_src, v_out = refs[half - 1], refs[-1]
    keys = v_out.shape[1]
    for hd in range(HEADS_B):
        v_out[0, :, hd * 128:(hd + 1) * 128] = v_src[0, pl.ds(hd, keys, stride=HEADS_B), :].astype(BF16)


def _cache_prep(arrays, v_by_head):
    blk = lambda b: (b, 0, 0)
    n_sets, rows, width = v_by_head.shape
    v_shape = (n_sets, rows // HEADS_B, width * HEADS_B)
    return pl.pallas_call(
        _cache_prep_kernel,
        grid=(n_sets,),
        in_specs=[pl.BlockSpec((1,) + a.shape[1:], blk) for a in arrays + [v_by_head]],
        out_specs=[pl.BlockSpec((1,) + a.shape[1:], blk) for a in arrays] + [pl.BlockSpec((1,) + v_shape[1:], blk)],
        out_shape=[jax.ShapeDtypeStruct(a.shape, BF16) for a in arrays] + [jax.ShapeDtypeStruct(v_shape, BF16)],
        name="cache_prep",
        compiler_params=_cparams(1),
    )(*arrays, v_by_head)


def _softmax_parts(scores):
    m = scores[0].max(axis=-1, keepdims=True)
    for s in scores[1:]:
        m = jnp.maximum(m, s.max(axis=-1, keepdims=True))
    es = [jnp.exp2(s - m) for s in scores]
    l = es[0].sum(axis=-1, keepdims=True)
    for e in es[1:]:
        l = l + e.sum(axis=-1, keepdims=True)
    return es, l


def _pad_rows(x, first):
    z = jnp.zeros_like(x)
    return jnp.concatenate([x, z] if first else [z, x], axis=0)


def _gqa_kernel(*refs, n_src):
    q_ref = refs[0]
    k_refs = refs[1:1 + n_src]
    v_refs = refs[1 + n_src:1 + 2 * n_src]
    o_ref = refs[1 + 2 * n_src]
    tq = q_ref.shape[0]
    lane = lax.broadcasted_iota(jnp.int32, (2 * tq, LANES), 1)
    lo_half = lane < HEAD_DIM
    for g in range(KV_HEADS_A):
        c0 = g * 256
        f0 = g * HEAD_DIM
        q = jnp.concatenate([q_ref[:, c0:c0 + 128], q_ref[:, c0 + 128:c0 + 256]], axis=0)
        kts = [k[0, f0:f0 + HEAD_DIM, :] for k in k_refs]
        vts = [v[0, f0:f0 + HEAD_DIM, :] for v in v_refs]
        s_lo = [_dot(q, _pad_rows(kt, True)) for kt in kts]
        s_hi = [_dot(q, _pad_rows(kt, False)) for kt in kts]
        e_lo, l_lo = _softmax_parts(s_lo)
        e_hi, l_hi = _softmax_parts(s_hi)
        o = None
        for e_part, vt in zip(e_lo, vts):
            t = _dot_nt(e_part.astype(BF16), _pad_rows(vt, True))
            o = t if o is None else o + t
        for e_part, vt in zip(e_hi, vts):
            o = o + _dot_nt(e_part.astype(BF16), _pad_rows(vt, False))
        o = o * jnp.where(lo_half, 1.0 / l_lo, 1.0 / l_hi)
        o_ref[:, c0:c0 + 128] = o[0:tq].astype(BF16)
        o_ref[:, c0 + 128:c0 + 256] = o[tq:2 * tq].astype(BF16)


def _diff_kernel(*refs, n_src, lam_init):
    q_ref = refs[0]
    k_refs = refs[1:1 + n_src]
    v_refs = refs[1 + n_src:1 + 2 * n_src]
    lq1, lk1, lq2, lk2, gd_ref, o_ref = refs[1 + 2 * n_src:]
    lam_all = (jnp.exp(jnp.sum(lq1[...] * lk1[...], axis=-1, keepdims=True))
               - jnp.exp(jnp.sum(lq2[...] * lk2[...], axis=-1, keepdims=True)) + lam_init)
    for j in range(HEADS_B):
        lam = lam_all[j:j + 1, :]
        q = q_ref[:, j * 128:(j + 1) * 128]
        s0 = [_dot(q, _pad_rows(k[0, j * 128:j * 128 + HEAD_DIM, :], True)) for k in k_refs]
        s1 = [_dot(q, _pad_rows(k[0, j * 128 + HEAD_DIM:(j + 1) * 128, :], False)) for k in k_refs]
        e0, l0 = _softmax_parts(s0)
        e1, l1 = _softmax_parts(s1)
        r0 = 1.0 / l0
        r1 = lam / l1
        o = None
        for a0, a1, v in zip(e0, e1, v_refs):
            t = _dot((a0 * r0 - a1 * r1).astype(BF16), v[0, :, j * 128:(j + 1) * 128])
            o = t if o is None else o + t
        o = _rms(o, gd_ref[...]) * (1.0 - lam_init)
        o_ref[:, j * 128:(j + 1) * 128] = o.astype(BF16)


def _attention(kind, q, ks, vs, set_rows, tq, extra, extra_kw):
    t, qw = q.shape
    n_sets = t // set_rows
    q_tiles = set_rows // tq
    in_specs = [pl.BlockSpec((tq, qw), lambda b, i: (b * q_tiles + i, 0))]
    for a in list(ks) + list(vs):
        in_specs.append(pl.BlockSpec((1,) + a.shape[1:], lambda b, i: (b, 0, 0)))
    for e in extra:
        in_specs.append(pl.BlockSpec(e.shape, lambda b, i: (0, 0)))
    body = functools.partial(_gqa_kernel if kind == "gqa" else _diff_kernel, n_src=len(ks), **extra_kw)
    return pl.pallas_call(
        body,
        grid=(n_sets, q_tiles),
        in_specs=in_specs,
        out_specs=pl.BlockSpec((tq, qw), lambda b, i: (b * q_tiles + i, 0)),
        out_shape=jax.ShapeDtypeStruct((t, qw), BF16),
        name=f"{kind}_attn_{len(ks)}src",
        compiler_params=_cparams(2),
    )(q, *ks, *vs, *extra)


def _post_kernel(x_ref, ya_ref, yb_ref, mod_ref, g1_ref, g2_ref, win32_ref, wba32_ref, wbb32_ref, wo32_ref, wr_ref,
                 x1_o, h2_o, aff_o, wg_ref, wba_ref, wbb_ref, wo_ref, wr2_ref):
    d = x_ref.shape[1]

    @pl.when(pl.program_id(0) == 0)
    def _():
        wg_ref[...] = win32_ref[:, win32_ref.shape[1] - 2 * d:].astype(BF16)
        wba_ref[...] = wba32_ref[...].astype(BF16)
        wbb_ref[...] = wbb32_ref[...].astype(BF16)
        wo_ref[...] = wo32_ref[...].astype(BF16)
        wr_hi, wr_lo = _split(wr_ref[...])
        wr2_ref[...] = jnp.concatenate([wr_hi, wr_lo], axis=1)

    mod = mod_ref[0]
    for r0 in range(0, x_ref.shape[0], CHAIN_ROWS):
        rows = slice(r0, r0 + CHAIN_ROWS)
        x = x_ref[rows, :]
        h = (_rms(x, g1_ref[...]) * (1.0 + mod[1:2]) + mod[0:1]).astype(BF16)
        ga = jax.nn.sigmoid(_dot(h, wg_ref[:, 0:d]))
        merged = ga * _dot(ya_ref[rows, :], wba_ref[...])
        gb = jax.nn.sigmoid(_dot(h, wg_ref[:, d:2 * d]))
        merged = merged + gb * _dot(yb_ref[rows, :], wbb_ref[...])
        m = _dot(merged.astype(BF16), wo_ref[...])
        x1 = x + mod[2:3] * m
        x1_o[rows, :] = x1
        h2 = _rms(x1, g2_ref[...]) * (1.0 + mod[4:5]) + mod[3:4]
        h2_o[rows, :] = h2.astype(BF16)
        h2_hi, h2_lo = _split(h2)
        both = _dot(h2_hi, wr2_ref[...])
        logits = both[:, 0:LANES] + both[:, LANES:2 * LANES] + _dot(h2_lo, wr2_ref[:, 0:LANES])
        lane = lax.broadcasted_iota(jnp.int32, logits.shape, 1)
        logits = jnp.where(lane < N_EXPERTS, logits, NEG_BIG)
        e = jnp.exp(logits - logits.max(axis=-1, keepdims=True))
        aff_o[rows, :] = e / e.sum(axis=-1, keepdims=True)


def _post(x2, ya, yb, mod, set_rows, g1, g2, w_in, w_ba, w_bb, w_out, w_router_p):
    t, d = x2.shape
    tm = ROW_TILE
    tiles_per_set = max(1, set_rows // tm)
    assert mod.shape[0] == 1 or set_rows % tm == 0
    row = lambda i: (i, 0)
    const = lambda i: (0, 0)
    once = pl.Buffered(1)
    half = w_in.shape[1] // 2
    assert w_in.shape[1] == 2 * half and half % LANES == 0 and half >= 2 * d
    return pl.pallas_call(
        _post_kernel,
        grid=(t // tm,),
        in_specs=[pl.BlockSpec((tm, d), row),
                  pl.BlockSpec((tm, 512), row),
                  pl.BlockSpec((tm, 512), row),
                  pl.BlockSpec((1, 8, d), _mod_index(mod, tiles_per_set)),
                  pl.BlockSpec((1, d), const),
                  pl.BlockSpec((1, d), const),
                  pl.BlockSpec((d, half), lambda i: (0, 1), pipeline_mode=once),
                  pl.BlockSpec(w_ba.shape, const, pipeline_mode=once),
                  pl.BlockSpec(w_bb.shape, const, pipeline_mode=once),
                  pl.BlockSpec(w_out.shape, const, pipeline_mode=once),
                  pl.BlockSpec(w_router_p.shape, const)],
        out_specs=[pl.BlockSpec((tm, d), row), pl.BlockSpec((tm, d), row), pl.BlockSpec((tm, LANES), row)],
        out_shape=[jax.ShapeDtypeStruct((t, d), F32), jax.ShapeDtypeStruct((t, d), BF16),
                   jax.ShapeDtypeStruct((t, LANES), F32)],
        scratch_shapes=[pltpu.VMEM((d, 2 * d), BF16), pltpu.VMEM(w_ba.shape, BF16),
                        pltpu.VMEM(w_bb.shape, BF16), pltpu.VMEM(w_out.shape, BF16),
                        pltpu.VMEM((d, 2 * LANES), BF16)],
        name="post_attn",
        compiler_params=_cparams(1),
    )(x2, ya, yb, mod, g1, g2, w_in, w_ba, w_bb, w_out, w_router_p)


def _select_kernel(aff_ref, slot_o, slot_t_o, affb_o, *, cap):
    aff = aff_ref[0]
    n = aff.shape[0]
    capf = float(cap)

    def enough(cand):
        return jnp.sum(jnp.where(aff >= cand, 1.0, 0.0), axis=0, keepdims=True) >= capf

    pw = jnp.ones((1, LANES), F32)
    for k in (64, 32, 16, 8, 4, 2, 1):
        pw = jnp.where(enough(pw * 2.0 ** -(k - 1)), pw, pw * 2.0 ** -k)

    def mantissa_step(_, carry):
        thr, step = carry
        cand = thr + step
        return jnp.where(enough(cand), cand, thr), step * 0.5

    thr, _ = lax.fori_loop(0, 23, mantissa_step, (pw, pw * 0.5))
    above = aff > thr
    tied = aff == thr
    need = capf - jnp.sum(jnp.where(above, 1.0, 0.0), axis=0, keepdims=True)
    r_i = lax.broadcasted_iota(jnp.int32, (n, n), 0)
    c_i = lax.broadcasted_iota(jnp.int32, (n, n), 1)
    before = jnp.where(c_i < r_i, 1.0, 0.0).astype(BF16)
    tie_rank = _dot(before, jnp.where(tied, 1.0, 0.0).astype(BF16))
    sel = above | (tied & (tie_rank < need))
    slot = _dot(before, jnp.where(sel, 1.0, 0.0).astype(BF16))
    slot = jnp.where(sel, slot, NOT_SELECTED)
    slot_o[0] = slot.astype(BF16)
    slot_t_o[0] = slot.T
    affb_o[0] = aff.astype(BF16)


def _dispatch_kernel(slot_ref, affb_ref, slot_t_ref, h2_ref, xe_o, gt_o, *, cap):
    n = h2_ref.shape[0]
    h2 = h2_ref[...]
    slot_t = slot_t_ref[0]

    per = min(N_EXPERTS, DISPATCH_ROWS // cap)
    slot_iota = lax.broadcasted_iota(jnp.int32, (cap, n), 0).astype(F32)
    for e0 in range(0, N_EXPERTS, per):
        onehot = jnp.concatenate(
            [jnp.where(slot_t[e:e + 1, :] == slot_iota, 1.0, 0.0) for e in range(e0, e0 + per)],
            axis=0).astype(BF16)
        rows = _dot(onehot, h2).astype(BF16)
        for j in range(per):
            xe_o[e0 + j] = rows[j * cap:(j + 1) * cap]

    first_lane = (pl.program_id(0) % SETS_PER_PACK) * N_EXPERTS
    shift = cap.bit_length() - 1
    for c0 in range(0, N_EXPERTS * cap, DISPATCH_ROWS):
        src = lax.broadcasted_iota(jnp.int32, (LANES, DISPATCH_ROWS), 0)
        col = lax.broadcasted_iota(jnp.int32, (LANES, DISPATCH_ROWS), 1) + c0
        spread = jnp.where(src == first_lane + lax.shift_right_logical(col, shift), 1.0, 0.0).astype(BF16)
        slot_x = _dot(slot_ref[0], spread)
        aff_x = _dot(affb_ref[0], spread)
        want = (lax.broadcasted_iota(jnp.int32, (n, DISPATCH_ROWS), 1) & (cap - 1)).astype(F32)
        gt_o[:, c0:c0 + DISPATCH_ROWS] = jnp.where(slot_x == want, aff_x, 0.0).astype(BF16)


def _route(aff, h2, n_sets, set_rows):
    t, d = h2.shape
    n = set_rows
    cap = CAPACITY_FACTOR * n // N_EXPERTS
    assert cap & (cap - 1) == 0 and cap % 16 == 0 and DISPATCH_ROWS % cap == 0
    n_packs = -(-n_sets // SETS_PER_PACK)
    aff16 = aff[:, :N_EXPERTS].reshape(n_sets, n, N_EXPERTS)
    aff16 = jnp.pad(aff16, ((0, n_packs * SETS_PER_PACK - n_sets), (0, 0), (0, 0)))
    packed = aff16.reshape(n_packs, SETS_PER_PACK, n, N_EXPERTS).transpose(0, 2, 1, 3).reshape(n_packs, n, LANES)
    pack_blk = lambda p: (p, 0, 0)
    slot, slot_t, affb = pl.pallas_call(
        functools.partial(_select_kernel, cap=cap),
        grid=(n_packs,),
        in_specs=[pl.BlockSpec((1, n, LANES), pack_blk)],
        out_specs=[pl.BlockSpec((1, n, LANES), pack_blk), pl.BlockSpec((1, LANES, n), pack_blk),
                   pl.BlockSpec((1, n, LANES), pack_blk)],
        out_shape=[jax.ShapeDtypeStruct((n_packs, n, LANES), BF16),
                   jax.ShapeDtypeStruct((n_packs, LANES, n), F32),
                   jax.ShapeDtypeStruct((n_packs, n, LANES), BF16)],
        name=f"select_cap{cap}",
        compiler_params=_cparams(1),
    )(packed)
    of_set = lambda s: (s // SETS_PER_PACK, 0, 0)
    return pl.pallas_call(
        functools.partial(_dispatch_kernel, cap=cap),
        grid=(n_sets,),
        in_specs=[pl.BlockSpec((1, n, LANES), of_set),
                  pl.BlockSpec((1, n, LANES), of_set),
                  pl.BlockSpec((1, N_EXPERTS, n), lambda s: (s // SETS_PER_PACK, s % SETS_PER_PACK, 0)),
                  pl.BlockSpec((n, d), lambda s: (s, 0))],
        out_specs=[pl.BlockSpec((N_EXPERTS, cap, d), lambda s: (0, s, 0)),
                   pl.BlockSpec((n, N_EXPERTS * cap), lambda s: (s, 0))],
        out_shape=[jax.ShapeDtypeStruct((N_EXPERTS, n_sets * cap, d), BF16),
                   jax.ShapeDtypeStruct((t, N_EXPERTS * cap), BF16)],
        name=f"dispatch_cap{cap}",
        compiler_params=_cparams(1),
    )(slot, affb, slot_t, h2)


def _expert_kernel(*refs, n_groups):
    x_refs = refs[:n_groups]
    wg_ref, wu_ref, wd_ref = refs[n_groups:n_groups + 3]
    o_refs = refs[n_groups + 3:2 * n_groups + 3]
    wg_s, wu_s, wd_s = refs[2 * n_groups + 3:]
    wg_s[...] = wg_ref[0].astype(BF16)
    wu_s[...] = wu_ref[0].astype(BF16)
    wd_s[...] = wd_ref[0].astype(BF16)
    sub = EXPERT_ROWS
    for x_ref, o_ref in zip(x_refs, o_refs):
        for r0 in range(0, x_ref.shape[1], sub):
            x = x_ref[0, r0:r0 + sub, :]
            a = _dot(x, wg_s[...])
            u = _dot(x, wu_s[...])
            hmid = (a * jax.nn.sigmoid(a) * u).astype(BF16)
            o_ref[0, r0:r0 + sub, :] = _dot(hmid, wd_s[...]).astype(BF16)


def _experts(xes, w_gate, w_up, w_down):
    e, d, f = w_gate.shape
    blk = lambda i: (i, 0, 0)
    return pl.pallas_call(
        functools.partial(_expert_kernel, n_groups=len(xes)),
        grid=(e,),
        in_specs=[pl.BlockSpec((1, x.shape[1], d), blk) for x in xes]
        + [pl.BlockSpec((1, d, f), blk), pl.BlockSpec((1, d, f), blk), pl.BlockSpec((1, f, d), blk)],
        out_specs=[pl.BlockSpec((1, x.shape[1], d), blk) for x in xes],
        out_shape=[jax.ShapeDtypeStruct(x.shape, BF16) for x in xes],
        scratch_shapes=[pltpu.VMEM((d, f), BF16), pltpu.VMEM((d, f), BF16), pltpu.VMEM((f, d), BF16)],
        name="experts",
        compiler_params=_cparams(1),
    )(*xes, w_gate, w_up, w_down)


def _combine_kernel(gt_ref, ye_ref, x1_ref, mod_ref, gf_ref, y_o):
    e, cap, d = ye_ref.shape
    ye = ye_ref[...].reshape(e * cap, d)
    moe = _dot(gt_ref[...], ye)
    x = x1_ref[...] + mod_ref[0][5:6] * moe
    y_o[...] = _rms(x, gf_ref[...])


def _combine(gt, ye, x1, mod, n_sets, set_rows, g_final):
    t, d = x1.shape
    tm = min(ROW_TILE, set_rows)
    tiles = set_rows // tm
    cap = ye.shape[1] // n_sets
    return pl.pallas_call(
        _combine_kernel,
        grid=(n_sets, tiles),
        in_specs=[pl.BlockSpec((tm, gt.shape[1]), lambda s, i: (s * tiles + i, 0)),
                  pl.BlockSpec((N_EXPERTS, cap, d), lambda s, i: (0, s, 0)),
                  pl.BlockSpec((tm, d), lambda s, i: (s * tiles + i, 0)),
                  pl.BlockSpec((1, 8, d), (lambda s, i: (s, 0, 0)) if mod.shape[0] > 1 else (lambda s, i: (0, 0, 0))),
                  pl.BlockSpec((1, d), lambda s, i: (0, 0))],
        out_specs=pl.BlockSpec((tm, d), lambda s, i: (s * tiles + i, 0)),
        out_shape=jax.ShapeDtypeStruct((t, d), F32),
        name=f"combine_cap{cap}",
        compiler_params=_cparams(2),
    )(gt, ye, x1, mod, g_final)


def _rope_tables(n_tokens):
    n_rows = n_tokens // GRID_W
    rowp = jnp.repeat(jnp.arange(n_rows), GRID_W).astype(F32)
    colp = jnp.tile(jnp.arange(GRID_W), n_rows).astype(F32)
    quarter = HEAD_DIM // 4
    freqs = ROPE_THETA ** (-jnp.arange(quarter, dtype=F32) / quarter)
    ang = jnp.stack([rowp[:, None] * freqs, colp[:, None] * freqs], axis=1)
    cos, sin = jnp.cos(ang), jnp.sin(ang)
    zero = jnp.zeros_like(sin)
    c = jnp.stack([cos, cos], axis=2).reshape(n_tokens, HEAD_DIM)
    s_up = jnp.stack([-sin, zero], axis=2).reshape(n_tokens, HEAD_DIM)
    s_dn = jnp.stack([zero, sin], axis=2).reshape(n_tokens, HEAD_DIM)
    return tuple(jnp.tile(t, (1, LANES // HEAD_DIM)) for t in (c, s_up, s_dn))


def _mix_and_route(x2, n_sets, set_rows, mod, lw, rope_tabs, cache, emit_cache, lam_init, tq):
    (g_attn, g_ffn, w_in, gq_t, gk_t, lq1, lk1, lq2, lk2, g_diff, w_ba, w_bb, w_out, w_router_p) = lw
    outs = _inproj(x2, mod, set_rows, g_attn, w_in, gq_t, gk_t, rope_tabs, emit_cache)
    qa, kta, vta, qb, ktb, vb16 = outs[:6]
    vb16 = vb16.reshape(n_sets, set_rows, vb16.shape[1])
    ks_a, vs_a, ks_b, vs_b = [kta], [vta], [ktb], [vb16]
    if cache is not None:
        ckta, cvta, cktb, cvb = cache
        ks_a, vs_a, ks_b, vs_b = [ckta, kta], [cvta, vta], [cktb, ktb], [cvb, vb16]
    ya = _attention("gqa", qa, ks_a, vs_a, set_rows, tq, (), {})
    yb = _attention("diff", qb, ks_b, vs_b, set_rows, tq, (lq1, lk1, lq2, lk2, g_diff), {"lam_init": lam_init})
    x1, h2, aff = _post(x2, ya, yb, mod, set_rows, g_attn, g_ffn, w_in, w_ba, w_bb, w_out, w_router_p)
    xe, gt = _route(aff, h2, n_sets, set_rows)
    return x1, xe, gt, outs[6:]


def kernel(x_prompt, x_sample, cache_attn_k, cache_attn_v, cache_diff_k, cache_diff_v, c, c_ctx, w_mod, b_mod,
           g_attn_norm, g_ffn_norm, w_in, g_q_norm, g_k_norm, lambda_q1, lambda_k1, lambda_q2, lambda_k2,
           g_diff_norm, w_branch_a, w_branch_b, w_out, w_router, w_exp_gate, w_exp_up, w_exp_down, g_final):
    batch, seq, d = x_prompt.shape
    dec_batch, dec_seq, _ = x_sample.shape
    depth = w_in.shape[0]
    assert depth == 1, "the final norm is fused into the layer's combine step"
    past = cache_attn_k.shape[2]
    assert w_in.shape[2] == N_QKV + 2 * d

    xp = x_prompt.reshape(batch * seq, d)
    xs = x_sample.reshape(dec_batch * dec_seq, d)
    rope_tabs = _rope_tables(dec_seq)
    c_rows = jnp.concatenate([c, c_ctx[None, :], jnp.zeros((16 - dec_batch - 1, d), F32)], axis=0)
    yp = ys = None
    caches = []
    for l in range(depth):
        lam_init = 0.8 - 0.6 * math.exp(-0.3 * l)
        mod6 = _modulation(c_rows, w_mod[l], b_mod[l])
        mod = jnp.pad(jnp.transpose(mod6, (1, 0, 2)), ((0, 0), (0, 2), (0, 0)))
        mod_lat, mod_ctx = mod[:dec_batch], mod[dec_batch:dec_batch + 1]
        lw = (g_attn_norm[l][None, :], g_ffn_norm[l][None, :], w_in[l],
              jnp.tile(g_q_norm[l], HEADS_A)[None, :], jnp.tile(g_k_norm[l], KV_HEADS_A)[None, :],
              lambda_q1[l], lambda_k1[l], lambda_q2[l], lambda_k2[l], g_diff_norm[l][None, :],
              w_branch_a[l], w_branch_b[l], w_out[l],
              jnp.pad(w_router[l], ((0, 0), (0, LANES - N_EXPERTS))))
        x1p, xe_p, gt_p, cache_out = _mix_and_route(xp, batch, seq, mod_ctx, lw, None, None, True, lam_init, seq)
        caches.append(cache_out)
        feat_major = lambda a: jnp.moveaxis(a[:, l], 1, -1).reshape(dec_batch, -1, past)
        cache_l = _cache_prep([feat_major(cache_attn_k), feat_major(cache_attn_v), feat_major(cache_diff_k)],
                              cache_diff_v[:, l].reshape(dec_batch, past * HEADS_B, -1))
        x1s, xe_s, gt_s, _ = _mix_and_route(xs, dec_batch, dec_seq, mod_lat, lw, rope_tabs, cache_l, False,
                                            lam_init, LATENT_Q_TILE)
        ye_p, ye_s = _experts([xe_p, xe_s], w_exp_gate[l], w_exp_up[l], w_exp_down[l])
        yp = _combine(gt_p, ye_p, x1p, mod_ctx, batch, seq, g_final[None, :])
        ys = _combine(gt_s, ye_s, x1s, mod_lat, dec_batch, dec_seq, g_final[None, :])
    y_prompt = yp.reshape(batch, seq, d)
    y_sample = ys.reshape(dec_batch, dec_seq, d)
    tok_major = lambda a, dims: jnp.moveaxis(a.reshape((batch,) + dims + (seq,)), -1, 1)
    new_attn_k = jnp.stack([tok_major(cc[0], (KV_HEADS_A, HEAD_DIM)) for cc in caches], axis=1)
    new_attn_v = jnp.stack([tok_major(cc[1], (KV_HEADS_A, HEAD_DIM)) for cc in caches], axis=1)
    new_diff_k = jnp.stack([tok_major(cc[2], (HEADS_B, 2, HEAD_DIM)) for cc in caches], axis=1)
    new_diff_v = jnp.stack([cc[3].reshape(batch, seq, HEADS_B, 2 * HEAD_DIM) for cc in caches], axis=1)
    return (y_prompt, y_sample, new_attn_k, new_attn_v, new_diff_k, new_diff_v)
```

```python
import functools
import math

import numpy as np
import jax
import jax.numpy as jnp
from jax import lax
from jax.experimental import pallas as pl
from jax.experimental.pallas import tpu as pltpu

F32 = jnp.float32
BF16 = jnp.bfloat16

HEAD_DIM = 64
HEADS_A = 8
KV_HEADS_A = 2
HEADS_B = 4
N_QKV = HEADS_A * HEAD_DIM + 2 * KV_HEADS_A * HEAD_DIM + 3 * HEADS_B * 2 * HEAD_DIM
N_EXPERTS = 16
CAPACITY_FACTOR = 2
GRID_W = 64
ROPE_THETA = 10000.0
EPS = 1e-6
LANES = 128
ROW_TILE = 512
CHAIN_ROWS = 256
EXPERT_ROWS = 256
LATENT_Q_TILE = 256
CONTEXT_SETS_PER_STEP = 4
SETS_PER_PACK = LANES // N_EXPERTS
DISPATCH_ROWS = 512
NEG_BIG = -1e30
NOT_SELECTED = -1.0
VMEM_LIMIT = 56 * 1024 * 1024


def _cparams(n_axes):
    return pltpu.CompilerParams(dimension_semantics=("arbitrary",) * n_axes,
                                vmem_limit_bytes=VMEM_LIMIT)


def _dot(a, b):
    return jnp.dot(a, b, preferred_element_type=F32)


def _dot_nt(a, b):
    return lax.dot_general(a, b, (((1,), (1,)), ((), ())), preferred_element_type=F32)


def _split(a):
    hi = a.astype(BF16)
    lo = (a - hi.astype(F32)).astype(BF16)
    return hi, lo


def _dot3(a, b):
    a_hi, a_lo = _split(a)
    b_hi, b_lo = _split(b)
    return _dot(a_hi, b_hi) + _dot(a_lo, b_hi) + _dot(a_hi, b_lo)


def _rms(x, g):
    return x * lax.rsqrt(jnp.mean(x * x, axis=-1, keepdims=True) + EPS) * g


def _mod_index(mod, tiles_per_set):
    if mod.shape[0] == 1:
        return lambda i: (0, 0, 0)
    return lambda i: (i // tiles_per_set, 0, 0)


def _mod_kernel(c_ref, w_ref, b_ref, o_ref):
    c = c_ref[...]
    a = c * jax.nn.sigmoid(c)
    o_ref[0] = _dot3(a, w_ref[...]) + b_ref[0]


def _modulation(c_rows, w_mod, b_mod):
    r, d = c_rows.shape
    return pl.pallas_call(
        _mod_kernel,
        grid=(6,),
        in_specs=[pl.BlockSpec((r, d), lambda j: (0, 0)),
                  pl.BlockSpec((d, d), lambda j: (0, j)),
                  pl.BlockSpec((1, 1, d), lambda j: (j, 0, 0))],
        out_specs=pl.BlockSpec((1, r, d), lambda j: (j, 0, 0)),
        out_shape=jax.ShapeDtypeStruct((6, r, d), F32),
        name="mod",
        compiler_params=_cparams(1),
    )(c_rows, w_mod, b_mod.reshape(6, 1, d))


def _seg_sumsq(x, ones_blockdiag):
    hi, lo = _split(x * x)
    return _dot(hi, ones_blockdiag) + _dot(lo, ones_blockdiag)


def _rope(x, c, s_up, s_dn):
    w = x.shape[1]
    reps = w // c.shape[1]
    if reps > 1:
        c, s_up, s_dn = (jnp.concatenate([t] * reps, axis=1) for t in (c, s_up, s_dn))
    return x * c + pltpu.roll(x, w - 16, 1) * s_up + pltpu.roll(x, 16, 1) * s_dn


def _inproj_kernel(*refs, rope, emit_cache):
    (x_ref, mod_ref, g_ref, w32_ref, gq_ref, gk_ref, bd512_ref, bd128_ref), refs = refs[:8], refs[8:]
    if rope:
        (c_ref, su_ref, sd_ref), refs = refs[:3], refs[3:]
    qa_o, kta_o, vta_o, qb_o, ktb_o, vb_o = refs[:6]
    if emit_cache:
        ka_c, va_c, kb_c, vb_c = refs[6:10]
    w_ref = refs[-1]

    @pl.when(pl.program_id(0) == 0)
    def _():
        w_ref[...] = w32_ref[...].astype(BF16)

    x = x_ref[...]
    mod = mod_ref[0]
    h = (_rms(x, g_ref[...]) * (1.0 + mod[1:2]) + mod[0:1]).astype(BF16)
    if rope:
        tabs = (c_ref[...], su_ref[...], sd_ref[...])
    wa = HEADS_A * HEAD_DIM
    wkv = KV_HEADS_A * HEAD_DIM
    wb = HEADS_B * 2 * HEAD_DIM
    o_ka, o_va, o_qb = wa, wa + wkv, wa + 2 * wkv
    o_kb, o_vb = o_qb + wb, o_qb + 2 * wb
    scale = HEAD_DIM ** -0.5 * math.log2(math.e)

    qa = _dot(h, w_ref[:, 0:wa])
    qa = qa * lax.rsqrt(_seg_sumsq(qa, bd512_ref[...]) * (1.0 / HEAD_DIM) + EPS) * gq_ref[...]
    if rope:
        qa = _rope(qa, *tabs)
    qa_o[...] = (qa * scale).astype(BF16)

    kv = _dot(h, w_ref[:, o_ka:o_qb])
    ka, va = kv[:, 0:wkv], kv[:, wkv:2 * wkv]
    ka = ka * lax.rsqrt(_seg_sumsq(ka, bd128_ref[...]) * (1.0 / HEAD_DIM) + EPS) * gk_ref[...]
    def put_feat(val, out_bf16, out_f32):
        rows = out_bf16.shape[2]
        for s in range(out_bf16.shape[0]):
            t = val[s * rows:(s + 1) * rows].T
            if out_f32 is not None:
                out_f32[s] = t
            if out_bf16 is not None:
                out_bf16[s] = t.astype(BF16)

    if rope:
        put_feat(_rope(ka, *tabs), kta_o, None)
    else:
        put_feat(ka, kta_o, ka_c if emit_cache else None)
    put_feat(va, vta_o, va_c if emit_cache else None)

    qb = _dot(h, w_ref[:, o_qb:o_kb])
    if rope:
        qb = _rope(qb, *tabs)
    qb_o[...] = (qb * scale).astype(BF16)

    kb = _dot(h, w_ref[:, o_kb:o_vb])
    if rope:
        put_feat(_rope(kb, *tabs), ktb_o, None)
    else:
        put_feat(kb, ktb_o, kb_c if emit_cache else None)

    vb = _dot(h, w_ref[:, o_vb:o_vb + wb])
    if emit_cache:
        for hd in range(HEADS_B):
            vb_c[pl.ds(hd, vb.shape[0], stride=HEADS_B), :] = vb[:, hd * 128:(hd + 1) * 128]
    vb_o[...] = vb.astype(BF16)


def _blockdiag_ones(width):
    g = np.arange(width) // HEAD_DIM
    return jnp.asarray((g[:, None] == g[None, :]).astype(np.float32), dtype=BF16)


def _inproj(x2, mod, set_rows, g_attn, w_in, gq_t, gk_t, rope_tabs, emit_cache):
    t, d = x2.shape
    tm = ROW_TILE
    assert tm % set_rows == 0 or set_rows % tm == 0
    tiles_per_set = max(1, set_rows // tm)
    sets_per_tile = max(1, tm // set_rows)
    rope = rope_tabs is not None
    assert not (rope and emit_cache), "cached keys are the position-free ones"
    nq = N_QKV
    row = lambda i: (i, 0)
    const = lambda i: (0, 0)
    in_specs = [pl.BlockSpec((tm, d), row),
                pl.BlockSpec((1, 8, d), _mod_index(mod, tiles_per_set)),
                pl.BlockSpec((1, d), const),
                pl.BlockSpec((d, nq), const, pipeline_mode=pl.Buffered(1)),
                pl.BlockSpec((1, 512), const),
                pl.BlockSpec((1, 128), const),
                pl.BlockSpec((512, 512), const),
                pl.BlockSpec((128, 128), const)]
    args = [x2, mod, g_attn, w_in, gq_t, gk_t, _blockdiag_ones(512), _blockdiag_ones(128)]
    if rope:
        in_specs += [pl.BlockSpec((tm, LANES), lambda i: (i % tiles_per_set, 0))] * 3
        args += list(rope_tabs)
    n_sets = t // set_rows
    wkv, wb = KV_HEADS_A * HEAD_DIM, HEADS_B * 2 * HEAD_DIM
    outs = [("tok", 512, BF16), ("feat", wkv, BF16), ("feat", wkv, BF16),
            ("tok", 512, BF16), ("feat", wb, BF16), ("tok", wb, BF16)]
    if emit_cache:
        outs += [("feat", wkv, F32), ("feat", wkv, F32), ("feat", wb, F32), ("tokhead", wb, F32)]
    feat = lambda i: (i // tiles_per_set, 0, i % tiles_per_set)
    feat_rows = min(tm, set_rows)

    def out_block(kind, w):
        if kind == "tok":
            return pl.BlockSpec((tm, w), row)
        if kind == "tokhead":
            return pl.BlockSpec((tm * HEADS_B, w // HEADS_B), row)
        return pl.BlockSpec((sets_per_tile, w, feat_rows), feat)

    def out_array(kind, w, dt):
        shape = {"tok": (t, w), "tokhead": (t * HEADS_B, w // HEADS_B), "feat": (n_sets, w, set_rows)}[kind]
        return jax.ShapeDtypeStruct(shape, dt)

    return pl.pallas_call(
        functools.partial(_inproj_kernel, rope=rope, emit_cache=emit_cache),
        grid=(t // tm,),
        in_specs=in_specs,
        out_specs=[out_block(kind, w) for kind, w, _ in outs],
        out_shape=[out_array(kind, w, dt) for kind, w, dt in outs],
        scratch_shapes=[pltpu.VMEM((d, nq), BF16)],
        name="inproj_rope" if rope else "inproj",
        compiler_params=_cparams(1),
    )(*args)


def _cache_prep_kernel(*refs):
    half = len(refs) // 2
    for src, out in zip(refs[:half - 1], refs[half:-1]):
        out[...] = src[...].astype(BF16)
    v_src, v_out = refs[half - 1], refs[-1]
    keys = v_out.shape[1]
    for hd in range(HEADS_B):
        v_out[0, :, hd * 128:(hd + 1) * 128] = v_src[0, pl.ds(hd, keys, stride=HEADS_B), :].astype(BF16)


def _cache_prep(arrays, v_by_head):
    blk = lambda b: (b, 0, 0)
    n_sets, rows, width = v_by_head.shape
    v_shape = (n_sets, rows // HEADS_B, width * HEADS_B)
    return pl.pallas_call(
        _cache_prep_kernel,
        grid=(n_sets,),
        in_specs=[pl.BlockSpec((1,) + a.shape[1:], blk) for a in arrays + [v_by_head]],
        out_specs=[pl.BlockSpec((1,) + a.shape[1:], blk) for a in arrays] + [pl.BlockSpec((1,) + v_shape[1:], blk)],
        out_shape=[jax.ShapeDtypeStruct(a.shape, BF16) for a in arrays] + [jax.ShapeDtypeStruct(v_shape, BF16)],
        name="cache_prep",
        compiler_params=_cparams(1),
    )(*arrays, v_by_head)


def _exp_parts(scores):
    m = scores[0].max(axis=-1, keepdims=True)
    for s in scores[1:]:
        m = jnp.maximum(m, s.max(axis=-1, keepdims=True))
    return [jnp.exp2(s - m) for s in scores]


def _row_sum(parts):
    l = parts[0].sum(axis=-1, keepdims=True)
    for e in parts[1:]:
        l = l + e.sum(axis=-1, keepdims=True)
    return l


def _pad_rows(x, first, ones_row=False):
    if ones_row:
        z = jnp.where(lax.broadcasted_iota(jnp.int32, x.shape, 0) == 0, 1.0, 0.0).astype(x.dtype)
    else:
        z = jnp.zeros_like(x)
    return jnp.concatenate([x, z] if first else [z, x], axis=0)


def _gqa_kernel(*refs, n_src):
    q_ref = refs[0]
    k_refs = refs[1:1 + n_src]
    v_refs = refs[1 + n_src:1 + 2 * n_src]
    o_ref = refs[1 + 2 * n_src]
    sets_here = k_refs[0].shape[0]
    tq = q_ref.shape[0] // sets_here
    lane = lax.broadcasted_iota(jnp.int32, (2 * tq, LANES), 1)
    lo_half = lane < HEAD_DIM
    for s in range(sets_here):
        rows = slice(s * tq, (s + 1) * tq)
        for g in range(KV_HEADS_A):
            c0 = g * 256
            f0 = g * HEAD_DIM
            q = jnp.concatenate([q_ref[rows, c0:c0 + 128], q_ref[rows, c0 + 128:c0 + 256]], axis=0)
            kts = [k[s, f0:f0 + HEAD_DIM, :] for k in k_refs]
            vts = [v[s, f0:f0 + HEAD_DIM, :] for v in v_refs]
            e_lo = _exp_parts([_dot(q, _pad_rows(kt, True)) for kt in kts])
            e_hi = _exp_parts([_dot(q, _pad_rows(kt, False)) for kt in kts])
            o_lo = o_hi = None
            for e_part, vt in zip(e_lo, vts):
                t = _dot_nt(e_part.astype(BF16), _pad_rows(vt, True, ones_row=True))
                o_lo = t if o_lo is None else o_lo + t
            for e_part, vt in zip(e_hi, vts):
                t = _dot_nt(e_part.astype(BF16), _pad_rows(vt, False, ones_row=True))
                o_hi = t if o_hi is None else o_hi + t
            o = jnp.where(lo_half, o_lo * (1.0 / o_lo[:, HEAD_DIM:HEAD_DIM + 1]), o_hi * (1.0 / o_hi[:, 0:1]))
            o_ref[rows, c0:c0 + 128] = o[0:tq].astype(BF16)
            o_ref[rows, c0 + 128:c0 + 256] = o[tq:2 * tq].astype(BF16)


def _diff_kernel(*refs, n_src, lam_init):
    q_ref = refs[0]
    k_refs = refs[1:1 + n_src]
    v_refs = refs[1 + n_src:1 + 2 * n_src]
    lq1, lk1, lq2, lk2, gd_ref, o_ref = refs[1 + 2 * n_src:]
    lam_all = (jnp.exp(jnp.sum(lq1[...] * lk1[...], axis=-1, keepdims=True))
               - jnp.exp(jnp.sum(lq2[...] * lk2[...], axis=-1, keepdims=True)) + lam_init)
    sets_here = k_refs[0].shape[0]
    tq = q_ref.shape[0] // sets_here
    for s in range(sets_here):
        rows = slice(s * tq, (s + 1) * tq)
        for j in range(HEADS_B):
            lam = lam_all[j:j + 1, :]
            q = q_ref[rows, j * 128:(j + 1) * 128]
            k0s = [_pad_rows(k[s, j * 128:j * 128 + HEAD_DIM, :], True) for k in k_refs]
            k1s = [_pad_rows(k[s, j * 128 + HEAD_DIM:(j + 1) * 128, :], False) for k in k_refs]
            e0 = _exp_parts([_dot(q, k0) for k0 in k0s])
            e1 = _exp_parts([_dot(q, k1) for k1 in k1s])
            l0 = _row_sum(e0)
            l1 = _row_sum(e1)
            c1 = lam * l0 / l1
            o = None
            for a0, a1, v in zip(e0, e1, v_refs):
                t = _dot((a0 - a1 * c1).astype(BF16), v[s, :, j * 128:(j + 1) * 128])
                o = t if o is None else o + t
            o = _rms(o * (1.0 / l0), gd_ref[...]) * (1.0 - lam_init)
            o_ref[rows, j * 128:(j + 1) * 128] = o.astype(BF16)


def _attention(kind, q, ks, vs, set_rows, tq, sets_per_step, extra, extra_kw):
    t, qw = q.shape
    n_sets = t // set_rows
    q_tiles = set_rows // tq
    assert sets_per_step == 1 or q_tiles == 1
    in_specs = [pl.BlockSpec((sets_per_step * tq, qw), lambda b, i: (b * q_tiles + i, 0))]
    for a in list(ks) + list(vs):
        in_specs.append(pl.BlockSpec((sets_per_step,) + a.shape[1:], lambda b, i: (b, 0, 0)))
    for e in extra:
        in_specs.append(pl.BlockSpec(e.shape, lambda b, i: (0, 0)))
    body = functools.partial(_gqa_kernel if kind == "gqa" else _diff_kernel, n_src=len(ks), **extra_kw)
    return pl.pallas_call(
        body,
        grid=(n_sets // sets_per_step, q_tiles),
        in_specs=in_specs,
        out_specs=pl.BlockSpec((sets_per_step * tq, qw), lambda b, i: (b * q_tiles + i, 0)),
        out_shape=jax.ShapeDtypeStruct((t, qw), BF16),
        name=f"{kind}_attn_{len(ks)}src",
        compiler_params=_cparams(2),
    )(q, *ks, *vs, *extra)


def _post_kernel(x_ref, ya_ref, yb_ref, mod_ref, g1_ref, g2_ref, win32_ref, wba32_ref, wbb32_ref, wo32_ref, wr_ref,
                 x1_o, h2_o, aff_o, wg_ref, wba_ref, wbb_ref, wo_ref, wr2_ref):
    d = x_ref.shape[1]

    @pl.when(pl.program_id(0) == 0)
    def _():
        wg_ref[...] = win32_ref[:, win32_ref.shape[1] - 2 * d:].astype(BF16)
        wba_ref[...] = wba32_ref[...].astype(BF16)
        wbb_ref[...] = wbb32_ref[...].astype(BF16)
        wo_ref[...] = wo32_ref[...].astype(BF16)
        wr_hi, wr_lo = _split(wr_ref[...])
        wr2_ref[...] = jnp.concatenate([wr_hi, wr_lo], axis=1)

    mod = mod_ref[0]
    for r0 in range(0, x_ref.shape[0], CHAIN_ROWS):
        rows = slice(r0, r0 + CHAIN_ROWS)
        x = x_ref[rows, :]
        h = (_rms(x, g1_ref[...]) * (1.0 + mod[1:2]) + mod[0:1]).astype(BF16)
        ga = jax.nn.sigmoid(_dot(h, wg_ref[:, 0:d]))
        merged = ga * _dot(ya_ref[rows, :], wba_ref[...])
        gb = jax.nn.sigmoid(_dot(h, wg_ref[:, d:2 * d]))
        merged = merged + gb * _dot(yb_ref[rows, :], wbb_ref[...])
        m = _dot(merged.astype(BF16), wo_ref[...])
        x1 = x + mod[2:3] * m
        x1_o[rows, :] = x1
        h2 = _rms(x1, g2_ref[...]) * (1.0 + mod[4:5]) + mod[3:4]
        h2_o[rows, :] = h2.astype(BF16)
        h2_hi, h2_lo = _split(h2)
        both = _dot(h2_hi, wr2_ref[...])
        logits = both[:, 0:LANES] + both[:, LANES:2 * LANES] + _dot(h2_lo, wr2_ref[:, 0:LANES])
        lane = lax.broadcasted_iota(jnp.int32, logits.shape, 1)
        logits = jnp.where(lane < N_EXPERTS, logits, NEG_BIG)
        e = jnp.exp(logits - logits.max(axis=-1, keepdims=True))
        aff_o[rows, :] = e / e.sum(axis=-1, keepdims=True)


def _post(x2, ya, yb, mod, set_rows, g1, g2, w_in, w_ba, w_bb, w_out, w_router_p):
    t, d = x2.shape
    tm = ROW_TILE
    tiles_per_set = max(1, set_rows // tm)
    assert mod.shape[0] == 1 or set_rows % tm == 0
    row = lambda i: (i, 0)
    const = lambda i: (0, 0)
    once = pl.Buffered(1)
    half = w_in.shape[1] // 2
    assert w_in.shape[1] == 2 * half and half % LANES == 0 and half >= 2 * d
    return pl.pallas_call(
        _post_kernel,
        grid=(t // tm,),
        in_specs=[pl.BlockSpec((tm, d), row),
                  pl.BlockSpec((tm, 512), row),
                  pl.BlockSpec((tm, 512), row),
                  pl.BlockSpec((1, 8, d), _mod_index(mod, tiles_per_set)),
                  pl.BlockSpec((1, d), const),
                  pl.BlockSpec((1, d), const),
                  pl.BlockSpec((d, half), lambda i: (0, 1), pipeline_mode=once),
                  pl.BlockSpec(w_ba.shape, const, pipeline_mode=once),
                  pl.BlockSpec(w_bb.shape, const, pipeline_mode=once),
                  pl.BlockSpec(w_out.shape, const, pipeline_mode=once),
                  pl.BlockSpec(w_router_p.shape, const)],
        out_specs=[pl.BlockSpec((tm, d), row), pl.BlockSpec((tm, d), row), pl.BlockSpec((tm, LANES), row)],
        out_shape=[jax.ShapeDtypeStruct((t, d), F32), jax.ShapeDtypeStruct((t, d), BF16),
                   jax.ShapeDtypeStruct((t, LANES), F32)],
        scratch_shapes=[pltpu.VMEM((d, 2 * d), BF16), pltpu.VMEM(w_ba.shape, BF16),
                        pltpu.VMEM(w_bb.shape, BF16), pltpu.VMEM(w_out.shape, BF16),
                        pltpu.VMEM((d, 2 * LANES), BF16)],
        name="post_attn",
        compiler_params=_cparams(1),
    )(x2, ya, yb, mod, g1, g2, w_in, w_ba, w_bb, w_out, w_router_p)


def _select_kernel(aff_ref, slot_o, slot_t_o, affb_o, *, cap):
    aff = aff_ref[0]
    n = aff.shape[0]
    capf = float(cap)

    def enough(cand):
        return jnp.sum(jnp.where(aff >= cand, 1.0, 0.0), axis=0, keepdims=True) >= capf

    pw = jnp.ones((1, LANES), F32)
    for k in (64, 32, 16, 8, 4, 2, 1):
        pw = jnp.where(enough(pw * 2.0 ** -(k - 1)), pw, pw * 2.0 ** -k)

    def mantissa_step(_, carry):
        thr, step = carry
        cand = thr + step
        return jnp.where(enough(cand), cand, thr), step * 0.5

    thr, _ = lax.fori_loop(0, 23, mantissa_step, (pw, pw * 0.5))
    above = aff > thr
    tied = aff == thr
    need = capf - jnp.sum(jnp.where(above, 1.0, 0.0), axis=0, keepdims=True)
    r_i = lax.broadcasted_iota(jnp.int32, (n, n), 0)
    c_i = lax.broadcasted_iota(jnp.int32, (n, n), 1)
    before = jnp.where(c_i < r_i, 1.0, 0.0).astype(BF16)
    tie_rank = _dot(before, jnp.where(tied, 1.0, 0.0).astype(BF16))
    sel = above | (tied & (tie_rank < need))
    slot = _dot(before, jnp.where(sel, 1.0, 0.0).astype(BF16))
    slot = jnp.where(sel, slot, NOT_SELECTED)
    slot_o[0] = slot.astype(BF16)
    slot_t_o[0] = slot.T
    affb_o[0] = aff.astype(BF16)


def _dispatch_kernel(slot_ref, affb_ref, slot_t_ref, h2_ref, xe_o, gt_o, *, cap):
    n = h2_ref.shape[0]
    h2 = h2_ref[...]
    slot_t = slot_t_ref[0]

    per = min(N_EXPERTS, DISPATCH_ROWS // cap)
    slot_iota = lax.broadcasted_iota(jnp.int32, (cap, n), 0).astype(F32)
    for e0 in range(0, N_EXPERTS, per):
        onehot = jnp.concatenate(
            [jnp.where(slot_t[e:e + 1, :] == slot_iota, 1.0, 0.0) for e in range(e0, e0 + per)],
            axis=0).astype(BF16)
        rows = _dot(onehot, h2).astype(BF16)
        for j in range(per):
            xe_o[e0 + j] = rows[j * cap:(j + 1) * cap]

    first_lane = (pl.program_id(0) % SETS_PER_PACK) * N_EXPERTS
    shift = cap.bit_length() - 1
    for c0 in range(0, N_EXPERTS * cap, DISPATCH_ROWS):
        src = lax.broadcasted_iota(jnp.int32, (LANES, DISPATCH_ROWS), 0)
        col = lax.broadcasted_iota(jnp.int32, (LANES, DISPATCH_ROWS), 1) + c0
        spread = jnp.where(src == first_lane + lax.shift_right_logical(col, shift), 1.0, 0.0).astype(BF16)
        slot_x = _dot(slot_ref[0], spread)
        aff_x = _dot(affb_ref[0], spread)
        want = (lax.broadcasted_iota(jnp.int32, (n, DISPATCH_ROWS), 1) & (cap - 1)).astype(F32)
        gt_o[:, c0:c0 + DISPATCH_ROWS] = jnp.where(slot_x == want, aff_x, 0.0).astype(BF16)


def _route(aff, h2, n_sets, set_rows):
    t, d = h2.shape
    n = set_rows
    cap = CAPACITY_FACTOR * n // N_EXPERTS
    assert cap & (cap - 1) == 0 and cap % 16 == 0 and DISPATCH_ROWS % cap == 0
    n_packs = -(-n_sets // SETS_PER_PACK)
    aff16 = aff[:, :N_EXPERTS].reshape(n_sets, n, N_EXPERTS)
    aff16 = jnp.pad(aff16, ((0, n_packs * SETS_PER_PACK - n_sets), (0, 0), (0, 0)))
    packed = aff16.reshape(n_packs, SETS_PER_PACK, n, N_EXPERTS).transpose(0, 2, 1, 3).reshape(n_packs, n, LANES)
    pack_blk = lambda p: (p, 0, 0)
    slot, slot_t, affb = pl.pallas_call(
        functools.partial(_select_kernel, cap=cap),
        grid=(n_packs,),
        in_specs=[pl.BlockSpec((1, n, LANES), pack_blk)],
        out_specs=[pl.BlockSpec((1, n, LANES), pack_blk), pl.BlockSpec((1, LANES, n), pack_blk),
                   pl.BlockSpec((1, n, LANES), pack_blk)],
        out_shape=[jax.ShapeDtypeStruct((n_packs, n, LANES), BF16),
                   jax.ShapeDtypeStruct((n_packs, LANES, n), F32),
                   jax.ShapeDtypeStruct((n_packs, n, LANES), BF16)],
        name=f"select_cap{cap}",
        compiler_params=_cparams(1),
    )(packed)
    of_set = lambda s: (s // SETS_PER_PACK, 0, 0)
    return pl.pallas_call(
        functools.partial(_dispatch_kernel, cap=cap),
        grid=(n_sets,),
        in_specs=[pl.BlockSpec((1, n, LANES), of_set),
                  pl.BlockSpec((1, n, LANES), of_set),
                  pl.BlockSpec((1, N_EXPERTS, n), lambda s: (s // SETS_PER_PACK, s % SETS_PER_PACK, 0)),
                  pl.BlockSpec((n, d), lambda s: (s, 0))],
        out_specs=[pl.BlockSpec((N_EXPERTS, cap, d), lambda s: (0, s, 0)),
                   pl.BlockSpec((n, N_EXPERTS * cap), lambda s: (s, 0))],
        out_shape=[jax.ShapeDtypeStruct((N_EXPERTS, n_sets * cap, d), BF16),
                   jax.ShapeDtypeStruct((t, N_EXPERTS * cap), BF16)],
        name=f"dispatch_cap{cap}",
        compiler_params=_cparams(1),
    )(slot, affb, slot_t, h2)


def _expert_kernel(*refs, n_groups):
    x_refs = refs[:n_groups]
    wg_ref, wu_ref, wd_ref = refs[n_groups:n_groups + 3]
    o_refs = refs[n_groups + 3:2 * n_groups + 3]
    wg_s, wu_s, wd_s = refs[2 * n_groups + 3:]
    wg_s[...] = wg_ref[0].astype(BF16)
    wu_s[...] = wu_ref[0].astype(BF16)
    wd_s[...] = wd_ref[0].astype(BF16)
    sub = EXPERT_ROWS
    for x_ref, o_ref in zip(x_refs, o_refs):
        for r0 in range(0, x_ref.shape[1], sub):
            x = x_ref[0, r0:r0 + sub, :]
            a = _dot(x, wg_s[...])
            u = _dot(x, wu_s[...])
            hmid = (a * jax.nn.sigmoid(a) * u).astype(BF16)
            o_ref[0, r0:r0 + sub, :] = _dot(hmid, wd_s[...]).astype(BF16)


def _experts(xes, w_gate, w_up, w_down):
    e, d, f = w_gate.shape
    blk = lambda i: (i, 0, 0)
    return pl.pallas_call(
        functools.partial(_expert_kernel, n_groups=len(xes)),
        grid=(e,),
        in_specs=[pl.BlockSpec((1, x.shape[1], d), blk) for x in xes]
        + [pl.BlockSpec((1, d, f), blk), pl.BlockSpec((1, d, f), blk), pl.BlockSpec((1, f, d), blk)],
        out_specs=[pl.BlockSpec((1, x.shape[1], d), blk) for x in xes],
        out_shape=[jax.ShapeDtypeStruct(x.shape, BF16) for x in xes],
        scratch_shapes=[pltpu.VMEM((d, f), BF16), pltpu.VMEM((d, f), BF16), pltpu.VMEM((f, d), BF16)],
        name="experts",
        compiler_params=_cparams(1),
    )(*xes, w_gate, w_up, w_down)


def _combine_kernel(gt_ref, ye_ref, x1_ref, mod_ref, gf_ref, y_o):
    e, cap, d = ye_ref.shape
    ye = ye_ref[...].reshape(e * cap, d)
    moe = _dot(gt_ref[...], ye)
    x = x1_ref[...] + mod_ref[0][5:6] * moe
    y_o[...] = _rms(x, gf_ref[...])


def _combine(gt, ye, x1, mod, n_sets, set_rows, g_final):
    t, d = x1.shape
    tm = min(ROW_TILE, set_rows)
    tiles = set_rows // tm
    cap = ye.shape[1] // n_sets
    return pl.pallas_call(
        _combine_kernel,
        grid=(n_sets, tiles),
        in_specs=[pl.BlockSpec((tm, gt.shape[1]), lambda s, i: (s * tiles + i, 0)),
                  pl.BlockSpec((N_EXPERTS, cap, d), lambda s, i: (0, s, 0)),
                  pl.BlockSpec((tm, d), lambda s, i: (s * tiles + i, 0)),
                  pl.BlockSpec((1, 8, d), (lambda s, i: (s, 0, 0)) if mod.shape[0] > 1 else (lambda s, i: (0, 0, 0))),
                  pl.BlockSpec((1, d), lambda s, i: (0, 0))],
        out_specs=pl.BlockSpec((tm, d), lambda s, i: (s * tiles + i, 0)),
        out_shape=jax.ShapeDtypeStruct((t, d), F32),
        name=f"combine_cap{cap}",
        compiler_params=_cparams(2),
    )(gt, ye, x1, mod, g_final)


def _rope_tables(n_tokens):
    n_rows = n_tokens // GRID_W
    rowp = jnp.repeat(jnp.arange(n_rows), GRID_W).astype(F32)
    colp = jnp.tile(jnp.arange(GRID_W), n_rows).astype(F32)
    quarter = HEAD_DIM // 4
    freqs = ROPE_THETA ** (-jnp.arange(quarter, dtype=F32) / quarter)
    ang = jnp.stack([rowp[:, None] * freqs, colp[:, None] * freqs], axis=1)
    cos, sin = jnp.cos(ang), jnp.sin(ang)
    zero = jnp.zeros_like(sin)
    c = jnp.stack([cos, cos], axis=2).reshape(n_tokens, HEAD_DIM)
    s_up = jnp.stack([-sin, zero], axis=2).reshape(n_tokens, HEAD_DIM)
    s_dn = jnp.stack([zero, sin], axis=2).reshape(n_tokens, HEAD_DIM)
    return tuple(jnp.tile(t, (1, LANES // HEAD_DIM)) for t in (c, s_up, s_dn))


def _mix_and_route(x2, n_sets, set_rows, mod, lw, rope_tabs, cache, emit_cache, lam_init, tq):
    (g_attn, g_ffn, w_in, gq_t, gk_t, lq1, lk1, lq2, lk2, g_diff, w_ba, w_bb, w_out, w_router_p) = lw
    outs = _inproj(x2, mod, set_rows, g_attn, w_in, gq_t, gk_t, rope_tabs, emit_cache)
    qa, kta, vta, qb, ktb, vb16 = outs[:6]
    vb16 = vb16.reshape(n_sets, set_rows, vb16.shape[1])
    ks_a, vs_a, ks_b, vs_b = [kta], [vta], [ktb], [vb16]
    if cache is not None:
        ckta, cvta, cktb, cvb = cache
        ks_a, vs_a, ks_b, vs_b = [ckta, kta], [cvta, vta], [cktb, ktb], [cvb, vb16]
    per_step = CONTEXT_SETS_PER_STEP if (tq == set_rows and n_sets % CONTEXT_SETS_PER_STEP == 0) else 1
    ya = _attention("gqa", qa, ks_a, vs_a, set_rows, tq, per_step, (), {})
    yb = _attention("diff", qb, ks_b, vs_b, set_rows, tq, per_step, (lq1, lk1, lq2, lk2, g_diff),
                    {"lam_init": lam_init})
    x1, h2, aff = _post(x2, ya, yb, mod, set_rows, g_attn, g_ffn, w_in, w_ba, w_bb, w_out, w_router_p)
    xe, gt = _route(aff, h2, n_sets, set_rows)
    return x1, xe, gt, outs[6:]


def kernel(x_prompt, x_sample, cache_attn_k, cache_attn_v, cache_diff_k, cache_diff_v, c, c_ctx, w_mod, b_mod,
           g_attn_norm, g_ffn_norm, w_in, g_q_norm, g_k_norm, lambda_q1, lambda_k1, lambda_q2, lambda_k2,
           g_diff_norm, w_branch_a, w_branch_b, w_out, w_router, w_exp_gate, w_exp_up, w_exp_down, g_final):
    batch, seq, d = x_prompt.shape
    dec_batch, dec_seq, _ = x_sample.shape
    depth = w_in.shape[0]
    assert depth == 1, "the final norm is fused into the layer's combine step"
    past = cache_attn_k.shape[2]
    assert w_in.shape[2] == N_QKV + 2 * d

    xp = x_prompt.reshape(batch * seq, d)
    xs = x_sample.reshape(dec_batch * dec_seq, d)
    rope_tabs = _rope_tables(dec_seq)
    c_rows = jnp.concatenate([c, c_ctx[None, :], jnp.zeros((16 - dec_batch - 1, d), F32)], axis=0)
    yp = ys = None
    caches = []
    for l in range(depth):
        lam_init = 0.8 - 0.6 * math.exp(-0.3 * l)
        mod6 = _modulation(c_rows, w_mod[l], b_mod[l])
        mod = jnp.pad(jnp.transpose(mod6, (1, 0, 2)), ((0, 0), (0, 2), (0, 0)))
        mod_lat, mod_ctx = mod[:dec_batch], mod[dec_batch:dec_batch + 1]
        lw = (g_attn_norm[l][None, :], g_ffn_norm[l][None, :], w_in[l],
              jnp.tile(g_q_norm[l], HEADS_A)[None, :], jnp.tile(g_k_norm[l], KV_HEADS_A)[None, :],
              lambda_q1[l], lambda_k1[l], lambda_q2[l], lambda_k2[l], g_diff_norm[l][None, :],
              w_branch_a[l], w_branch_b[l], w_out[l],
              jnp.pad(w_router[l], ((0, 0), (0, LANES - N_EXPERTS))))
        x1p, xe_p, gt_p, cache_out = _mix_and_route(xp, batch, seq, mod_ctx, lw, None, None, True, lam_init, seq)
        caches.append(cache_out)
        feat_major = lambda a: jnp.moveaxis(a[:, l], 1, -1).reshape(dec_batch, -1, past)
        cache_l = _cache_prep([feat_major(cache_attn_k), feat_major(cache_attn_v), feat_major(cache_diff_k)],
                              cache_diff_v[:, l].reshape(dec_batch, past * HEADS_B, -1))
        x1s, xe_s, gt_s, _ = _mix_and_route(xs, dec_batch, dec_seq, mod_lat, lw, rope_tabs, cache_l, False,
                                            lam_init, LATENT_Q_TILE)
        ye_p, ye_s = _experts([xe_p, xe_s], w_exp_gate[l], w_exp_up[l], w_exp_down[l])
        yp = _combine(gt_p, ye_p, x1p, mod_ctx, batch, seq, g_final[None, :])
        ys = _combine(gt_s, ye_s, x1s, mod_lat, dec_batch, dec_seq, g_final[None, :])
    y_prompt = yp.reshape(batch, seq, d)
    y_sample = ys.reshape(dec_batch, dec_seq, d)
    tok_major = lambda a, dims: jnp.moveaxis(a.reshape((batch,) + dims + (seq,)), -1, 1)
    new_attn_k = jnp.stack([tok_major(cc[0], (KV_HEADS_A, HEAD_DIM)) for cc in caches], axis=1)
    new_attn_v = jnp.stack([tok_major(cc[1], (KV_HEADS_A, HEAD_DIM)) for cc in caches], axis=1)
    new_diff_k = jnp.stack([tok_major(cc[2], (HEADS_B, 2, HEAD_DIM)) for cc in caches], axis=1)
    new_diff_v = jnp.stack([cc[3].reshape(batch, seq, HEADS_B, 2 * HEAD_DIM) for cc in caches], axis=1)
    return (y_prompt, y_sample, new_attn_k, new_attn_v, new_diff_k, new_diff_v)
```

```python
import functools
import math

import numpy as np
import jax
import jax.numpy as jnp
from jax import lax
from jax.experimental import pallas as pl
from jax.experimental.pallas import tpu as pltpu

F32 = jnp.float32
BF16 = jnp.bfloat16

HEAD_DIM = 64
HEADS_A = 8
KV_HEADS_A = 2
HEADS_B = 4
N_QKV = HEADS_A * HEAD_DIM + 2 * KV_HEADS_A * HEAD_DIM + 3 * HEADS_B * 2 * HEAD_DIM
N_EXPERTS = 16
CAPACITY_FACTOR = 2
GRID_W = 64
ROPE_THETA = 10000.0
EPS = 1e-6
LANES = 128
ROW_TILE = 512
CHAIN_ROWS = 256
EXPERT_ROWS = 256
LATENT_Q_TILE = 256
CONTEXT_SETS_PER_STEP = 4
SETS_PER_PACK = LANES // N_EXPERTS
DISPATCH_ROWS = 512
NEG_BIG = -1e30
NOT_SELECTED = -1.0
VMEM_LIMIT = 56 * 1024 * 1024


def _cparams(n_axes):
    return pltpu.CompilerParams(dimension_semantics=("arbitrary",) * n_axes,
                                vmem_limit_bytes=VMEM_LIMIT)


def _dot(a, b):
    return jnp.dot(a, b, preferred_element_type=F32)


def _dot_nt(a, b):
    return lax.dot_general(a, b, (((1,), (1,)), ((), ())), preferred_element_type=F32)


def _split(a):
    hi = a.astype(BF16)
    lo = (a - hi.astype(F32)).astype(BF16)
    return hi, lo


def _dot3(a, b):
    a_hi, a_lo = _split(a)
    b_hi, b_lo = _split(b)
    return _dot(a_hi, b_hi) + _dot(a_lo, b_hi) + _dot(a_hi, b_lo)


def _rms(x, g):
    return x * lax.rsqrt(jnp.mean(x * x, axis=-1, keepdims=True) + EPS) * g


def _mod_index(mod, tiles_per_set):
    if mod.shape[0] == 1:
        return lambda i: (0, 0, 0)
    return lambda i: (i // tiles_per_set, 0, 0)


def _mod_kernel(c_ref, w_ref, b_ref, o_ref):
    c = c_ref[...]
    a = c * jax.nn.sigmoid(c)
    o_ref[0] = _dot3(a, w_ref[...]) + b_ref[0]


def _modulation(c_rows, w_mod, b_mod):
    r, d = c_rows.shape
    return pl.pallas_call(
        _mod_kernel,
        grid=(6,),
        in_specs=[pl.BlockSpec((r, d), lambda j: (0, 0)),
                  pl.BlockSpec((d, d), lambda j: (0, j)),
                  pl.BlockSpec((1, 1, d), lambda j: (j, 0, 0))],
        out_specs=pl.BlockSpec((1, r, d), lambda j: (j, 0, 0)),
        out_shape=jax.ShapeDtypeStruct((6, r, d), F32),
        name="mod",
        compiler_params=_cparams(1),
    )(c_rows, w_mod, b_mod.reshape(6, 1, d))


def _seg_sumsq(x, ones_blockdiag):
    hi, lo = _split(x * x)
    return _dot(hi, ones_blockdiag) + _dot(lo, ones_blockdiag)


def _rope(x, c, s_up, s_dn):
    w = x.shape[1]
    reps = w // c.shape[1]
    if reps > 1:
        c, s_up, s_dn = (jnp.concatenate([t] * reps, axis=1) for t in (c, s_up, s_dn))
    return x * c + pltpu.roll(x, w - 16, 1) * s_up + pltpu.roll(x, 16, 1) * s_dn


def _inproj_kernel(*refs, rope, emit_cache):
    (x_ref, mod_ref, g_ref, w32_ref, gq_ref, gk_ref, bd512_ref, bd128_ref), refs = refs[:8], refs[8:]
    if rope:
        (c_ref, su_ref, sd_ref), refs = refs[:3], refs[3:]
    qa_o, kta_o, vta_o, qb_o, ktb_o, vb_o = refs[:6]
    if emit_cache:
        ka_c, va_c, kb_c, vb_c = refs[6:10]
    w_ref = refs[-1]

    @pl.when(pl.program_id(0) == 0)
    def _():
        w_ref[...] = w32_ref[...].astype(BF16)

    mod = mod_ref[0]
    wa = HEADS_A * HEAD_DIM
    wkv = KV_HEADS_A * HEAD_DIM
    wb = HEADS_B * 2 * HEAD_DIM
    o_ka, o_va, o_qb = wa, wa + wkv, wa + 2 * wkv
    o_kb, o_vb = o_qb + wb, o_qb + 2 * wb
    scale = HEAD_DIM ** -0.5 * math.log2(math.e)

    x = x_ref[...]
    h = (_rms(x, g_ref[...]) * (1.0 + mod[1:2]) + mod[0:1]).astype(BF16)
    if rope:
        tabs = (c_ref[...], su_ref[...], sd_ref[...])

    qa = _dot(h, w_ref[:, 0:wa])
    qa = qa * lax.rsqrt(_seg_sumsq(qa, bd512_ref[...]) * (1.0 / HEAD_DIM) + EPS) * gq_ref[...]
    if rope:
        qa = _rope(qa, *tabs)
    qa_o[...] = (qa * scale).astype(BF16)

    kv = _dot(h, w_ref[:, o_ka:o_qb])
    ka, va = kv[:, 0:wkv], kv[:, wkv:2 * wkv]
    ka = ka * lax.rsqrt(_seg_sumsq(ka, bd128_ref[...]) * (1.0 / HEAD_DIM) + EPS) * gk_ref[...]

    def put_feat(val, out_bf16, out_f32):
        rows = out_bf16.shape[2]
        for s in range(out_bf16.shape[0]):
            t = val[s * rows:(s + 1) * rows].T
            if out_f32 is not None:
                out_f32[s] = t
            out_bf16[s] = t.astype(BF16)

    if rope:
        put_feat(_rope(ka, *tabs), kta_o, None)
    else:
        put_feat(ka, kta_o, ka_c if emit_cache else None)
    put_feat(va, vta_o, va_c if emit_cache else None)

    qb = _dot(h, w_ref[:, o_qb:o_kb])
    if rope:
        qb = _rope(qb, *tabs)
    qb_o[...] = (qb * scale).astype(BF16)

    kb = _dot(h, w_ref[:, o_kb:o_vb])
    if rope:
        put_feat(_rope(kb, *tabs), ktb_o, None)
    else:
        put_feat(kb, ktb_o, kb_c if emit_cache else None)

    vb = _dot(h, w_ref[:, o_vb:o_vb + wb])
    if emit_cache:
        for hd in range(HEADS_B):
            vb_c[pl.ds(hd, vb.shape[0], stride=HEADS_B), :] = vb[:, hd * 128:(hd + 1) * 128]
    vb_o[...] = vb.astype(BF16)


def _blockdiag_ones(width):
    g = np.arange(width) // HEAD_DIM
    return jnp.asarray((g[:, None] == g[None, :]).astype(np.float32), dtype=BF16)


def _inproj(x2, mod, set_rows, g_attn, w_in, gq_t, gk_t, rope_tabs, emit_cache):
    t, d = x2.shape
    tm = ROW_TILE
    assert tm % set_rows == 0 or set_rows % tm == 0
    tiles_per_set = max(1, set_rows // tm)
    sets_per_tile = max(1, tm // set_rows)
    rope = rope_tabs is not None
    assert not (rope and emit_cache), "cached keys are the position-free ones"
    nq = N_QKV
    row = lambda i: (i, 0)
    const = lambda i: (0, 0)
    in_specs = [pl.BlockSpec((tm, d), row),
                pl.BlockSpec((1, 8, d), _mod_index(mod, tiles_per_set)),
                pl.BlockSpec((1, d), const),
                pl.BlockSpec((d, nq), const, pipeline_mode=pl.Buffered(1)),
                pl.BlockSpec((1, 512), const),
                pl.BlockSpec((1, 128), const),
                pl.BlockSpec((512, 512), const),
                pl.BlockSpec((128, 128), const)]
    args = [x2, mod, g_attn, w_in, gq_t, gk_t, _blockdiag_ones(512), _blockdiag_ones(128)]
    if rope:
        in_specs += [pl.BlockSpec((tm, LANES), lambda i: (i % tiles_per_set, 0))] * 3
        args += list(rope_tabs)
    n_sets = t // set_rows
    wkv, wb = KV_HEADS_A * HEAD_DIM, HEADS_B * 2 * HEAD_DIM
    outs = [("tok", 512, BF16), ("feat", wkv, BF16), ("feat", wkv, BF16),
            ("tok", 512, BF16), ("feat", wb, BF16), ("tok", wb, BF16)]
    if emit_cache:
        outs += [("feat", wkv, F32), ("feat", wkv, F32), ("feat", wb, F32), ("tokhead", wb, F32)]
    feat = lambda i: (i // tiles_per_set, 0, i % tiles_per_set)
    feat_rows = min(tm, set_rows)

    def out_block(kind, w):
        if kind == "tok":
            return pl.BlockSpec((tm, w), row)
        if kind == "tokhead":
            return pl.BlockSpec((tm * HEADS_B, w // HEADS_B), row)
        return pl.BlockSpec((sets_per_tile, w, feat_rows), feat)

    def out_array(kind, w, dt):
        shape = {"tok": (t, w), "tokhead": (t * HEADS_B, w // HEADS_B), "feat": (n_sets, w, set_rows)}[kind]
        return jax.ShapeDtypeStruct(shape, dt)

    return pl.pallas_call(
        functools.partial(_inproj_kernel, rope=rope, emit_cache=emit_cache),
        grid=(t // tm,),
        in_specs=in_specs,
        out_specs=[out_block(kind, w) for kind, w, _ in outs],
        out_shape=[out_array(kind, w, dt) for kind, w, dt in outs],
        scratch_shapes=[pltpu.VMEM((d, nq), BF16)],
        name="inproj_rope" if rope else "inproj",
        compiler_params=_cparams(1),
    )(*args)


def _cache_prep_kernel(*refs):
    half = len(refs) // 2
    for src, out in zip(refs[:half - 1], refs[half:-1]):
        out[...] = src[...].astype(BF16)
    v_src, v_out = refs[half - 1], refs[-1]
    keys = v_out.shape[1]
    for hd in range(HEADS_B):
        v_out[0, :, hd * 128:(hd + 1) * 128] = v_src[0, pl.ds(hd, keys, stride=HEADS_B), :].astype(BF16)


def _cache_prep(arrays, v_by_head):
    blk = lambda b: (b, 0, 0)
    n_sets, rows, width = v_by_head.shape
    v_shape = (n_sets, rows // HEADS_B, width * HEADS_B)
    return pl.pallas_call(
        _cache_prep_kernel,
        grid=(n_sets,),
        in_specs=[pl.BlockSpec((1,) + a.shape[1:], blk) for a in arrays + [v_by_head]],
        out_specs=[pl.BlockSpec((1,) + a.shape[1:], blk) for a in arrays] + [pl.BlockSpec((1,) + v_shape[1:], blk)],
        out_shape=[jax.ShapeDtypeStruct(a.shape, BF16) for a in arrays] + [jax.ShapeDtypeStruct(v_shape, BF16)],
        name="cache_prep",
        compiler_params=_cparams(1),
    )(*arrays, v_by_head)


def _exp_parts(scores):
    m = scores[0].max(axis=-1, keepdims=True)
    for s in scores[1:]:
        m = jnp.maximum(m, s.max(axis=-1, keepdims=True))
    return [jnp.exp2(s - m) for s in scores]


def _row_sum(parts):
    l = parts[0].sum(axis=-1, keepdims=True)
    for e in parts[1:]:
        l = l + e.sum(axis=-1, keepdims=True)
    return l


def _pad_rows(x, first, ones_row=False):
    if ones_row:
        z = jnp.where(lax.broadcasted_iota(jnp.int32, x.shape, 0) == 0, 1.0, 0.0).astype(x.dtype)
    else:
        z = jnp.zeros_like(x)
    return jnp.concatenate([x, z] if first else [z, x], axis=0)


def _gqa_kernel(*refs, n_src, mxu_sums):
    q_ref = refs[0]
    k_refs = refs[1:1 + n_src]
    v_refs = refs[1 + n_src:1 + 2 * n_src]
    o_ref = refs[1 + 2 * n_src]
    sets_here = k_refs[0].shape[0]
    tq = q_ref.shape[0] // sets_here
    lane = lax.broadcasted_iota(jnp.int32, (2 * tq, LANES), 1)
    lo_half = lane < HEAD_DIM
    for s in range(sets_here):
        rows = slice(s * tq, (s + 1) * tq)
        for g in range(KV_HEADS_A):
            c0 = g * 256
            f0 = g * HEAD_DIM
            q = jnp.concatenate([q_ref[rows, c0:c0 + 128], q_ref[rows, c0 + 128:c0 + 256]], axis=0)
            kts = [k[s, f0:f0 + HEAD_DIM, :] for k in k_refs]
            vts = [v[s, f0:f0 + HEAD_DIM, :] for v in v_refs]
            e_lo = _exp_parts([_dot(q, _pad_rows(kt, True)) for kt in kts])
            e_hi = _exp_parts([_dot(q, _pad_rows(kt, False)) for kt in kts])
            o_lo = o_hi = None
            for e_part, vt in zip(e_lo, vts):
                t = _dot_nt(e_part.astype(BF16), _pad_rows(vt, True, ones_row=mxu_sums))
                o_lo = t if o_lo is None else o_lo + t
            for e_part, vt in zip(e_hi, vts):
                t = _dot_nt(e_part.astype(BF16), _pad_rows(vt, False, ones_row=mxu_sums))
                o_hi = t if o_hi is None else o_hi + t
            if mxu_sums:
                l_lo, l_hi = o_lo[:, HEAD_DIM:HEAD_DIM + 1], o_hi[:, 0:1]
                o = jnp.where(lo_half, o_lo * (1.0 / l_lo), o_hi * (1.0 / l_hi))
            else:
                o = (o_lo + o_hi) * jnp.where(lo_half, 1.0 / _row_sum(e_lo), 1.0 / _row_sum(e_hi))
            o_ref[rows, c0:c0 + 128] = o[0:tq].astype(BF16)
            o_ref[rows, c0 + 128:c0 + 256] = o[tq:2 * tq].astype(BF16)


def _diff_kernel(*refs, n_src, lam_init, stack_maps):
    q_ref = refs[0]
    k_refs = refs[1:1 + n_src]
    v_refs = refs[1 + n_src:1 + 2 * n_src]
    lq1, lk1, lq2, lk2, gd_ref, o_ref = refs[1 + 2 * n_src:]
    lam_all = (jnp.exp(jnp.sum(lq1[...] * lk1[...], axis=-1, keepdims=True))
               - jnp.exp(jnp.sum(lq2[...] * lk2[...], axis=-1, keepdims=True)) + lam_init)
    sets_here = k_refs[0].shape[0]
    tq = q_ref.shape[0] // sets_here
    for s in range(sets_here):
        rows = slice(s * tq, (s + 1) * tq)
        for j in range(HEADS_B):
            lam = lam_all[j:j + 1, :]
            q = q_ref[rows, j * 128:(j + 1) * 128]
            k0s = [_pad_rows(k[s, j * 128:j * 128 + HEAD_DIM, :], True) for k in k_refs]
            k1s = [_pad_rows(k[s, j * 128 + HEAD_DIM:(j + 1) * 128, :], False) for k in k_refs]
            e0 = _exp_parts([_dot(q, k0) for k0 in k0s])
            e1 = _exp_parts([_dot(q, k1) for k1 in k1s])
            o0 = o1 = None
            for a0, a1, v in zip(e0, e1, v_refs):
                val = v[s, :, j * 128:(j + 1) * 128]
                if stack_maps:
                    t = _dot(jnp.concatenate([a0.astype(BF16), a1.astype(BF16)], axis=0), val)
                    t0, t1 = t[0:tq], t[tq:2 * tq]
                else:
                    t0, t1 = _dot(a0.astype(BF16), val), _dot(a1.astype(BF16), val)
                o0 = t0 if o0 is None else o0 + t0
                o1 = t1 if o1 is None else o1 + t1
            o = o0 * (1.0 / _row_sum(e0)) - o1 * (lam / _row_sum(e1))
            o = _rms(o, gd_ref[...]) * (1.0 - lam_init)
            o_ref[rows, j * 128:(j + 1) * 128] = o.astype(BF16)


def _attention(kind, q, ks, vs, set_rows, tq, sets_per_step, extra, extra_kw):
    t, qw = q.shape
    n_sets = t // set_rows
    q_tiles = set_rows // tq
    assert sets_per_step == 1 or q_tiles == 1
    in_specs = [pl.BlockSpec((sets_per_step * tq, qw), lambda b, i: (b * q_tiles + i, 0))]
    for a in list(ks) + list(vs):
        in_specs.append(pl.BlockSpec((sets_per_step,) + a.shape[1:], lambda b, i: (b, 0, 0)))
    for e in extra:
        in_specs.append(pl.BlockSpec(e.shape, lambda b, i: (0, 0)))
    body = functools.partial(_gqa_kernel if kind == "gqa" else _diff_kernel, n_src=len(ks), **extra_kw)
    return pl.pallas_call(
        body,
        grid=(n_sets // sets_per_step, q_tiles),
        in_specs=in_specs,
        out_specs=pl.BlockSpec((sets_per_step * tq, qw), lambda b, i: (b * q_tiles + i, 0)),
        out_shape=jax.ShapeDtypeStruct((t, qw), BF16),
        name=f"{kind}_attn_{len(ks)}src",
        compiler_params=_cparams(2),
    )(q, *ks, *vs, *extra)


def _post_kernel(x_ref, ya_ref, yb_ref, mod_ref, g1_ref, g2_ref, win32_ref, wba32_ref, wbb32_ref, wo32_ref, wr_ref,
                 x1_o, h2_o, aff_o, wg_ref, wba_ref, wbb_ref, wo_ref, wr2_ref):
    d = x_ref.shape[1]

    @pl.when(pl.program_id(0) == 0)
    def _():
        wg_ref[...] = win32_ref[:, win32_ref.shape[1] - 2 * d:].astype(BF16)
        wba_ref[...] = wba32_ref[...].astype(BF16)
        wbb_ref[...] = wbb32_ref[...].astype(BF16)
        wo_ref[...] = wo32_ref[...].astype(BF16)
        wr_hi, wr_lo = _split(wr_ref[...])
        wr2_ref[...] = jnp.concatenate([wr_hi, wr_lo], axis=1)

    mod = mod_ref[0]
    for r0 in range(0, x_ref.shape[0], CHAIN_ROWS):
        rows = slice(r0, r0 + CHAIN_ROWS)
        x = x_ref[rows, :]
        h = (_rms(x, g1_ref[...]) * (1.0 + mod[1:2]) + mod[0:1]).astype(BF16)
        ga = jax.nn.sigmoid(_dot(h, wg_ref[:, 0:d]))
        merged = ga * _dot(ya_ref[rows, :], wba_ref[...])
        gb = jax.nn.sigmoid(_dot(h, wg_ref[:, d:2 * d]))
        merged = merged + gb * _dot(yb_ref[rows, :], wbb_ref[...])
        m = _dot(merged.astype(BF16), wo_ref[...])
        x1 = x + mod[2:3] * m
        x1_o[rows, :] = x1
        h2 = _rms(x1, g2_ref[...]) * (1.0 + mod[4:5]) + mod[3:4]
        h2_o[rows, :] = h2.astype(BF16)
        h2_hi, h2_lo = _split(h2)
        both = _dot(h2_hi, wr2_ref[...])
        logits = both[:, 0:LANES] + both[:, LANES:2 * LANES] + _dot(h2_lo, wr2_ref[:, 0:LANES])
        lane = lax.broadcasted_iota(jnp.int32, logits.shape, 1)
        logits = jnp.where(lane < N_EXPERTS, logits, NEG_BIG)
        e = jnp.exp(logits - logits.max(axis=-1, keepdims=True))
        aff_o[rows, :] = e / e.sum(axis=-1, keepdims=True)


def _post(x2, ya, yb, mod, set_rows, g1, g2, w_in, w_ba, w_bb, w_out, w_router_p):
    t, d = x2.shape
    tm = ROW_TILE
    tiles_per_set = max(1, set_rows // tm)
    assert mod.shape[0] == 1 or set_rows % tm == 0
    row = lambda i: (i, 0)
    const = lambda i: (0, 0)
    once = pl.Buffered(1)
    half = w_in.shape[1] // 2
    assert w_in.shape[1] == 2 * half and half % LANES == 0 and half >= 2 * d
    return pl.pallas_call(
        _post_kernel,
        grid=(t // tm,),
        in_specs=[pl.BlockSpec((tm, d), row),
                  pl.BlockSpec((tm, 512), row),
                  pl.BlockSpec((tm, 512), row),
                  pl.BlockSpec((1, 8, d), _mod_index(mod, tiles_per_set)),
                  pl.BlockSpec((1, d), const),
                  pl.BlockSpec((1, d), const),
                  pl.BlockSpec((d, half), lambda i: (0, 1), pipeline_mode=once),
                  pl.BlockSpec(w_ba.shape, const, pipeline_mode=once),
                  pl.BlockSpec(w_bb.shape, const, pipeline_mode=once),
                  pl.BlockSpec(w_out.shape, const, pipeline_mode=once),
                  pl.BlockSpec(w_router_p.shape, const)],
        out_specs=[pl.BlockSpec((tm, d), row), pl.BlockSpec((tm, d), row), pl.BlockSpec((tm, LANES), row)],
        out_shape=[jax.ShapeDtypeStruct((t, d), F32), jax.ShapeDtypeStruct((t, d), BF16),
                   jax.ShapeDtypeStruct((t, LANES), F32)],
        scratch_shapes=[pltpu.VMEM((d, 2 * d), BF16), pltpu.VMEM(w_ba.shape, BF16),
                        pltpu.VMEM(w_bb.shape, BF16), pltpu.VMEM(w_out.shape, BF16),
                        pltpu.VMEM((d, 2 * LANES), BF16)],
        name="post_attn",
        compiler_params=_cparams(1),
    )(x2, ya, yb, mod, g1, g2, w_in, w_ba, w_bb, w_out, w_router_p)


def _select_kernel(aff_ref, slot_o, slot_t_o, affb_o, *, cap):
    aff = aff_ref[0]
    n = aff.shape[0]
    capf = float(cap)

    def enough(cand):
        return jnp.sum(jnp.where(aff >= cand, 1.0, 0.0), axis=0, keepdims=True) >= capf

    pw = jnp.ones((1, LANES), F32)
    for k in (64, 32, 16, 8, 4, 2, 1):
        pw = jnp.where(enough(pw * 2.0 ** -(k - 1)), pw, pw * 2.0 ** -k)

    def mantissa_step(_, carry):
        thr, step = carry
        cand = thr + step
        return jnp.where(enough(cand), cand, thr), step * 0.5

    thr, _ = lax.fori_loop(0, 23, mantissa_step, (pw, pw * 0.5))
    above = aff > thr
    tied = aff == thr
    need = capf - jnp.sum(jnp.where(above, 1.0, 0.0), axis=0, keepdims=True)
    r_i = lax.broadcasted_iota(jnp.int32, (n, n), 0)
    c_i = lax.broadcasted_iota(jnp.int32, (n, n), 1)
    before = jnp.where(c_i < r_i, 1.0, 0.0).astype(BF16)
    tie_rank = _dot(before, jnp.where(tied, 1.0, 0.0).astype(BF16))
    sel = above | (tied & (tie_rank < need))
    slot = _dot(before, jnp.where(sel, 1.0, 0.0).astype(BF16))
    slot = jnp.where(sel, slot, NOT_SELECTED)
    slot_o[0] = slot.astype(BF16)
    slot_t_o[0] = slot.T
    affb_o[0] = aff.astype(BF16)


def _dispatch_kernel(slot_ref, affb_ref, slot_t_ref, h2_ref, xe_o, gt_o, *, cap):
    n = h2_ref.shape[0]
    h2 = h2_ref[...]
    slot_t = slot_t_ref[0]

    per = min(N_EXPERTS, DISPATCH_ROWS // cap)
    slot_iota = lax.broadcasted_iota(jnp.int32, (cap, n), 0).astype(F32)
    for e0 in range(0, N_EXPERTS, per):
        onehot = jnp.concatenate(
            [jnp.where(slot_t[e:e + 1, :] == slot_iota, 1.0, 0.0) for e in range(e0, e0 + per)],
            axis=0).astype(BF16)
        rows = _dot(onehot, h2).astype(BF16)
        for j in range(per):
            xe_o[e0 + j] = rows[j * cap:(j + 1) * cap]

    first_lane = (pl.program_id(0) % SETS_PER_PACK) * N_EXPERTS
    shift = cap.bit_length() - 1
    for c0 in range(0, N_EXPERTS * cap, DISPATCH_ROWS):
        src = lax.broadcasted_iota(jnp.int32, (LANES, DISPATCH_ROWS), 0)
        col = lax.broadcasted_iota(jnp.int32, (LANES, DISPATCH_ROWS), 1) + c0
        spread = jnp.where(src == first_lane + lax.shift_right_logical(col, shift), 1.0, 0.0).astype(BF16)
        slot_x = _dot(slot_ref[0], spread)
        aff_x = _dot(affb_ref[0], spread)
        want = (lax.broadcasted_iota(jnp.int32, (n, DISPATCH_ROWS), 1) & (cap - 1)).astype(F32)
        gt_o[:, c0:c0 + DISPATCH_ROWS] = jnp.where(slot_x == want, aff_x, 0.0).astype(BF16)


def _route(aff, h2, n_sets, set_rows):
    t, d = h2.shape
    n = set_rows
    cap = CAPACITY_FACTOR * n // N_EXPERTS
    assert cap & (cap - 1) == 0 and cap % 16 == 0 and DISPATCH_ROWS % cap == 0
    n_packs = -(-n_sets // SETS_PER_PACK)
    aff16 = aff[:, :N_EXPERTS].reshape(n_sets, n, N_EXPERTS)
    aff16 = jnp.pad(aff16, ((0, n_packs * SETS_PER_PACK - n_sets), (0, 0), (0, 0)))
    packed = aff16.reshape(n_packs, SETS_PER_PACK, n, N_EXPERTS).transpose(0, 2, 1, 3).reshape(n_packs, n, LANES)
    pack_blk = lambda p: (p, 0, 0)
    slot, slot_t, affb = pl.pallas_call(
        functools.partial(_select_kernel, cap=cap),
        grid=(n_packs,),
        in_specs=[pl.BlockSpec((1, n, LANES), pack_blk)],
        out_specs=[pl.BlockSpec((1, n, LANES), pack_blk), pl.BlockSpec((1, LANES, n), pack_blk),
                   pl.BlockSpec((1, n, LANES), pack_blk)],
        out_shape=[jax.ShapeDtypeStruct((n_packs, n, LANES), BF16),
                   jax.ShapeDtypeStruct((n_packs, LANES, n), F32),
                   jax.ShapeDtypeStruct((n_packs, n, LANES), BF16)],
        name=f"select_cap{cap}",
        compiler_params=_cparams(1),
    )(packed)
    of_set = lambda s: (s // SETS_PER_PACK, 0, 0)
    return pl.pallas_call(
        functools.partial(_dispatch_kernel, cap=cap),
        grid=(n_sets,),
        in_specs=[pl.BlockSpec((1, n, LANES), of_set),
                  pl.BlockSpec((1, n, LANES), of_set),
                  pl.BlockSpec((1, N_EXPERTS, n), lambda s: (s // SETS_PER_PACK, s % SETS_PER_PACK, 0)),
                  pl.BlockSpec((n, d), lambda s: (s, 0))],
        out_specs=[pl.BlockSpec((N_EXPERTS, cap, d), lambda s: (0, s, 0)),
                   pl.BlockSpec((n, N_EXPERTS * cap), lambda s: (s, 0))],
        out_shape=[jax.ShapeDtypeStruct((N_EXPERTS, n_sets * cap, d), BF16),
                   jax.ShapeDtypeStruct((t, N_EXPERTS * cap), BF16)],
        name=f"dispatch_cap{cap}",
        compiler_params=_cparams(1),
    )(slot, affb, slot_t, h2)


def _expert_kernel(*refs, n_groups):
    x_refs = refs[:n_groups]
    wg_ref, wu_ref, wd_ref = refs[n_groups:n_groups + 3]
    o_refs = refs[n_groups + 3:2 * n_groups + 3]
    wg_s, wu_s, wd_s = refs[2 * n_groups + 3:]
    wg_s[...] = wg_ref[0].astype(BF16)
    wu_s[...] = wu_ref[0].astype(BF16)
    wd_s[...] = wd_ref[0].astype(BF16)
    sub = EXPERT_ROWS
    for x_ref, o_ref in zip(x_refs, o_refs):
        for r0 in range(0, x_ref.shape[1], sub):
            x = x_ref[0, r0:r0 + sub, :]
            a = _dot(x, wg_s[...])
            u = _dot(x, wu_s[...])
            hmid = (a * jax.nn.sigmoid(a) * u).astype(BF16)
            o_ref[0, r0:r0 + sub, :] = _dot(hmid, wd_s[...]).astype(BF16)


def _experts(xes, w_gate, w_up, w_down):
    e, d, f = w_gate.shape
    blk = lambda i: (i, 0, 0)
    return pl.pallas_call(
        functools.partial(_expert_kernel, n_groups=len(xes)),
        grid=(e,),
        in_specs=[pl.BlockSpec((1, x.shape[1], d), blk) for x in xes]
        + [pl.BlockSpec((1, d, f), blk), pl.BlockSpec((1, d, f), blk), pl.BlockSpec((1, f, d), blk)],
        out_specs=[pl.BlockSpec((1, x.shape[1], d), blk) for x in xes],
        out_shape=[jax.ShapeDtypeStruct(x.shape, BF16) for x in xes],
        scratch_shapes=[pltpu.VMEM((d, f), BF16), pltpu.VMEM((d, f), BF16), pltpu.VMEM((f, d), BF16)],
        name="experts",
        compiler_params=_cparams(1),
    )(*xes, w_gate, w_up, w_down)


def _combine_kernel(gt_ref, ye_ref, x1_ref, mod_ref, gf_ref, y_o):
    e, cap, d = ye_ref.shape
    ye = ye_ref[...].reshape(e * cap, d)
    moe = _dot(gt_ref[...], ye)
    x = x1_ref[...] + mod_ref[0][5:6] * moe
    y_o[...] = _rms(x, gf_ref[...])


def _combine(gt, ye, x1, mod, n_sets, set_rows, g_final):
    t, d = x1.shape
    tm = min(ROW_TILE, set_rows)
    tiles = set_rows // tm
    cap = ye.shape[1] // n_sets
    return pl.pallas_call(
        _combine_kernel,
        grid=(n_sets, tiles),
        in_specs=[pl.BlockSpec((tm, gt.shape[1]), lambda s, i: (s * tiles + i, 0)),
                  pl.BlockSpec((N_EXPERTS, cap, d), lambda s, i: (0, s, 0)),
                  pl.BlockSpec((tm, d), lambda s, i: (s * tiles + i, 0)),
                  pl.BlockSpec((1, 8, d), (lambda s, i: (s, 0, 0)) if mod.shape[0] > 1 else (lambda s, i: (0, 0, 0))),
                  pl.BlockSpec((1, d), lambda s, i: (0, 0))],
        out_specs=pl.BlockSpec((tm, d), lambda s, i: (s * tiles + i, 0)),
        out_shape=jax.ShapeDtypeStruct((t, d), F32),
        name=f"combine_cap{cap}",
        compiler_params=_cparams(2),
    )(gt, ye, x1, mod, g_final)


def _rope_tables(n_tokens):
    n_rows = n_tokens // GRID_W
    rowp = jnp.repeat(jnp.arange(n_rows), GRID_W).astype(F32)
    colp = jnp.tile(jnp.arange(GRID_W), n_rows).astype(F32)
    quarter = HEAD_DIM // 4
    freqs = ROPE_THETA ** (-jnp.arange(quarter, dtype=F32) / quarter)
    ang = jnp.stack([rowp[:, None] * freqs, colp[:, None] * freqs], axis=1)
    cos, sin = jnp.cos(ang), jnp.sin(ang)
    zero = jnp.zeros_like(sin)
    c = jnp.stack([cos, cos], axis=2).reshape(n_tokens, HEAD_DIM)
    s_up = jnp.stack([-sin, zero], axis=2).reshape(n_tokens, HEAD_DIM)
    s_dn = jnp.stack([zero, sin], axis=2).reshape(n_tokens, HEAD_DIM)
    return tuple(jnp.tile(t, (1, LANES // HEAD_DIM)) for t in (c, s_up, s_dn))


def _mix_and_route(x2, n_sets, set_rows, mod, lw, rope_tabs, cache, emit_cache, lam_init, tq):
    (g_attn, g_ffn, w_in, gq_t, gk_t, lq1, lk1, lq2, lk2, g_diff, w_ba, w_bb, w_out, w_router_p) = lw
    outs = _inproj(x2, mod, set_rows, g_attn, w_in, gq_t, gk_t, rope_tabs, emit_cache)
    qa, kta, vta, qb, ktb, vb16 = outs[:6]
    vb16 = vb16.reshape(n_sets, set_rows, vb16.shape[1])
    ks_a, vs_a, ks_b, vs_b = [kta], [vta], [ktb], [vb16]
    if cache is not None:
        ckta, cvta, cktb, cvb = cache
        ks_a, vs_a, ks_b, vs_b = [ckta, kta], [cvta, vta], [cktb, ktb], [cvb, vb16]
    per_step = CONTEXT_SETS_PER_STEP if (tq == set_rows and n_sets % CONTEXT_SETS_PER_STEP == 0) else 1
    ya = _attention("gqa", qa, ks_a, vs_a, set_rows, tq, per_step, (), {"mxu_sums": cache is not None})
    yb = _attention("diff", qb, ks_b, vs_b, set_rows, tq, per_step, (lq1, lk1, lq2, lk2, g_diff),
                    {"lam_init": lam_init, "stack_maps": cache is None})
    x1, h2, aff = _post(x2, ya, yb, mod, set_rows, g_attn, g_ffn, w_in, w_ba, w_bb, w_out, w_router_p)
    xe, gt = _route(aff, h2, n_sets, set_rows)
    return x1, xe, gt, outs[6:]


def kernel(x_prompt, x_sample, cache_attn_k, cache_attn_v, cache_diff_k, cache_diff_v, c, c_ctx, w_mod, b_mod,
           g_attn_norm, g_ffn_norm, w_in, g_q_norm, g_k_norm, lambda_q1, lambda_k1, lambda_q2, lambda_k2,
           g_diff_norm, w_branch_a, w_branch_b, w_out, w_router, w_exp_gate, w_exp_up, w_exp_down, g_final):
    batch, seq, d = x_prompt.shape
    dec_batch, dec_seq, _ = x_sample.shape
    depth = w_in.shape[0]
    assert depth == 1, "the final norm is fused into the layer's combine step"
    past = cache_attn_k.shape[2]
    assert w_in.shape[2] == N_QKV + 2 * d

    xp = x_prompt.reshape(batch * seq, d)
    xs = x_sample.reshape(dec_batch * dec_seq, d)
    rope_tabs = _rope_tables(dec_seq)
    c_rows = jnp.concatenate([c, c_ctx[None, :], jnp.zeros((16 - dec_batch - 1, d), F32)], axis=0)
    yp = ys = None
    caches = []
    for l in range(depth):
        lam_init = 0.8 - 0.6 * math.exp(-0.3 * l)
        mod6 = _modulation(c_rows, w_mod[l], b_mod[l])
        mod = jnp.pad(jnp.transpose(mod6, (1, 0, 2)), ((0, 0), (0, 2), (0, 0)))
        mod_lat, mod_ctx = mod[:dec_batch], mod[dec_batch:dec_batch + 1]
        lw = (g_attn_norm[l][None, :], g_ffn_norm[l][None, :], w_in[l],
              jnp.tile(g_q_norm[l], HEADS_A)[None, :], jnp.tile(g_k_norm[l], KV_HEADS_A)[None, :],
              lambda_q1[l], lambda_k1[l], lambda_q2[l], lambda_k2[l], g_diff_norm[l][None, :],
              w_branch_a[l], w_branch_b[l], w_out[l],
              jnp.pad(w_router[l], ((0, 0), (0, LANES - N_EXPERTS))))
        x1p, xe_p, gt_p, cache_out = _mix_and_route(xp, batch, seq, mod_ctx, lw, None, None, True, lam_init, seq)
        caches.append(cache_out)
        feat_major = lambda a: jnp.moveaxis(a[:, l], 1, -1).reshape(dec_batch, -1, past)
        cache_l = _cache_prep([feat_major(cache_attn_k), feat_major(cache_attn_v), feat_major(cache_diff_k)],
                              cache_diff_v[:, l].reshape(dec_batch, past * HEADS_B, -1))
        x1s, xe_s, gt_s, _ = _mix_and_route(xs, dec_batch, dec_seq, mod_lat, lw, rope_tabs, cache_l, False,
                                            lam_init, LATENT_Q_TILE)
        ye_p, ye_s = _experts([xe_p, xe_s], w_exp_gate[l], w_exp_up[l], w_exp_down[l])
        yp = _combine(gt_p, ye_p, x1p, mod_ctx, batch, seq, g_final[None, :])
        ys = _combine(gt_s, ye_s, x1s, mod_lat, dec_batch, dec_seq, g_final[None, :])
    y_prompt = yp.reshape(batch, seq, d)
    y_sample = ys.reshape(dec_batch, dec_seq, d)
    tok_major = lambda a, dims: jnp.moveaxis(a.reshape((batch,) + dims + (seq,)), -1, 1)
    new_attn_k = jnp.stack([tok_major(cc[0], (KV_HEADS_A, HEAD_DIM)) for cc in caches], axis=1)
    new_attn_v = jnp.stack([tok_major(cc[1], (KV_HEADS_A, HEAD_DIM)) for cc in caches], axis=1)
    new_diff_k = jnp.stack([tok_major(cc[2], (HEADS_B, 2, HEAD_DIM)) for cc in caches], axis=1)
    new_diff_v = jnp.stack([cc[3].reshape(batch, seq, HEADS_B, 2 * HEAD_DIM) for cc in caches], axis=1)
    return (y_prompt, y_sample, new_attn_k, new_attn_v, new_diff_k, new_diff_v)
```

```python
import functools
import math

import numpy as np
import jax
import jax.numpy as jnp
from jax import lax
from jax.experimental import pallas as pl
from jax.experimental.pallas import tpu as pltpu

F32 = jnp.float32
BF16 = jnp.bfloat16

HEAD_DIM = 64
HEADS_A = 8
KV_HEADS_A = 2
HEADS_B = 4
N_QKV = HEADS_A * HEAD_DIM + 2 * KV_HEADS_A * HEAD_DIM + 3 * HEADS_B * 2 * HEAD_DIM
N_EXPERTS = 16
CAPACITY_FACTOR = 2
GRID_W = 64
ROPE_THETA = 10000.0
EPS = 1e-6
LANES = 128
ROW_TILE = 512
CHAIN_ROWS = 256
EXPERT_ROWS = 256
LATENT_Q_TILE = 256
CONTEXT_SETS_PER_STEP = 4
SETS_PER_PACK = LANES // N_EXPERTS
DISPATCH_ROWS = 512
NEG_BIG = -1e30
NOT_SELECTED = -1.0
VMEM_LIMIT = 56 * 1024 * 1024


def _cparams(n_axes):
    return pltpu.CompilerParams(dimension_semantics=("arbitrary",) * n_axes,
                                vmem_limit_bytes=VMEM_LIMIT)


def _dot(a, b):
    return jnp.dot(a, b, preferred_element_type=F32)


def _dot_nt(a, b):
    return lax.dot_general(a, b, (((1,), (1,)), ((), ())), preferred_element_type=F32)


def _split(a):
    hi = a.astype(BF16)
    lo = (a - hi.astype(F32)).astype(BF16)
    return hi, lo


def _dot3(a, b):
    a_hi, a_lo = _split(a)
    b_hi, b_lo = _split(b)
    return _dot(a_hi, b_hi) + _dot(a_lo, b_hi) + _dot(a_hi, b_lo)


def _rms(x, g):
    return x * lax.rsqrt(jnp.mean(x * x, axis=-1, keepdims=True) + EPS) * g


def _mod_index(mod, tiles_per_set):
    if mod.shape[0] == 1:
        return lambda i: (0, 0, 0)
    return lambda i: (i // tiles_per_set, 0, 0)


def _mod_kernel(c_ref, w_ref, b_ref, o_ref):
    c = c_ref[...]
    a = c * jax.nn.sigmoid(c)
    o_ref[0] = _dot3(a, w_ref[...]) + b_ref[0]


def _modulation(c_rows, w_mod, b_mod):
    r, d = c_rows.shape
    return pl.pallas_call(
        _mod_kernel,
        grid=(6,),
        in_specs=[pl.BlockSpec((r, d), lambda j: (0, 0)),
                  pl.BlockSpec((d, d), lambda j: (0, j)),
                  pl.BlockSpec((1, 1, d), lambda j: (j, 0, 0))],
        out_specs=pl.BlockSpec((1, r, d), lambda j: (j, 0, 0)),
        out_shape=jax.ShapeDtypeStruct((6, r, d), F32),
        name="mod",
        compiler_params=_cparams(1),
    )(c_rows, w_mod, b_mod.reshape(6, 1, d))


def _round_weights_kernel(*refs):
    n = len(refs) // 2
    for src, out in zip(refs[:n - 1], refs[n:-1]):
        out[...] = src[...].astype(BF16)
    hi, lo = _split(refs[n - 1][...])
    refs[-1][...] = jnp.concatenate([hi, lo], axis=1)


def _round_weights(weights, w_router_p):
    steps = 8
    arrays = list(weights) + [w_router_p]
    blk = lambda a, cols: pl.BlockSpec((a.shape[0] // steps, cols), lambda i: (i, 0))
    out_cols = [a.shape[1] for a in weights] + [2 * w_router_p.shape[1]]
    return pl.pallas_call(
        _round_weights_kernel,
        grid=(steps,),
        in_specs=[blk(a, a.shape[1]) for a in arrays],
        out_specs=[blk(a, c) for a, c in zip(arrays, out_cols)],
        out_shape=[jax.ShapeDtypeStruct((a.shape[0], c), BF16) for a, c in zip(arrays, out_cols)],
        name="round_weights",
        compiler_params=_cparams(1),
    )(*arrays)


def _seg_sumsq(x, ones_blockdiag):
    hi, lo = _split(x * x)
    return _dot(hi, ones_blockdiag) + _dot(lo, ones_blockdiag)


def _rope(x, c, s_up, s_dn):
    w = x.shape[1]
    reps = w // c.shape[1]
    if reps > 1:
        c, s_up, s_dn = (jnp.concatenate([t] * reps, axis=1) for t in (c, s_up, s_dn))
    return x * c + pltpu.roll(x, w - 16, 1) * s_up + pltpu.roll(x, 16, 1) * s_dn


def _inproj_kernel(*refs, rope, emit_cache):
    (x_ref, mod_ref, g_ref, w_ref, gq_ref, gk_ref, bd512_ref, bd128_ref), refs = refs[:8], refs[8:]
    if rope:
        (c_ref, su_ref, sd_ref), refs = refs[:3], refs[3:]
    qa_o, kta_o, vta_o, qb_o, ktb_o, vb_o = refs[:6]
    if emit_cache:
        ka_c, va_c, kb_c, vb_c = refs[6:10]

    mod = mod_ref[0]
    wa = HEADS_A * HEAD_DIM
    wkv = KV_HEADS_A * HEAD_DIM
    wb = HEADS_B * 2 * HEAD_DIM
    o_ka, o_va, o_qb = wa, wa + wkv, wa + 2 * wkv
    o_kb, o_vb = o_qb + wb, o_qb + 2 * wb
    scale = HEAD_DIM ** -0.5 * math.log2(math.e)

    x = x_ref[...]
    h = (_rms(x, g_ref[...]) * (1.0 + mod[1:2]) + mod[0:1]).astype(BF16)
    if rope:
        tabs = (c_ref[...], su_ref[...], sd_ref[...])

    qa = _dot(h, w_ref[:, 0:wa])
    qa = qa * lax.rsqrt(_seg_sumsq(qa, bd512_ref[...]) * (1.0 / HEAD_DIM) + EPS) * gq_ref[...]
    if rope:
        qa = _rope(qa, *tabs)
    qa_o[...] = (qa * scale).astype(BF16)

    kv = _dot(h, w_ref[:, o_ka:o_qb])
    ka, va = kv[:, 0:wkv], kv[:, wkv:2 * wkv]
    ka = ka * lax.rsqrt(_seg_sumsq(ka, bd128_ref[...]) * (1.0 / HEAD_DIM) + EPS) * gk_ref[...]

    def put_feat(val, out_bf16, out_f32):
        rows = out_bf16.shape[2]
        for s in range(out_bf16.shape[0]):
            t = val[s * rows:(s + 1) * rows].T
            if out_f32 is not None:
                out_f32[s] = t
            out_bf16[s] = t.astype(BF16)

    if rope:
        put_feat(_rope(ka, *tabs), kta_o, None)
    else:
        put_feat(ka, kta_o, ka_c if emit_cache else None)
    put_feat(va, vta_o, va_c if emit_cache else None)

    qb = _dot(h, w_ref[:, o_qb:o_kb])
    if rope:
        qb = _rope(qb, *tabs)
    qb_o[...] = (qb * scale).astype(BF16)

    kb = _dot(h, w_ref[:, o_kb:o_vb])
    if rope:
        put_feat(_rope(kb, *tabs), ktb_o, None)
    else:
        put_feat(kb, ktb_o, kb_c if emit_cache else None)

    vb = _dot(h, w_ref[:, o_vb:o_vb + wb])
    if emit_cache:
        for hd in range(HEADS_B):
            vb_c[pl.ds(hd, vb.shape[0], stride=HEADS_B), :] = vb[:, hd * 128:(hd + 1) * 128]
    vb_o[...] = vb.astype(BF16)


def _blockdiag_ones(width):
    g = np.arange(width) // HEAD_DIM
    return jnp.asarray((g[:, None] == g[None, :]).astype(np.float32), dtype=BF16)


def _inproj(x2, mod, set_rows, g_attn, w_in, gq_t, gk_t, rope_tabs, emit_cache):
    t, d = x2.shape
    tm = ROW_TILE
    assert tm % set_rows == 0 or set_rows % tm == 0
    tiles_per_set = max(1, set_rows // tm)
    sets_per_tile = max(1, tm // set_rows)
    rope = rope_tabs is not None
    assert not (rope and emit_cache), "cached keys are the position-free ones"
    nq = N_QKV
    row = lambda i: (i, 0)
    const = lambda i: (0, 0)
    in_specs = [pl.BlockSpec((tm, d), row),
                pl.BlockSpec((1, 8, d), _mod_index(mod, tiles_per_set)),
                pl.BlockSpec((1, d), const),
                pl.BlockSpec((d, nq), const, pipeline_mode=pl.Buffered(1)),
                pl.BlockSpec((1, 512), const),
                pl.BlockSpec((1, 128), const),
                pl.BlockSpec((512, 512), const),
                pl.BlockSpec((128, 128), const)]
    args = [x2, mod, g_attn, w_in, gq_t, gk_t, _blockdiag_ones(512), _blockdiag_ones(128)]
    if rope:
        in_specs += [pl.BlockSpec((tm, LANES), lambda i: (i % tiles_per_set, 0))] * 3
        args += list(rope_tabs)
    n_sets = t // set_rows
    wkv, wb = KV_HEADS_A * HEAD_DIM, HEADS_B * 2 * HEAD_DIM
    outs = [("tok", 512, BF16), ("feat", wkv, BF16), ("feat", wkv, BF16),
            ("tok", 512, BF16), ("feat", wb, BF16), ("tok", wb, BF16)]
    if emit_cache:
        outs += [("feat", wkv, F32), ("feat", wkv, F32), ("feat", wb, F32), ("tokhead", wb, F32)]
    feat = lambda i: (i // tiles_per_set, 0, i % tiles_per_set)
    feat_rows = min(tm, set_rows)

    def out_block(kind, w):
        if kind == "tok":
            return pl.BlockSpec((tm, w), row)
        if kind == "tokhead":
            return pl.BlockSpec((tm * HEADS_B, w // HEADS_B), row)
        return pl.BlockSpec((sets_per_tile, w, feat_rows), feat)

    def out_array(kind, w, dt):
        shape = {"tok": (t, w), "tokhead": (t * HEADS_B, w // HEADS_B), "feat": (n_sets, w, set_rows)}[kind]
        return jax.ShapeDtypeStruct(shape, dt)

    return pl.pallas_call(
        functools.partial(_inproj_kernel, rope=rope, emit_cache=emit_cache),
        grid=(t // tm,),
        in_specs=in_specs,
        out_specs=[out_block(kind, w) for kind, w, _ in outs],
        out_shape=[out_array(kind, w, dt) for kind, w, dt in outs],
        name="inproj_rope" if rope else "inproj",
        compiler_params=_cparams(1),
    )(*args)


def _cache_prep_kernel(*refs):
    half = len(refs) // 2
    for src, out in zip(refs[:half - 1], refs[half:-1]):
        out[...] = src[...].astype(BF16)
    v_src, v_out = refs[half - 1], refs[-1]
    keys = v_out.shape[1]
    for hd in range(HEADS_B):
        v_out[0, :, hd * 128:(hd + 1) * 128] = v_src[0, pl.ds(hd, keys, stride=HEADS_B), :].astype(BF16)


def _cache_prep(arrays, v_by_head):
    blk = lambda b: (b, 0, 0)
    n_sets, rows, width = v_by_head.shape
    v_shape = (n_sets, rows // HEADS_B, width * HEADS_B)
    return pl.pallas_call(
        _cache_prep_kernel,
        grid=(n_sets,),
        in_specs=[pl.BlockSpec((1,) + a.shape[1:], blk) for a in arrays + [v_by_head]],
        out_specs=[pl.BlockSpec((1,) + a.shape[1:], blk) for a in arrays] + [pl.BlockSpec((1,) + v_shape[1:], blk)],
        out_shape=[jax.ShapeDtypeStruct(a.shape, BF16) for a in arrays] + [jax.ShapeDtypeStruct(v_shape, BF16)],
        name="cache_prep",
        compiler_params=_cparams(1),
    )(*arrays, v_by_head)


def _exp_parts(scores):
    m = scores[0].max(axis=-1, keepdims=True)
    for s in scores[1:]:
        m = jnp.maximum(m, s.max(axis=-1, keepdims=True))
    return [jnp.exp2(s - m) for s in scores]


def _row_sum(parts):
    l = parts[0].sum(axis=-1, keepdims=True)
    for e in parts[1:]:
        l = l + e.sum(axis=-1, keepdims=True)
    return l


def _pad_rows(x, first, ones_row=False):
    if ones_row:
        z = jnp.where(lax.broadcasted_iota(jnp.int32, x.shape, 0) == 0, 1.0, 0.0).astype(x.dtype)
    else:
        z = jnp.zeros_like(x)
    return jnp.concatenate([x, z] if first else [z, x], axis=0)


def _gqa_kernel(*refs, n_src, mxu_sums):
    q_ref = refs[0]
    k_refs = refs[1:1 + n_src]
    v_refs = refs[1 + n_src:1 + 2 * n_src]
    o_ref = refs[1 + 2 * n_src]
    sets_here = k_refs[0].shape[0]
    tq = q_ref.shape[0] // sets_here
    lane = lax.broadcasted_iota(jnp.int32, (2 * tq, LANES), 1)
    lo_half = lane < HEAD_DIM
    for s in range(sets_here):
        rows = slice(s * tq, (s + 1) * tq)
        for g in range(KV_HEADS_A):
            c0 = g * 256
            f0 = g * HEAD_DIM
            q = jnp.concatenate([q_ref[rows, c0:c0 + 128], q_ref[rows, c0 + 128:c0 + 256]], axis=0)
            kts = [k[s, f0:f0 + HEAD_DIM, :] for k in k_refs]
            vts = [v[s, f0:f0 + HEAD_DIM, :] for v in v_refs]
            e_lo = _exp_parts([_dot(q, _pad_rows(kt, True)) for kt in kts])
            e_hi = _exp_parts([_dot(q, _pad_rows(kt, False)) for kt in kts])
            o_lo = o_hi = None
            for e_part, vt in zip(e_lo, vts):
                t = _dot_nt(e_part.astype(BF16), _pad_rows(vt, True, ones_row=mxu_sums))
                o_lo = t if o_lo is None else o_lo + t
            for e_part, vt in zip(e_hi, vts):
                t = _dot_nt(e_part.astype(BF16), _pad_rows(vt, False, ones_row=mxu_sums))
                o_hi = t if o_hi is None else o_hi + t
            if mxu_sums:
                l_lo, l_hi = o_lo[:, HEAD_DIM:HEAD_DIM + 1], o_hi[:, 0:1]
                o = jnp.where(lo_half, o_lo * (1.0 / l_lo), o_hi * (1.0 / l_hi))
            else:
                o = (o_lo + o_hi) * jnp.where(lo_half, 1.0 / _row_sum(e_lo), 1.0 / _row_sum(e_hi))
            o_ref[rows, c0:c0 + 128] = o[0:tq].astype(BF16)
            o_ref[rows, c0 + 128:c0 + 256] = o[tq:2 * tq].astype(BF16)


def _diff_kernel(*refs, n_src, lam_init, stack_maps):
    q_ref = refs[0]
    k_refs = refs[1:1 + n_src]
    v_refs = refs[1 + n_src:1 + 2 * n_src]
    lq1, lk1, lq2, lk2, gd_ref, o_ref = refs[1 + 2 * n_src:]
    lam_all = (jnp.exp(jnp.sum(lq1[...] * lk1[...], axis=-1, keepdims=True))
               - jnp.exp(jnp.sum(lq2[...] * lk2[...], axis=-1, keepdims=True)) + lam_init)
    sets_here = k_refs[0].shape[0]
    tq = q_ref.shape[0] // sets_here
    for s in range(sets_here):
        rows = slice(s * tq, (s + 1) * tq)
        for j in range(HEADS_B):
            lam = lam_all[j:j + 1, :]
            q = q_ref[rows, j * 128:(j + 1) * 128]
            k0s = [_pad_rows(k[s, j * 128:j * 128 + HEAD_DIM, :], True) for k in k_refs]
            k1s = [_pad_rows(k[s, j * 128 + HEAD_DIM:(j + 1) * 128, :], False) for k in k_refs]
            e0 = _exp_parts([_dot(q, k0) for k0 in k0s])
            e1 = _exp_parts([_dot(q, k1) for k1 in k1s])
            o0 = o1 = None
            for a0, a1, v in zip(e0, e1, v_refs):
                val = v[s, :, j * 128:(j + 1) * 128]
                if stack_maps:
                    t = _dot(jnp.concatenate([a0.astype(BF16), a1.astype(BF16)], axis=0), val)
                    t0, t1 = t[0:tq], t[tq:2 * tq]
                else:
                    t0, t1 = _dot(a0.astype(BF16), val), _dot(a1.astype(BF16), val)
                o0 = t0 if o0 is None else o0 + t0
                o1 = t1 if o1 is None else o1 + t1
            o = o0 * (1.0 / _row_sum(e0)) - o1 * (lam / _row_sum(e1))
            o = _rms(o, gd_ref[...]) * (1.0 - lam_init)
            o_ref[rows, j * 128:(j + 1) * 128] = o.astype(BF16)


def _attention(kind, q, ks, vs, set_rows, tq, sets_per_step, extra, extra_kw):
    t, qw = q.shape
    n_sets = t // set_rows
    q_tiles = set_rows // tq
    assert sets_per_step == 1 or q_tiles == 1
    in_specs = [pl.BlockSpec((sets_per_step * tq, qw), lambda b, i: (b * q_tiles + i, 0))]
    for a in list(ks) + list(vs):
        in_specs.append(pl.BlockSpec((sets_per_step,) + a.shape[1:], lambda b, i: (b, 0, 0)))
    for e in extra:
        in_specs.append(pl.BlockSpec(e.shape, lambda b, i: (0, 0)))
    body = functools.partial(_gqa_kernel if kind == "gqa" else _diff_kernel, n_src=len(ks), **extra_kw)
    return pl.pallas_call(
        body,
        grid=(n_sets // sets_per_step, q_tiles),
        in_specs=in_specs,
        out_specs=pl.BlockSpec((sets_per_step * tq, qw), lambda b, i: (b * q_tiles + i, 0)),
        out_shape=jax.ShapeDtypeStruct((t, qw), BF16),
        name=f"{kind}_attn_{len(ks)}src",
        compiler_params=_cparams(2),
    )(q, *ks, *vs, *extra)


def _post_kernel(x_ref, ya_ref, yb_ref, mod_ref, g1_ref, g2_ref, win_ref, wba_ref, wbb_ref, wo_ref, wr2_ref,
                 x1_o, h2_o, aff_o):
    d = x_ref.shape[1]
    g0 = win_ref.shape[1] - 2 * d
    mod = mod_ref[0]
    for r0 in range(0, x_ref.shape[0], CHAIN_ROWS):
        rows = slice(r0, r0 + CHAIN_ROWS)
        x = x_ref[rows, :]
        h = (_rms(x, g1_ref[...]) * (1.0 + mod[1:2]) + mod[0:1]).astype(BF16)
        ga = jax.nn.sigmoid(_dot(h, win_ref[:, g0:g0 + d]))
        merged = ga * _dot(ya_ref[rows, :], wba_ref[...])
        gb = jax.nn.sigmoid(_dot(h, win_ref[:, g0 + d:g0 + 2 * d]))
        merged = merged + gb * _dot(yb_ref[rows, :], wbb_ref[...])
        m = _dot(merged.astype(BF16), wo_ref[...])
        x1 = x + mod[2:3] * m
        x1_o[rows, :] = x1
        h2 = _rms(x1, g2_ref[...]) * (1.0 + mod[4:5]) + mod[3:4]
        h2_o[rows, :] = h2.astype(BF16)
        h2_hi, h2_lo = _split(h2)
        both = _dot(h2_hi, wr2_ref[...])
        logits = both[:, 0:LANES] + both[:, LANES:2 * LANES] + _dot(h2_lo, wr2_ref[:, 0:LANES])
        lane = lax.broadcasted_iota(jnp.int32, logits.shape, 1)
        logits = jnp.where(lane < N_EXPERTS, logits, NEG_BIG)
        e = jnp.exp(logits - logits.max(axis=-1, keepdims=True))
        aff_o[rows, :] = e / e.sum(axis=-1, keepdims=True)


def _post(x2, ya, yb, mod, set_rows, g1, g2, w_in, w_ba, w_bb, w_out, w_router2):
    t, d = x2.shape
    tm = ROW_TILE
    tiles_per_set = max(1, set_rows // tm)
    assert mod.shape[0] == 1 or set_rows % tm == 0
    row = lambda i: (i, 0)
    const = lambda i: (0, 0)
    once = pl.Buffered(1)
    half = w_in.shape[1] // 2
    assert w_in.shape[1] == 2 * half and half % LANES == 0 and half >= 2 * d
    return pl.pallas_call(
        _post_kernel,
        grid=(t // tm,),
        in_specs=[pl.BlockSpec((tm, d), row),
                  pl.BlockSpec((tm, 512), row),
                  pl.BlockSpec((tm, 512), row),
                  pl.BlockSpec((1, 8, d), _mod_index(mod, tiles_per_set)),
                  pl.BlockSpec((1, d), const),
                  pl.BlockSpec((1, d), const),
                  pl.BlockSpec((d, half), lambda i: (0, 1), pipeline_mode=once),
                  pl.BlockSpec(w_ba.shape, const, pipeline_mode=once),
                  pl.BlockSpec(w_bb.shape, const, pipeline_mode=once),
                  pl.BlockSpec(w_out.shape, const, pipeline_mode=once),
                  pl.BlockSpec(w_router2.shape, const, pipeline_mode=once)],
        out_specs=[pl.BlockSpec((tm, d), row), pl.BlockSpec((tm, d), row), pl.BlockSpec((tm, LANES), row)],
        out_shape=[jax.ShapeDtypeStruct((t, d), F32), jax.ShapeDtypeStruct((t, d), BF16),
                   jax.ShapeDtypeStruct((t, LANES), F32)],
        name="post_attn",
        compiler_params=_cparams(1),
    )(x2, ya, yb, mod, g1, g2, w_in, w_ba, w_bb, w_out, w_router2)


def _select_kernel(aff_ref, slot_o, slot_t_o, affb_o, *, cap):
    aff = aff_ref[0]
    n = aff.shape[0]
    capf = float(cap)

    def enough(cand):
        return jnp.sum(jnp.where(aff >= cand, 1.0, 0.0), axis=0, keepdims=True) >= capf

    pw = jnp.ones((1, LANES), F32)
    for k in (64, 32, 16, 8, 4, 2, 1):
        pw = jnp.where(enough(pw * 2.0 ** -(k - 1)), pw, pw * 2.0 ** -k)

    def mantissa_step(_, carry):
        thr, step = carry
        cand = thr + step
        return jnp.where(enough(cand), cand, thr), step * 0.5

    thr, _ = lax.fori_loop(0, 23, mantissa_step, (pw, pw * 0.5))
    above = aff > thr
    tied = aff == thr
    need = capf - jnp.sum(jnp.where(above, 1.0, 0.0), axis=0, keepdims=True)
    r_i = lax.broadcasted_iota(jnp.int32, (n, n), 0)
    c_i = lax.broadcasted_iota(jnp.int32, (n, n), 1)
    before = jnp.where(c_i < r_i, 1.0, 0.0).astype(BF16)
    tie_rank = _dot(before, jnp.where(tied, 1.0, 0.0).astype(BF16))
    sel = above | (tied & (tie_rank < need))
    slot = _dot(before, jnp.where(sel, 1.0, 0.0).astype(BF16))
    slot = jnp.where(sel, slot, NOT_SELECTED)
    slot_o[0] = slot.astype(BF16)
    slot_t_o[0] = slot.T
    affb_o[0] = aff.astype(BF16)


def _dispatch_kernel(slot_ref, affb_ref, slot_t_ref, h2_ref, xe_o, gt_o, *, cap):
    n = h2_ref.shape[0]
    h2 = h2_ref[...]
    slot_t = slot_t_ref[0]

    per = min(N_EXPERTS, DISPATCH_ROWS // cap)
    slot_iota = lax.broadcasted_iota(jnp.int32, (cap, n), 0).astype(F32)
    for e0 in range(0, N_EXPERTS, per):
        onehot = jnp.concatenate(
            [jnp.where(slot_t[e:e + 1, :] == slot_iota, 1.0, 0.0) for e in range(e0, e0 + per)],
            axis=0).astype(BF16)
        rows = _dot(onehot, h2).astype(BF16)
        for j in range(per):
            xe_o[e0 + j] = rows[j * cap:(j + 1) * cap]

    first_lane = (pl.program_id(0) % SETS_PER_PACK) * N_EXPERTS
    shift = cap.bit_length() - 1
    for c0 in range(0, N_EXPERTS * cap, DISPATCH_ROWS):
        src = lax.broadcasted_iota(jnp.int32, (LANES, DISPATCH_ROWS), 0)
        col = lax.broadcasted_iota(jnp.int32, (LANES, DISPATCH_ROWS), 1) + c0
        spread = jnp.where(src == first_lane + lax.shift_right_logical(col, shift), 1.0, 0.0).astype(BF16)
        slot_x = _dot(slot_ref[0], spread)
        aff_x = _dot(affb_ref[0], spread)
        want = (lax.broadcasted_iota(jnp.int32, (n, DISPATCH_ROWS), 1) & (cap - 1)).astype(F32)
        gt_o[:, c0:c0 + DISPATCH_ROWS] = jnp.where(slot_x == want, aff_x, 0.0).astype(BF16)


def _route(aff, h2, n_sets, set_rows):
    t, d = h2.shape
    n = set_rows
    cap = CAPACITY_FACTOR * n // N_EXPERTS
    assert cap & (cap - 1) == 0 and cap % 16 == 0 and DISPATCH_ROWS % cap == 0
    n_packs = -(-n_sets // SETS_PER_PACK)
    aff16 = aff[:, :N_EXPERTS].reshape(n_sets, n, N_EXPERTS)
    aff16 = jnp.pad(aff16, ((0, n_packs * SETS_PER_PACK - n_sets), (0, 0), (0, 0)))
    packed = aff16.reshape(n_packs, SETS_PER_PACK, n, N_EXPERTS).transpose(0, 2, 1, 3).reshape(n_packs, n, LANES)
    pack_blk = lambda p: (p, 0, 0)
    slot, slot_t, affb = pl.pallas_call(
        functools.partial(_select_kernel, cap=cap),
        grid=(n_packs,),
        in_specs=[pl.BlockSpec((1, n, LANES), pack_blk)],
        out_specs=[pl.BlockSpec((1, n, LANES), pack_blk), pl.BlockSpec((1, LANES, n), pack_blk),
                   pl.BlockSpec((1, n, LANES), pack_blk)],
        out_shape=[jax.ShapeDtypeStruct((n_packs, n, LANES), BF16),
                   jax.ShapeDtypeStruct((n_packs, LANES, n), F32),
                   jax.ShapeDtypeStruct((n_packs, n, LANES), BF16)],
        name=f"select_cap{cap}",
        compiler_params=_cparams(1),
    )(packed)
    of_set = lambda s: (s // SETS_PER_PACK, 0, 0)
    return pl.pallas_call(
        functools.partial(_dispatch_kernel, cap=cap),
        grid=(n_sets,),
        in_specs=[pl.BlockSpec((1, n, LANES), of_set),
                  pl.BlockSpec((1, n, LANES), of_set),
                  pl.BlockSpec((1, N_EXPERTS, n), lambda s: (s // SETS_PER_PACK, s % SETS_PER_PACK, 0)),
                  pl.BlockSpec((n, d), lambda s: (s, 0))],
        out_specs=[pl.BlockSpec((N_EXPERTS, cap, d), lambda s: (0, s, 0)),
                   pl.BlockSpec((n, N_EXPERTS * cap), lambda s: (s, 0))],
        out_shape=[jax.ShapeDtypeStruct((N_EXPERTS, n_sets * cap, d), BF16),
                   jax.ShapeDtypeStruct((t, N_EXPERTS * cap), BF16)],
        name=f"dispatch_cap{cap}",
        compiler_params=_cparams(1),
    )(slot, affb, slot_t, h2)


def _expert_kernel(*refs, n_groups):
    x_refs = refs[:n_groups]
    wg_ref, wu_ref, wd_ref = refs[n_groups:n_groups + 3]
    o_refs = refs[n_groups + 3:2 * n_groups + 3]
    wg_s, wu_s, wd_s = refs[2 * n_groups + 3:]
    wg_s[...] = wg_ref[0].astype(BF16)
    wu_s[...] = wu_ref[0].astype(BF16)
    wd_s[...] = wd_ref[0].astype(BF16)
    sub = EXPERT_ROWS
    for x_ref, o_ref in zip(x_refs, o_refs):
        for r0 in range(0, x_ref.shape[1], sub):
            x = x_ref[0, r0:r0 + sub, :]
            a = _dot(x, wg_s[...])
            u = _dot(x, wu_s[...])
            hmid = (a * jax.nn.sigmoid(a) * u).astype(BF16)
            o_ref[0, r0:r0 + sub, :] = _dot(hmid, wd_s[...]).astype(BF16)


def _experts(xes, w_gate, w_up, w_down):
    e, d, f = w_gate.shape
    blk = lambda i: (i, 0, 0)
    return pl.pallas_call(
        functools.partial(_expert_kernel, n_groups=len(xes)),
        grid=(e,),
        in_specs=[pl.BlockSpec((1, x.shape[1], d), blk) for x in xes]
        + [pl.BlockSpec((1, d, f), blk), pl.BlockSpec((1, d, f), blk), pl.BlockSpec((1, f, d), blk)],
        out_specs=[pl.BlockSpec((1, x.shape[1], d), blk) for x in xes],
        out_shape=[jax.ShapeDtypeStruct(x.shape, BF16) for x in xes],
        scratch_shapes=[pltpu.VMEM((d, f), BF16), pltpu.VMEM((d, f), BF16), pltpu.VMEM((f, d), BF16)],
        name="experts",
        compiler_params=_cparams(1),
    )(*xes, w_gate, w_up, w_down)


def _combine_kernel(gt_ref, ye_ref, x1_ref, mod_ref, gf_ref, y_o):
    e, cap, d = ye_ref.shape
    ye = ye_ref[...].reshape(e * cap, d)
    moe = _dot(gt_ref[...], ye)
    x = x1_ref[...] + mod_ref[0][5:6] * moe
    y_o[...] = _rms(x, gf_ref[...])


def _combine(gt, ye, x1, mod, n_sets, set_rows, g_final):
    t, d = x1.shape
    tm = min(ROW_TILE, set_rows)
    tiles = set_rows // tm
    cap = ye.shape[1] // n_sets
    return pl.pallas_call(
        _combine_kernel,
        grid=(n_sets, tiles),
        in_specs=[pl.BlockSpec((tm, gt.shape[1]), lambda s, i: (s * tiles + i, 0)),
                  pl.BlockSpec((N_EXPERTS, cap, d), lambda s, i: (0, s, 0)),
                  pl.BlockSpec((tm, d), lambda s, i: (s * tiles + i, 0)),
                  pl.BlockSpec((1, 8, d), (lambda s, i: (s, 0, 0)) if mod.shape[0] > 1 else (lambda s, i: (0, 0, 0))),
                  pl.BlockSpec((1, d), lambda s, i: (0, 0))],
        out_specs=pl.BlockSpec((tm, d), lambda s, i: (s * tiles + i, 0)),
        out_shape=jax.ShapeDtypeStruct((t, d), F32),
        name=f"combine_cap{cap}",
        compiler_params=_cparams(2),
    )(gt, ye, x1, mod, g_final)


def _rope_tables(n_tokens):
    n_rows = n_tokens // GRID_W
    rowp = jnp.repeat(jnp.arange(n_rows), GRID_W).astype(F32)
    colp = jnp.tile(jnp.arange(GRID_W), n_rows).astype(F32)
    quarter = HEAD_DIM // 4
    freqs = ROPE_THETA ** (-jnp.arange(quarter, dtype=F32) / quarter)
    ang = jnp.stack([rowp[:, None] * freqs, colp[:, None] * freqs], axis=1)
    cos, sin = jnp.cos(ang), jnp.sin(ang)
    zero = jnp.zeros_like(sin)
    c = jnp.stack([cos, cos], axis=2).reshape(n_tokens, HEAD_DIM)
    s_up = jnp.stack([-sin, zero], axis=2).reshape(n_tokens, HEAD_DIM)
    s_dn = jnp.stack([zero, sin], axis=2).reshape(n_tokens, HEAD_DIM)
    return tuple(jnp.tile(t, (1, LANES // HEAD_DIM)) for t in (c, s_up, s_dn))


def _mix_and_route(x2, n_sets, set_rows, mod, lw, rope_tabs, cache, emit_cache, lam_init, tq):
    (g_attn, g_ffn, w_in, gq_t, gk_t, lq1, lk1, lq2, lk2, g_diff, w_ba, w_bb, w_out, w_router2) = lw
    outs = _inproj(x2, mod, set_rows, g_attn, w_in, gq_t, gk_t, rope_tabs, emit_cache)
    qa, kta, vta, qb, ktb, vb16 = outs[:6]
    vb16 = vb16.reshape(n_sets, set_rows, vb16.shape[1])
    ks_a, vs_a, ks_b, vs_b = [kta], [vta], [ktb], [vb16]
    if cache is not None:
        ckta, cvta, cktb, cvb = cache
        ks_a, vs_a, ks_b, vs_b = [ckta, kta], [cvta, vta], [cktb, ktb], [cvb, vb16]
    per_step = CONTEXT_SETS_PER_STEP if (tq == set_rows and n_sets % CONTEXT_SETS_PER_STEP == 0) else 1
    ya = _attention("gqa", qa, ks_a, vs_a, set_rows, tq, per_step, (), {"mxu_sums": cache is not None})
    yb = _attention("diff", qb, ks_b, vs_b, set_rows, tq, per_step, (lq1, lk1, lq2, lk2, g_diff),
                    {"lam_init": lam_init, "stack_maps": cache is None})
    x1, h2, aff = _post(x2, ya, yb, mod, set_rows, g_attn, g_ffn, w_in, w_ba, w_bb, w_out, w_router2)
    xe, gt = _route(aff, h2, n_sets, set_rows)
    return x1, xe, gt, outs[6:]


def kernel(x_prompt, x_sample, cache_attn_k, cache_attn_v, cache_diff_k, cache_diff_v, c, c_ctx, w_mod, b_mod,
           g_attn_norm, g_ffn_norm, w_in, g_q_norm, g_k_norm, lambda_q1, lambda_k1, lambda_q2, lambda_k2,
           g_diff_norm, w_branch_a, w_branch_b, w_out, w_router, w_exp_gate, w_exp_up, w_exp_down, g_final):
    batch, seq, d = x_prompt.shape
    dec_batch, dec_seq, _ = x_sample.shape
    depth = w_in.shape[0]
    assert depth == 1, "the final norm is fused into the layer's combine step"
    past = cache_attn_k.shape[2]
    assert w_in.shape[2] == N_QKV + 2 * d

    xp = x_prompt.reshape(batch * seq, d)
    xs = x_sample.reshape(dec_batch * dec_seq, d)
    rope_tabs = _rope_tables(dec_seq)
    c_rows = jnp.concatenate([c, c_ctx[None, :], jnp.zeros((16 - dec_batch - 1, d), F32)], axis=0)
    yp = ys = None
    caches = []
    for l in range(depth):
        lam_init = 0.8 - 0.6 * math.exp(-0.3 * l)
        mod6 = _modulation(c_rows, w_mod[l], b_mod[l])
        mod = jnp.pad(jnp.transpose(mod6, (1, 0, 2)), ((0, 0), (0, 2), (0, 0)))
        mod_lat, mod_ctx = mod[:dec_batch], mod[dec_batch:dec_batch + 1]
        w_in16, w_ba16, w_bb16, w_out16, w_router2 = _round_weights(
            [w_in[l], w_branch_a[l], w_branch_b[l], w_out[l]],
            jnp.pad(w_router[l], ((0, 0), (0, LANES - N_EXPERTS))))
        lw = (g_attn_norm[l][None, :], g_ffn_norm[l][None, :], w_in16,
              jnp.tile(g_q_norm[l], HEADS_A)[None, :], jnp.tile(g_k_norm[l], KV_HEADS_A)[None, :],
              lambda_q1[l], lambda_k1[l], lambda_q2[l], lambda_k2[l], g_diff_norm[l][None, :],
              w_ba16, w_bb16, w_out16, w_router2)
        x1p, xe_p, gt_p, cache_out = _mix_and_route(xp, batch, seq, mod_ctx, lw, None, None, True, lam_init, seq)
        caches.append(cache_out)
        feat_major = lambda a: jnp.moveaxis(a[:, l], 1, -1).reshape(dec_batch, -1, past)
        cache_l = _cache_prep([feat_major(cache_attn_k), feat_major(cache_attn_v), feat_major(cache_diff_k)],
                              cache_diff_v[:, l].reshape(dec_batch, past * HEADS_B, -1))
        x1s, xe_s, gt_s, _ = _mix_and_route(xs, dec_batch, dec_seq, mod_lat, lw, rope_tabs, cache_l, False,
                                            lam_init, LATENT_Q_TILE)
        ye_p, ye_s = _experts([xe_p, xe_s], w_exp_gate[l], w_exp_up[l], w_exp_down[l])
        yp = _combine(gt_p, ye_p, x1p, mod_ctx, batch, seq, g_final[None, :])
        ys = _combine(gt_s, ye_s, x1s, mod_lat, dec_batch, dec_seq, g_final[None, :])
    y_prompt = yp.reshape(batch, seq, d)
    y_sample = ys.reshape(dec_batch, dec_seq, d)
    tok_major = lambda a, dims: jnp.moveaxis(a.reshape((batch,) + dims + (seq,)), -1, 1)
    new_attn_k = jnp.stack([tok_major(cc[0], (KV_HEADS_A, HEAD_DIM)) for cc in caches], axis=1)
    new_attn_v = jnp.stack([tok_major(cc[1], (KV_HEADS_A, HEAD_DIM)) for cc in caches], axis=1)
    new_diff_k = jnp.stack([tok_major(cc[2], (HEADS_B, 2, HEAD_DIM)) for cc in caches], axis=1)
    new_diff_v = jnp.stack([cc[3].reshape(batch, seq, HEADS_B, 2 * HEAD_DIM) for cc in caches], axis=1)
    return (y_prompt, y_sample, new_attn_k, new_attn_v, new_diff_k, new_diff_v)
```

```python
import functools
import math

import numpy as np
import jax
import jax.numpy as jnp
from jax import lax
from jax.experimental import pallas as pl
from jax.experimental.pallas import tpu as pltpu

F32 = jnp.float32
BF16 = jnp.bfloat16

HEAD_DIM = 64
HEADS_A = 8
KV_HEADS_A = 2
HEADS_B = 4
N_QKV = HEADS_A * HEAD_DIM + 2 * KV_HEADS_A * HEAD_DIM + 3 * HEADS_B * 2 * HEAD_DIM
N_EXPERTS = 16
CAPACITY_FACTOR = 2
GRID_W = 64
ROPE_THETA = 10000.0
EPS = 1e-6
LANES = 128
ROW_TILE = 512
POST_ROWS = 1024
CHAIN_ROWS = 256
EXPERT_ROWS = 256
LATENT_Q_TILE = 512
CONTEXT_SETS_PER_STEP = 4
SETS_PER_PACK = LANES // N_EXPERTS
DISPATCH_ROWS = 512
NEG_BIG = -1e30
NOT_SELECTED = -1.0
VMEM_LIMIT = 56 * 1024 * 1024


def _cparams(n_axes):
    return pltpu.CompilerParams(dimension_semantics=("arbitrary",) * n_axes,
                                vmem_limit_bytes=VMEM_LIMIT)


def _dot(a, b):
    return jnp.dot(a, b, preferred_element_type=F32)


def _dot_nt(a, b):
    return lax.dot_general(a, b, (((1,), (1,)), ((), ())), preferred_element_type=F32)


def _split(a):
    hi = a.astype(BF16)
    lo = (a - hi.astype(F32)).astype(BF16)
    return hi, lo


def _dot3(a, b):
    a_hi, a_lo = _split(a)
    b_hi, b_lo = _split(b)
    return _dot(a_hi, b_hi) + _dot(a_lo, b_hi) + _dot(a_hi, b_lo)


def _rms(x, g):
    return x * lax.rsqrt(jnp.mean(x * x, axis=-1, keepdims=True) + EPS) * g


def _mod_index(mod, tiles_per_set):
    if mod.shape[0] == 1:
        return lambda i: (0, 0, 0)
    return lambda i: (i // tiles_per_set, 0, 0)


def _mod_kernel(c_ref, w_ref, b_ref, o_ref):
    c = c_ref[...]
    a = c * jax.nn.sigmoid(c)
    o_ref[0] = _dot3(a, w_ref[...]) + b_ref[0]


def _modulation(c_rows, w_mod, b_mod):
    r, d = c_rows.shape
    return pl.pallas_call(
        _mod_kernel,
        grid=(6,),
        in_specs=[pl.BlockSpec((r, d), lambda j: (0, 0)),
                  pl.BlockSpec((d, d), lambda j: (0, j)),
                  pl.BlockSpec((1, 1, d), lambda j: (j, 0, 0))],
        out_specs=pl.BlockSpec((1, r, d), lambda j: (j, 0, 0)),
        out_shape=jax.ShapeDtypeStruct((6, r, d), F32),
        name="mod",
        compiler_params=_cparams(1),
    )(c_rows, w_mod, b_mod.reshape(6, 1, d))


def _round_weights_kernel(*refs):
    n = len(refs) // 2
    for src, out in zip(refs[:n - 1], refs[n:-1]):
        out[...] = src[...].astype(BF16)
    hi, lo = _split(refs[n - 1][...])
    refs[-1][...] = jnp.concatenate([hi, lo], axis=1)


def _round_weights(weights, w_router_p):
    steps = 8
    arrays = list(weights) + [w_router_p]
    blk = lambda a, cols: pl.BlockSpec((a.shape[0] // steps, cols), lambda i: (i, 0))
    out_cols = [a.shape[1] for a in weights] + [2 * w_router_p.shape[1]]
    return pl.pallas_call(
        _round_weights_kernel,
        grid=(steps,),
        in_specs=[blk(a, a.shape[1]) for a in arrays],
        out_specs=[blk(a, c) for a, c in zip(arrays, out_cols)],
        out_shape=[jax.ShapeDtypeStruct((a.shape[0], c), BF16) for a, c in zip(arrays, out_cols)],
        name="round_weights",
        compiler_params=_cparams(1),
    )(*arrays)


def _seg_sumsq(x, ones_blockdiag):
    hi, lo = _split(x * x)
    return _dot(hi, ones_blockdiag) + _dot(lo, ones_blockdiag)


def _rope(x, c, s_up, s_dn):
    w = x.shape[1]
    reps = w // c.shape[1]
    if reps > 1:
        c, s_up, s_dn = (jnp.concatenate([t] * reps, axis=1) for t in (c, s_up, s_dn))
    return x * c + pltpu.roll(x, w - 16, 1) * s_up + pltpu.roll(x, 16, 1) * s_dn


def _inproj_kernel(*refs, rope, emit_cache):
    (x_ref, mod_ref, g_ref, w_ref, gq_ref, gk_ref, bd512_ref, bd128_ref), refs = refs[:8], refs[8:]
    if rope:
        (c_ref, su_ref, sd_ref), refs = refs[:3], refs[3:]
    qa_o, kta_o, vta_o, qb_o, ktb_o, vb_o = refs[:6]
    if emit_cache:
        ka_c, va_c, kb_c, vb_c = refs[6:10]

    mod = mod_ref[0]
    wa = HEADS_A * HEAD_DIM
    wkv = KV_HEADS_A * HEAD_DIM
    wb = HEADS_B * 2 * HEAD_DIM
    o_ka, o_va, o_qb = wa, wa + wkv, wa + 2 * wkv
    o_kb, o_vb = o_qb + wb, o_qb + 2 * wb
    scale = HEAD_DIM ** -0.5 * math.log2(math.e)

    x = x_ref[...]
    h = (_rms(x, g_ref[...]) * (1.0 + mod[1:2]) + mod[0:1]).astype(BF16)
    if rope:
        tabs = (c_ref[...], su_ref[...], sd_ref[...])

    qa = _dot(h, w_ref[:, 0:wa])
    qa = qa * lax.rsqrt(_seg_sumsq(qa, bd512_ref[...]) * (1.0 / HEAD_DIM) + EPS) * gq_ref[...]
    if rope:
        qa = _rope(qa, *tabs)
    qa_o[...] = (qa * scale).astype(BF16)

    kv = _dot(h, w_ref[:, o_ka:o_qb])
    ka, va = kv[:, 0:wkv], kv[:, wkv:2 * wkv]
    ka = ka * lax.rsqrt(_seg_sumsq(ka, bd128_ref[...]) * (1.0 / HEAD_DIM) + EPS) * gk_ref[...]

    def put_feat(val, out_bf16, out_f32):
        rows = out_bf16.shape[2]
        for s in range(out_bf16.shape[0]):
            t = val[s * rows:(s + 1) * rows].T
            if out_f32 is not None:
                out_f32[s] = t
            out_bf16[s] = t.astype(BF16)

    if rope:
        put_feat(_rope(ka, *tabs), kta_o, None)
    else:
        put_feat(ka, kta_o, ka_c if emit_cache else None)
    put_feat(va, vta_o, va_c if emit_cache else None)

    qb = _dot(h, w_ref[:, o_qb:o_kb])
    if rope:
        qb = _rope(qb, *tabs)
    qb_o[...] = (qb * scale).astype(BF16)

    kb = _dot(h, w_ref[:, o_kb:o_vb])
    if rope:
        put_feat(_rope(kb, *tabs), ktb_o, None)
    else:
        put_feat(kb, ktb_o, kb_c if emit_cache else None)

    vb = _dot(h, w_ref[:, o_vb:o_vb + wb])
    if emit_cache:
        for hd in range(HEADS_B):
            vb_c[pl.ds(hd, vb.shape[0], stride=HEADS_B), :] = vb[:, hd * 128:(hd + 1) * 128]
    vb_o[...] = vb.astype(BF16)


def _blockdiag_ones(width):
    g = np.arange(width) // HEAD_DIM
    return jnp.asarray((g[:, None] == g[None, :]).astype(np.float32), dtype=BF16)


def _inproj(x2, mod, set_rows, g_attn, w_in, gq_t, gk_t, rope_tabs, emit_cache):
    t, d = x2.shape
    tm = ROW_TILE
    assert tm % set_rows == 0 or set_rows % tm == 0
    tiles_per_set = max(1, set_rows // tm)
    sets_per_tile = max(1, tm // set_rows)
    rope = rope_tabs is not None
    assert not (rope and emit_cache), "cached keys are the position-free ones"
    nq = N_QKV
    row = lambda i: (i, 0)
    const = lambda i: (0, 0)
    in_specs = [pl.BlockSpec((tm, d), row),
                pl.BlockSpec((1, 8, d), _mod_index(mod, tiles_per_set)),
                pl.BlockSpec((1, d), const),
                pl.BlockSpec((d, nq), const, pipeline_mode=pl.Buffered(1)),
                pl.BlockSpec((1, 512), const),
                pl.BlockSpec((1, 128), const),
                pl.BlockSpec((512, 512), const),
                pl.BlockSpec((128, 128), const)]
    args = [x2, mod, g_attn, w_in, gq_t, gk_t, _blockdiag_ones(512), _blockdiag_ones(128)]
    if rope:
        in_specs += [pl.BlockSpec((tm, LANES), lambda i: (i % tiles_per_set, 0))] * 3
        args += list(rope_tabs)
    n_sets = t // set_rows
    wkv, wb = KV_HEADS_A * HEAD_DIM, HEADS_B * 2 * HEAD_DIM
    outs = [("tok", 512, BF16), ("feat", wkv, BF16), ("feat", wkv, BF16),
            ("tok", 512, BF16), ("feat", wb, BF16), ("tok", wb, BF16)]
    if emit_cache:
        outs += [("feat", wkv, F32), ("feat", wkv, F32), ("feat", wb, F32), ("tokhead", wb, F32)]
    feat = lambda i: (i // tiles_per_set, 0, i % tiles_per_set)
    feat_rows = min(tm, set_rows)

    def out_block(kind, w):
        if kind == "tok":
            return pl.BlockSpec((tm, w), row)
        if kind == "tokhead":
            return pl.BlockSpec((tm * HEADS_B, w // HEADS_B), row)
        return pl.BlockSpec((sets_per_tile, w, feat_rows), feat)

    def out_array(kind, w, dt):
        shape = {"tok": (t, w), "tokhead": (t * HEADS_B, w // HEADS_B), "feat": (n_sets, w, set_rows)}[kind]
        return jax.ShapeDtypeStruct(shape, dt)

    return pl.pallas_call(
        functools.partial(_inproj_kernel, rope=rope, emit_cache=emit_cache),
        grid=(t // tm,),
        in_specs=in_specs,
        out_specs=[out_block(kind, w) for kind, w, _ in outs],
        out_shape=[out_array(kind, w, dt) for kind, w, dt in outs],
        name="inproj_rope" if rope else "inproj",
        compiler_params=_cparams(1),
    )(*args)


def _cache_prep_kernel(*refs):
    half = len(refs) // 2
    for src, out in zip(refs[:half - 1], refs[half:-1]):
        out[...] = src[...].astype(BF16)
    v_src, v_out = refs[half - 1], refs[-1]
    keys = v_out.shape[1]
    for hd in range(HEADS_B):
        v_out[0, :, hd * 128:(hd + 1) * 128] = v_src[0, pl.ds(hd, keys, stride=HEADS_B), :].astype(BF16)


def _cache_prep(arrays, v_by_head):
    blk = lambda b: (b, 0, 0)
    n_sets, rows, width = v_by_head.shape
    v_shape = (n_sets, rows // HEADS_B, width * HEADS_B)
    return pl.pallas_call(
        _cache_prep_kernel,
        grid=(n_sets,),
        in_specs=[pl.BlockSpec((1,) + a.shape[1:], blk) for a in arrays + [v_by_head]],
        out_specs=[pl.BlockSpec((1,) + a.shape[1:], blk) for a in arrays] + [pl.BlockSpec((1,) + v_shape[1:], blk)],
        out_shape=[jax.ShapeDtypeStruct(a.shape, BF16) for a in arrays] + [jax.ShapeDtypeStruct(v_shape, BF16)],
        name="cache_prep",
        compiler_params=_cparams(1),
    )(*arrays, v_by_head)


def _exp_parts(scores):
    m = scores[0].max(axis=-1, keepdims=True)
    for s in scores[1:]:
        m = jnp.maximum(m, s.max(axis=-1, keepdims=True))
    return [jnp.exp2(s - m) for s in scores]


def _row_sum(parts):
    l = parts[0].sum(axis=-1, keepdims=True)
    for e in parts[1:]:
        l = l + e.sum(axis=-1, keepdims=True)
    return l


def _pad_rows(x, first, ones_row=False):
    if ones_row:
        z = jnp.where(lax.broadcasted_iota(jnp.int32, x.shape, 0) == 0, 1.0, 0.0).astype(x.dtype)
    else:
        z = jnp.zeros_like(x)
    return jnp.concatenate([x, z] if first else [z, x], axis=0)


def _gqa_kernel(*refs, n_src, mxu_sums):
    q_ref = refs[0]
    k_refs = refs[1:1 + n_src]
    v_refs = refs[1 + n_src:1 + 2 * n_src]
    o_ref = refs[1 + 2 * n_src]
    sets_here = k_refs[0].shape[0]
    tq = q_ref.shape[0] // sets_here
    lane = lax.broadcasted_iota(jnp.int32, (2 * tq, LANES), 1)
    lo_half = lane < HEAD_DIM
    for s in range(sets_here):
        rows = slice(s * tq, (s + 1) * tq)
        for g in range(KV_HEADS_A):
            c0 = g * 256
            f0 = g * HEAD_DIM
            q = jnp.concatenate([q_ref[rows, c0:c0 + 128], q_ref[rows, c0 + 128:c0 + 256]], axis=0)
            kts = [k[s, f0:f0 + HEAD_DIM, :] for k in k_refs]
            vts = [v[s, f0:f0 + HEAD_DIM, :] for v in v_refs]
            e_lo = _exp_parts([_dot(q, _pad_rows(kt, True)) for kt in kts])
            e_hi = _exp_parts([_dot(q, _pad_rows(kt, False)) for kt in kts])
            o_lo = o_hi = None
            for e_part, vt in zip(e_lo, vts):
                t = _dot_nt(e_part.astype(BF16), _pad_rows(vt, True, ones_row=mxu_sums))
                o_lo = t if o_lo is None else o_lo + t
            for e_part, vt in zip(e_hi, vts):
                t = _dot_nt(e_part.astype(BF16), _pad_rows(vt, False, ones_row=mxu_sums))
                o_hi = t if o_hi is None else o_hi + t
            if mxu_sums:
                l_lo, l_hi = o_lo[:, HEAD_DIM:HEAD_DIM + 1], o_hi[:, 0:1]
                o = jnp.where(lo_half, o_lo * (1.0 / l_lo), o_hi * (1.0 / l_hi))
            else:
                o = (o_lo + o_hi) * jnp.where(lo_half, 1.0 / _row_sum(e_lo), 1.0 / _row_sum(e_hi))
            o_ref[rows, c0:c0 + 128] = o[0:tq].astype(BF16)
            o_ref[rows, c0 + 128:c0 + 256] = o[tq:2 * tq].astype(BF16)


def _diff_kernel(*refs, n_src, lam_init, stack_maps):
    q_ref = refs[0]
    k_refs = refs[1:1 + n_src]
    v_refs = refs[1 + n_src:1 + 2 * n_src]
    lq1, lk1, lq2, lk2, gd_ref, o_ref = refs[1 + 2 * n_src:]
    lam_all = (jnp.exp(jnp.sum(lq1[...] * lk1[...], axis=-1, keepdims=True))
               - jnp.exp(jnp.sum(lq2[...] * lk2[...], axis=-1, keepdims=True)) + lam_init)
    sets_here = k_refs[0].shape[0]
    tq = q_ref.shape[0] // sets_here
    for s in range(sets_here):
        rows = slice(s * tq, (s + 1) * tq)
        for j in range(HEADS_B):
            lam = lam_all[j:j + 1, :]
            q = q_ref[rows, j * 128:(j + 1) * 128]
            k0s = [_pad_rows(k[s, j * 128:j * 128 + HEAD_DIM, :], True) for k in k_refs]
            k1s = [_pad_rows(k[s, j * 128 + HEAD_DIM:(j + 1) * 128, :], False) for k in k_refs]
            e0 = _exp_parts([_dot(q, k0) for k0 in k0s])
            e1 = _exp_parts([_dot(q, k1) for k1 in k1s])
            o0 = o1 = None
            for a0, a1, v in zip(e0, e1, v_refs):
                val = v[s, :, j * 128:(j + 1) * 128]
                if stack_maps:
                    t = _dot(jnp.concatenate([a0.astype(BF16), a1.astype(BF16)], axis=0), val)
                    t0, t1 = t[0:tq], t[tq:2 * tq]
                else:
                    t0, t1 = _dot(a0.astype(BF16), val), _dot(a1.astype(BF16), val)
                o0 = t0 if o0 is None else o0 + t0
                o1 = t1 if o1 is None else o1 + t1
            o = o0 * (1.0 / _row_sum(e0)) - o1 * (lam / _row_sum(e1))
            o = _rms(o, gd_ref[...]) * (1.0 - lam_init)
            o_ref[rows, j * 128:(j + 1) * 128] = o.astype(BF16)


def _attention(kind, q, ks, vs, set_rows, tq, sets_per_step, extra, extra_kw):
    t, qw = q.shape
    n_sets = t // set_rows
    q_tiles = set_rows // tq
    assert sets_per_step == 1 or q_tiles == 1
    in_specs = [pl.BlockSpec((sets_per_step * tq, qw), lambda b, i: (b * q_tiles + i, 0))]
    for a in list(ks) + list(vs):
        in_specs.append(pl.BlockSpec((sets_per_step,) + a.shape[1:], lambda b, i: (b, 0, 0)))
    for e in extra:
        in_specs.append(pl.BlockSpec(e.shape, lambda b, i: (0, 0)))
    body = functools.partial(_gqa_kernel if kind == "gqa" else _diff_kernel, n_src=len(ks), **extra_kw)
    return pl.pallas_call(
        body,
        grid=(n_sets // sets_per_step, q_tiles),
        in_specs=in_specs,
        out_specs=pl.BlockSpec((sets_per_step * tq, qw), lambda b, i: (b * q_tiles + i, 0)),
        out_shape=jax.ShapeDtypeStruct((t, qw), BF16),
        name=f"{kind}_attn_{len(ks)}src",
        compiler_params=_cparams(2),
    )(q, *ks, *vs, *extra)


def _post_kernel(x_ref, ya_ref, yb_ref, mod_ref, g1_ref, g2_ref, win_ref, wba_ref, wbb_ref, wo_ref, wr2_ref,
                 x1_o, h2_o, aff_o):
    d = x_ref.shape[1]
    g0 = win_ref.shape[1] - 2 * d
    mod = mod_ref[0]
    for r0 in range(0, x_ref.shape[0], CHAIN_ROWS):
        rows = slice(r0, r0 + CHAIN_ROWS)
        x = x_ref[rows, :]
        h = (_rms(x, g1_ref[...]) * (1.0 + mod[1:2]) + mod[0:1]).astype(BF16)
        ga = jax.nn.sigmoid(_dot(h, win_ref[:, g0:g0 + d]))
        merged = ga * _dot(ya_ref[rows, :], wba_ref[...])
        gb = jax.nn.sigmoid(_dot(h, win_ref[:, g0 + d:g0 + 2 * d]))
        merged = merged + gb * _dot(yb_ref[rows, :], wbb_ref[...])
        m = _dot(merged.astype(BF16), wo_ref[...])
        x1 = x + mod[2:3] * m
        x1_o[rows, :] = x1
        h2 = _rms(x1, g2_ref[...]) * (1.0 + mod[4:5]) + mod[3:4]
        h2_o[rows, :] = h2.astype(BF16)
        h2_hi, h2_lo = _split(h2)
        both = _dot(h2_hi, wr2_ref[...])
        logits = both[:, 0:LANES] + both[:, LANES:2 * LANES] + _dot(h2_lo, wr2_ref[:, 0:LANES])
        lane = lax.broadcasted_iota(jnp.int32, logits.shape, 1)
        logits = jnp.where(lane < N_EXPERTS, logits, NEG_BIG)
        e = jnp.exp(logits - logits.max(axis=-1, keepdims=True))
        aff_o[rows, :] = e / e.sum(axis=-1, keepdims=True)


def _post(x2, ya, yb, mod, set_rows, g1, g2, w_in, w_ba, w_bb, w_out, w_router2):
    t, d = x2.shape
    tm = POST_ROWS
    tiles_per_set = max(1, set_rows // tm)
    assert mod.shape[0] == 1 or set_rows % tm == 0
    row = lambda i: (i, 0)
    const = lambda i: (0, 0)
    once = pl.Buffered(1)
    half = w_in.shape[1] // 2
    assert w_in.shape[1] == 2 * half and half % LANES == 0 and half >= 2 * d
    return pl.pallas_call(
        _post_kernel,
        grid=(t // tm,),
        in_specs=[pl.BlockSpec((tm, d), row),
                  pl.BlockSpec((tm, 512), row),
                  pl.BlockSpec((tm, 512), row),
                  pl.BlockSpec((1, 8, d), _mod_index(mod, tiles_per_set)),
                  pl.BlockSpec((1, d), const),
                  pl.BlockSpec((1, d), const),
                  pl.BlockSpec((d, half), lambda i: (0, 1), pipeline_mode=once),
                  pl.BlockSpec(w_ba.shape, const, pipeline_mode=once),
                  pl.BlockSpec(w_bb.shape, const, pipeline_mode=once),
                  pl.BlockSpec(w_out.shape, const, pipeline_mode=once),
                  pl.BlockSpec(w_router2.shape, const, pipeline_mode=once)],
        out_specs=[pl.BlockSpec((tm, d), row), pl.BlockSpec((tm, d), row), pl.BlockSpec((tm, LANES), row)],
        out_shape=[jax.ShapeDtypeStruct((t, d), F32), jax.ShapeDtypeStruct((t, d), BF16),
                   jax.ShapeDtypeStruct((t, LANES), F32)],
        name="post_attn",
        compiler_params=_cparams(1),
    )(x2, ya, yb, mod, g1, g2, w_in, w_ba, w_bb, w_out, w_router2)


def _select_kernel(aff_ref, slot_o, slot_t_o, affb_o, *, cap):
    aff = aff_ref[0]
    n = aff.shape[0]
    capf = float(cap)

    def enough(cand):
        return jnp.sum(jnp.where(aff >= cand, 1.0, 0.0), axis=0, keepdims=True) >= capf

    pw = jnp.ones((1, LANES), F32)
    for k in (64, 32, 16, 8, 4, 2, 1):
        pw = jnp.where(enough(pw * 2.0 ** -(k - 1)), pw, pw * 2.0 ** -k)

    def mantissa_step(_, carry):
        thr, step = carry
        cand = thr + step
        return jnp.where(enough(cand), cand, thr), step * 0.5

    thr, _ = lax.fori_loop(0, 23, mantissa_step, (pw, pw * 0.5))
    above = aff > thr
    tied = aff == thr
    need = capf - jnp.sum(jnp.where(above, 1.0, 0.0), axis=0, keepdims=True)
    r_i = lax.broadcasted_iota(jnp.int32, (n, n), 0)
    c_i = lax.broadcasted_iota(jnp.int32, (n, n), 1)
    before = jnp.where(c_i < r_i, 1.0, 0.0).astype(BF16)
    tie_rank = _dot(before, jnp.where(tied, 1.0, 0.0).astype(BF16))
    sel = above | (tied & (tie_rank < need))
    slot = _dot(before, jnp.where(sel, 1.0, 0.0).astype(BF16))
    slot = jnp.where(sel, slot, NOT_SELECTED)
    slot_o[0] = slot.astype(BF16)
    slot_t_o[0] = slot.T
    affb_o[0] = aff.astype(BF16)


def _dispatch_kernel(slot_ref, affb_ref, slot_t_ref, h2_ref, xe_o, gt_o, *, cap):
    n = slot_ref.shape[1]
    g = h2_ref.shape[0] // n
    per = min(N_EXPERTS, DISPATCH_ROWS // cap)
    shift = cap.bit_length() - 1
    slot_iota = lax.broadcasted_iota(jnp.int32, (cap, n), 0).astype(F32)
    for k in range(g):
        h2 = h2_ref[k * n:(k + 1) * n, :]
        slot_t = slot_t_ref[0, k * N_EXPERTS:(k + 1) * N_EXPERTS, :]

        for e0 in range(0, N_EXPERTS, per):
            onehot = jnp.concatenate(
                [jnp.where(slot_t[e:e + 1, :] == slot_iota, 1.0, 0.0) for e in range(e0, e0 + per)],
                axis=0).astype(BF16)
            rows = _dot(onehot, h2).astype(BF16)
            for j in range(per):
                xe_o[e0 + j, k * cap:(k + 1) * cap, :] = rows[j * cap:(j + 1) * cap]

        first_lane = ((pl.program_id(0) * g + k) % SETS_PER_PACK) * N_EXPERTS
        for c0 in range(0, N_EXPERTS * cap, DISPATCH_ROWS):
            src = lax.broadcasted_iota(jnp.int32, (LANES, DISPATCH_ROWS), 0)
            col = lax.broadcasted_iota(jnp.int32, (LANES, DISPATCH_ROWS), 1) + c0
            spread = jnp.where(src == first_lane + lax.shift_right_logical(col, shift), 1.0, 0.0).astype(BF16)
            slot_x = _dot(slot_ref[0], spread)
            aff_x = _dot(affb_ref[0], spread)
            want = (lax.broadcasted_iota(jnp.int32, (n, DISPATCH_ROWS), 1) & (cap - 1)).astype(F32)
            gt_o[k * n:(k + 1) * n, c0:c0 + DISPATCH_ROWS] = jnp.where(slot_x == want, aff_x, 0.0).astype(BF16)


def _route(aff, h2, n_sets, set_rows):
    t, d = h2.shape
    n = set_rows
    cap = CAPACITY_FACTOR * n // N_EXPERTS
    assert cap & (cap - 1) == 0 and cap % 16 == 0 and DISPATCH_ROWS % cap == 0
    n_packs = -(-n_sets // SETS_PER_PACK)
    aff16 = aff[:, :N_EXPERTS].reshape(n_sets, n, N_EXPERTS)
    aff16 = jnp.pad(aff16, ((0, n_packs * SETS_PER_PACK - n_sets), (0, 0), (0, 0)))
    packed = aff16.reshape(n_packs, SETS_PER_PACK, n, N_EXPERTS).transpose(0, 2, 1, 3).reshape(n_packs, n, LANES)
    pack_blk = lambda p: (p, 0, 0)
    slot, slot_t, affb = pl.pallas_call(
        functools.partial(_select_kernel, cap=cap),
        grid=(n_packs,),
        in_specs=[pl.BlockSpec((1, n, LANES), pack_blk)],
        out_specs=[pl.BlockSpec((1, n, LANES), pack_blk), pl.BlockSpec((1, LANES, n), pack_blk),
                   pl.BlockSpec((1, n, LANES), pack_blk)],
        out_shape=[jax.ShapeDtypeStruct((n_packs, n, LANES), BF16),
                   jax.ShapeDtypeStruct((n_packs, LANES, n), F32),
                   jax.ShapeDtypeStruct((n_packs, n, LANES), BF16)],
        name=f"select_cap{cap}",
        compiler_params=_cparams(1),
    )(packed)
    g = _sets_per_step(n_sets, n)
    per_pack = SETS_PER_PACK // g
    of_step = lambda s: (s // per_pack, 0, 0)
    return pl.pallas_call(
        functools.partial(_dispatch_kernel, cap=cap),
        grid=(n_sets // g,),
        in_specs=[pl.BlockSpec((1, n, LANES), of_step),
                  pl.BlockSpec((1, n, LANES), of_step),
                  pl.BlockSpec((1, g * N_EXPERTS, n), lambda s: (s // per_pack, s % per_pack, 0)),
                  pl.BlockSpec((g * n, d), lambda s: (s, 0))],
        out_specs=[pl.BlockSpec((N_EXPERTS, g * cap, d), lambda s: (0, s, 0)),
                   pl.BlockSpec((g * n, N_EXPERTS * cap), lambda s: (s, 0))],
        out_shape=[jax.ShapeDtypeStruct((N_EXPERTS, n_sets * cap, d), BF16),
                   jax.ShapeDtypeStruct((t, N_EXPERTS * cap), BF16)],
        name=f"dispatch_cap{cap}",
        compiler_params=_cparams(1),
    )(slot, affb, slot_t, h2)


def _expert_kernel(*refs, n_groups):
    x_refs = refs[:n_groups]
    wg_ref, wu_ref, wd_ref = refs[n_groups:n_groups + 3]
    o_refs = refs[n_groups + 3:2 * n_groups + 3]
    wg_s, wu_s, wd_s = refs[2 * n_groups + 3:]
    wg_s[...] = wg_ref[0].astype(BF16)
    wu_s[...] = wu_ref[0].astype(BF16)
    wd_s[...] = wd_ref[0].astype(BF16)
    sub = EXPERT_ROWS
    for x_ref, o_ref in zip(x_refs, o_refs):
        for r0 in range(0, x_ref.shape[1], sub):
            x = x_ref[0, r0:r0 + sub, :]
            a = _dot(x, wg_s[...])
            u = _dot(x, wu_s[...])
            hmid = (a * jax.nn.sigmoid(a) * u).astype(BF16)
            o_ref[0, r0:r0 + sub, :] = _dot(hmid, wd_s[...]).astype(BF16)


def _experts(xes, w_gate, w_up, w_down):
    e, d, f = w_gate.shape
    blk = lambda i: (i, 0, 0)
    return pl.pallas_call(
        functools.partial(_expert_kernel, n_groups=len(xes)),
        grid=(e,),
        in_specs=[pl.BlockSpec((1, x.shape[1], d), blk) for x in xes]
        + [pl.BlockSpec((1, d, f), blk), pl.BlockSpec((1, d, f), blk), pl.BlockSpec((1, f, d), blk)],
        out_specs=[pl.BlockSpec((1, x.shape[1], d), blk) for x in xes],
        out_shape=[jax.ShapeDtypeStruct(x.shape, BF16) for x in xes],
        scratch_shapes=[pltpu.VMEM((d, f), BF16), pltpu.VMEM((d, f), BF16), pltpu.VMEM((f, d), BF16)],
        name="experts",
        compiler_params=_cparams(1),
    )(*xes, w_gate, w_up, w_down)


def _combine_kernel(gt_ref, ye_ref, x1_ref, mod_ref, gf_ref, y_o, *, cap):
    e, gcap, d = ye_ref.shape
    g = gcap // cap
    n = x1_ref.shape[0] // g
    for k in range(g):
        rows = slice(k * n, (k + 1) * n)
        ye = ye_ref[:, k * cap:(k + 1) * cap, :].reshape(e * cap, d)
        moe = _dot(gt_ref[rows, :], ye)
        x = x1_ref[rows, :] + mod_ref[0][5:6] * moe
        y_o[rows, :] = _rms(x, gf_ref[...])


def _sets_per_step(n_sets, set_rows):
    g = max(1, min(SETS_PER_PACK, ROW_TILE * 2 // set_rows))
    return g if n_sets % g == 0 else 1


def _combine(gt, ye, x1, mod, n_sets, set_rows, g_final):
    t, d = x1.shape
    cap = ye.shape[1] // n_sets
    g = _sets_per_step(n_sets, set_rows) if mod.shape[0] == 1 else 1
    tm = min(ROW_TILE, set_rows) if g == 1 else g * set_rows
    tiles = max(1, set_rows // tm)
    return pl.pallas_call(
        functools.partial(_combine_kernel, cap=cap),
        grid=(n_sets // g, tiles),
        in_specs=[pl.BlockSpec((tm, gt.shape[1]), lambda s, i: (s * tiles + i, 0)),
                  pl.BlockSpec((N_EXPERTS, g * cap, d), lambda s, i: (0, s, 0)),
                  pl.BlockSpec((tm, d), lambda s, i: (s * tiles + i, 0)),
                  pl.BlockSpec((1, 8, d), (lambda s, i: (s, 0, 0)) if mod.shape[0] > 1 else (lambda s, i: (0, 0, 0))),
                  pl.BlockSpec((1, d), lambda s, i: (0, 0))],
        out_specs=pl.BlockSpec((tm, d), lambda s, i: (s * tiles + i, 0)),
        out_shape=jax.ShapeDtypeStruct((t, d), F32),
        name=f"combine_cap{cap}",
        compiler_params=_cparams(2),
    )(gt, ye, x1, mod, g_final)


def _rope_tables(n_tokens):
    n_rows = n_tokens // GRID_W
    rowp = jnp.repeat(jnp.arange(n_rows), GRID_W).astype(F32)
    colp = jnp.tile(jnp.arange(GRID_W), n_rows).astype(F32)
    quarter = HEAD_DIM // 4
    freqs = ROPE_THETA ** (-jnp.arange(quarter, dtype=F32) / quarter)
    ang = jnp.stack([rowp[:, None] * freqs, colp[:, None] * freqs], axis=1)
    cos, sin = jnp.cos(ang), jnp.sin(ang)
    zero = jnp.zeros_like(sin)
    c = jnp.stack([cos, cos], axis=2).reshape(n_tokens, HEAD_DIM)
    s_up = jnp.stack([-sin, zero], axis=2).reshape(n_tokens, HEAD_DIM)
    s_dn = jnp.stack([zero, sin], axis=2).reshape(n_tokens, HEAD_DIM)
    return tuple(jnp.tile(t, (1, LANES // HEAD_DIM)) for t in (c, s_up, s_dn))


def _mix_and_route(x2, n_sets, set_rows, mod, lw, rope_tabs, cache, emit_cache, lam_init, tq):
    (g_attn, g_ffn, w_in, gq_t, gk_t, lq1, lk1, lq2, lk2, g_diff, w_ba, w_bb, w_out, w_router2) = lw
    outs = _inproj(x2, mod, set_rows, g_attn, w_in, gq_t, gk_t, rope_tabs, emit_cache)
    qa, kta, vta, qb, ktb, vb16 = outs[:6]
    vb16 = vb16.reshape(n_sets, set_rows, vb16.shape[1])
    ks_a, vs_a, ks_b, vs_b = [kta], [vta], [ktb], [vb16]
    if cache is not None:
        ckta, cvta, cktb, cvb = cache
        ks_a, vs_a, ks_b, vs_b = [ckta, kta], [cvta, vta], [cktb, ktb], [cvb, vb16]
    per_step = CONTEXT_SETS_PER_STEP if (tq == set_rows and n_sets % CONTEXT_SETS_PER_STEP == 0) else 1
    ya = _attention("gqa", qa, ks_a, vs_a, set_rows, tq, per_step, (), {"mxu_sums": cache is not None})
    yb = _attention("diff", qb, ks_b, vs_b, set_rows, tq, per_step, (lq1, lk1, lq2, lk2, g_diff),
                    {"lam_init": lam_init, "stack_maps": cache is None})
    x1, h2, aff = _post(x2, ya, yb, mod, set_rows, g_attn, g_ffn, w_in, w_ba, w_bb, w_out, w_router2)
    xe, gt = _route(aff, h2, n_sets, set_rows)
    return x1, xe, gt, outs[6:]


def kernel(x_prompt, x_sample, cache_attn_k, cache_attn_v, cache_diff_k, cache_diff_v, c, c_ctx, w_mod, b_mod,
           g_attn_norm, g_ffn_norm, w_in, g_q_norm, g_k_norm, lambda_q1, lambda_k1, lambda_q2, lambda_k2,
           g_diff_norm, w_branch_a, w_branch_b, w_out, w_router, w_exp_gate, w_exp_up, w_exp_down, g_final):
    batch, seq, d = x_prompt.shape
    dec_batch, dec_seq, _ = x_sample.shape
    depth = w_in.shape[0]
    assert depth == 1, "the final norm is fused into the layer's combine step"
    past = cache_attn_k.shape[2]
    assert w_in.shape[2] == N_QKV + 2 * d

    xp = x_prompt.reshape(batch * seq, d)
    xs = x_sample.reshape(dec_batch * dec_seq, d)
    rope_tabs = _rope_tables(dec_seq)
    c_rows = jnp.concatenate([c, c_ctx[None, :], jnp.zeros((16 - dec_batch - 1, d), F32)], axis=0)
    yp = ys = None
    caches = []
    for l in range(depth):
        lam_init = 0.8 - 0.6 * math.exp(-0.3 * l)
        mod6 = _modulation(c_rows, w_mod[l], b_mod[l])
        mod = jnp.pad(jnp.transpose(mod6, (1, 0, 2)), ((0, 0), (0, 2), (0, 0)))
        mod_lat, mod_ctx = mod[:dec_batch], mod[dec_batch:dec_batch + 1]
        w_in16, w_ba16, w_bb16, w_out16, w_router2 = _round_weights(
            [w_in[l], w_branch_a[l], w_branch_b[l], w_out[l]],
            jnp.pad(w_router[l], ((0, 0), (0, LANES - N_EXPERTS))))
        lw = (g_attn_norm[l][None, :], g_ffn_norm[l][None, :], w_in16,
              jnp.tile(g_q_norm[l], HEADS_A)[None, :], jnp.tile(g_k_norm[l], KV_HEADS_A)[None, :],
              lambda_q1[l], lambda_k1[l], lambda_q2[l], lambda_k2[l], g_diff_norm[l][None, :],
              w_ba16, w_bb16, w_out16, w_router2)
        x1p, xe_p, gt_p, cache_out = _mix_and_route(xp, batch, seq, mod_ctx, lw, None, None, True, lam_init, seq)
        caches.append(cache_out)
        feat_major = lambda a: jnp.moveaxis(a[:, l], 1, -1).reshape(dec_batch, -1, past)
        cache_l = _cache_prep([feat_major(cache_attn_k), feat_major(cache_attn_v), feat_major(cache_diff_k)],
                              cache_diff_v[:, l].reshape(dec_batch, past * HEADS_B, -1))
        x1s, xe_s, gt_s, _ = _mix_and_route(xs, dec_batch, dec_seq, mod_lat, lw, rope_tabs, cache_l, False,
                                            lam_init, LATENT_Q_TILE)
        ye_p, ye_s = _experts([xe_p, xe_s], w_exp_gate[l], w_exp_up[l], w_exp_down[l])
        yp = _combine(gt_p, ye_p, x1p, mod_ctx, batch, seq, g_final[None, :])
        ys = _combine(gt_s, ye_s, x1s, mod_lat, dec_batch, dec_seq, g_final[None, :])
    y_prompt = yp.reshape(batch, seq, d)
    y_sample = ys.reshape(dec_batch, dec_seq, d)
    tok_major = lambda a, dims: jnp.moveaxis(a.reshape((batch,) + dims + (seq,)), -1, 1)
    new_attn_k = jnp.stack([tok_major(cc[0], (KV_HEADS_A, HEAD_DIM)) for cc in caches], axis=1)
    new_attn_v = jnp.stack([tok_major(cc[1], (KV_HEADS_A, HEAD_DIM)) for cc in caches], axis=1)
    new_diff_k = jnp.stack([tok_major(cc[2], (HEADS_B, 2, HEAD_DIM)) for cc in caches], axis=1)
    new_diff_v = jnp.stack([cc[3].reshape(batch, seq, HEADS_B, 2 * HEAD_DIM) for cc in caches], axis=1)
    return (y_prompt, y_sample, new_attn_k, new_attn_v, new_diff_k, new_diff_v)
```

```python
import functools
import math

import numpy as np
import jax
import jax.numpy as jnp
from jax import lax
from jax.experimental import pallas as pl
from jax.experimental.pallas import tpu as pltpu

F32 = jnp.float32
BF16 = jnp.bfloat16

HEAD_DIM = 64
HEADS_A = 8
KV_HEADS_A = 2
HEADS_B = 4
N_QKV = HEADS_A * HEAD_DIM + 2 * KV_HEADS_A * HEAD_DIM + 3 * HEADS_B * 2 * HEAD_DIM
N_EXPERTS = 16
CAPACITY_FACTOR = 2
GRID_W = 64
ROPE_THETA = 10000.0
EPS = 1e-6
LANES = 128
ROW_TILE = 1024
POST_ROWS = 1024
CHAIN_ROWS = 256
EXPERT_ROWS = 256
LATENT_Q_TILE = 512
CONTEXT_SETS_PER_STEP = 4
SETS_PER_PACK = LANES // N_EXPERTS
DISPATCH_ROWS = 512
NEG_BIG = -1e30
NOT_SELECTED = -1.0
VMEM_LIMIT = 56 * 1024 * 1024


def _cparams(n_axes):
    return pltpu.CompilerParams(dimension_semantics=("arbitrary",) * n_axes,
                                vmem_limit_bytes=VMEM_LIMIT)


def _dot(a, b):
    return jnp.dot(a, b, preferred_element_type=F32)


def _dot_nt(a, b):
    return lax.dot_general(a, b, (((1,), (1,)), ((), ())), preferred_element_type=F32)


def _split(a):
    hi = a.astype(BF16)
    lo = (a - hi.astype(F32)).astype(BF16)
    return hi, lo


def _dot3(a, b):
    a_hi, a_lo = _split(a)
    b_hi, b_lo = _split(b)
    return _dot(a_hi, b_hi) + _dot(a_lo, b_hi) + _dot(a_hi, b_lo)


def _rms(x, g):
    return x * lax.rsqrt(jnp.mean(x * x, axis=-1, keepdims=True) + EPS) * g


def _mod_index(mod, tiles_per_set):
    if mod.shape[0] == 1:
        return lambda i: (0, 0, 0)
    return lambda i: (i // tiles_per_set, 0, 0)


def _mod_kernel(c_ref, w_ref, b_ref, o_ref):
    c = c_ref[...]
    a = c * jax.nn.sigmoid(c)
    o_ref[0] = _dot3(a, w_ref[...]) + b_ref[0]


def _modulation(c_rows, w_mod, b_mod):
    r, d = c_rows.shape
    return pl.pallas_call(
        _mod_kernel,
        grid=(6,),
        in_specs=[pl.BlockSpec((r, d), lambda j: (0, 0)),
                  pl.BlockSpec((d, d), lambda j: (0, j)),
                  pl.BlockSpec((1, 1, d), lambda j: (j, 0, 0))],
        out_specs=pl.BlockSpec((1, r, d), lambda j: (j, 0, 0)),
        out_shape=jax.ShapeDtypeStruct((6, r, d), F32),
        name="mod",
        compiler_params=_cparams(1),
    )(c_rows, w_mod, b_mod.reshape(6, 1, d))


def _round_weights_kernel(*refs):
    n = len(refs) // 2
    for src, out in zip(refs[:n - 1], refs[n:-1]):
        out[...] = src[...].astype(BF16)
    hi, lo = _split(refs[n - 1][...])
    refs[-1][...] = jnp.concatenate([hi, lo], axis=1)


def _round_weights(weights, w_router_p):
    steps = 8
    arrays = list(weights) + [w_router_p]
    blk = lambda a, cols: pl.BlockSpec((a.shape[0] // steps, cols), lambda i: (i, 0))
    out_cols = [a.shape[1] for a in weights] + [2 * w_router_p.shape[1]]
    return pl.pallas_call(
        _round_weights_kernel,
        grid=(steps,),
        in_specs=[blk(a, a.shape[1]) for a in arrays],
        out_specs=[blk(a, c) for a, c in zip(arrays, out_cols)],
        out_shape=[jax.ShapeDtypeStruct((a.shape[0], c), BF16) for a, c in zip(arrays, out_cols)],
        name="round_weights",
        compiler_params=_cparams(1),
    )(*arrays)


def _seg_sumsq(x, ones_blockdiag):
    return _dot((x * x).astype(BF16), ones_blockdiag)


def _rope(x, c, s_up, s_dn):
    w = x.shape[1]
    reps = w // c.shape[1]
    if reps > 1:
        c, s_up, s_dn = (jnp.concatenate([t] * reps, axis=1) for t in (c, s_up, s_dn))
    return x * c + pltpu.roll(x, w - 16, 1) * s_up + pltpu.roll(x, 16, 1) * s_dn


def _inproj_kernel(*refs, rope, emit_cache):
    (x_ref, mod_ref, g_ref, w_ref, gq_ref, gk_ref, bd512_ref, bd128_ref), refs = refs[:8], refs[8:]
    if rope:
        (c_ref, su_ref, sd_ref), refs = refs[:3], refs[3:]
    qa_o, kta_o, vta_o, qb_o, ktb_o, vb_o = refs[:6]
    if emit_cache:
        ka_c, va_c, kb_c, vb_c = refs[6:10]

    mod = mod_ref[0]
    wa = HEADS_A * HEAD_DIM
    wkv = KV_HEADS_A * HEAD_DIM
    wb = HEADS_B * 2 * HEAD_DIM
    o_ka, o_va, o_qb = wa, wa + wkv, wa + 2 * wkv
    o_kb, o_vb = o_qb + wb, o_qb + 2 * wb
    scale = HEAD_DIM ** -0.5 * math.log2(math.e)

    x = x_ref[...]
    h = (_rms(x, g_ref[...]) * (1.0 + mod[1:2]) + mod[0:1]).astype(BF16)
    if rope:
        tabs = (c_ref[...], su_ref[...], sd_ref[...])

    qa = _dot(h, w_ref[:, 0:wa])
    qa = qa * lax.rsqrt(_seg_sumsq(qa, bd512_ref[...]) * (1.0 / HEAD_DIM) + EPS) * gq_ref[...]
    if rope:
        qa = _rope(qa, *tabs)
    qa_o[...] = (qa * scale).astype(BF16)

    kv = _dot(h, w_ref[:, o_ka:o_qb])
    ka, va = kv[:, 0:wkv], kv[:, wkv:2 * wkv]
    ka = ka * lax.rsqrt(_seg_sumsq(ka, bd128_ref[...]) * (1.0 / HEAD_DIM) + EPS) * gk_ref[...]

    def put_feat(val, out_bf16, out_f32):
        rows = out_bf16.shape[2]
        for s in range(out_bf16.shape[0]):
            t = val[s * rows:(s + 1) * rows].T
            if out_f32 is not None:
                out_f32[s] = t
            out_bf16[s] = t.astype(BF16)

    if rope:
        put_feat(_rope(ka, *tabs), kta_o, None)
    else:
        put_feat(ka, kta_o, ka_c if emit_cache else None)
    put_feat(va, vta_o, va_c if emit_cache else None)

    qb = _dot(h, w_ref[:, o_qb:o_kb])
    if rope:
        qb = _rope(qb, *tabs)
    qb_o[...] = (qb * scale).astype(BF16)

    kb = _dot(h, w_ref[:, o_kb:o_vb])
    if rope:
        put_feat(_rope(kb, *tabs), ktb_o, None)
    else:
        put_feat(kb, ktb_o, kb_c if emit_cache else None)

    vb = _dot(h, w_ref[:, o_vb:o_vb + wb])
    if emit_cache:
        for hd in range(HEADS_B):
            vb_c[pl.ds(hd, vb.shape[0], stride=HEADS_B), :] = vb[:, hd * 128:(hd + 1) * 128]
    vb_o[...] = vb.astype(BF16)


def _blockdiag_ones(width):
    g = np.arange(width) // HEAD_DIM
    return jnp.asarray((g[:, None] == g[None, :]).astype(np.float32), dtype=BF16)


def _inproj(x2, mod, set_rows, g_attn, w_in, gq_t, gk_t, rope_tabs, emit_cache):
    t, d = x2.shape
    tm = ROW_TILE
    assert tm % set_rows == 0 or set_rows % tm == 0
    tiles_per_set = max(1, set_rows // tm)
    sets_per_tile = max(1, tm // set_rows)
    rope = rope_tabs is not None
    assert not (rope and emit_cache), "cached keys are the position-free ones"
    nq = N_QKV
    row = lambda i: (i, 0)
    const = lambda i: (0, 0)
    in_specs = [pl.BlockSpec((tm, d), row),
                pl.BlockSpec((1, 8, d), _mod_index(mod, tiles_per_set)),
                pl.BlockSpec((1, d), const),
                pl.BlockSpec((d, nq), const, pipeline_mode=pl.Buffered(1)),
                pl.BlockSpec((1, 512), const),
                pl.BlockSpec((1, 128), const),
                pl.BlockSpec((512, 512), const),
                pl.BlockSpec((128, 128), const)]
    args = [x2, mod, g_attn, w_in, gq_t, gk_t, _blockdiag_ones(512), _blockdiag_ones(128)]
    if rope:
        in_specs += [pl.BlockSpec((tm, LANES), lambda i: (i % tiles_per_set, 0))] * 3
        args += list(rope_tabs)
    n_sets = t // set_rows
    wkv, wb = KV_HEADS_A * HEAD_DIM, HEADS_B * 2 * HEAD_DIM
    outs = [("tok", 512, BF16), ("feat", wkv, BF16), ("feat", wkv, BF16),
            ("tok", 512, BF16), ("feat", wb, BF16), ("tok", wb, BF16)]
    if emit_cache:
        outs += [("feat", wkv, F32), ("feat", wkv, F32), ("feat", wb, F32), ("tokhead", wb, F32)]
    feat = lambda i: (i // tiles_per_set, 0, i % tiles_per_set)
    feat_rows = min(tm, set_rows)

    def out_block(kind, w):
        if kind == "tok":
            return pl.BlockSpec((tm, w), row)
        if kind == "tokhead":
            return pl.BlockSpec((tm * HEADS_B, w // HEADS_B), row)
        return pl.BlockSpec((sets_per_tile, w, feat_rows), feat)

    def out_array(kind, w, dt):
        shape = {"tok": (t, w), "tokhead": (t * HEADS_B, w // HEADS_B), "feat": (n_sets, w, set_rows)}[kind]
        return jax.ShapeDtypeStruct(shape, dt)

    return pl.pallas_call(
        functools.partial(_inproj_kernel, rope=rope, emit_cache=emit_cache),
        grid=(t // tm,),
        in_specs=in_specs,
        out_specs=[out_block(kind, w) for kind, w, _ in outs],
        out_shape=[out_array(kind, w, dt) for kind, w, dt in outs],
        name="inproj_rope" if rope else "inproj",
        compiler_params=_cparams(1),
    )(*args)


def _cache_prep_kernel(*refs):
    half = len(refs) // 2
    for src, out in zip(refs[:half - 1], refs[half:-1]):
        out[...] = src[...].astype(BF16)
    v_src, v_out = refs[half - 1], refs[-1]
    keys = v_out.shape[1]
    for hd in range(HEADS_B):
        v_out[0, :, hd * 128:(hd + 1) * 128] = v_src[0, pl.ds(hd, keys, stride=HEADS_B), :].astype(BF16)


def _cache_prep(arrays, v_by_head):
    blk = lambda b: (b, 0, 0)
    n_sets, rows, width = v_by_head.shape
    v_shape = (n_sets, rows // HEADS_B, width * HEADS_B)
    return pl.pallas_call(
        _cache_prep_kernel,
        grid=(n_sets,),
        in_specs=[pl.BlockSpec((1,) + a.shape[1:], blk) for a in arrays + [v_by_head]],
        out_specs=[pl.BlockSpec((1,) + a.shape[1:], blk) for a in arrays] + [pl.BlockSpec((1,) + v_shape[1:], blk)],
        out_shape=[jax.ShapeDtypeStruct(a.shape, BF16) for a in arrays] + [jax.ShapeDtypeStruct(v_shape, BF16)],
        name="cache_prep",
        compiler_params=_cparams(1),
    )(*arrays, v_by_head)


def _exp_parts(scores):
    m = scores[0].max(axis=-1, keepdims=True)
    for s in scores[1:]:
        m = jnp.maximum(m, s.max(axis=-1, keepdims=True))
    return [jnp.exp2(s - m) for s in scores]


def _row_sum(parts):
    l = parts[0].sum(axis=-1, keepdims=True)
    for e in parts[1:]:
        l = l + e.sum(axis=-1, keepdims=True)
    return l


def _pad_rows(x, first, ones_row=False):
    if ones_row:
        z = jnp.where(lax.broadcasted_iota(jnp.int32, x.shape, 0) == 0, 1.0, 0.0).astype(x.dtype)
    else:
        z = jnp.zeros_like(x)
    return jnp.concatenate([x, z] if first else [z, x], axis=0)


def _gqa_kernel(*refs, n_src, mxu_sums):
    q_ref = refs[0]
    k_refs = refs[1:1 + n_src]
    v_refs = refs[1 + n_src:1 + 2 * n_src]
    o_ref = refs[1 + 2 * n_src]
    sets_here = k_refs[0].shape[0]
    tq = q_ref.shape[0] // sets_here
    lane = lax.broadcasted_iota(jnp.int32, (2 * tq, LANES), 1)
    lo_half = lane < HEAD_DIM
    for s in range(sets_here):
        rows = slice(s * tq, (s + 1) * tq)
        for g in range(KV_HEADS_A):
            c0 = g * 256
            f0 = g * HEAD_DIM
            q = jnp.concatenate([q_ref[rows, c0:c0 + 128], q_ref[rows, c0 + 128:c0 + 256]], axis=0)
            kts = [k[s, f0:f0 + HEAD_DIM, :] for k in k_refs]
            vts = [v[s, f0:f0 + HEAD_DIM, :] for v in v_refs]
            e_lo = _exp_parts([_dot(q, _pad_rows(kt, True)) for kt in kts])
            e_hi = _exp_parts([_dot(q, _pad_rows(kt, False)) for kt in kts])
            o_lo = o_hi = None
            for e_part, vt in zip(e_lo, vts):
                t = _dot_nt(e_part.astype(BF16), _pad_rows(vt, True, ones_row=mxu_sums))
                o_lo = t if o_lo is None else o_lo + t
            for e_part, vt in zip(e_hi, vts):
                t = _dot_nt(e_part.astype(BF16), _pad_rows(vt, False, ones_row=mxu_sums))
                o_hi = t if o_hi is None else o_hi + t
            if mxu_sums:
                l_lo, l_hi = o_lo[:, HEAD_DIM:HEAD_DIM + 1], o_hi[:, 0:1]
                o = jnp.where(lo_half, o_lo * (1.0 / l_lo), o_hi * (1.0 / l_hi))
            else:
                o = (o_lo + o_hi) * jnp.where(lo_half, 1.0 / _row_sum(e_lo), 1.0 / _row_sum(e_hi))
            o_ref[rows, c0:c0 + 128] = o[0:tq].astype(BF16)
            o_ref[rows, c0 + 128:c0 + 256] = o[tq:2 * tq].astype(BF16)


def _diff_kernel(*refs, n_src, lam_init, stack_maps):
    q_ref = refs[0]
    k_refs = refs[1:1 + n_src]
    v_refs = refs[1 + n_src:1 + 2 * n_src]
    lq1, lk1, lq2, lk2, gd_ref, o_ref = refs[1 + 2 * n_src:]
    lam_all = (jnp.exp(jnp.sum(lq1[...] * lk1[...], axis=-1, keepdims=True))
               - jnp.exp(jnp.sum(lq2[...] * lk2[...], axis=-1, keepdims=True)) + lam_init)
    sets_here = k_refs[0].shape[0]
    tq = q_ref.shape[0] // sets_here
    for s in range(sets_here):
        rows = slice(s * tq, (s + 1) * tq)
        for j in range(HEADS_B):
            lam = lam_all[j:j + 1, :]
            q = q_ref[rows, j * 128:(j + 1) * 128]
            k0s = [_pad_rows(k[s, j * 128:j * 128 + HEAD_DIM, :], True) for k in k_refs]
            k1s = [_pad_rows(k[s, j * 128 + HEAD_DIM:(j + 1) * 128, :], False) for k in k_refs]
            e0 = _exp_parts([_dot(q, k0) for k0 in k0s])
            e1 = _exp_parts([_dot(q, k1) for k1 in k1s])
            o0 = o1 = None
            for a0, a1, v in zip(e0, e1, v_refs):
                val = v[s, :, j * 128:(j + 1) * 128]
                if stack_maps:
                    t = _dot(jnp.concatenate([a0.astype(BF16), a1.astype(BF16)], axis=0), val)
                    t0, t1 = t[0:tq], t[tq:2 * tq]
                else:
                    t0, t1 = _dot(a0.astype(BF16), val), _dot(a1.astype(BF16), val)
                o0 = t0 if o0 is None else o0 + t0
                o1 = t1 if o1 is None else o1 + t1
            o = o0 * (1.0 / _row_sum(e0)) - o1 * (lam / _row_sum(e1))
            o = _rms(o, gd_ref[...]) * (1.0 - lam_init)
            o_ref[rows, j * 128:(j + 1) * 128] = o.astype(BF16)


def _attention(kind, q, ks, vs, set_rows, tq, sets_per_step, extra, extra_kw):
    t, qw = q.shape
    n_sets = t // set_rows
    q_tiles = set_rows // tq
    assert sets_per_step == 1 or q_tiles == 1
    in_specs = [pl.BlockSpec((sets_per_step * tq, qw), lambda b, i: (b * q_tiles + i, 0))]
    for a in list(ks) + list(vs):
        in_specs.append(pl.BlockSpec((sets_per_step,) + a.shape[1:], lambda b, i: (b, 0, 0)))
    for e in extra:
        in_specs.append(pl.BlockSpec(e.shape, lambda b, i: (0, 0)))
    body = functools.partial(_gqa_kernel if kind == "gqa" else _diff_kernel, n_src=len(ks), **extra_kw)
    return pl.pallas_call(
        body,
        grid=(n_sets // sets_per_step, q_tiles),
        in_specs=in_specs,
        out_specs=pl.BlockSpec((sets_per_step * tq, qw), lambda b, i: (b * q_tiles + i, 0)),
        out_shape=jax.ShapeDtypeStruct((t, qw), BF16),
        name=f"{kind}_attn_{len(ks)}src",
        compiler_params=_cparams(2),
    )(q, *ks, *vs, *extra)


def _post_kernel(x_ref, ya_ref, yb_ref, mod_ref, g1_ref, g2_ref, win_ref, wba_ref, wbb_ref, wo_ref, wr2_ref,
                 x1_o, h2_o, aff_o):
    d = x_ref.shape[1]
    g0 = win_ref.shape[1] - 2 * d
    mod = mod_ref[0]
    for r0 in range(0, x_ref.shape[0], CHAIN_ROWS):
        rows = slice(r0, r0 + CHAIN_ROWS)
        x = x_ref[rows, :]
        h = (_rms(x, g1_ref[...]) * (1.0 + mod[1:2]) + mod[0:1]).astype(BF16)
        ga = jax.nn.sigmoid(_dot(h, win_ref[:, g0:g0 + d]))
        merged = ga * _dot(ya_ref[rows, :], wba_ref[...])
        gb = jax.nn.sigmoid(_dot(h, win_ref[:, g0 + d:g0 + 2 * d]))
        merged = merged + gb * _dot(yb_ref[rows, :], wbb_ref[...])
        m = _dot(merged.astype(BF16), wo_ref[...])
        x1 = x + mod[2:3] * m
        x1_o[rows, :] = x1
        h2 = _rms(x1, g2_ref[...]) * (1.0 + mod[4:5]) + mod[3:4]
        h2_o[rows, :] = h2.astype(BF16)
        h2_hi, h2_lo = _split(h2)
        both = _dot(h2_hi, wr2_ref[...])
        logits = both[:, 0:LANES] + both[:, LANES:2 * LANES] + _dot(h2_lo, wr2_ref[:, 0:LANES])
        lane = lax.broadcasted_iota(jnp.int32, logits.shape, 1)
        logits = jnp.where(lane < N_EXPERTS, logits, NEG_BIG)
        e = jnp.exp(logits - logits.max(axis=-1, keepdims=True))
        aff_o[rows, :] = e / e.sum(axis=-1, keepdims=True)


def _post(x2, ya, yb, mod, set_rows, g1, g2, w_in, w_ba, w_bb, w_out, w_router2):
    t, d = x2.shape
    tm = POST_ROWS
    tiles_per_set = max(1, set_rows // tm)
    assert mod.shape[0] == 1 or set_rows % tm == 0
    row = lambda i: (i, 0)
    const = lambda i: (0, 0)
    once = pl.Buffered(1)
    half = w_in.shape[1] // 2
    assert w_in.shape[1] == 2 * half and half % LANES == 0 and half >= 2 * d
    return pl.pallas_call(
        _post_kernel,
        grid=(t // tm,),
        in_specs=[pl.BlockSpec((tm, d), row),
                  pl.BlockSpec((tm, 512), row),
                  pl.BlockSpec((tm, 512), row),
                  pl.BlockSpec((1, 8, d), _mod_index(mod, tiles_per_set)),
                  pl.BlockSpec((1, d), const),
                  pl.BlockSpec((1, d), const),
                  pl.BlockSpec((d, half), lambda i: (0, 1), pipeline_mode=once),
                  pl.BlockSpec(w_ba.shape, const, pipeline_mode=once),
                  pl.BlockSpec(w_bb.shape, const, pipeline_mode=once),
                  pl.BlockSpec(w_out.shape, const, pipeline_mode=once),
                  pl.BlockSpec(w_router2.shape, const, pipeline_mode=once)],
        out_specs=[pl.BlockSpec((tm, d), row), pl.BlockSpec((tm, d), row), pl.BlockSpec((tm, LANES), row)],
        out_shape=[jax.ShapeDtypeStruct((t, d), F32), jax.ShapeDtypeStruct((t, d), BF16),
                   jax.ShapeDtypeStruct((t, LANES), F32)],
        name="post_attn",
        compiler_params=_cparams(1),
    )(x2, ya, yb, mod, g1, g2, w_in, w_ba, w_bb, w_out, w_router2)


def _select_kernel(aff_ref, slot_o, slot_t_o, affb_o, *, cap):
    aff = aff_ref[0]
    n = aff.shape[0]
    capf = float(cap)

    def enough(cand):
        return jnp.sum(jnp.where(aff >= cand, 1.0, 0.0), axis=0, keepdims=True) >= capf

    pw = jnp.ones((1, LANES), F32)
    for k in (64, 32, 16, 8, 4, 2, 1):
        pw = jnp.where(enough(pw * 2.0 ** -(k - 1)), pw, pw * 2.0 ** -k)

    def mantissa_step(_, carry):
        thr, step = carry
        cand = thr + step
        return jnp.where(enough(cand), cand, thr), step * 0.5

    thr, _ = lax.fori_loop(0, 23, mantissa_step, (pw, pw * 0.5))
    above = aff > thr
    tied = aff == thr
    need = capf - jnp.sum(jnp.where(above, 1.0, 0.0), axis=0, keepdims=True)
    r_i = lax.broadcasted_iota(jnp.int32, (n, n), 0)
    c_i = lax.broadcasted_iota(jnp.int32, (n, n), 1)
    before = jnp.where(c_i < r_i, 1.0, 0.0).astype(BF16)
    tie_rank = _dot(before, jnp.where(tied, 1.0, 0.0).astype(BF16))
    sel = above | (tied & (tie_rank < need))
    slot = _dot(before, jnp.where(sel, 1.0, 0.0).astype(BF16))
    slot = jnp.where(sel, slot, NOT_SELECTED)
    slot_o[0] = slot.astype(BF16)
    slot_t_o[0] = slot.T
    affb_o[0] = aff.astype(BF16)


def _dispatch_kernel(slot_ref, affb_ref, slot_t_ref, h2_ref, xe_o, gt_o, *, cap):
    n = slot_ref.shape[1]
    g = h2_ref.shape[0] // n
    per = min(N_EXPERTS, DISPATCH_ROWS // cap)
    shift = cap.bit_length() - 1
    slot_iota = lax.broadcasted_iota(jnp.int32, (cap, n), 0).astype(F32)
    for k in range(g):
        h2 = h2_ref[k * n:(k + 1) * n, :]
        slot_t = slot_t_ref[0, k * N_EXPERTS:(k + 1) * N_EXPERTS, :]

        for e0 in range(0, N_EXPERTS, per):
            onehot = jnp.concatenate(
                [jnp.where(slot_t[e:e + 1, :] == slot_iota, 1.0, 0.0) for e in range(e0, e0 + per)],
                axis=0).astype(BF16)
            rows = _dot(onehot, h2).astype(BF16)
            for j in range(per):
                xe_o[e0 + j, k * cap:(k + 1) * cap, :] = rows[j * cap:(j + 1) * cap]

        first_lane = ((pl.program_id(0) * g + k) % SETS_PER_PACK) * N_EXPERTS
        for c0 in range(0, N_EXPERTS * cap, DISPATCH_ROWS):
            src = lax.broadcasted_iota(jnp.int32, (LANES, DISPATCH_ROWS), 0)
            col = lax.broadcasted_iota(jnp.int32, (LANES, DISPATCH_ROWS), 1) + c0
            spread = jnp.where(src == first_lane + lax.shift_right_logical(col, shift), 1.0, 0.0).astype(BF16)
            slot_x = _dot(slot_ref[0], spread)
            aff_x = _dot(affb_ref[0], spread)
            want = (lax.broadcasted_iota(jnp.int32, (n, DISPATCH_ROWS), 1) & (cap - 1)).astype(F32)
            gt_o[k * n:(k + 1) * n, c0:c0 + DISPATCH_ROWS] = jnp.where(slot_x == want, aff_x, 0.0).astype(BF16)


def _route(aff, h2, n_sets, set_rows):
    t, d = h2.shape
    n = set_rows
    cap = CAPACITY_FACTOR * n // N_EXPERTS
    assert cap & (cap - 1) == 0 and cap % 16 == 0 and DISPATCH_ROWS % cap == 0
    n_packs = -(-n_sets // SETS_PER_PACK)
    aff16 = aff[:, :N_EXPERTS].reshape(n_sets, n, N_EXPERTS)
    aff16 = jnp.pad(aff16, ((0, n_packs * SETS_PER_PACK - n_sets), (0, 0), (0, 0)))
    packed = aff16.reshape(n_packs, SETS_PER_PACK, n, N_EXPERTS).transpose(0, 2, 1, 3).reshape(n_packs, n, LANES)
    pack_blk = lambda p: (p, 0, 0)
    slot, slot_t, affb = pl.pallas_call(
        functools.partial(_select_kernel, cap=cap),
        grid=(n_packs,),
        in_specs=[pl.BlockSpec((1, n, LANES), pack_blk)],
        out_specs=[pl.BlockSpec((1, n, LANES), pack_blk), pl.BlockSpec((1, LANES, n), pack_blk),
                   pl.BlockSpec((1, n, LANES), pack_blk)],
        out_shape=[jax.ShapeDtypeStruct((n_packs, n, LANES), BF16),
                   jax.ShapeDtypeStruct((n_packs, LANES, n), F32),
                   jax.ShapeDtypeStruct((n_packs, n, LANES), BF16)],
        name=f"select_cap{cap}",
        compiler_params=_cparams(1),
    )(packed)
    g = _sets_per_step(n_sets, n)
    per_pack = SETS_PER_PACK // g
    of_step = lambda s: (s // per_pack, 0, 0)
    return pl.pallas_call(
        functools.partial(_dispatch_kernel, cap=cap),
        grid=(n_sets // g,),
        in_specs=[pl.BlockSpec((1, n, LANES), of_step),
                  pl.BlockSpec((1, n, LANES), of_step),
                  pl.BlockSpec((1, g * N_EXPERTS, n), lambda s: (s // per_pack, s % per_pack, 0)),
                  pl.BlockSpec((g * n, d), lambda s: (s, 0))],
        out_specs=[pl.BlockSpec((N_EXPERTS, g * cap, d), lambda s: (0, s, 0)),
                   pl.BlockSpec((g * n, N_EXPERTS * cap), lambda s: (s, 0))],
        out_shape=[jax.ShapeDtypeStruct((N_EXPERTS, n_sets * cap, d), BF16),
                   jax.ShapeDtypeStruct((t, N_EXPERTS * cap), BF16)],
        name=f"dispatch_cap{cap}",
        compiler_params=_cparams(1),
    )(slot, affb, slot_t, h2)


def _expert_kernel(*refs, n_groups):
    x_refs = refs[:n_groups]
    wg_ref, wu_ref, wd_ref = refs[n_groups:n_groups + 3]
    o_refs = refs[n_groups + 3:2 * n_groups + 3]
    wg_s, wu_s, wd_s = refs[2 * n_groups + 3:]
    wg_s[...] = wg_ref[0].astype(BF16)
    wu_s[...] = wu_ref[0].astype(BF16)
    wd_s[...] = wd_ref[0].astype(BF16)
    sub = EXPERT_ROWS
    for x_ref, o_ref in zip(x_refs, o_refs):
        for r0 in range(0, x_ref.shape[1], sub):
            x = x_ref[0, r0:r0 + sub, :]
            a = _dot(x, wg_s[...])
            u = _dot(x, wu_s[...])
            hmid = (a * jax.nn.sigmoid(a) * u).astype(BF16)
            o_ref[0, r0:r0 + sub, :] = _dot(hmid, wd_s[...]).astype(BF16)


def _experts(xes, w_gate, w_up, w_down):
    e, d, f = w_gate.shape
    blk = lambda i: (i, 0, 0)
    return pl.pallas_call(
        functools.partial(_expert_kernel, n_groups=len(xes)),
        grid=(e,),
        in_specs=[pl.BlockSpec((1, x.shape[1], d), blk) for x in xes]
        + [pl.BlockSpec((1, d, f), blk), pl.BlockSpec((1, d, f), blk), pl.BlockSpec((1, f, d), blk)],
        out_specs=[pl.BlockSpec((1, x.shape[1], d), blk) for x in xes],
        out_shape=[jax.ShapeDtypeStruct(x.shape, BF16) for x in xes],
        scratch_shapes=[pltpu.VMEM((d, f), BF16), pltpu.VMEM((d, f), BF16), pltpu.VMEM((f, d), BF16)],
        name="experts",
        compiler_params=_cparams(1),
    )(*xes, w_gate, w_up, w_down)


def _combine_kernel(gt_ref, ye_ref, x1_ref, mod_ref, gf_ref, y_o, *, cap):
    e, gcap, d = ye_ref.shape
    g = gcap // cap
    n = x1_ref.shape[0] // g
    for k in range(g):
        rows = slice(k * n, (k + 1) * n)
        ye = ye_ref[:, k * cap:(k + 1) * cap, :].reshape(e * cap, d)
        moe = _dot(gt_ref[rows, :], ye)
        x = x1_ref[rows, :] + mod_ref[0][5:6] * moe
        y_o[rows, :] = _rms(x, gf_ref[...])


def _sets_per_step(n_sets, set_rows):
    g = max(1, min(SETS_PER_PACK, ROW_TILE * 2 // set_rows))
    return g if n_sets % g == 0 else 1


def _combine(gt, ye, x1, mod, n_sets, set_rows, g_final):
    t, d = x1.shape
    cap = ye.shape[1] // n_sets
    g = _sets_per_step(n_sets, set_rows) if mod.shape[0] == 1 else 1
    tm = min(ROW_TILE, set_rows) if g == 1 else g * set_rows
    tiles = max(1, set_rows // tm)
    return pl.pallas_call(
        functools.partial(_combine_kernel, cap=cap),
        grid=(n_sets // g, tiles),
        in_specs=[pl.BlockSpec((tm, gt.shape[1]), lambda s, i: (s * tiles + i, 0)),
                  pl.BlockSpec((N_EXPERTS, g * cap, d), lambda s, i: (0, s, 0)),
                  pl.BlockSpec((tm, d), lambda s, i: (s * tiles + i, 0)),
                  pl.BlockSpec((1, 8, d), (lambda s, i: (s, 0, 0)) if mod.shape[0] > 1 else (lambda s, i: (0, 0, 0))),
                  pl.BlockSpec((1, d), lambda s, i: (0, 0))],
        out_specs=pl.BlockSpec((tm, d), lambda s, i: (s * tiles + i, 0)),
        out_shape=jax.ShapeDtypeStruct((t, d), F32),
        name=f"combine_cap{cap}",
        compiler_params=_cparams(2),
    )(gt, ye, x1, mod, g_final)


def _rope_tables(n_tokens):
    n_rows = n_tokens // GRID_W
    rowp = jnp.repeat(jnp.arange(n_rows), GRID_W).astype(F32)
    colp = jnp.tile(jnp.arange(GRID_W), n_rows).astype(F32)
    quarter = HEAD_DIM // 4
    freqs = ROPE_THETA ** (-jnp.arange(quarter, dtype=F32) / quarter)
    ang = jnp.stack([rowp[:, None] * freqs, colp[:, None] * freqs], axis=1)
    cos, sin = jnp.cos(ang), jnp.sin(ang)
    zero = jnp.zeros_like(sin)
    c = jnp.stack([cos, cos], axis=2).reshape(n_tokens, HEAD_DIM)
    s_up = jnp.stack([-sin, zero], axis=2).reshape(n_tokens, HEAD_DIM)
    s_dn = jnp.stack([zero, sin], axis=2).reshape(n_tokens, HEAD_DIM)
    return tuple(jnp.tile(t, (1, LANES // HEAD_DIM)) for t in (c, s_up, s_dn))


def _mix_and_route(x2, n_sets, set_rows, mod, lw, rope_tabs, cache, emit_cache, lam_init, tq):
    (g_attn, g_ffn, w_in, gq_t, gk_t, lq1, lk1, lq2, lk2, g_diff, w_ba, w_bb, w_out, w_router2) = lw
    outs = _inproj(x2, mod, set_rows, g_attn, w_in, gq_t, gk_t, rope_tabs, emit_cache)
    qa, kta, vta, qb, ktb, vb16 = outs[:6]
    vb16 = vb16.reshape(n_sets, set_rows, vb16.shape[1])
    ks_a, vs_a, ks_b, vs_b = [kta], [vta], [ktb], [vb16]
    if cache is not None:
        ckta, cvta, cktb, cvb = cache
        ks_a, vs_a, ks_b, vs_b = [ckta, kta], [cvta, vta], [cktb, ktb], [cvb, vb16]
    per_step = CONTEXT_SETS_PER_STEP if (tq == set_rows and n_sets % CONTEXT_SETS_PER_STEP == 0) else 1
    ya = _attention("gqa", qa, ks_a, vs_a, set_rows, tq, per_step, (), {"mxu_sums": cache is not None})
    yb = _attention("diff", qb, ks_b, vs_b, set_rows, tq, per_step, (lq1, lk1, lq2, lk2, g_diff),
                    {"lam_init": lam_init, "stack_maps": cache is None})
    x1, h2, aff = _post(x2, ya, yb, mod, set_rows, g_attn, g_ffn, w_in, w_ba, w_bb, w_out, w_router2)
    xe, gt = _route(aff, h2, n_sets, set_rows)
    return x1, xe, gt, outs[6:]


def kernel(x_prompt, x_sample, cache_attn_k, cache_attn_v, cache_diff_k, cache_diff_v, c, c_ctx, w_mod, b_mod,
           g_attn_norm, g_ffn_norm, w_in, g_q_norm, g_k_norm, lambda_q1, lambda_k1, lambda_q2, lambda_k2,
           g_diff_norm, w_branch_a, w_branch_b, w_out, w_router, w_exp_gate, w_exp_up, w_exp_down, g_final):
    batch, seq, d = x_prompt.shape
    dec_batch, dec_seq, _ = x_sample.shape
    depth = w_in.shape[0]
    assert depth == 1, "the final norm is fused into the layer's combine step"
    past = cache_attn_k.shape[2]
    assert w_in.shape[2] == N_QKV + 2 * d

    xp = x_prompt.reshape(batch * seq, d)
    xs = x_sample.reshape(dec_batch * dec_seq, d)
    rope_tabs = _rope_tables(dec_seq)
    c_rows = jnp.concatenate([c, c_ctx[None, :], jnp.zeros((16 - dec_batch - 1, d), F32)], axis=0)
    yp = ys = None
    caches = []
    for l in range(depth):
        lam_init = 0.8 - 0.6 * math.exp(-0.3 * l)
        mod6 = _modulation(c_rows, w_mod[l], b_mod[l])
        mod = jnp.pad(jnp.transpose(mod6, (1, 0, 2)), ((0, 0), (0, 2), (0, 0)))
        mod_lat, mod_ctx = mod[:dec_batch], mod[dec_batch:dec_batch + 1]
        w_in16, w_ba16, w_bb16, w_out16, w_router2 = _round_weights(
            [w_in[l], w_branch_a[l], w_branch_b[l], w_out[l]],
            jnp.pad(w_router[l], ((0, 0), (0, LANES - N_EXPERTS))))
        lw = (g_attn_norm[l][None, :], g_ffn_norm[l][None, :], w_in16,
              jnp.tile(g_q_norm[l], HEADS_A)[None, :], jnp.tile(g_k_norm[l], KV_HEADS_A)[None, :],
              lambda_q1[l], lambda_k1[l], lambda_q2[l], lambda_k2[l], g_diff_norm[l][None, :],
              w_ba16, w_bb16, w_out16, w_router2)
        x1p, xe_p, gt_p, cache_out = _mix_and_route(xp, batch, seq, mod_ctx, lw, None, None, True, lam_init, seq)
        caches.append(cache_out)
        feat_major = lambda a: jnp.moveaxis(a[:, l], 1, -1).reshape(dec_batch, -1, past)
        cache_l = _cache_prep([feat_major(cache_attn_k), feat_major(cache_attn_v), feat_major(cache_diff_k)],
                              cache_diff_v[:, l].reshape(dec_batch, past * HEADS_B, -1))
        x1s, xe_s, gt_s, _ = _mix_and_route(xs, dec_batch, dec_seq, mod_lat, lw, rope_tabs, cache_l, False,
                                            lam_init, LATENT_Q_TILE)
        ye_p, ye_s = _experts([xe_p, xe_s], w_exp_gate[l], w_exp_up[l], w_exp_down[l])
        yp = _combine(gt_p, ye_p, x1p, mod_ctx, batch, seq, g_final[None, :])
        ys = _combine(gt_s, ye_s, x1s, mod_lat, dec_batch, dec_seq, g_final[None, :])
    y_prompt = yp.reshape(batch, seq, d)
    y_sample = ys.reshape(dec_batch, dec_seq, d)
    tok_major = lambda a, dims: jnp.moveaxis(a.reshape((batch,) + dims + (seq,)), -1, 1)
    new_attn_k = jnp.stack([tok_major(cc[0], (KV_HEADS_A, HEAD_DIM)) for cc in caches], axis=1)
    new_attn_v = jnp.stack([tok_major(cc[1], (KV_HEADS_A, HEAD_DIM)) for cc in caches], axis=1)
    new_diff_k = jnp.stack([tok_major(cc[2], (HEADS_B, 2, HEAD_DIM)) for cc in caches], axis=1)
    new_diff_v = jnp.stack([cc[3].reshape(batch, seq, HEADS_B, 2 * HEAD_DIM) for cc in caches], axis=1)
    return (y_prompt, y_sample, new_attn_k, new_attn_v, new_diff_k, new_diff_v)
```

```python
import functools
import math

import numpy as np
import jax
import jax.numpy as jnp
from jax import lax
from jax.experimental import pallas as pl
from jax.experimental.pallas import tpu as pltpu

F32 = jnp.float32
BF16 = jnp.bfloat16

HEAD_DIM = 64
HEADS_A = 8
KV_HEADS_A = 2
HEADS_B = 4
N_QKV = HEADS_A * HEAD_DIM + 2 * KV_HEADS_A * HEAD_DIM + 3 * HEADS_B * 2 * HEAD_DIM
N_EXPERTS = 16
CAPACITY_FACTOR = 2
GRID_W = 64
ROPE_THETA = 10000.0
EPS = 1e-6
LANES = 128
ROW_TILE = 1024
POST_ROWS = 1024
CHAIN_ROWS = 256
EXPERT_ROWS = 256
LATENT_Q_TILE = 512
CONTEXT_SETS_PER_STEP = 4
SETS_PER_PACK = LANES // N_EXPERTS
DISPATCH_ROWS = 512
NEG_BIG = -1e30
NOT_SELECTED = -1.0
SLOT_RADIX = 256.0
VMEM_LIMIT = 56 * 1024 * 1024


def _cparams(n_axes):
    return pltpu.CompilerParams(dimension_semantics=("arbitrary",) * n_axes,
                                vmem_limit_bytes=VMEM_LIMIT)


def _dot(a, b):
    return jnp.dot(a, b, preferred_element_type=F32)


def _dot_nt(a, b):
    return lax.dot_general(a, b, (((1,), (1,)), ((), ())), preferred_element_type=F32)


def _split(a):
    hi = a.astype(BF16)
    lo = (a - hi.astype(F32)).astype(BF16)
    return hi, lo


def _dot3(a, b):
    a_hi, a_lo = _split(a)
    b_hi, b_lo = _split(b)
    return _dot(a_hi, b_hi) + _dot(a_lo, b_hi) + _dot(a_hi, b_lo)


def _rms(x, g):
    return x * lax.rsqrt(jnp.mean(x * x, axis=-1, keepdims=True) + EPS) * g


def _mod_index(mod, tiles_per_set):
    if mod.shape[0] == 1:
        return lambda i: (0, 0, 0)
    return lambda i: (i // tiles_per_set, 0, 0)


def _mod_kernel(c_ref, w_ref, b_ref, o_ref):
    c = c_ref[...]
    a = c * jax.nn.sigmoid(c)
    o_ref[0] = _dot3(a, w_ref[...]) + b_ref[0]


def _modulation(c_rows, w_mod, b_mod):
    r, d = c_rows.shape
    return pl.pallas_call(
        _mod_kernel,
        grid=(6,),
        in_specs=[pl.BlockSpec((r, d), lambda j: (0, 0)),
                  pl.BlockSpec((d, d), lambda j: (0, j)),
                  pl.BlockSpec((1, 1, d), lambda j: (j, 0, 0))],
        out_specs=pl.BlockSpec((1, r, d), lambda j: (j, 0, 0)),
        out_shape=jax.ShapeDtypeStruct((6, r, d), F32),
        name="mod",
        compiler_params=_cparams(1),
    )(c_rows, w_mod, b_mod.reshape(6, 1, d))


def _round_weights_kernel(*refs):
    n = len(refs) // 2
    for src, out in zip(refs[:n - 1], refs[n:-1]):
        out[...] = src[...].astype(BF16)
    hi, lo = _split(refs[n - 1][...])
    refs[-1][...] = jnp.concatenate([hi, lo], axis=1)


def _round_weights(weights, w_router_p):
    steps = 8
    arrays = list(weights) + [w_router_p]
    blk = lambda a, cols: pl.BlockSpec((a.shape[0] // steps, cols), lambda i: (i, 0))
    out_cols = [a.shape[1] for a in weights] + [2 * w_router_p.shape[1]]
    return pl.pallas_call(
        _round_weights_kernel,
        grid=(steps,),
        in_specs=[blk(a, a.shape[1]) for a in arrays],
        out_specs=[blk(a, c) for a, c in zip(arrays, out_cols)],
        out_shape=[jax.ShapeDtypeStruct((a.shape[0], c), BF16) for a, c in zip(arrays, out_cols)],
        name="round_weights",
        compiler_params=_cparams(1),
    )(*arrays)


def _seg_sumsq(x, ones_blockdiag):
    return _dot((x * x).astype(BF16), ones_blockdiag)


def _rope(x, c, s_up, s_dn):
    w = x.shape[1]
    reps = w // c.shape[1]
    if reps > 1:
        c, s_up, s_dn = (jnp.concatenate([t] * reps, axis=1) for t in (c, s_up, s_dn))
    return x * c + pltpu.roll(x, w - 16, 1) * s_up + pltpu.roll(x, 16, 1) * s_dn


def _inproj_kernel(*refs, rope, emit_cache):
    (x_ref, mod_ref, g_ref, w_ref, gq_ref, gk_ref, bd512_ref, bd128_ref), refs = refs[:8], refs[8:]
    if rope:
        (c_ref, su_ref, sd_ref), refs = refs[:3], refs[3:]
    qa_o, kta_o, vta_o, qb_o, ktb_o, vb_o = refs[:6]
    if emit_cache:
        ka_c, va_c, kb_c, vb_c = refs[6:10]

    mod = mod_ref[0]
    wa = HEADS_A * HEAD_DIM
    wkv = KV_HEADS_A * HEAD_DIM
    wb = HEADS_B * 2 * HEAD_DIM
    o_ka, o_va, o_qb = wa, wa + wkv, wa + 2 * wkv
    o_kb, o_vb = o_qb + wb, o_qb + 2 * wb
    scale = HEAD_DIM ** -0.5 * math.log2(math.e)

    x = x_ref[...]
    h = (_rms(x, g_ref[...]) * (1.0 + mod[1:2]) + mod[0:1]).astype(BF16)
    if rope:
        tabs = (c_ref[...], su_ref[...], sd_ref[...])

    qa = _dot(h, w_ref[:, 0:wa])
    qa = qa * lax.rsqrt(_seg_sumsq(qa, bd512_ref[...]) * (1.0 / HEAD_DIM) + EPS) * gq_ref[...]
    if rope:
        qa = _rope(qa, *tabs)
    qa_o[...] = (qa * scale).astype(BF16)

    kv = _dot(h, w_ref[:, o_ka:o_qb])
    ka, va = kv[:, 0:wkv], kv[:, wkv:2 * wkv]
    ka = ka * lax.rsqrt(_seg_sumsq(ka, bd128_ref[...]) * (1.0 / HEAD_DIM) + EPS) * gk_ref[...]

    def put_feat(val, out_bf16, out_f32):
        rows = out_bf16.shape[2]
        for s in range(out_bf16.shape[0]):
            t = val[s * rows:(s + 1) * rows].T
            if out_f32 is not None:
                out_f32[s] = t
            out_bf16[s] = t.astype(BF16)

    if rope:
        put_feat(_rope(ka, *tabs), kta_o, None)
    else:
        put_feat(ka, kta_o, ka_c if emit_cache else None)
    put_feat(va, vta_o, va_c if emit_cache else None)

    qb = _dot(h, w_ref[:, o_qb:o_kb])
    if rope:
        qb = _rope(qb, *tabs)
    qb_o[...] = (qb * scale).astype(BF16)

    kb = _dot(h, w_ref[:, o_kb:o_vb])
    if rope:
        put_feat(_rope(kb, *tabs), ktb_o, None)
    else:
        put_feat(kb, ktb_o, kb_c if emit_cache else None)

    vb = _dot(h, w_ref[:, o_vb:o_vb + wb])
    if emit_cache:
        for hd in range(HEADS_B):
            vb_c[pl.ds(hd, vb.shape[0], stride=HEADS_B), :] = vb[:, hd * 128:(hd + 1) * 128]
    vb_o[...] = vb.astype(BF16)


def _blockdiag_ones(width):
    g = np.arange(width) // HEAD_DIM
    return jnp.asarray((g[:, None] == g[None, :]).astype(np.float32), dtype=BF16)


def _inproj(x2, mod, set_rows, g_attn, w_in, gq_t, gk_t, rope_tabs, emit_cache):
    t, d = x2.shape
    tm = ROW_TILE
    assert tm % set_rows == 0 or set_rows % tm == 0
    tiles_per_set = max(1, set_rows // tm)
    sets_per_tile = max(1, tm // set_rows)
    rope = rope_tabs is not None
    assert not (rope and emit_cache), "cached keys are the position-free ones"
    nq = N_QKV
    row = lambda i: (i, 0)
    const = lambda i: (0, 0)
    in_specs = [pl.BlockSpec((tm, d), row),
                pl.BlockSpec((1, 8, d), _mod_index(mod, tiles_per_set)),
                pl.BlockSpec((1, d), const),
                pl.BlockSpec((d, nq), const, pipeline_mode=pl.Buffered(1)),
                pl.BlockSpec((1, 512), const),
                pl.BlockSpec((1, 128), const),
                pl.BlockSpec((512, 512), const),
                pl.BlockSpec((128, 128), const)]
    args = [x2, mod, g_attn, w_in, gq_t, gk_t, _blockdiag_ones(512), _blockdiag_ones(128)]
    if rope:
        in_specs += [pl.BlockSpec((tm, LANES), lambda i: (i % tiles_per_set, 0))] * 3
        args += list(rope_tabs)
    n_sets = t // set_rows
    wkv, wb = KV_HEADS_A * HEAD_DIM, HEADS_B * 2 * HEAD_DIM
    outs = [("tok", 512, BF16), ("feat", wkv, BF16), ("feat", wkv, BF16),
            ("tok", 512, BF16), ("feat", wb, BF16), ("tok", wb, BF16)]
    if emit_cache:
        outs += [("feat", wkv, F32), ("feat", wkv, F32), ("feat", wb, F32), ("tokhead", wb, F32)]
    feat = lambda i: (i // tiles_per_set, 0, i % tiles_per_set)
    feat_rows = min(tm, set_rows)

    def out_block(kind, w):
        if kind == "tok":
            return pl.BlockSpec((tm, w), row)
        if kind == "tokhead":
            return pl.BlockSpec((tm * HEADS_B, w // HEADS_B), row)
        return pl.BlockSpec((sets_per_tile, w, feat_rows), feat)

    def out_array(kind, w, dt):
        shape = {"tok": (t, w), "tokhead": (t * HEADS_B, w // HEADS_B), "feat": (n_sets, w, set_rows)}[kind]
        return jax.ShapeDtypeStruct(shape, dt)

    return pl.pallas_call(
        functools.partial(_inproj_kernel, rope=rope, emit_cache=emit_cache),
        grid=(t // tm,),
        in_specs=in_specs,
        out_specs=[out_block(kind, w) for kind, w, _ in outs],
        out_shape=[out_array(kind, w, dt) for kind, w, dt in outs],
        name="inproj_rope" if rope else "inproj",
        compiler_params=_cparams(1),
    )(*args)


def _cache_prep_kernel(*refs):
    half = len(refs) // 2
    for src, out in zip(refs[:half - 1], refs[half:-1]):
        out[...] = src[...].astype(BF16)
    v_src, v_out = refs[half - 1], refs[-1]
    keys = v_out.shape[1]
    for hd in range(HEADS_B):
        v_out[0, :, hd * 128:(hd + 1) * 128] = v_src[0, pl.ds(hd, keys, stride=HEADS_B), :].astype(BF16)


def _cache_prep(arrays, v_by_head):
    blk = lambda b: (b, 0, 0)
    n_sets, rows, width = v_by_head.shape
    v_shape = (n_sets, rows // HEADS_B, width * HEADS_B)
    return pl.pallas_call(
        _cache_prep_kernel,
        grid=(n_sets,),
        in_specs=[pl.BlockSpec((1,) + a.shape[1:], blk) for a in arrays + [v_by_head]],
        out_specs=[pl.BlockSpec((1,) + a.shape[1:], blk) for a in arrays] + [pl.BlockSpec((1,) + v_shape[1:], blk)],
        out_shape=[jax.ShapeDtypeStruct(a.shape, BF16) for a in arrays] + [jax.ShapeDtypeStruct(v_shape, BF16)],
        name="cache_prep",
        compiler_params=_cparams(1),
    )(*arrays, v_by_head)


def _exp_parts(scores):
    m = scores[0].max(axis=-1, keepdims=True)
    for s in scores[1:]:
        m = jnp.maximum(m, s.max(axis=-1, keepdims=True))
    return [jnp.exp2(s - m) for s in scores]


def _row_sum(parts):
    l = parts[0].sum(axis=-1, keepdims=True)
    for e in parts[1:]:
        l = l + e.sum(axis=-1, keepdims=True)
    return l


def _pad_rows(x, first, ones_row=False):
    if ones_row:
        z = jnp.where(lax.broadcasted_iota(jnp.int32, x.shape, 0) == 0, 1.0, 0.0).astype(x.dtype)
    else:
        z = jnp.zeros_like(x)
    return jnp.concatenate([x, z] if first else [z, x], axis=0)


def _gqa_kernel(*refs, n_src, mxu_sums):
    q_ref = refs[0]
    k_refs = refs[1:1 + n_src]
    v_refs = refs[1 + n_src:1 + 2 * n_src]
    o_ref = refs[1 + 2 * n_src]
    sets_here = k_refs[0].shape[0]
    tq = q_ref.shape[0] // sets_here
    lane = lax.broadcasted_iota(jnp.int32, (2 * tq, LANES), 1)
    lo_half = lane < HEAD_DIM
    for s in range(sets_here):
        rows = slice(s * tq, (s + 1) * tq)
        for g in range(KV_HEADS_A):
            c0 = g * 256
            f0 = g * HEAD_DIM
            q = jnp.concatenate([q_ref[rows, c0:c0 + 128], q_ref[rows, c0 + 128:c0 + 256]], axis=0)
            kts = [k[s, f0:f0 + HEAD_DIM, :] for k in k_refs]
            vts = [v[s, f0:f0 + HEAD_DIM, :] for v in v_refs]
            e_lo = _exp_parts([_dot(q, _pad_rows(kt, True)) for kt in kts])
            e_hi = _exp_parts([_dot(q, _pad_rows(kt, False)) for kt in kts])
            o_lo = o_hi = None
            for e_part, vt in zip(e_lo, vts):
                t = _dot_nt(e_part.astype(BF16), _pad_rows(vt, True, ones_row=mxu_sums))
                o_lo = t if o_lo is None else o_lo + t
            for e_part, vt in zip(e_hi, vts):
                t = _dot_nt(e_part.astype(BF16), _pad_rows(vt, False, ones_row=mxu_sums))
                o_hi = t if o_hi is None else o_hi + t
            if mxu_sums:
                l_lo, l_hi = o_lo[:, HEAD_DIM:HEAD_DIM + 1], o_hi[:, 0:1]
                o = jnp.where(lo_half, o_lo * (1.0 / l_lo), o_hi * (1.0 / l_hi))
            else:
                o = (o_lo + o_hi) * jnp.where(lo_half, 1.0 / _row_sum(e_lo), 1.0 / _row_sum(e_hi))
            o_ref[rows, c0:c0 + 128] = o[0:tq].astype(BF16)
            o_ref[rows, c0 + 128:c0 + 256] = o[tq:2 * tq].astype(BF16)


def _diff_kernel(*refs, n_src, lam_init, stack_maps):
    q_ref = refs[0]
    k_refs = refs[1:1 + n_src]
    v_refs = refs[1 + n_src:1 + 2 * n_src]
    lq1, lk1, lq2, lk2, gd_ref, o_ref = refs[1 + 2 * n_src:]
    lam_all = (jnp.exp(jnp.sum(lq1[...] * lk1[...], axis=-1, keepdims=True))
               - jnp.exp(jnp.sum(lq2[...] * lk2[...], axis=-1, keepdims=True)) + lam_init)
    sets_here = k_refs[0].shape[0]
    tq = q_ref.shape[0] // sets_here
    for s in range(sets_here):
        rows = slice(s * tq, (s + 1) * tq)
        for j in range(HEADS_B):
            lam = lam_all[j:j + 1, :]
            q = q_ref[rows, j * 128:(j + 1) * 128]
            k0s = [_pad_rows(k[s, j * 128:j * 128 + HEAD_DIM, :], True) for k in k_refs]
            k1s = [_pad_rows(k[s, j * 128 + HEAD_DIM:(j + 1) * 128, :], False) for k in k_refs]
            e0 = _exp_parts([_dot(q, k0) for k0 in k0s])
            e1 = _exp_parts([_dot(q, k1) for k1 in k1s])
            o0 = o1 = None
            for a0, a1, v in zip(e0, e1, v_refs):
                val = v[s, :, j * 128:(j + 1) * 128]
                if stack_maps:
                    t = _dot(jnp.concatenate([a0.astype(BF16), a1.astype(BF16)], axis=0), val)
                    t0, t1 = t[0:tq], t[tq:2 * tq]
                else:
                    t0, t1 = _dot(a0.astype(BF16), val), _dot(a1.astype(BF16), val)
                o0 = t0 if o0 is None else o0 + t0
                o1 = t1 if o1 is None else o1 + t1
            o = o0 * (1.0 / _row_sum(e0)) - o1 * (lam / _row_sum(e1))
            o = _rms(o, gd_ref[...]) * (1.0 - lam_init)
            o_ref[rows, j * 128:(j + 1) * 128] = o.astype(BF16)


def _attention(kind, q, ks, vs, set_rows, tq, sets_per_step, extra, extra_kw):
    t, qw = q.shape
    n_sets = t // set_rows
    q_tiles = set_rows // tq
    assert sets_per_step == 1 or q_tiles == 1
    in_specs = [pl.BlockSpec((sets_per_step * tq, qw), lambda b, i: (b * q_tiles + i, 0))]
    for a in list(ks) + list(vs):
        in_specs.append(pl.BlockSpec((sets_per_step,) + a.shape[1:], lambda b, i: (b, 0, 0)))
    for e in extra:
        in_specs.append(pl.BlockSpec(e.shape, lambda b, i: (0, 0)))
    body = functools.partial(_gqa_kernel if kind == "gqa" else _diff_kernel, n_src=len(ks), **extra_kw)
    return pl.pallas_call(
        body,
        grid=(n_sets // sets_per_step, q_tiles),
        in_specs=in_specs,
        out_specs=pl.BlockSpec((sets_per_step * tq, qw), lambda b, i: (b * q_tiles + i, 0)),
        out_shape=jax.ShapeDtypeStruct((t, qw), BF16),
        name=f"{kind}_attn_{len(ks)}src",
        compiler_params=_cparams(2),
    )(q, *ks, *vs, *extra)


def _post_kernel(x_ref, ya_ref, yb_ref, mod_ref, g1_ref, g2_ref, win_ref, wba_ref, wbb_ref, wo_ref, wr2_ref,
                 x1_o, h2_o, aff_o):
    d = x_ref.shape[1]
    g0 = win_ref.shape[1] - 2 * d
    mod = mod_ref[0]
    for r0 in range(0, x_ref.shape[0], CHAIN_ROWS):
        rows = slice(r0, r0 + CHAIN_ROWS)
        x = x_ref[rows, :]
        h = (_rms(x, g1_ref[...]) * (1.0 + mod[1:2]) + mod[0:1]).astype(BF16)
        ga = jax.nn.sigmoid(_dot(h, win_ref[:, g0:g0 + d]))
        merged = ga * _dot(ya_ref[rows, :], wba_ref[...])
        gb = jax.nn.sigmoid(_dot(h, win_ref[:, g0 + d:g0 + 2 * d]))
        merged = merged + gb * _dot(yb_ref[rows, :], wbb_ref[...])
        m = _dot(merged.astype(BF16), wo_ref[...])
        x1 = x + mod[2:3] * m
        x1_o[rows, :] = x1
        h2 = _rms(x1, g2_ref[...]) * (1.0 + mod[4:5]) + mod[3:4]
        h2_o[rows, :] = h2.astype(BF16)
        h2_hi, h2_lo = _split(h2)
        both = _dot(h2_hi, wr2_ref[...])
        logits = both[:, 0:LANES] + both[:, LANES:2 * LANES] + _dot(h2_lo, wr2_ref[:, 0:LANES])
        lane = lax.broadcasted_iota(jnp.int32, logits.shape, 1)
        logits = jnp.where(lane < N_EXPERTS, logits, NEG_BIG)
        e = jnp.exp(logits - logits.max(axis=-1, keepdims=True))
        aff_o[rows, :] = e / e.sum(axis=-1, keepdims=True)


def _post(x2, ya, yb, mod, set_rows, g1, g2, w_in, w_ba, w_bb, w_out, w_router2):
    t, d = x2.shape
    tm = POST_ROWS
    tiles_per_set = max(1, set_rows // tm)
    assert mod.shape[0] == 1 or set_rows % tm == 0
    row = lambda i: (i, 0)
    const = lambda i: (0, 0)
    once = pl.Buffered(1)
    half = w_in.shape[1] // 2
    assert w_in.shape[1] == 2 * half and half % LANES == 0 and half >= 2 * d
    return pl.pallas_call(
        _post_kernel,
        grid=(t // tm,),
        in_specs=[pl.BlockSpec((tm, d), row),
                  pl.BlockSpec((tm, 512), row),
                  pl.BlockSpec((tm, 512), row),
                  pl.BlockSpec((1, 8, d), _mod_index(mod, tiles_per_set)),
                  pl.BlockSpec((1, d), const),
                  pl.BlockSpec((1, d), const),
                  pl.BlockSpec((d, half), lambda i: (0, 1), pipeline_mode=once),
                  pl.BlockSpec(w_ba.shape, const, pipeline_mode=once),
                  pl.BlockSpec(w_bb.shape, const, pipeline_mode=once),
                  pl.BlockSpec(w_out.shape, const, pipeline_mode=once),
                  pl.BlockSpec(w_router2.shape, const, pipeline_mode=once)],
        out_specs=[pl.BlockSpec((tm, d), row), pl.BlockSpec((tm, d), row), pl.BlockSpec((tm, LANES), row)],
        out_shape=[jax.ShapeDtypeStruct((t, d), F32), jax.ShapeDtypeStruct((t, d), BF16),
                   jax.ShapeDtypeStruct((t, LANES), F32)],
        name="post_attn",
        compiler_params=_cparams(1),
    )(x2, ya, yb, mod, g1, g2, w_in, w_ba, w_bb, w_out, w_router2)


def _select_kernel(aff_ref, slot_o, slot_t_o, affb_o, *, cap):
    aff = aff_ref[0]
    n = aff.shape[0]
    capf = float(cap)

    def enough(cand):
        return jnp.sum(jnp.where(aff >= cand, 1.0, 0.0), axis=0, keepdims=True) >= capf

    pw = jnp.ones((1, LANES), F32)
    for k in (64, 32, 16, 8, 4, 2, 1):
        pw = jnp.where(enough(pw * 2.0 ** -(k - 1)), pw, pw * 2.0 ** -k)

    def mantissa_step(_, carry):
        thr, step = carry
        cand = thr + step
        return jnp.where(enough(cand), cand, thr), step * 0.5

    thr, _ = lax.fori_loop(0, 23, mantissa_step, (pw, pw * 0.5))
    above = aff > thr
    tied = aff == thr
    need = capf - jnp.sum(jnp.where(above, 1.0, 0.0), axis=0, keepdims=True)
    r_i = lax.broadcasted_iota(jnp.int32, (n, n), 0)
    c_i = lax.broadcasted_iota(jnp.int32, (n, n), 1)
    before = jnp.where(c_i < r_i, 1.0, 0.0).astype(BF16)
    tie_rank = _dot(before, jnp.where(tied, 1.0, 0.0).astype(BF16))
    sel = above | (tied & (tie_rank < need))
    slot = _dot(before, jnp.where(sel, 1.0, 0.0).astype(BF16))
    slot = jnp.where(sel, slot, NOT_SELECTED)
    slot_o[0] = slot.astype(BF16)
    slot_t_o[0] = slot.T
    affb_o[0] = aff.astype(BF16)


def _dispatch_kernel(slot_ref, affb_ref, slot_t_ref, h2_ref, xe_o, gt_o, *, cap):
    n = slot_ref.shape[1]
    g = h2_ref.shape[0] // n
    per = min(N_EXPERTS, DISPATCH_ROWS // cap)
    shift = cap.bit_length() - 1
    slot_iota = lax.broadcasted_iota(jnp.int32, (cap, n), 0).astype(F32)
    for k in range(g):
        h2 = h2_ref[k * n:(k + 1) * n, :]
        slot_t = slot_t_ref[0, k * N_EXPERTS:(k + 1) * N_EXPERTS, :]

        for e0 in range(0, N_EXPERTS, per):
            onehot = jnp.concatenate(
                [jnp.where(slot_t[e:e + 1, :] == slot_iota, 1.0, 0.0) for e in range(e0, e0 + per)],
                axis=0).astype(BF16)
            rows = _dot(onehot, h2).astype(BF16)
            for j in range(per):
                xe_o[e0 + j, k * cap:(k + 1) * cap, :] = rows[j * cap:(j + 1) * cap]

        first_lane = ((pl.program_id(0) * g + k) % SETS_PER_PACK) * N_EXPERTS
        both = jnp.concatenate([slot_ref[0], affb_ref[0]], axis=1)
        for c0 in range(0, N_EXPERTS * cap, DISPATCH_ROWS):
            src = lax.broadcasted_iota(jnp.int32, (2 * LANES, DISPATCH_ROWS), 0)
            col = lax.broadcasted_iota(jnp.int32, (2 * LANES, DISPATCH_ROWS), 1) + c0
            lane_of_col = first_lane + lax.shift_right_logical(col, shift)
            spread = jnp.where(src == lane_of_col, SLOT_RADIX,
                               jnp.where(src == lane_of_col + LANES, 1.0, 0.0)).astype(BF16)
            want = (lax.broadcasted_iota(jnp.int32, (n, DISPATCH_ROWS), 1) & (cap - 1)).astype(F32)
            rest = _dot(both, spread) - want * SLOT_RADIX
            gate = jnp.where(rest >= 0.0, jnp.where(rest <= 1.0, rest, 0.0), 0.0)
            gt_o[k * n:(k + 1) * n, c0:c0 + DISPATCH_ROWS] = gate.astype(BF16)


def _route(aff, h2, n_sets, set_rows):
    t, d = h2.shape
    n = set_rows
    cap = CAPACITY_FACTOR * n // N_EXPERTS
    assert cap & (cap - 1) == 0 and cap % 16 == 0 and DISPATCH_ROWS % cap == 0
    n_packs = -(-n_sets // SETS_PER_PACK)
    aff16 = aff[:, :N_EXPERTS].reshape(n_sets, n, N_EXPERTS)
    aff16 = jnp.pad(aff16, ((0, n_packs * SETS_PER_PACK - n_sets), (0, 0), (0, 0)))
    packed = aff16.reshape(n_packs, SETS_PER_PACK, n, N_EXPERTS).transpose(0, 2, 1, 3).reshape(n_packs, n, LANES)
    pack_blk = lambda p: (p, 0, 0)
    slot, slot_t, affb = pl.pallas_call(
        functools.partial(_select_kernel, cap=cap),
        grid=(n_packs,),
        in_specs=[pl.BlockSpec((1, n, LANES), pack_blk)],
        out_specs=[pl.BlockSpec((1, n, LANES), pack_blk), pl.BlockSpec((1, LANES, n), pack_blk),
                   pl.BlockSpec((1, n, LANES), pack_blk)],
        out_shape=[jax.ShapeDtypeStruct((n_packs, n, LANES), BF16),
                   jax.ShapeDtypeStruct((n_packs, LANES, n), F32),
                   jax.ShapeDtypeStruct((n_packs, n, LANES), BF16)],
        name=f"select_cap{cap}",
        compiler_params=_cparams(1),
    )(packed)
    g = _sets_per_step(n_sets, n)
    per_pack = SETS_PER_PACK // g
    of_step = lambda s: (s // per_pack, 0, 0)
    return pl.pallas_call(
        functools.partial(_dispatch_kernel, cap=cap),
        grid=(n_sets // g,),
        in_specs=[pl.BlockSpec((1, n, LANES), of_step),
                  pl.BlockSpec((1, n, LANES), of_step),
                  pl.BlockSpec((1, g * N_EXPERTS, n), lambda s: (s // per_pack, s % per_pack, 0)),
                  pl.BlockSpec((g * n, d), lambda s: (s, 0))],
        out_specs=[pl.BlockSpec((N_EXPERTS, g * cap, d), lambda s: (0, s, 0)),
                   pl.BlockSpec((g * n, N_EXPERTS * cap), lambda s: (s, 0))],
        out_shape=[jax.ShapeDtypeStruct((N_EXPERTS, n_sets * cap, d), BF16),
                   jax.ShapeDtypeStruct((t, N_EXPERTS * cap), BF16)],
        name=f"dispatch_cap{cap}",
        compiler_params=_cparams(1),
    )(slot, affb, slot_t, h2)


def _expert_kernel(*refs, n_groups):
    x_refs = refs[:n_groups]
    wg_ref, wu_ref, wd_ref = refs[n_groups:n_groups + 3]
    o_refs = refs[n_groups + 3:2 * n_groups + 3]
    wg_s, wu_s, wd_s = refs[2 * n_groups + 3:]
    wg_s[...] = wg_ref[0].astype(BF16)
    wu_s[...] = wu_ref[0].astype(BF16)
    wd_s[...] = wd_ref[0].astype(BF16)
    sub = EXPERT_ROWS
    for x_ref, o_ref in zip(x_refs, o_refs):
        for r0 in range(0, x_ref.shape[1], sub):
            x = x_ref[0, r0:r0 + sub, :]
            a = _dot(x, wg_s[...])
            u = _dot(x, wu_s[...])
            hmid = (a * jax.nn.sigmoid(a) * u).astype(BF16)
            o_ref[0, r0:r0 + sub, :] = _dot(hmid, wd_s[...]).astype(BF16)


def _experts(xes, w_gate, w_up, w_down):
    e, d, f = w_gate.shape
    blk = lambda i: (i, 0, 0)
    return pl.pallas_call(
        functools.partial(_expert_kernel, n_groups=len(xes)),
        grid=(e,),
        in_specs=[pl.BlockSpec((1, x.shape[1], d), blk) for x in xes]
        + [pl.BlockSpec((1, d, f), blk), pl.BlockSpec((1, d, f), blk), pl.BlockSpec((1, f, d), blk)],
        out_specs=[pl.BlockSpec((1, x.shape[1], d), blk) for x in xes],
        out_shape=[jax.ShapeDtypeStruct(x.shape, BF16) for x in xes],
        scratch_shapes=[pltpu.VMEM((d, f), BF16), pltpu.VMEM((d, f), BF16), pltpu.VMEM((f, d), BF16)],
        name="experts",
        compiler_params=_cparams(1),
    )(*xes, w_gate, w_up, w_down)


def _combine_kernel(gt_ref, ye_ref, x1_ref, mod_ref, gf_ref, y_o, *, cap):
    e, gcap, d = ye_ref.shape
    g = gcap // cap
    n = x1_ref.shape[0] // g
    for k in range(g):
        rows = slice(k * n, (k + 1) * n)
        ye = ye_ref[:, k * cap:(k + 1) * cap, :].reshape(e * cap, d)
        moe = _dot(gt_ref[rows, :], ye)
        x = x1_ref[rows, :] + mod_ref[0][5:6] * moe
        y_o[rows, :] = _rms(x, gf_ref[...])


def _sets_per_step(n_sets, set_rows):
    g = max(1, min(SETS_PER_PACK, ROW_TILE * 2 // set_rows))
    return g if n_sets % g == 0 else 1


def _combine(gt, ye, x1, mod, n_sets, set_rows, g_final):
    t, d = x1.shape
    cap = ye.shape[1] // n_sets
    g = _sets_per_step(n_sets, set_rows) if mod.shape[0] == 1 else 1
    tm = min(ROW_TILE, set_rows) if g == 1 else g * set_rows
    tiles = max(1, set_rows // tm)
    return pl.pallas_call(
        functools.partial(_combine_kernel, cap=cap),
        grid=(n_sets // g, tiles),
        in_specs=[pl.BlockSpec((tm, gt.shape[1]), lambda s, i: (s * tiles + i, 0)),
                  pl.BlockSpec((N_EXPERTS, g * cap, d), lambda s, i: (0, s, 0)),
                  pl.BlockSpec((tm, d), lambda s, i: (s * tiles + i, 0)),
                  pl.BlockSpec((1, 8, d), (lambda s, i: (s, 0, 0)) if mod.shape[0] > 1 else (lambda s, i: (0, 0, 0))),
                  pl.BlockSpec((1, d), lambda s, i: (0, 0))],
        out_specs=pl.BlockSpec((tm, d), lambda s, i: (s * tiles + i, 0)),
        out_shape=jax.ShapeDtypeStruct((t, d), F32),
        name=f"combine_cap{cap}",
        compiler_params=_cparams(2),
    )(gt, ye, x1, mod, g_final)


def _rope_tables(n_tokens):
    n_rows = n_tokens // GRID_W
    rowp = jnp.repeat(jnp.arange(n_rows), GRID_W).astype(F32)
    colp = jnp.tile(jnp.arange(GRID_W), n_rows).astype(F32)
    quarter = HEAD_DIM // 4
    freqs = ROPE_THETA ** (-jnp.arange(quarter, dtype=F32) / quarter)
    ang = jnp.stack([rowp[:, None] * freqs, colp[:, None] * freqs], axis=1)
    cos, sin = jnp.cos(ang), jnp.sin(ang)
    zero = jnp.zeros_like(sin)
    c = jnp.stack([cos, cos], axis=2).reshape(n_tokens, HEAD_DIM)
    s_up = jnp.stack([-sin, zero], axis=2).reshape(n_tokens, HEAD_DIM)
    s_dn = jnp.stack([zero, sin], axis=2).reshape(n_tokens, HEAD_DIM)
    return tuple(jnp.tile(t, (1, LANES // HEAD_DIM)) for t in (c, s_up, s_dn))


def _mix_and_route(x2, n_sets, set_rows, mod, lw, rope_tabs, cache, emit_cache, lam_init, tq):
    (g_attn, g_ffn, w_in, gq_t, gk_t, lq1, lk1, lq2, lk2, g_diff, w_ba, w_bb, w_out, w_router2) = lw
    outs = _inproj(x2, mod, set_rows, g_attn, w_in, gq_t, gk_t, rope_tabs, emit_cache)
    qa, kta, vta, qb, ktb, vb16 = outs[:6]
    vb16 = vb16.reshape(n_sets, set_rows, vb16.shape[1])
    ks_a, vs_a, ks_b, vs_b = [kta], [vta], [ktb], [vb16]
    if cache is not None:
        ckta, cvta, cktb, cvb = cache
        ks_a, vs_a, ks_b, vs_b = [ckta, kta], [cvta, vta], [cktb, ktb], [cvb, vb16]
    per_step = CONTEXT_SETS_PER_STEP if (tq == set_rows and n_sets % CONTEXT_SETS_PER_STEP == 0) else 1
    ya = _attention("gqa", qa, ks_a, vs_a, set_rows, tq, per_step, (), {"mxu_sums": cache is not None})
    yb = _attention("diff", qb, ks_b, vs_b, set_rows, tq, per_step, (lq1, lk1, lq2, lk2, g_diff),
                    {"lam_init": lam_init, "stack_maps": cache is None})
    x1, h2, aff = _post(x2, ya, yb, mod, set_rows, g_attn, g_ffn, w_in, w_ba, w_bb, w_out, w_router2)
    xe, gt = _route(aff, h2, n_sets, set_rows)
    return x1, xe, gt, outs[6:]


def kernel(x_prompt, x_sample, cache_attn_k, cache_attn_v, cache_diff_k, cache_diff_v, c, c_ctx, w_mod, b_mod,
           g_attn_norm, g_ffn_norm, w_in, g_q_norm, g_k_norm, lambda_q1, lambda_k1, lambda_q2, lambda_k2,
           g_diff_norm, w_branch_a, w_branch_b, w_out, w_router, w_exp_gate, w_exp_up, w_exp_down, g_final):
    batch, seq, d = x_prompt.shape
    dec_batch, dec_seq, _ = x_sample.shape
    depth = w_in.shape[0]
    assert depth == 1, "the final norm is fused into the layer's combine step"
    past = cache_attn_k.shape[2]
    assert w_in.shape[2] == N_QKV + 2 * d

    xp = x_prompt.reshape(batch * seq, d)
    xs = x_sample.reshape(dec_batch * dec_seq, d)
    rope_tabs = _rope_tables(dec_seq)
    c_rows = jnp.concatenate([c, c_ctx[None, :], jnp.zeros((16 - dec_batch - 1, d), F32)], axis=0)
    yp = ys = None
    caches = []
    for l in range(depth):
        lam_init = 0.8 - 0.6 * math.exp(-0.3 * l)
        mod6 = _modulation(c_rows, w_mod[l], b_mod[l])
        mod = jnp.pad(jnp.transpose(mod6, (1, 0, 2)), ((0, 0), (0, 2), (0, 0)))
        mod_lat, mod_ctx = mod[:dec_batch], mod[dec_batch:dec_batch + 1]
        w_in16, w_ba16, w_bb16, w_out16, w_router2 = _round_weights(
            [w_in[l], w_branch_a[l], w_branch_b[l], w_out[l]],
            jnp.pad(w_router[l], ((0, 0), (0, LANES - N_EXPERTS))))
        lw = (g_attn_norm[l][None, :], g_ffn_norm[l][None, :], w_in16,
              jnp.tile(g_q_norm[l], HEADS_A)[None, :], jnp.tile(g_k_norm[l], KV_HEADS_A)[None, :],
              lambda_q1[l], lambda_k1[l], lambda_q2[l], lambda_k2[l], g_diff_norm[l][None, :],
              w_ba16, w_bb16, w_out16, w_router2)
        x1p, xe_p, gt_p, cache_out = _mix_and_route(xp, batch, seq, mod_ctx, lw, None, None, True, lam_init, seq)
        caches.append(cache_out)
        feat_major = lambda a: jnp.moveaxis(a[:, l], 1, -1).reshape(dec_batch, -1, past)
        cache_l = _cache_prep([feat_major(cache_attn_k), feat_major(cache_attn_v), feat_major(cache_diff_k)],
                              cache_diff_v[:, l].reshape(dec_batch, past * HEADS_B, -1))
        x1s, xe_s, gt_s, _ = _mix_and_route(xs, dec_batch, dec_seq, mod_lat, lw, rope_tabs, cache_l, False,
                                            lam_init, LATENT_Q_TILE)
        ye_p, ye_s = _experts([xe_p, xe_s], w_exp_gate[l], w_exp_up[l], w_exp_down[l])
        yp = _combine(gt_p, ye_p, x1p, mod_ctx, batch, seq, g_final[None, :])
        ys = _combine(gt_s, ye_s, x1s, mod_lat, dec_batch, dec_seq, g_final[None, :])
    y_prompt = yp.reshape(batch, seq, d)
    y_sample = ys.reshape(dec_batch, dec_seq, d)
    tok_major = lambda a, dims: jnp.moveaxis(a.reshape((batch,) + dims + (seq,)), -1, 1)
    new_attn_k = jnp.stack([tok_major(cc[0], (KV_HEADS_A, HEAD_DIM)) for cc in caches], axis=1)
    new_attn_v = jnp.stack([tok_major(cc[1], (KV_HEADS_A, HEAD_DIM)) for cc in caches], axis=1)
    new_diff_k = jnp.stack([tok_major(cc[2], (HEADS_B, 2, HEAD_DIM)) for cc in caches], axis=1)
    new_diff_v = jnp.stack([cc[3].reshape(batch, seq, HEADS_B, 2 * HEAD_DIM) for cc in caches], axis=1)
    return (y_prompt, y_sample, new_attn_k, new_attn_v, new_diff_k, new_diff_v)
```

```python
import functools
import math

import numpy as np
import jax
import jax.numpy as jnp
from jax import lax
from jax.experimental import pallas as pl
from jax.experimental.pallas import tpu as pltpu

F32 = jnp.float32
BF16 = jnp.bfloat16

HEAD_DIM = 64
HEADS_A = 8
KV_HEADS_A = 2
HEADS_B = 4
N_QKV = HEADS_A * HEAD_DIM + 2 * KV_HEADS_A * HEAD_DIM + 3 * HEADS_B * 2 * HEAD_DIM
N_EXPERTS = 16
CAPACITY_FACTOR = 2
GRID_W = 64
ROPE_THETA = 10000.0
EPS = 1e-6
LANES = 128
ROW_TILE = 1024
POST_ROWS = 1024
CHAIN_ROWS = 512
EXPERT_ROWS = 256
LATENT_Q_TILE = 512
CONTEXT_SETS_PER_STEP = 4
SETS_PER_PACK = LANES // N_EXPERTS
DISPATCH_ROWS = 512
NEG_BIG = -1e30
NOT_SELECTED = -1.0
SLOT_RADIX = 256.0
VMEM_LIMIT = 56 * 1024 * 1024
ROUND_STEPS = 8


def _cparams(n_axes):
    return pltpu.CompilerParams(dimension_semantics=("arbitrary",) * n_axes,
                                vmem_limit_bytes=VMEM_LIMIT)


def _dot(a, b):
    return jnp.dot(a, b, preferred_element_type=F32)


def _dot_nt(a, b):
    return lax.dot_general(a, b, (((1,), (1,)), ((), ())), preferred_element_type=F32)


def _split(a):
    hi = a.astype(BF16)
    lo = (a - hi.astype(F32)).astype(BF16)
    return hi, lo


def _dot3(a, b):
    a_hi, a_lo = _split(a)
    b_hi, b_lo = _split(b)
    return _dot(a_hi, b_hi) + _dot(a_lo, b_hi) + _dot(a_hi, b_lo)


def _rms(x, g):
    return x * lax.rsqrt(jnp.mean(x * x, axis=-1, keepdims=True) + EPS) * g


def _mod_index(mod, tiles_per_set):
    if mod.shape[0] == 1:
        return lambda i: (0, 0, 0)
    return lambda i: (i // tiles_per_set, 0, 0)


def _mod_kernel(c_ref, w_ref, b_ref, o_ref):
    c = c_ref[...]
    a = c * jax.nn.sigmoid(c)
    o_ref[0] = _dot3(a, w_ref[...]) + b_ref[0]


def _modulation(c_rows, w_mod, b_mod):
    r, d = c_rows.shape
    return pl.pallas_call(
        _mod_kernel,
        grid=(6,),
        in_specs=[pl.BlockSpec((r, d), lambda j: (0, 0)),
                  pl.BlockSpec((d, d), lambda j: (0, j)),
                  pl.BlockSpec((1, 1, d), lambda j: (j, 0, 0))],
        out_specs=pl.BlockSpec((1, r, d), lambda j: (j, 0, 0)),
        out_shape=jax.ShapeDtypeStruct((6, r, d), F32),
        name="mod",
        compiler_params=_cparams(1),
    )(c_rows, w_mod, b_mod.reshape(6, 1, d))


def _round_weights_kernel(*refs):
    n = len(refs) // 2
    for src, out in zip(refs[:n - 1], refs[n:-1]):
        out[...] = src[...].astype(BF16)
    hi, lo = _split(refs[n - 1][...])
    refs[-1][...] = jnp.concatenate([hi, lo], axis=1)


def _round_weights(weights, w_router_p):
    steps = ROUND_STEPS
    arrays = list(weights) + [w_router_p]
    blk = lambda a, cols: pl.BlockSpec((a.shape[0] // steps, cols), lambda i: (i, 0))
    out_cols = [a.shape[1] for a in weights] + [2 * w_router_p.shape[1]]
    return pl.pallas_call(
        _round_weights_kernel,
        grid=(steps,),
        in_specs=[blk(a, a.shape[1]) for a in arrays],
        out_specs=[blk(a, c) for a, c in zip(arrays, out_cols)],
        out_shape=[jax.ShapeDtypeStruct((a.shape[0], c), BF16) for a, c in zip(arrays, out_cols)],
        name="round_weights",
        compiler_params=_cparams(1),
    )(*arrays)


def _seg_sumsq(x, ones_blockdiag):
    return _dot((x * x).astype(BF16), ones_blockdiag)


def _rope(x, c, s_up, s_dn):
    w = x.shape[1]
    reps = w // c.shape[1]
    if reps > 1:
        c, s_up, s_dn = (jnp.concatenate([t] * reps, axis=1) for t in (c, s_up, s_dn))
    return x * c + pltpu.roll(x, w - 16, 1) * s_up + pltpu.roll(x, 16, 1) * s_dn


def _inproj_kernel(*refs, rope, emit_cache):
    (x_ref, mod_ref, g_ref, w_ref, gq_ref, gk_ref, bd512_ref, bd128_ref), refs = refs[:8], refs[8:]
    if rope:
        (c_ref, su_ref, sd_ref), refs = refs[:3], refs[3:]
    qa_o, kta_o, vta_o, qb_o, ktb_o, vb_o = refs[:6]
    if emit_cache:
        ka_c, va_c, kb_c, vb_c = refs[6:10]

    mod = mod_ref[0]
    wa = HEADS_A * HEAD_DIM
    wkv = KV_HEADS_A * HEAD_DIM
    wb = HEADS_B * 2 * HEAD_DIM
    o_ka, o_va, o_qb = wa, wa + wkv, wa + 2 * wkv
    o_kb, o_vb = o_qb + wb, o_qb + 2 * wb
    scale = HEAD_DIM ** -0.5 * math.log2(math.e)

    x = x_ref[...]
    h = (_rms(x, g_ref[...]) * (1.0 + mod[1:2]) + mod[0:1]).astype(BF16)
    if rope:
        tabs = (c_ref[...], su_ref[...], sd_ref[...])

    qa = _dot(h, w_ref[:, 0:wa])
    qa = qa * lax.rsqrt(_seg_sumsq(qa, bd512_ref[...]) * (1.0 / HEAD_DIM) + EPS) * gq_ref[...]
    if rope:
        qa = _rope(qa, *tabs)
    qa_o[...] = (qa * scale).astype(BF16)

    kv = _dot(h, w_ref[:, o_ka:o_qb])
    ka, va = kv[:, 0:wkv], kv[:, wkv:2 * wkv]
    ka = ka * lax.rsqrt(_seg_sumsq(ka, bd128_ref[...]) * (1.0 / HEAD_DIM) + EPS) * gk_ref[...]

    def put_feat(val, out_bf16, out_f32):
        rows = out_bf16.shape[2]
        for s in range(out_bf16.shape[0]):
            t = val[s * rows:(s + 1) * rows].T
            if out_f32 is not None:
                out_f32[s] = t
            out_bf16[s] = t.astype(BF16)

    if rope:
        put_feat(_rope(ka, *tabs), kta_o, None)
    else:
        put_feat(ka, kta_o, ka_c if emit_cache else None)
    put_feat(va, vta_o, va_c if emit_cache else None)

    qb = _dot(h, w_ref[:, o_qb:o_kb])
    if rope:
        qb = _rope(qb, *tabs)
    qb_o[...] = (qb * scale).astype(BF16)

    kb = _dot(h, w_ref[:, o_kb:o_vb])
    if rope:
        put_feat(_rope(kb, *tabs), ktb_o, None)
    else:
        put_feat(kb, ktb_o, kb_c if emit_cache else None)

    vb = _dot(h, w_ref[:, o_vb:o_vb + wb])
    if emit_cache:
        for hd in range(HEADS_B):
            vb_c[pl.ds(hd, vb.shape[0], stride=HEADS_B), :] = vb[:, hd * 128:(hd + 1) * 128]
    vb_o[...] = vb.astype(BF16)


def _blockdiag_ones(width):
    g = np.arange(width) // HEAD_DIM
    return jnp.asarray((g[:, None] == g[None, :]).astype(np.float32), dtype=BF16)


def _inproj(x2, mod, set_rows, g_attn, w_in, gq_t, gk_t, rope_tabs, emit_cache):
    t, d = x2.shape
    tm = ROW_TILE
    assert tm % set_rows == 0 or set_rows % tm == 0
    tiles_per_set = max(1, set_rows // tm)
    sets_per_tile = max(1, tm // set_rows)
    rope = rope_tabs is not None
    assert not (rope and emit_cache), "cached keys are the position-free ones"
    nq = N_QKV
    row = lambda i: (i, 0)
    const = lambda i: (0, 0)
    in_specs = [pl.BlockSpec((tm, d), row),
                pl.BlockSpec((1, 8, d), _mod_index(mod, tiles_per_set)),
                pl.BlockSpec((1, d), const),
                pl.BlockSpec((d, nq), const, pipeline_mode=pl.Buffered(1)),
                pl.BlockSpec((1, 512), const),
                pl.BlockSpec((1, 128), const),
                pl.BlockSpec((512, 512), const),
                pl.BlockSpec((128, 128), const)]
    args = [x2, mod, g_attn, w_in, gq_t, gk_t, _blockdiag_ones(512), _blockdiag_ones(128)]
    if rope:
        in_specs += [pl.BlockSpec((tm, LANES), lambda i: (i % tiles_per_set, 0))] * 3
        args += list(rope_tabs)
    n_sets = t // set_rows
    wkv, wb = KV_HEADS_A * HEAD_DIM, HEADS_B * 2 * HEAD_DIM
    outs = [("tok", 512, BF16), ("feat", wkv, BF16), ("feat", wkv, BF16),
            ("tok", 512, BF16), ("feat", wb, BF16), ("tok", wb, BF16)]
    if emit_cache:
        outs += [("feat", wkv, F32), ("feat", wkv, F32), ("feat", wb, F32), ("tokhead", wb, F32)]
    feat = lambda i: (i // tiles_per_set, 0, i % tiles_per_set)
    feat_rows = min(tm, set_rows)

    def out_block(kind, w):
        if kind == "tok":
            return pl.BlockSpec((tm, w), row)
        if kind == "tokhead":
            return pl.BlockSpec((tm * HEADS_B, w // HEADS_B), row)
        return pl.BlockSpec((sets_per_tile, w, feat_rows), feat)

    def out_array(kind, w, dt):
        shape = {"tok": (t, w), "tokhead": (t * HEADS_B, w // HEADS_B), "feat": (n_sets, w, set_rows)}[kind]
        return jax.ShapeDtypeStruct(shape, dt)

    return pl.pallas_call(
        functools.partial(_inproj_kernel, rope=rope, emit_cache=emit_cache),
        grid=(t // tm,),
        in_specs=in_specs,
        out_specs=[out_block(kind, w) for kind, w, _ in outs],
        out_shape=[out_array(kind, w, dt) for kind, w, dt in outs],
        name="inproj_rope" if rope else "inproj",
        compiler_params=_cparams(1),
    )(*args)


def _cache_prep_kernel(*refs):
    half = len(refs) // 2
    for src, out in zip(refs[:half - 1], refs[half:-1]):
        out[...] = src[...].astype(BF16)
    v_src, v_out = refs[half - 1], refs[-1]
    keys = v_out.shape[1]
    for hd in range(HEADS_B):
        v_out[0, :, hd * 128:(hd + 1) * 128] = v_src[0, pl.ds(hd, keys, stride=HEADS_B), :].astype(BF16)


def _cache_prep(arrays, v_by_head):
    blk = lambda b: (b, 0, 0)
    n_sets, rows, width = v_by_head.shape
    v_shape = (n_sets, rows // HEADS_B, width * HEADS_B)
    return pl.pallas_call(
        _cache_prep_kernel,
        grid=(n_sets,),
        in_specs=[pl.BlockSpec((1,) + a.shape[1:], blk) for a in arrays + [v_by_head]],
        out_specs=[pl.BlockSpec((1,) + a.shape[1:], blk) for a in arrays] + [pl.BlockSpec((1,) + v_shape[1:], blk)],
        out_shape=[jax.ShapeDtypeStruct(a.shape, BF16) for a in arrays] + [jax.ShapeDtypeStruct(v_shape, BF16)],
        name="cache_prep",
        compiler_params=_cparams(1),
    )(*arrays, v_by_head)


def _exp_parts(scores):
    m = scores[0].max(axis=-1, keepdims=True)
    for s in scores[1:]:
        m = jnp.maximum(m, s.max(axis=-1, keepdims=True))
    return [jnp.exp2(s - m) for s in scores]


def _row_sum(parts):
    l = parts[0].sum(axis=-1, keepdims=True)
    for e in parts[1:]:
        l = l + e.sum(axis=-1, keepdims=True)
    return l


def _pad_rows(x, first, ones_row=False):
    if ones_row:
        z = jnp.where(lax.broadcasted_iota(jnp.int32, x.shape, 0) == 0, 1.0, 0.0).astype(x.dtype)
    else:
        z = jnp.zeros_like(x)
    return jnp.concatenate([x, z] if first else [z, x], axis=0)


def _gqa_kernel(*refs, n_src, mxu_sums):
    q_ref = refs[0]
    k_refs = refs[1:1 + n_src]
    v_refs = refs[1 + n_src:1 + 2 * n_src]
    o_ref = refs[1 + 2 * n_src]
    sets_here = k_refs[0].shape[0]
    tq = q_ref.shape[0] // sets_here
    lane = lax.broadcasted_iota(jnp.int32, (2 * tq, LANES), 1)
    lo_half = lane < HEAD_DIM
    for s in range(sets_here):
        rows = slice(s * tq, (s + 1) * tq)
        for g in range(KV_HEADS_A):
            c0 = g * 256
            f0 = g * HEAD_DIM
            q = jnp.concatenate([q_ref[rows, c0:c0 + 128], q_ref[rows, c0 + 128:c0 + 256]], axis=0)
            kts = [k[s, f0:f0 + HEAD_DIM, :] for k in k_refs]
            vts = [v[s, f0:f0 + HEAD_DIM, :] for v in v_refs]
            e_lo = _exp_parts([_dot(q, _pad_rows(kt, True)) for kt in kts])
            e_hi = _exp_parts([_dot(q, _pad_rows(kt, False)) for kt in kts])
            o_lo = o_hi = None
            for e_part, vt in zip(e_lo, vts):
                t = _dot_nt(e_part.astype(BF16), _pad_rows(vt, True, ones_row=mxu_sums))
                o_lo = t if o_lo is None else o_lo + t
            for e_part, vt in zip(e_hi, vts):
                t = _dot_nt(e_part.astype(BF16), _pad_rows(vt, False, ones_row=mxu_sums))
                o_hi = t if o_hi is None else o_hi + t
            if mxu_sums:
                l_lo, l_hi = o_lo[:, HEAD_DIM:HEAD_DIM + 1], o_hi[:, 0:1]
                o = jnp.where(lo_half, o_lo * (1.0 / l_lo), o_hi * (1.0 / l_hi))
            else:
                o = (o_lo + o_hi) * jnp.where(lo_half, 1.0 / _row_sum(e_lo), 1.0 / _row_sum(e_hi))
            o_ref[rows, c0:c0 + 128] = o[0:tq].astype(BF16)
            o_ref[rows, c0 + 128:c0 + 256] = o[tq:2 * tq].astype(BF16)


def _diff_kernel(*refs, n_src, lam_init, stack_maps):
    q_ref = refs[0]
    k_refs = refs[1:1 + n_src]
    v_refs = refs[1 + n_src:1 + 2 * n_src]
    lq1, lk1, lq2, lk2, gd_ref, o_ref = refs[1 + 2 * n_src:]
    lam_all = (jnp.exp(jnp.sum(lq1[...] * lk1[...], axis=-1, keepdims=True))
               - jnp.exp(jnp.sum(lq2[...] * lk2[...], axis=-1, keepdims=True)) + lam_init)
    sets_here = k_refs[0].shape[0]
    tq = q_ref.shape[0] // sets_here
    for s in range(sets_here):
        rows = slice(s * tq, (s + 1) * tq)
        for j in range(HEADS_B):
            lam = lam_all[j:j + 1, :]
            q = q_ref[rows, j * 128:(j + 1) * 128]
            k0s = [_pad_rows(k[s, j * 128:j * 128 + HEAD_DIM, :], True) for k in k_refs]
            k1s = [_pad_rows(k[s, j * 128 + HEAD_DIM:(j + 1) * 128, :], False) for k in k_refs]
            e0 = _exp_parts([_dot(q, k0) for k0 in k0s])
            e1 = _exp_parts([_dot(q, k1) for k1 in k1s])
            o0 = o1 = None
            for a0, a1, v in zip(e0, e1, v_refs):
                val = v[s, :, j * 128:(j + 1) * 128]
                if stack_maps:
                    t = _dot(jnp.concatenate([a0.astype(BF16), a1.astype(BF16)], axis=0), val)
                    t0, t1 = t[0:tq], t[tq:2 * tq]
                else:
                    t0, t1 = _dot(a0.astype(BF16), val), _dot(a1.astype(BF16), val)
                o0 = t0 if o0 is None else o0 + t0
                o1 = t1 if o1 is None else o1 + t1
            o = o0 * (1.0 / _row_sum(e0)) - o1 * (lam / _row_sum(e1))
            o = _rms(o, gd_ref[...]) * (1.0 - lam_init)
            o_ref[rows, j * 128:(j + 1) * 128] = o.astype(BF16)


def _attention(kind, q, ks, vs, set_rows, tq, sets_per_step, extra, extra_kw):
    t, qw = q.shape
    n_sets = t // set_rows
    q_tiles = set_rows // tq
    assert sets_per_step == 1 or q_tiles == 1
    in_specs = [pl.BlockSpec((sets_per_step * tq, qw), lambda b, i: (b * q_tiles + i, 0))]
    for a in list(ks) + list(vs):
        in_specs.append(pl.BlockSpec((sets_per_step,) + a.shape[1:], lambda b, i: (b, 0, 0)))
    for e in extra:
        in_specs.append(pl.BlockSpec(e.shape, lambda b, i: (0, 0)))
    body = functools.partial(_gqa_kernel if kind == "gqa" else _diff_kernel, n_src=len(ks), **extra_kw)
    return pl.pallas_call(
        body,
        grid=(n_sets // sets_per_step, q_tiles),
        in_specs=in_specs,
        out_specs=pl.BlockSpec((sets_per_step * tq, qw), lambda b, i: (b * q_tiles + i, 0)),
        out_shape=jax.ShapeDtypeStruct((t, qw), BF16),
        name=f"{kind}_attn_{len(ks)}src",
        compiler_params=_cparams(2),
    )(q, *ks, *vs, *extra)


def _post_kernel(x_ref, ya_ref, yb_ref, mod_ref, g1_ref, g2_ref, win_ref, wba_ref, wbb_ref, wo_ref, wr2_ref,
                 x1_o, h2_o, aff_o):
    d = x_ref.shape[1]
    g0 = win_ref.shape[1] - 2 * d
    mod = mod_ref[0]
    for r0 in range(0, x_ref.shape[0], CHAIN_ROWS):
        rows = slice(r0, r0 + CHAIN_ROWS)
        x = x_ref[rows, :]
        h = (_rms(x, g1_ref[...]) * (1.0 + mod[1:2]) + mod[0:1]).astype(BF16)
        ga = jax.nn.sigmoid(_dot(h, win_ref[:, g0:g0 + d]))
        merged = ga * _dot(ya_ref[rows, :], wba_ref[...])
        gb = jax.nn.sigmoid(_dot(h, win_ref[:, g0 + d:g0 + 2 * d]))
        merged = merged + gb * _dot(yb_ref[rows, :], wbb_ref[...])
        m = _dot(merged.astype(BF16), wo_ref[...])
        x1 = x + mod[2:3] * m
        x1_o[rows, :] = x1
        h2 = _rms(x1, g2_ref[...]) * (1.0 + mod[4:5]) + mod[3:4]
        h2_o[rows, :] = h2.astype(BF16)
        h2_hi, h2_lo = _split(h2)
        both = _dot(h2_hi, wr2_ref[...])
        logits = both[:, 0:LANES] + both[:, LANES:2 * LANES] + _dot(h2_lo, wr2_ref[:, 0:LANES])
        lane = lax.broadcasted_iota(jnp.int32, logits.shape, 1)
        logits = jnp.where(lane < N_EXPERTS, logits, NEG_BIG)
        e = jnp.exp(logits - logits.max(axis=-1, keepdims=True))
        aff_o[rows, :] = e / e.sum(axis=-1, keepdims=True)


def _post(x2, ya, yb, mod, set_rows, g1, g2, w_in, w_ba, w_bb, w_out, w_router2):
    t, d = x2.shape
    tm = POST_ROWS
    tiles_per_set = max(1, set_rows // tm)
    assert mod.shape[0] == 1 or set_rows % tm == 0
    row = lambda i: (i, 0)
    const = lambda i: (0, 0)
    once = pl.Buffered(1)
    half = w_in.shape[1] // 2
    assert w_in.shape[1] == 2 * half and half % LANES == 0 and half >= 2 * d
    return pl.pallas_call(
        _post_kernel,
        grid=(t // tm,),
        in_specs=[pl.BlockSpec((tm, d), row),
                  pl.BlockSpec((tm, 512), row),
                  pl.BlockSpec((tm, 512), row),
                  pl.BlockSpec((1, 8, d), _mod_index(mod, tiles_per_set)),
                  pl.BlockSpec((1, d), const),
                  pl.BlockSpec((1, d), const),
                  pl.BlockSpec((d, half), lambda i: (0, 1), pipeline_mode=once),
                  pl.BlockSpec(w_ba.shape, const, pipeline_mode=once),
                  pl.BlockSpec(w_bb.shape, const, pipeline_mode=once),
                  pl.BlockSpec(w_out.shape, const, pipeline_mode=once),
                  pl.BlockSpec(w_router2.shape, const, pipeline_mode=once)],
        out_specs=[pl.BlockSpec((tm, d), row), pl.BlockSpec((tm, d), row), pl.BlockSpec((tm, LANES), row)],
        out_shape=[jax.ShapeDtypeStruct((t, d), F32), jax.ShapeDtypeStruct((t, d), BF16),
                   jax.ShapeDtypeStruct((t, LANES), F32)],
        name="post_attn",
        compiler_params=_cparams(1),
    )(x2, ya, yb, mod, g1, g2, w_in, w_ba, w_bb, w_out, w_router2)


def _select_kernel(aff_ref, slot_o, slot_t_o, affb_o, *, cap):
    aff = aff_ref[0]
    n = aff.shape[0]
    capf = float(cap)

    def enough(cand):
        return jnp.sum(jnp.where(aff >= cand, 1.0, 0.0), axis=0, keepdims=True) >= capf

    pw = jnp.ones((1, LANES), F32)
    for k in (64, 32, 16, 8, 4, 2, 1):
        pw = jnp.where(enough(pw * 2.0 ** -(k - 1)), pw, pw * 2.0 ** -k)

    def mantissa_step(_, carry):
        thr, step = carry
        cand = thr + step
        return jnp.where(enough(cand), cand, thr), step * 0.5

    thr, _ = lax.fori_loop(0, 23, mantissa_step, (pw, pw * 0.5))
    above = aff > thr
    tied = aff == thr
    need = capf - jnp.sum(jnp.where(above, 1.0, 0.0), axis=0, keepdims=True)
    r_i = lax.broadcasted_iota(jnp.int32, (n, n), 0)
    c_i = lax.broadcasted_iota(jnp.int32, (n, n), 1)
    before = jnp.where(c_i < r_i, 1.0, 0.0).astype(BF16)
    tie_rank = _dot(before, jnp.where(tied, 1.0, 0.0).astype(BF16))
    sel = above | (tied & (tie_rank < need))
    slot = _dot(before, jnp.where(sel, 1.0, 0.0).astype(BF16))
    slot = jnp.where(sel, slot, NOT_SELECTED)
    slot_o[0] = slot.astype(BF16)
    slot_t_o[0] = slot.T
    affb_o[0] = aff.astype(BF16)


def _dispatch_kernel(slot_ref, affb_ref, slot_t_ref, h2_ref, xe_o, gt_o, *, cap):
    n = slot_ref.shape[1]
    g = h2_ref.shape[0] // n
    per = min(N_EXPERTS, DISPATCH_ROWS // cap)
    shift = cap.bit_length() - 1
    slot_iota = lax.broadcasted_iota(jnp.int32, (cap, n), 0).astype(F32)
    for k in range(g):
        h2 = h2_ref[k * n:(k + 1) * n, :]
        slot_t = slot_t_ref[0, k * N_EXPERTS:(k + 1) * N_EXPERTS, :]

        for e0 in range(0, N_EXPERTS, per):
            onehot = jnp.concatenate(
                [jnp.where(slot_t[e:e + 1, :] == slot_iota, 1.0, 0.0) for e in range(e0, e0 + per)],
                axis=0).astype(BF16)
            rows = _dot(onehot, h2).astype(BF16)
            for j in range(per):
                xe_o[e0 + j, k * cap:(k + 1) * cap, :] = rows[j * cap:(j + 1) * cap]

        first_lane = ((pl.program_id(0) * g + k) % SETS_PER_PACK) * N_EXPERTS
        both = jnp.concatenate([slot_ref[0], affb_ref[0]], axis=1)
        for c0 in range(0, N_EXPERTS * cap, DISPATCH_ROWS):
            src = lax.broadcasted_iota(jnp.int32, (2 * LANES, DISPATCH_ROWS), 0)
            col = lax.broadcasted_iota(jnp.int32, (2 * LANES, DISPATCH_ROWS), 1) + c0
            lane_of_col = first_lane + lax.shift_right_logical(col, shift)
            spread = jnp.where(src == lane_of_col, SLOT_RADIX,
                               jnp.where(src == lane_of_col + LANES, 1.0, 0.0)).astype(BF16)
            want = (lax.broadcasted_iota(jnp.int32, (n, DISPATCH_ROWS), 1) & (cap - 1)).astype(F32)
            rest = _dot(both, spread) - want * SLOT_RADIX
            gate = jnp.where(rest >= 0.0, jnp.where(rest <= 1.0, rest, 0.0), 0.0)
            gt_o[k * n:(k + 1) * n, c0:c0 + DISPATCH_ROWS] = gate.astype(BF16)


def _route(aff, h2, n_sets, set_rows):
    t, d = h2.shape
    n = set_rows
    cap = CAPACITY_FACTOR * n // N_EXPERTS
    assert cap & (cap - 1) == 0 and cap % 16 == 0 and DISPATCH_ROWS % cap == 0
    n_packs = -(-n_sets // SETS_PER_PACK)
    aff16 = aff[:, :N_EXPERTS].reshape(n_sets, n, N_EXPERTS)
    aff16 = jnp.pad(aff16, ((0, n_packs * SETS_PER_PACK - n_sets), (0, 0), (0, 0)))
    packed = aff16.reshape(n_packs, SETS_PER_PACK, n, N_EXPERTS).transpose(0, 2, 1, 3).reshape(n_packs, n, LANES)
    pack_blk = lambda p: (p, 0, 0)
    slot, slot_t, affb = pl.pallas_call(
        functools.partial(_select_kernel, cap=cap),
        grid=(n_packs,),
        in_specs=[pl.BlockSpec((1, n, LANES), pack_blk)],
        out_specs=[pl.BlockSpec((1, n, LANES), pack_blk), pl.BlockSpec((1, LANES, n), pack_blk),
                   pl.BlockSpec((1, n, LANES), pack_blk)],
        out_shape=[jax.ShapeDtypeStruct((n_packs, n, LANES), BF16),
                   jax.ShapeDtypeStruct((n_packs, LANES, n), F32),
                   jax.ShapeDtypeStruct((n_packs, n, LANES), BF16)],
        name=f"select_cap{cap}",
        compiler_params=_cparams(1),
    )(packed)
    g = _sets_per_step(n_sets, n)
    per_pack = SETS_PER_PACK // g
    of_step = lambda s: (s // per_pack, 0, 0)
    return pl.pallas_call(
        functools.partial(_dispatch_kernel, cap=cap),
        grid=(n_sets // g,),
        in_specs=[pl.BlockSpec((1, n, LANES), of_step),
                  pl.BlockSpec((1, n, LANES), of_step),
                  pl.BlockSpec((1, g * N_EXPERTS, n), lambda s: (s // per_pack, s % per_pack, 0)),
                  pl.BlockSpec((g * n, d), lambda s: (s, 0))],
        out_specs=[pl.BlockSpec((N_EXPERTS, g * cap, d), lambda s: (0, s, 0)),
                   pl.BlockSpec((g * n, N_EXPERTS * cap), lambda s: (s, 0))],
        out_shape=[jax.ShapeDtypeStruct((N_EXPERTS, n_sets * cap, d), BF16),
                   jax.ShapeDtypeStruct((t, N_EXPERTS * cap), BF16)],
        name=f"dispatch_cap{cap}",
        compiler_params=_cparams(1),
    )(slot, affb, slot_t, h2)


def _expert_kernel(*refs, n_groups):
    x_refs = refs[:n_groups]
    wg_ref, wu_ref, wd_ref = refs[n_groups:n_groups + 3]
    o_refs = refs[n_groups + 3:2 * n_groups + 3]
    wg_s, wu_s, wd_s = refs[2 * n_groups + 3:]
    wg_s[...] = wg_ref[0].astype(BF16)
    wu_s[...] = wu_ref[0].astype(BF16)
    wd_s[...] = wd_ref[0].astype(BF16)
    sub = EXPERT_ROWS
    for x_ref, o_ref in zip(x_refs, o_refs):
        for r0 in range(0, x_ref.shape[1], sub):
            x = x_ref[0, r0:r0 + sub, :]
            a = _dot(x, wg_s[...])
            u = _dot(x, wu_s[...])
            hmid = (a * jax.nn.sigmoid(a) * u).astype(BF16)
            o_ref[0, r0:r0 + sub, :] = _dot(hmid, wd_s[...]).astype(BF16)


def _experts(xes, w_gate, w_up, w_down):
    e, d, f = w_gate.shape
    blk = lambda i: (i, 0, 0)
    return pl.pallas_call(
        functools.partial(_expert_kernel, n_groups=len(xes)),
        grid=(e,),
        in_specs=[pl.BlockSpec((1, x.shape[1], d), blk) for x in xes]
        + [pl.BlockSpec((1, d, f), blk), pl.BlockSpec((1, d, f), blk), pl.BlockSpec((1, f, d), blk)],
        out_specs=[pl.BlockSpec((1, x.shape[1], d), blk) for x in xes],
        out_shape=[jax.ShapeDtypeStruct(x.shape, BF16) for x in xes],
        scratch_shapes=[pltpu.VMEM((d, f), BF16), pltpu.VMEM((d, f), BF16), pltpu.VMEM((f, d), BF16)],
        name="experts",
        compiler_params=_cparams(1),
    )(*xes, w_gate, w_up, w_down)


def _combine_kernel(gt_ref, ye_ref, x1_ref, mod_ref, gf_ref, y_o, *, cap):
    e, gcap, d = ye_ref.shape
    g = gcap // cap
    n = x1_ref.shape[0] // g
    for k in range(g):
        rows = slice(k * n, (k + 1) * n)
        ye = ye_ref[:, k * cap:(k + 1) * cap, :].reshape(e * cap, d)
        moe = _dot(gt_ref[rows, :], ye)
        x = x1_ref[rows, :] + mod_ref[0][5:6] * moe
        y_o[rows, :] = _rms(x, gf_ref[...])


def _sets_per_step(n_sets, set_rows):
    g = max(1, min(SETS_PER_PACK, ROW_TILE * 2 // set_rows))
    return g if n_sets % g == 0 else 1


def _combine(gt, ye, x1, mod, n_sets, set_rows, g_final):
    t, d = x1.shape
    cap = ye.shape[1] // n_sets
    g = _sets_per_step(n_sets, set_rows) if mod.shape[0] == 1 else 1
    tm = min(ROW_TILE, set_rows) if g == 1 else g * set_rows
    tiles = max(1, set_rows // tm)
    return pl.pallas_call(
        functools.partial(_combine_kernel, cap=cap),
        grid=(n_sets // g, tiles),
        in_specs=[pl.BlockSpec((tm, gt.shape[1]), lambda s, i: (s * tiles + i, 0)),
                  pl.BlockSpec((N_EXPERTS, g * cap, d), lambda s, i: (0, s, 0)),
                  pl.BlockSpec((tm, d), lambda s, i: (s * tiles + i, 0)),
                  pl.BlockSpec((1, 8, d), (lambda s, i: (s, 0, 0)) if mod.shape[0] > 1 else (lambda s, i: (0, 0, 0))),
                  pl.BlockSpec((1, d), lambda s, i: (0, 0))],
        out_specs=pl.BlockSpec((tm, d), lambda s, i: (s * tiles + i, 0)),
        out_shape=jax.ShapeDtypeStruct((t, d), F32),
        name=f"combine_cap{cap}",
        compiler_params=_cparams(2),
    )(gt, ye, x1, mod, g_final)


def _rope_tables(n_tokens):
    n_rows = n_tokens // GRID_W
    rowp = jnp.repeat(jnp.arange(n_rows), GRID_W).astype(F32)
    colp = jnp.tile(jnp.arange(GRID_W), n_rows).astype(F32)
    quarter = HEAD_DIM // 4
    freqs = ROPE_THETA ** (-jnp.arange(quarter, dtype=F32) / quarter)
    ang = jnp.stack([rowp[:, None] * freqs, colp[:, None] * freqs], axis=1)
    cos, sin = jnp.cos(ang), jnp.sin(ang)
    zero = jnp.zeros_like(sin)
    c = jnp.stack([cos, cos], axis=2).reshape(n_tokens, HEAD_DIM)
    s_up = jnp.stack([-sin, zero], axis=2).reshape(n_tokens, HEAD_DIM)
    s_dn = jnp.stack([zero, sin], axis=2).reshape(n_tokens, HEAD_DIM)
    return tuple(jnp.tile(t, (1, LANES // HEAD_DIM)) for t in (c, s_up, s_dn))


def _mix_and_route(x2, n_sets, set_rows, mod, lw, rope_tabs, cache, emit_cache, lam_init, tq):
    (g_attn, g_ffn, w_in, gq_t, gk_t, lq1, lk1, lq2, lk2, g_diff, w_ba, w_bb, w_out, w_router2) = lw
    outs = _inproj(x2, mod, set_rows, g_attn, w_in, gq_t, gk_t, rope_tabs, emit_cache)
    qa, kta, vta, qb, ktb, vb16 = outs[:6]
    vb16 = vb16.reshape(n_sets, set_rows, vb16.shape[1])
    ks_a, vs_a, ks_b, vs_b = [kta], [vta], [ktb], [vb16]
    if cache is not None:
        ckta, cvta, cktb, cvb = cache
        ks_a, vs_a, ks_b, vs_b = [ckta, kta], [cvta, vta], [cktb, ktb], [cvb, vb16]
    per_step = CONTEXT_SETS_PER_STEP if (tq == set_rows and n_sets % CONTEXT_SETS_PER_STEP == 0) else 1
    ya = _attention("gqa", qa, ks_a, vs_a, set_rows, tq, per_step, (), {"mxu_sums": cache is not None})
    yb = _attention("diff", qb, ks_b, vs_b, set_rows, tq, per_step, (lq1, lk1, lq2, lk2, g_diff),
                    {"lam_init": lam_init, "stack_maps": cache is None})
    x1, h2, aff = _post(x2, ya, yb, mod, set_rows, g_attn, g_ffn, w_in, w_ba, w_bb, w_out, w_router2)
    xe, gt = _route(aff, h2, n_sets, set_rows)
    return x1, xe, gt, outs[6:]


def kernel(x_prompt, x_sample, cache_attn_k, cache_attn_v, cache_diff_k, cache_diff_v, c, c_ctx, w_mod, b_mod,
           g_attn_norm, g_ffn_norm, w_in, g_q_norm, g_k_norm, lambda_q1, lambda_k1, lambda_q2, lambda_k2,
           g_diff_norm, w_branch_a, w_branch_b, w_out, w_router, w_exp_gate, w_exp_up, w_exp_down, g_final):
    batch, seq, d = x_prompt.shape
    dec_batch, dec_seq, _ = x_sample.shape
    depth = w_in.shape[0]
    assert depth == 1, "the final norm is fused into the layer's combine step"
    past = cache_attn_k.shape[2]
    assert w_in.shape[2] == N_QKV + 2 * d

    xp = x_prompt.reshape(batch * seq, d)
    xs = x_sample.reshape(dec_batch * dec_seq, d)
    rope_tabs = _rope_tables(dec_seq)
    c_rows = jnp.concatenate([c, c_ctx[None, :], jnp.zeros((16 - dec_batch - 1, d), F32)], axis=0)
    yp = ys = None
    caches = []
    for l in range(depth):
        lam_init = 0.8 - 0.6 * math.exp(-0.3 * l)
        mod6 = _modulation(c_rows, w_mod[l], b_mod[l])
        mod = jnp.pad(jnp.transpose(mod6, (1, 0, 2)), ((0, 0), (0, 2), (0, 0)))
        mod_lat, mod_ctx = mod[:dec_batch], mod[dec_batch:dec_batch + 1]
        w_in16, w_ba16, w_bb16, w_out16, w_router2 = _round_weights(
            [w_in[l], w_branch_a[l], w_branch_b[l], w_out[l]],
            jnp.pad(w_router[l], ((0, 0), (0, LANES - N_EXPERTS))))
        lw = (g_attn_norm[l][None, :], g_ffn_norm[l][None, :], w_in16,
              jnp.tile(g_q_norm[l], HEADS_A)[None, :], jnp.tile(g_k_norm[l], KV_HEADS_A)[None, :],
              lambda_q1[l], lambda_k1[l], lambda_q2[l], lambda_k2[l], g_diff_norm[l][None, :],
              w_ba16, w_bb16, w_out16, w_router2)
        x1p, xe_p, gt_p, cache_out = _mix_and_route(xp, batch, seq, mod_ctx, lw, None, None, True, lam_init, seq)
        caches.append(cache_out)
        feat_major = lambda a: jnp.moveaxis(a[:, l], 1, -1).reshape(dec_batch, -1, past)
        cache_l = _cache_prep([feat_major(cache_attn_k), feat_major(cache_attn_v), feat_major(cache_diff_k)],
                              cache_diff_v[:, l].reshape(dec_batch, past * HEADS_B, -1))
        x1s, xe_s, gt_s, _ = _mix_and_route(xs, dec_batch, dec_seq, mod_lat, lw, rope_tabs, cache_l, False,
                                            lam_init, LATENT_Q_TILE)
        ye_p, ye_s = _experts([xe_p, xe_s], w_exp_gate[l], w_exp_up[l], w_exp_down[l])
        yp = _combine(gt_p, ye_p, x1p, mod_ctx, batch, seq, g_final[None, :])
        ys = _combine(gt_s, ye_s, x1s, mod_lat, dec_batch, dec_seq, g_final[None, :])
    y_prompt = yp.reshape(batch, seq, d)
    y_sample = ys.reshape(dec_batch, dec_seq, d)
    tok_major = lambda a, dims: jnp.moveaxis(a.reshape((batch,) + dims + (seq,)), -1, 1)
    new_attn_k = jnp.stack([tok_major(cc[0], (KV_HEADS_A, HEAD_DIM)) for cc in caches], axis=1)
    new_attn_v = jnp.stack([tok_major(cc[1], (KV_HEADS_A, HEAD_DIM)) for cc in caches], axis=1)
    new_diff_k = jnp.stack([tok_major(cc[2], (HEADS_B, 2, HEAD_DIM)) for cc in caches], axis=1)
    new_diff_v = jnp.stack([cc[3].reshape(batch, seq, HEADS_B, 2 * HEAD_DIM) for cc in caches], axis=1)
    return (y_prompt, y_sample, new_attn_k, new_attn_v, new_diff_k, new_diff_v)
```

```python
import functools
import math

import numpy as np
import jax
import jax.numpy as jnp
from jax import lax
from jax.experimental import pallas as pl
from jax.experimental.pallas import tpu as pltpu

F32 = jnp.float32
BF16 = jnp.bfloat16

HEAD_DIM = 64
HEADS_A = 8
KV_HEADS_A = 2
HEADS_B = 4
N_QKV = HEADS_A * HEAD_DIM + 2 * KV_HEADS_A * HEAD_DIM + 3 * HEADS_B * 2 * HEAD_DIM
N_EXPERTS = 16
CAPACITY_FACTOR = 2
GRID_W = 64
ROPE_THETA = 10000.0
EPS = 1e-6
LANES = 128
ROW_TILE = 1024
POST_ROWS = 1024
CHAIN_ROWS = 512
EXPERT_ROWS = 256
LATENT_Q_TILE = 512
CONTEXT_SETS_PER_STEP = 4
SETS_PER_PACK = LANES // N_EXPERTS
DISPATCH_ROWS = 512
NEG_BIG = -1e30
NOT_SELECTED = -1.0
SLOT_RADIX = 256.0
VMEM_LIMIT = 56 * 1024 * 1024
ROUND_STEPS = 8


def _cparams(n_axes):
    return pltpu.CompilerParams(dimension_semantics=("arbitrary",) * n_axes,
                                vmem_limit_bytes=VMEM_LIMIT)


def _dot(a, b):
    return jnp.dot(a, b, preferred_element_type=F32)


def _dot_nt(a, b):
    return lax.dot_general(a, b, (((1,), (1,)), ((), ())), preferred_element_type=F32)


def _split(a):
    hi = a.astype(BF16)
    lo = (a - hi.astype(F32)).astype(BF16)
    return hi, lo


def _dot3(a, b):
    a_hi, a_lo = _split(a)
    b_hi, b_lo = _split(b)
    return _dot(a_hi, b_hi) + _dot(a_lo, b_hi) + _dot(a_hi, b_lo)


def _rms(x, g):
    return x * lax.rsqrt(jnp.mean(x * x, axis=-1, keepdims=True) + EPS) * g


def _mod_index(mod, tiles_per_set):
    if mod.shape[0] == 1:
        return lambda i: (0, 0, 0)
    return lambda i: (i // tiles_per_set, 0, 0)


def _mod_kernel(c_ref, w_ref, b_ref, o_ref):
    c = c_ref[...]
    a = c * jax.nn.sigmoid(c)
    o_ref[0] = _dot3(a, w_ref[...]) + b_ref[0]


def _modulation(c_rows, w_mod, b_mod):
    r, d = c_rows.shape
    return pl.pallas_call(
        _mod_kernel,
        grid=(6,),
        in_specs=[pl.BlockSpec((r, d), lambda j: (0, 0)),
                  pl.BlockSpec((d, d), lambda j: (0, j)),
                  pl.BlockSpec((1, 1, d), lambda j: (j, 0, 0))],
        out_specs=pl.BlockSpec((1, r, d), lambda j: (j, 0, 0)),
        out_shape=jax.ShapeDtypeStruct((6, r, d), F32),
        name="mod",
        compiler_params=_cparams(1),
    )(c_rows, w_mod, b_mod.reshape(6, 1, d))


def _round_weights_kernel(*refs):
    n = len(refs) // 2
    for src, out in zip(refs[:n - 1], refs[n:-1]):
        out[...] = src[...].astype(BF16)
    hi, lo = _split(refs[n - 1][...])
    refs[-1][...] = jnp.concatenate([hi, lo], axis=1)


def _round_weights(weights, w_router_p):
    steps = ROUND_STEPS
    arrays = list(weights) + [w_router_p]
    blk = lambda a, cols: pl.BlockSpec((a.shape[0] // steps, cols), lambda i: (i, 0))
    out_cols = [a.shape[1] for a in weights] + [2 * w_router_p.shape[1]]
    return pl.pallas_call(
        _round_weights_kernel,
        grid=(steps,),
        in_specs=[blk(a, a.shape[1]) for a in arrays],
        out_specs=[blk(a, c) for a, c in zip(arrays, out_cols)],
        out_shape=[jax.ShapeDtypeStruct((a.shape[0], c), BF16) for a, c in zip(arrays, out_cols)],
        name="round_weights",
        compiler_params=_cparams(1),
    )(*arrays)


def _seg_sumsq(x, ones_blockdiag):
    return _dot((x * x).astype(BF16), ones_blockdiag)


def _rope(x, c, s_up, s_dn):
    w = x.shape[1]
    reps = w // c.shape[1]
    if reps > 1:
        c, s_up, s_dn = (jnp.concatenate([t] * reps, axis=1) for t in (c, s_up, s_dn))
    return x * c + pltpu.roll(x, w - 16, 1) * s_up + pltpu.roll(x, 16, 1) * s_dn


def _inproj_kernel(*refs, rope, emit_cache):
    (x_ref, mod_ref, g_ref, w_ref, gq_ref, gk_ref, bd512_ref, bd128_ref), refs = refs[:8], refs[8:]
    if rope:
        (c_ref, su_ref, sd_ref), refs = refs[:3], refs[3:]
    qa_o, kta_o, vta_o, qb_o, ktb_o, vb_o = refs[:6]
    if emit_cache:
        ka_c, va_c, kb_c, vb_c = refs[6:10]

    mod = mod_ref[0]
    wa = HEADS_A * HEAD_DIM
    wkv = KV_HEADS_A * HEAD_DIM
    wb = HEADS_B * 2 * HEAD_DIM
    o_ka, o_va, o_qb = wa, wa + wkv, wa + 2 * wkv
    o_kb, o_vb = o_qb + wb, o_qb + 2 * wb
    scale = HEAD_DIM ** -0.5 * math.log2(math.e)

    x = x_ref[...]
    h = (_rms(x, g_ref[...]) * (1.0 + mod[1:2]) + mod[0:1]).astype(BF16)
    if rope:
        tabs = (c_ref[...], su_ref[...], sd_ref[...])

    qa = _dot(h, w_ref[:, 0:wa])
    qa = qa * lax.rsqrt(_seg_sumsq(qa, bd512_ref[...]) * (1.0 / HEAD_DIM) + EPS) * gq_ref[...]
    if rope:
        qa = _rope(qa, *tabs)
    qa_o[...] = (qa * scale).astype(BF16)

    kv = _dot(h, w_ref[:, o_ka:o_qb])
    ka, va = kv[:, 0:wkv], kv[:, wkv:2 * wkv]
    ka = ka * lax.rsqrt(_seg_sumsq(ka, bd128_ref[...]) * (1.0 / HEAD_DIM) + EPS) * gk_ref[...]

    def put_feat(val, out_bf16, out_f32):
        rows = out_bf16.shape[2]
        for s in range(out_bf16.shape[0]):
            t = val[s * rows:(s + 1) * rows].T
            if out_f32 is not None:
                out_f32[s] = t
            out_bf16[s] = t.astype(BF16)

    if rope:
        put_feat(_rope(ka, *tabs), kta_o, None)
    else:
        put_feat(ka, kta_o, ka_c if emit_cache else None)
    put_feat(va, vta_o, va_c if emit_cache else None)

    qb = _dot(h, w_ref[:, o_qb:o_kb])
    if rope:
        qb = _rope(qb, *tabs)
    qb_o[...] = (qb * scale).astype(BF16)

    kb = _dot(h, w_ref[:, o_kb:o_vb])
    if rope:
        put_feat(_rope(kb, *tabs), ktb_o, None)
    else:
        put_feat(kb, ktb_o, kb_c if emit_cache else None)

    vb = _dot(h, w_ref[:, o_vb:o_vb + wb])
    if emit_cache:
        for hd in range(HEADS_B):
            vb_c[pl.ds(hd, vb.shape[0], stride=HEADS_B), :] = vb[:, hd * 128:(hd + 1) * 128]
    vb_o[...] = vb.astype(BF16)


def _blockdiag_ones(width):
    g = np.arange(width) // HEAD_DIM
    return jnp.asarray((g[:, None] == g[None, :]).astype(np.float32), dtype=BF16)


def _inproj(x2, mod, set_rows, g_attn, w_in, gq_t, gk_t, rope_tabs, emit_cache):
    t, d = x2.shape
    tm = ROW_TILE
    assert t % tm == 0 and (tm % set_rows == 0 or set_rows % tm == 0)
    tiles_per_set = max(1, set_rows // tm)
    sets_per_tile = max(1, tm // set_rows)
    rope = rope_tabs is not None
    assert not (rope and emit_cache), "cached keys are the position-free ones"
    nq = N_QKV
    row = lambda i: (i, 0)
    const = lambda i: (0, 0)
    in_specs = [pl.BlockSpec((tm, d), row),
                pl.BlockSpec((1, 8, d), _mod_index(mod, tiles_per_set)),
                pl.BlockSpec((1, d), const),
                pl.BlockSpec((d, nq), const, pipeline_mode=pl.Buffered(1)),
                pl.BlockSpec((1, 512), const),
                pl.BlockSpec((1, 128), const),
                pl.BlockSpec((512, 512), const),
                pl.BlockSpec((128, 128), const)]
    args = [x2, mod, g_attn, w_in, gq_t, gk_t, _blockdiag_ones(512), _blockdiag_ones(128)]
    if rope:
        in_specs += [pl.BlockSpec((tm, LANES), lambda i: (i % tiles_per_set, 0))] * 3
        args += list(rope_tabs)
    n_sets = t // set_rows
    wkv, wb = KV_HEADS_A * HEAD_DIM, HEADS_B * 2 * HEAD_DIM
    outs = [("tok", 512, BF16), ("feat", wkv, BF16), ("feat", wkv, BF16),
            ("tok", 512, BF16), ("feat", wb, BF16), ("tok", wb, BF16)]
    if emit_cache:
        outs += [("feat", wkv, F32), ("feat", wkv, F32), ("feat", wb, F32), ("tokhead", wb, F32)]
    feat = lambda i: (i // tiles_per_set, 0, i % tiles_per_set)
    feat_rows = min(tm, set_rows)

    def out_block(kind, w):
        if kind == "tok":
            return pl.BlockSpec((tm, w), row)
        if kind == "tokhead":
            return pl.BlockSpec((tm * HEADS_B, w // HEADS_B), row)
        return pl.BlockSpec((sets_per_tile, w, feat_rows), feat)

    def out_array(kind, w, dt):
        shape = {"tok": (t, w), "tokhead": (t * HEADS_B, w // HEADS_B), "feat": (n_sets, w, set_rows)}[kind]
        return jax.ShapeDtypeStruct(shape, dt)

    return pl.pallas_call(
        functools.partial(_inproj_kernel, rope=rope, emit_cache=emit_cache),
        grid=(t // tm,),
        in_specs=in_specs,
        out_specs=[out_block(kind, w) for kind, w, _ in outs],
        out_shape=[out_array(kind, w, dt) for kind, w, dt in outs],
        name="inproj_rope" if rope else "inproj",
        compiler_params=_cparams(1),
    )(*args)


def _cache_prep_kernel(*refs):
    half = len(refs) // 2
    for src, out in zip(refs[:half - 1], refs[half:-1]):
        out[...] = src[...].astype(BF16)
    v_src, v_out = refs[half - 1], refs[-1]
    keys = v_out.shape[1]
    for hd in range(HEADS_B):
        v_out[0, :, hd * 128:(hd + 1) * 128] = v_src[0, pl.ds(hd, keys, stride=HEADS_B), :].astype(BF16)


def _cache_prep(arrays, v_by_head):
    blk = lambda b: (b, 0, 0)
    n_sets, rows, width = v_by_head.shape
    v_shape = (n_sets, rows // HEADS_B, width * HEADS_B)
    return pl.pallas_call(
        _cache_prep_kernel,
        grid=(n_sets,),
        in_specs=[pl.BlockSpec((1,) + a.shape[1:], blk) for a in arrays + [v_by_head]],
        out_specs=[pl.BlockSpec((1,) + a.shape[1:], blk) for a in arrays] + [pl.BlockSpec((1,) + v_shape[1:], blk)],
        out_shape=[jax.ShapeDtypeStruct(a.shape, BF16) for a in arrays] + [jax.ShapeDtypeStruct(v_shape, BF16)],
        name="cache_prep",
        compiler_params=_cparams(1),
    )(*arrays, v_by_head)


def _exp_parts(scores):
    m = scores[0].max(axis=-1, keepdims=True)
    for s in scores[1:]:
        m = jnp.maximum(m, s.max(axis=-1, keepdims=True))
    return [jnp.exp2(s - m) for s in scores]


def _row_sum(parts):
    l = parts[0].sum(axis=-1, keepdims=True)
    for e in parts[1:]:
        l = l + e.sum(axis=-1, keepdims=True)
    return l


def _pad_rows(x, first, ones_row=False):
    if ones_row:
        z = jnp.where(lax.broadcasted_iota(jnp.int32, x.shape, 0) == 0, 1.0, 0.0).astype(x.dtype)
    else:
        z = jnp.zeros_like(x)
    return jnp.concatenate([x, z] if first else [z, x], axis=0)


def _gqa_kernel(*refs, n_src, mxu_sums):
    q_ref = refs[0]
    k_refs = refs[1:1 + n_src]
    v_refs = refs[1 + n_src:1 + 2 * n_src]
    o_ref = refs[1 + 2 * n_src]
    sets_here = k_refs[0].shape[0]
    tq = q_ref.shape[0] // sets_here
    lane = lax.broadcasted_iota(jnp.int32, (2 * tq, LANES), 1)
    lo_half = lane < HEAD_DIM
    for s in range(sets_here):
        rows = slice(s * tq, (s + 1) * tq)
        for g in range(KV_HEADS_A):
            c0 = g * 256
            f0 = g * HEAD_DIM
            q = jnp.concatenate([q_ref[rows, c0:c0 + 128], q_ref[rows, c0 + 128:c0 + 256]], axis=0)
            kts = [k[s, f0:f0 + HEAD_DIM, :] for k in k_refs]
            vts = [v[s, f0:f0 + HEAD_DIM, :] for v in v_refs]
            e_lo = _exp_parts([_dot(q, _pad_rows(kt, True)) for kt in kts])
            e_hi = _exp_parts([_dot(q, _pad_rows(kt, False)) for kt in kts])
            o_lo = o_hi = None
            for e_part, vt in zip(e_lo, vts):
                t = _dot_nt(e_part.astype(BF16), _pad_rows(vt, True, ones_row=mxu_sums))
                o_lo = t if o_lo is None else o_lo + t
            for e_part, vt in zip(e_hi, vts):
                t = _dot_nt(e_part.astype(BF16), _pad_rows(vt, False, ones_row=mxu_sums))
                o_hi = t if o_hi is None else o_hi + t
            if mxu_sums:
                l_lo, l_hi = o_lo[:, HEAD_DIM:HEAD_DIM + 1], o_hi[:, 0:1]
                o = jnp.where(lo_half, o_lo * (1.0 / l_lo), o_hi * (1.0 / l_hi))
            else:
                o = (o_lo + o_hi) * jnp.where(lo_half, 1.0 / _row_sum(e_lo), 1.0 / _row_sum(e_hi))
            o_ref[rows, c0:c0 + 128] = o[0:tq].astype(BF16)
            o_ref[rows, c0 + 128:c0 + 256] = o[tq:2 * tq].astype(BF16)


def _diff_kernel(*refs, n_src, lam_init, stack_maps):
    q_ref = refs[0]
    k_refs = refs[1:1 + n_src]
    v_refs = refs[1 + n_src:1 + 2 * n_src]
    lq1, lk1, lq2, lk2, gd_ref, o_ref = refs[1 + 2 * n_src:]
    lam_all = (jnp.exp(jnp.sum(lq1[...] * lk1[...], axis=-1, keepdims=True))
               - jnp.exp(jnp.sum(lq2[...] * lk2[...], axis=-1, keepdims=True)) + lam_init)
    sets_here = k_refs[0].shape[0]
    tq = q_ref.shape[0] // sets_here
    for s in range(sets_here):
        rows = slice(s * tq, (s + 1) * tq)
        for j in range(HEADS_B):
            lam = lam_all[j:j + 1, :]
            q = q_ref[rows, j * 128:(j + 1) * 128]
            k0s = [_pad_rows(k[s, j * 128:j * 128 + HEAD_DIM, :], True) for k in k_refs]
            k1s = [_pad_rows(k[s, j * 128 + HEAD_DIM:(j + 1) * 128, :], False) for k in k_refs]
            e0 = _exp_parts([_dot(q, k0) for k0 in k0s])
            e1 = _exp_parts([_dot(q, k1) for k1 in k1s])
            o0 = o1 = None
            for a0, a1, v in zip(e0, e1, v_refs):
                val = v[s, :, j * 128:(j + 1) * 128]
                if stack_maps:
                    t = _dot(jnp.concatenate([a0.astype(BF16), a1.astype(BF16)], axis=0), val)
                    t0, t1 = t[0:tq], t[tq:2 * tq]
                else:
                    t0, t1 = _dot(a0.astype(BF16), val), _dot(a1.astype(BF16), val)
                o0 = t0 if o0 is None else o0 + t0
                o1 = t1 if o1 is None else o1 + t1
            o = o0 * (1.0 / _row_sum(e0)) - o1 * (lam / _row_sum(e1))
            o = _rms(o, gd_ref[...]) * (1.0 - lam_init)
            o_ref[rows, j * 128:(j + 1) * 128] = o.astype(BF16)


def _attention(kind, q, ks, vs, set_rows, tq, sets_per_step, extra, extra_kw):
    t, qw = q.shape
    n_sets = t // set_rows
    q_tiles = set_rows // tq
    assert sets_per_step == 1 or q_tiles == 1
    in_specs = [pl.BlockSpec((sets_per_step * tq, qw), lambda b, i: (b * q_tiles + i, 0))]
    for a in list(ks) + list(vs):
        in_specs.append(pl.BlockSpec((sets_per_step,) + a.shape[1:], lambda b, i: (b, 0, 0)))
    for e in extra:
        in_specs.append(pl.BlockSpec(e.shape, lambda b, i: (0, 0)))
    body = functools.partial(_gqa_kernel if kind == "gqa" else _diff_kernel, n_src=len(ks), **extra_kw)
    return pl.pallas_call(
        body,
        grid=(n_sets // sets_per_step, q_tiles),
        in_specs=in_specs,
        out_specs=pl.BlockSpec((sets_per_step * tq, qw), lambda b, i: (b * q_tiles + i, 0)),
        out_shape=jax.ShapeDtypeStruct((t, qw), BF16),
        name=f"{kind}_attn_{len(ks)}src",
        compiler_params=_cparams(2),
    )(q, *ks, *vs, *extra)


def _post_kernel(x_ref, ya_ref, yb_ref, mod_ref, g1_ref, g2_ref, win_ref, wba_ref, wbb_ref, wo_ref, wr2_ref,
                 x1_o, h2_o, aff_o):
    d = x_ref.shape[1]
    g0 = win_ref.shape[1] - 2 * d
    mod = mod_ref[0]
    for r0 in range(0, x_ref.shape[0], CHAIN_ROWS):
        rows = slice(r0, r0 + CHAIN_ROWS)
        x = x_ref[rows, :]
        h = (_rms(x, g1_ref[...]) * (1.0 + mod[1:2]) + mod[0:1]).astype(BF16)
        ga = jax.nn.sigmoid(_dot(h, win_ref[:, g0:g0 + d]))
        merged = ga * _dot(ya_ref[rows, :], wba_ref[...])
        gb = jax.nn.sigmoid(_dot(h, win_ref[:, g0 + d:g0 + 2 * d]))
        merged = merged + gb * _dot(yb_ref[rows, :], wbb_ref[...])
        m = _dot(merged.astype(BF16), wo_ref[...])
        x1 = x + mod[2:3] * m
        x1_o[rows, :] = x1
        h2 = _rms(x1, g2_ref[...]) * (1.0 + mod[4:5]) + mod[3:4]
        h2_o[rows, :] = h2.astype(BF16)
        h2_hi, h2_lo = _split(h2)
        both = _dot(h2_hi, wr2_ref[...])
        logits = both[:, 0:LANES] + both[:, LANES:2 * LANES] + _dot(h2_lo, wr2_ref[:, 0:LANES])
        lane = lax.broadcasted_iota(jnp.int32, logits.shape, 1)
        logits = jnp.where(lane < N_EXPERTS, logits, NEG_BIG)
        e = jnp.exp(logits - logits.max(axis=-1, keepdims=True))
        aff_o[rows, :] = e / e.sum(axis=-1, keepdims=True)


def _post(x2, ya, yb, mod, set_rows, g1, g2, w_in, w_ba, w_bb, w_out, w_router2):
    t, d = x2.shape
    tm = POST_ROWS
    tiles_per_set = max(1, set_rows // tm)
    assert t % tm == 0 and tm % CHAIN_ROWS == 0 and (mod.shape[0] == 1 or set_rows % tm == 0)
    row = lambda i: (i, 0)
    const = lambda i: (0, 0)
    once = pl.Buffered(1)
    half = w_in.shape[1] // 2
    assert w_in.shape[1] == 2 * half and half % LANES == 0 and half >= 2 * d
    return pl.pallas_call(
        _post_kernel,
        grid=(t // tm,),
        in_specs=[pl.BlockSpec((tm, d), row),
                  pl.BlockSpec((tm, 512), row),
                  pl.BlockSpec((tm, 512), row),
                  pl.BlockSpec((1, 8, d), _mod_index(mod, tiles_per_set)),
                  pl.BlockSpec((1, d), const),
                  pl.BlockSpec((1, d), const),
                  pl.BlockSpec((d, half), lambda i: (0, 1), pipeline_mode=once),
                  pl.BlockSpec(w_ba.shape, const, pipeline_mode=once),
                  pl.BlockSpec(w_bb.shape, const, pipeline_mode=once),
                  pl.BlockSpec(w_out.shape, const, pipeline_mode=once),
                  pl.BlockSpec(w_router2.shape, const, pipeline_mode=once)],
        out_specs=[pl.BlockSpec((tm, d), row), pl.BlockSpec((tm, d), row), pl.BlockSpec((tm, LANES), row)],
        out_shape=[jax.ShapeDtypeStruct((t, d), F32), jax.ShapeDtypeStruct((t, d), BF16),
                   jax.ShapeDtypeStruct((t, LANES), F32)],
        name="post_attn",
        compiler_params=_cparams(1),
    )(x2, ya, yb, mod, g1, g2, w_in, w_ba, w_bb, w_out, w_router2)


def _select_kernel(aff_ref, slot_o, slot_t_o, affb_o, *, cap):
    aff = aff_ref[0]
    n = aff.shape[0]
    capf = float(cap)

    def enough(cand):
        return jnp.sum(jnp.where(aff >= cand, 1.0, 0.0), axis=0, keepdims=True) >= capf

    pw = jnp.ones((1, LANES), F32)
    for k in (64, 32, 16, 8, 4, 2, 1):
        pw = jnp.where(enough(pw * 2.0 ** -(k - 1)), pw, pw * 2.0 ** -k)

    def mantissa_step(_, carry):
        thr, step = carry
        cand = thr + step
        return jnp.where(enough(cand), cand, thr), step * 0.5

    thr, _ = lax.fori_loop(0, 23, mantissa_step, (pw, pw * 0.5))
    above = aff > thr
    tied = aff == thr
    need = capf - jnp.sum(jnp.where(above, 1.0, 0.0), axis=0, keepdims=True)
    r_i = lax.broadcasted_iota(jnp.int32, (n, n), 0)
    c_i = lax.broadcasted_iota(jnp.int32, (n, n), 1)
    before = jnp.where(c_i < r_i, 1.0, 0.0).astype(BF16)
    tie_rank = _dot(before, jnp.where(tied, 1.0, 0.0).astype(BF16))
    sel = above | (tied & (tie_rank < need))
    slot = _dot(before, jnp.where(sel, 1.0, 0.0).astype(BF16))
    slot = jnp.where(sel, slot, NOT_SELECTED)
    slot_o[0] = slot.astype(BF16)
    slot_t_o[0] = slot.T
    affb_o[0] = aff.astype(BF16)


def _dispatch_kernel(slot_ref, affb_ref, slot_t_ref, h2_ref, xe_o, gt_o, *, cap):
    n = slot_ref.shape[1]
    g = h2_ref.shape[0] // n
    per = min(N_EXPERTS, DISPATCH_ROWS // cap)
    shift = cap.bit_length() - 1
    slot_iota = lax.broadcasted_iota(jnp.int32, (cap, n), 0).astype(F32)
    for k in range(g):
        h2 = h2_ref[k * n:(k + 1) * n, :]
        slot_t = slot_t_ref[0, k * N_EXPERTS:(k + 1) * N_EXPERTS, :]

        for e0 in range(0, N_EXPERTS, per):
            onehot = jnp.concatenate(
                [jnp.where(slot_t[e:e + 1, :] == slot_iota, 1.0, 0.0) for e in range(e0, e0 + per)],
                axis=0).astype(BF16)
            rows = _dot(onehot, h2).astype(BF16)
            for j in range(per):
                xe_o[e0 + j, k * cap:(k + 1) * cap, :] = rows[j * cap:(j + 1) * cap]

        first_lane = ((pl.program_id(0) * g + k) % SETS_PER_PACK) * N_EXPERTS
        both = jnp.concatenate([slot_ref[0], affb_ref[0]], axis=1)
        for c0 in range(0, N_EXPERTS * cap, DISPATCH_ROWS):
            src = lax.broadcasted_iota(jnp.int32, (2 * LANES, DISPATCH_ROWS), 0)
            col = lax.broadcasted_iota(jnp.int32, (2 * LANES, DISPATCH_ROWS), 1) + c0
            lane_of_col = first_lane + lax.shift_right_logical(col, shift)
            spread = jnp.where(src == lane_of_col, SLOT_RADIX,
                               jnp.where(src == lane_of_col + LANES, 1.0, 0.0)).astype(BF16)
            want = (lax.broadcasted_iota(jnp.int32, (n, DISPATCH_ROWS), 1) & (cap - 1)).astype(F32)
            rest = _dot(both, spread) - want * SLOT_RADIX
            gate = jnp.where(rest >= 0.0, jnp.where(rest <= 1.0, rest, 0.0), 0.0)
            gt_o[k * n:(k + 1) * n, c0:c0 + DISPATCH_ROWS] = gate.astype(BF16)


def _route(aff, h2, n_sets, set_rows):
    t, d = h2.shape
    n = set_rows
    cap = CAPACITY_FACTOR * n // N_EXPERTS
    assert cap & (cap - 1) == 0 and cap % 16 == 0 and DISPATCH_ROWS % cap == 0
    n_packs = -(-n_sets // SETS_PER_PACK)
    aff16 = aff[:, :N_EXPERTS].reshape(n_sets, n, N_EXPERTS)
    aff16 = jnp.pad(aff16, ((0, n_packs * SETS_PER_PACK - n_sets), (0, 0), (0, 0)))
    packed = aff16.reshape(n_packs, SETS_PER_PACK, n, N_EXPERTS).transpose(0, 2, 1, 3).reshape(n_packs, n, LANES)
    pack_blk = lambda p: (p, 0, 0)
    slot, slot_t, affb = pl.pallas_call(
        functools.partial(_select_kernel, cap=cap),
        grid=(n_packs,),
        in_specs=[pl.BlockSpec((1, n, LANES), pack_blk)],
        out_specs=[pl.BlockSpec((1, n, LANES), pack_blk), pl.BlockSpec((1, LANES, n), pack_blk),
                   pl.BlockSpec((1, n, LANES), pack_blk)],
        out_shape=[jax.ShapeDtypeStruct((n_packs, n, LANES), BF16),
                   jax.ShapeDtypeStruct((n_packs, LANES, n), F32),
                   jax.ShapeDtypeStruct((n_packs, n, LANES), BF16)],
        name=f"select_cap{cap}",
        compiler_params=_cparams(1),
    )(packed)
    g = _sets_per_step(n_sets, n)
    per_pack = SETS_PER_PACK // g
    of_step = lambda s: (s // per_pack, 0, 0)
    return pl.pallas_call(
        functools.partial(_dispatch_kernel, cap=cap),
        grid=(n_sets // g,),
        in_specs=[pl.BlockSpec((1, n, LANES), of_step),
                  pl.BlockSpec((1, n, LANES), of_step),
                  pl.BlockSpec((1, g * N_EXPERTS, n), lambda s: (s // per_pack, s % per_pack, 0)),
                  pl.BlockSpec((g * n, d), lambda s: (s, 0))],
        out_specs=[pl.BlockSpec((N_EXPERTS, g * cap, d), lambda s: (0, s, 0)),
                   pl.BlockSpec((g * n, N_EXPERTS * cap), lambda s: (s, 0))],
        out_shape=[jax.ShapeDtypeStruct((N_EXPERTS, n_sets * cap, d), BF16),
                   jax.ShapeDtypeStruct((t, N_EXPERTS * cap), BF16)],
        name=f"dispatch_cap{cap}",
        compiler_params=_cparams(1),
    )(slot, affb, slot_t, h2)


def _expert_kernel(*refs, n_groups):
    x_refs = refs[:n_groups]
    wg_ref, wu_ref, wd_ref = refs[n_groups:n_groups + 3]
    o_refs = refs[n_groups + 3:2 * n_groups + 3]
    wg_s, wu_s, wd_s = refs[2 * n_groups + 3:]
    wg_s[...] = wg_ref[0].astype(BF16)
    wu_s[...] = wu_ref[0].astype(BF16)
    wd_s[...] = wd_ref[0].astype(BF16)
    sub = EXPERT_ROWS
    for x_ref, o_ref in zip(x_refs, o_refs):
        for r0 in range(0, x_ref.shape[1], sub):
            x = x_ref[0, r0:r0 + sub, :]
            a = _dot(x, wg_s[...])
            u = _dot(x, wu_s[...])
            hmid = (a * jax.nn.sigmoid(a) * u).astype(BF16)
            o_ref[0, r0:r0 + sub, :] = _dot(hmid, wd_s[...]).astype(BF16)


def _experts(xes, w_gate, w_up, w_down):
    e, d, f = w_gate.shape
    blk = lambda i: (i, 0, 0)
    return pl.pallas_call(
        functools.partial(_expert_kernel, n_groups=len(xes)),
        grid=(e,),
        in_specs=[pl.BlockSpec((1, x.shape[1], d), blk) for x in xes]
        + [pl.BlockSpec((1, d, f), blk), pl.BlockSpec((1, d, f), blk), pl.BlockSpec((1, f, d), blk)],
        out_specs=[pl.BlockSpec((1, x.shape[1], d), blk) for x in xes],
        out_shape=[jax.ShapeDtypeStruct(x.shape, BF16) for x in xes],
        scratch_shapes=[pltpu.VMEM((d, f), BF16), pltpu.VMEM((d, f), BF16), pltpu.VMEM((f, d), BF16)],
        name="experts",
        compiler_params=_cparams(1),
    )(*xes, w_gate, w_up, w_down)


def _combine_kernel(gt_ref, ye_ref, x1_ref, mod_ref, gf_ref, y_o, *, cap):
    e, gcap, d = ye_ref.shape
    g = gcap // cap
    n = x1_ref.shape[0] // g
    for k in range(g):
        rows = slice(k * n, (k + 1) * n)
        ye = ye_ref[:, k * cap:(k + 1) * cap, :].reshape(e * cap, d)
        moe = _dot(gt_ref[rows, :], ye)
        x = x1_ref[rows, :] + mod_ref[0][5:6] * moe
        y_o[rows, :] = _rms(x, gf_ref[...])


def _sets_per_step(n_sets, set_rows):
    g = max(1, min(SETS_PER_PACK, ROW_TILE * 2 // set_rows))
    return g if n_sets % g == 0 else 1


def _combine(gt, ye, x1, mod, n_sets, set_rows, g_final):
    t, d = x1.shape
    cap = ye.shape[1] // n_sets
    g = _sets_per_step(n_sets, set_rows) if mod.shape[0] == 1 else 1
    tm = min(ROW_TILE, set_rows) if g == 1 else g * set_rows
    tiles = max(1, set_rows // tm)
    return pl.pallas_call(
        functools.partial(_combine_kernel, cap=cap),
        grid=(n_sets // g, tiles),
        in_specs=[pl.BlockSpec((tm, gt.shape[1]), lambda s, i: (s * tiles + i, 0)),
                  pl.BlockSpec((N_EXPERTS, g * cap, d), lambda s, i: (0, s, 0)),
                  pl.BlockSpec((tm, d), lambda s, i: (s * tiles + i, 0)),
                  pl.BlockSpec((1, 8, d), (lambda s, i: (s, 0, 0)) if mod.shape[0] > 1 else (lambda s, i: (0, 0, 0))),
                  pl.BlockSpec((1, d), lambda s, i: (0, 0))],
        out_specs=pl.BlockSpec((tm, d), lambda s, i: (s * tiles + i, 0)),
        out_shape=jax.ShapeDtypeStruct((t, d), F32),
        name=f"combine_cap{cap}",
        compiler_params=_cparams(2),
    )(gt, ye, x1, mod, g_final)


def _rope_tables(n_tokens):
    n_rows = n_tokens // GRID_W
    rowp = jnp.repeat(jnp.arange(n_rows), GRID_W).astype(F32)
    colp = jnp.tile(jnp.arange(GRID_W), n_rows).astype(F32)
    quarter = HEAD_DIM // 4
    freqs = ROPE_THETA ** (-jnp.arange(quarter, dtype=F32) / quarter)
    ang = jnp.stack([rowp[:, None] * freqs, colp[:, None] * freqs], axis=1)
    cos, sin = jnp.cos(ang), jnp.sin(ang)
    zero = jnp.zeros_like(sin)
    c = jnp.stack([cos, cos], axis=2).reshape(n_tokens, HEAD_DIM)
    s_up = jnp.stack([-sin, zero], axis=2).reshape(n_tokens, HEAD_DIM)
    s_dn = jnp.stack([zero, sin], axis=2).reshape(n_tokens, HEAD_DIM)
    return tuple(jnp.tile(t, (1, LANES // HEAD_DIM)) for t in (c, s_up, s_dn))


def _mix_and_route(x2, n_sets, set_rows, mod, lw, rope_tabs, cache, emit_cache, lam_init, tq):
    (g_attn, g_ffn, w_in, gq_t, gk_t, lq1, lk1, lq2, lk2, g_diff, w_ba, w_bb, w_out, w_router2) = lw
    outs = _inproj(x2, mod, set_rows, g_attn, w_in, gq_t, gk_t, rope_tabs, emit_cache)
    qa, kta, vta, qb, ktb, vb16 = outs[:6]
    vb16 = vb16.reshape(n_sets, set_rows, vb16.shape[1])
    ks_a, vs_a, ks_b, vs_b = [kta], [vta], [ktb], [vb16]
    if cache is not None:
        ckta, cvta, cktb, cvb = cache
        ks_a, vs_a, ks_b, vs_b = [ckta, kta], [cvta, vta], [cktb, ktb], [cvb, vb16]
    per_step = CONTEXT_SETS_PER_STEP if (tq == set_rows and n_sets % CONTEXT_SETS_PER_STEP == 0) else 1
    ya = _attention("gqa", qa, ks_a, vs_a, set_rows, tq, per_step, (), {"mxu_sums": cache is not None})
    yb = _attention("diff", qb, ks_b, vs_b, set_rows, tq, per_step, (lq1, lk1, lq2, lk2, g_diff),
                    {"lam_init": lam_init, "stack_maps": cache is None})
    x1, h2, aff = _post(x2, ya, yb, mod, set_rows, g_attn, g_ffn, w_in, w_ba, w_bb, w_out, w_router2)
    xe, gt = _route(aff, h2, n_sets, set_rows)
    return x1, xe, gt, outs[6:]


def kernel(x_prompt, x_sample, cache_attn_k, cache_attn_v, cache_diff_k, cache_diff_v, c, c_ctx, w_mod, b_mod,
           g_attn_norm, g_ffn_norm, w_in, g_q_norm, g_k_norm, lambda_q1, lambda_k1, lambda_q2, lambda_k2,
           g_diff_norm, w_branch_a, w_branch_b, w_out, w_router, w_exp_gate, w_exp_up, w_exp_down, g_final):
    batch, seq, d = x_prompt.shape
    dec_batch, dec_seq, _ = x_sample.shape
    depth = w_in.shape[0]
    assert depth == 1, "the final norm is fused into the layer's combine step"
    past = cache_attn_k.shape[2]
    assert w_in.shape[2] == N_QKV + 2 * d

    xp = x_prompt.reshape(batch * seq, d)
    xs = x_sample.reshape(dec_batch * dec_seq, d)
    rope_tabs = _rope_tables(dec_seq)
    c_rows = jnp.concatenate([c, c_ctx[None, :], jnp.zeros((16 - dec_batch - 1, d), F32)], axis=0)
    yp = ys = None
    caches = []
    for l in range(depth):
        lam_init = 0.8 - 0.6 * math.exp(-0.3 * l)
        mod6 = _modulation(c_rows, w_mod[l], b_mod[l])
        mod = jnp.pad(jnp.transpose(mod6, (1, 0, 2)), ((0, 0), (0, 2), (0, 0)))
        mod_lat, mod_ctx = mod[:dec_batch], mod[dec_batch:dec_batch + 1]
        w_in16, w_ba16, w_bb16, w_out16, w_router2 = _round_weights(
            [w_in[l], w_branch_a[l], w_branch_b[l], w_out[l]],
            jnp.pad(w_router[l], ((0, 0), (0, LANES - N_EXPERTS))))
        lw = (g_attn_norm[l][None, :], g_ffn_norm[l][None, :], w_in16,
              jnp.tile(g_q_norm[l], HEADS_A)[None, :], jnp.tile(g_k_norm[l], KV_HEADS_A)[None, :],
              lambda_q1[l], lambda_k1[l], lambda_q2[l], lambda_k2[l], g_diff_norm[l][None, :],
              w_ba16, w_bb16, w_out16, w_router2)
        x1p, xe_p, gt_p, cache_out = _mix_and_route(xp, batch, seq, mod_ctx, lw, None, None, True, lam_init, seq)
        caches.append(cache_out)
        feat_major = lambda a: jnp.moveaxis(a[:, l], 1, -1).reshape(dec_batch, -1, past)
        cache_l = _cache_prep([feat_major(cache_attn_k), feat_major(cache_attn_v), feat_major(cache_diff_k)],
                              cache_diff_v[:, l].reshape(dec_batch, past * HEADS_B, -1))
        x1s, xe_s, gt_s, _ = _mix_and_route(xs, dec_batch, dec_seq, mod_lat, lw, rope_tabs, cache_l, False,
                                            lam_init, LATENT_Q_TILE)
        ye_p, ye_s = _experts([xe_p, xe_s], w_exp_gate[l], w_exp_up[l], w_exp_down[l])
        yp = _combine(gt_p, ye_p, x1p, mod_ctx, batch, seq, g_final[None, :])
        ys = _combine(gt_s, ye_s, x1s, mod_lat, dec_batch, dec_seq, g_final[None, :])
    y_prompt = yp.reshape(batch, seq, d)
    y_sample = ys.reshape(dec_batch, dec_seq, d)
    tok_major = lambda a, dims: jnp.moveaxis(a.reshape((batch,) + dims + (seq,)), -1, 1)
    new_attn_k = jnp.stack([tok_major(cc[0], (KV_HEADS_A, HEAD_DIM)) for cc in caches], axis=1)
    new_attn_v = jnp.stack([tok_major(cc[1], (KV_HEADS_A, HEAD_DIM)) for cc in caches], axis=1)
    new_diff_k = jnp.stack([tok_major(cc[2], (HEADS_B, 2, HEAD_DIM)) for cc in caches], axis=1)
    new_diff_v = jnp.stack([cc[3].reshape(batch, seq, HEADS_B, 2 * HEAD_DIM) for cc in caches], axis=1)
    return (y_prompt, y_sample, new_attn_k, new_attn_v, new_diff_k, new_diff_v)
```

```python
import functools
import math

import numpy as np
import jax
import jax.numpy as jnp
from jax import lax
from jax.experimental import pallas as pl
from jax.experimental.pallas import tpu as pltpu

F32 = jnp.float32
BF16 = jnp.bfloat16

HEAD_DIM = 64
HEADS_A = 8
KV_HEADS_A = 2
HEADS_B = 4
N_QKV = HEADS_A * HEAD_DIM + 2 * KV_HEADS_A * HEAD_DIM + 3 * HEADS_B * 2 * HEAD_DIM
N_EXPERTS = 16
CAPACITY_FACTOR = 2
GRID_W = 64
ROPE_THETA = 10000.0
EPS = 1e-6
LANES = 128
ROW_TILE = 1024
POST_ROWS = 1024
CHAIN_ROWS = 512
EXPERT_ROWS = 256
LATENT_Q_TILE = 512
CONTEXT_SETS_PER_STEP = 4
SETS_PER_PACK = LANES // N_EXPERTS
SHARED_STEP_ROWS = 2048
DISPATCH_ROWS = 512
NEG_BIG = -1e30
NOT_SELECTED = -1.0
SLOT_RADIX = 256.0
VMEM_LIMIT = 56 * 1024 * 1024
ROUND_STEPS = 8


def _cparams(n_axes):
    return pltpu.CompilerParams(dimension_semantics=("arbitrary",) * n_axes,
                                vmem_limit_bytes=VMEM_LIMIT)


def _dot(a, b):
    return jnp.dot(a, b, preferred_element_type=F32)


def _dot_nt(a, b):
    return lax.dot_general(a, b, (((1,), (1,)), ((), ())), preferred_element_type=F32)


def _split(a):
    hi = a.astype(BF16)
    lo = (a - hi.astype(F32)).astype(BF16)
    return hi, lo


def _dot3(a, b):
    a_hi, a_lo = _split(a)
    b_hi, b_lo = _split(b)
    return _dot(a_hi, b_hi) + _dot(a_lo, b_hi) + _dot(a_hi, b_lo)


def _rms(x, g):
    return x * lax.rsqrt(jnp.mean(x * x, axis=-1, keepdims=True) + EPS) * g


def _mod_index(mod, tiles_per_set):
    if mod.shape[0] == 1:
        return lambda i: (0, 0, 0)
    return lambda i: (i // tiles_per_set, 0, 0)


def _mod_kernel(c_ref, w_ref, b_ref, o_ref):
    c = c_ref[...]
    a = c * jax.nn.sigmoid(c)
    o_ref[0] = _dot3(a, w_ref[...]) + b_ref[0]


def _modulation(c_rows, w_mod, b_mod):
    r, d = c_rows.shape
    return pl.pallas_call(
        _mod_kernel,
        grid=(6,),
        in_specs=[pl.BlockSpec((r, d), lambda j: (0, 0)),
                  pl.BlockSpec((d, d), lambda j: (0, j)),
                  pl.BlockSpec((1, 1, d), lambda j: (j, 0, 0))],
        out_specs=pl.BlockSpec((1, r, d), lambda j: (j, 0, 0)),
        out_shape=jax.ShapeDtypeStruct((6, r, d), F32),
        name="mod",
        compiler_params=_cparams(1),
    )(c_rows, w_mod, b_mod.reshape(6, 1, d))


def _round_weights_kernel(*refs):
    n = len(refs) // 2
    for src, out in zip(refs[:n - 1], refs[n:-1]):
        out[...] = src[...].astype(BF16)
    hi, lo = _split(refs[n - 1][...])
    refs[-1][...] = jnp.concatenate([hi, lo], axis=1)


def _round_weights(weights, w_router_p):
    steps = ROUND_STEPS
    arrays = list(weights) + [w_router_p]
    blk = lambda a, cols: pl.BlockSpec((a.shape[0] // steps, cols), lambda i: (i, 0))
    out_cols = [a.shape[1] for a in weights] + [2 * w_router_p.shape[1]]
    return pl.pallas_call(
        _round_weights_kernel,
        grid=(steps,),
        in_specs=[blk(a, a.shape[1]) for a in arrays],
        out_specs=[blk(a, c) for a, c in zip(arrays, out_cols)],
        out_shape=[jax.ShapeDtypeStruct((a.shape[0], c), BF16) for a, c in zip(arrays, out_cols)],
        name="round_weights",
        compiler_params=_cparams(1),
    )(*arrays)


def _seg_sumsq(x, ones_blockdiag):
    return _dot((x * x).astype(BF16), ones_blockdiag)


def _rope(x, c, s_up, s_dn):
    w = x.shape[1]
    reps = w // c.shape[1]
    if reps > 1:
        c, s_up, s_dn = (jnp.concatenate([t] * reps, axis=1) for t in (c, s_up, s_dn))
    return x * c + pltpu.roll(x, w - 16, 1) * s_up + pltpu.roll(x, 16, 1) * s_dn


def _inproj_kernel(*refs, rope, emit_cache):
    (x_ref, mod_ref, g_ref, w_ref, gq_ref, gk_ref, bd512_ref, bd128_ref), refs = refs[:8], refs[8:]
    if rope:
        (c_ref, su_ref, sd_ref), refs = refs[:3], refs[3:]
    qa_o, kta_o, vta_o, qb_o, ktb_o, vb_o = refs[:6]
    if emit_cache:
        ka_c, va_c, kb_c, vb_c = refs[6:10]

    mod = mod_ref[0]
    wa = HEADS_A * HEAD_DIM
    wkv = KV_HEADS_A * HEAD_DIM
    wb = HEADS_B * 2 * HEAD_DIM
    o_ka, o_va, o_qb = wa, wa + wkv, wa + 2 * wkv
    o_kb, o_vb = o_qb + wb, o_qb + 2 * wb
    scale = HEAD_DIM ** -0.5 * math.log2(math.e)

    x = x_ref[...]
    h = (_rms(x, g_ref[...]) * (1.0 + mod[1:2]) + mod[0:1]).astype(BF16)
    if rope:
        tabs = (c_ref[...], su_ref[...], sd_ref[...])

    qa = _dot(h, w_ref[:, 0:wa])
    qa = qa * lax.rsqrt(_seg_sumsq(qa, bd512_ref[...]) * (1.0 / HEAD_DIM) + EPS) * gq_ref[...]
    if rope:
        qa = _rope(qa, *tabs)
    qa_o[...] = (qa * scale).astype(BF16)

    kv = _dot(h, w_ref[:, o_ka:o_qb])
    ka, va = kv[:, 0:wkv], kv[:, wkv:2 * wkv]
    ka = ka * lax.rsqrt(_seg_sumsq(ka, bd128_ref[...]) * (1.0 / HEAD_DIM) + EPS) * gk_ref[...]

    def put_feat(val, out_bf16, out_f32):
        rows = out_bf16.shape[2]
        for s in range(out_bf16.shape[0]):
            t = val[s * rows:(s + 1) * rows].T
            if out_f32 is not None:
                out_f32[s] = t
            out_bf16[s] = t.astype(BF16)

    if rope:
        put_feat(_rope(ka, *tabs), kta_o, None)
    else:
        put_feat(ka, kta_o, ka_c if emit_cache else None)
    put_feat(va, vta_o, va_c if emit_cache else None)

    qb = _dot(h, w_ref[:, o_qb:o_kb])
    if rope:
        qb = _rope(qb, *tabs)
    qb_o[...] = (qb * scale).astype(BF16)

    kb = _dot(h, w_ref[:, o_kb:o_vb])
    if rope:
        put_feat(_rope(kb, *tabs), ktb_o, None)
    else:
        put_feat(kb, ktb_o, kb_c if emit_cache else None)

    vb = _dot(h, w_ref[:, o_vb:o_vb + wb])
    if emit_cache:
        for hd in range(HEADS_B):
            vb_c[pl.ds(hd, vb.shape[0], stride=HEADS_B), :] = vb[:, hd * 128:(hd + 1) * 128]
    vb_o[...] = vb.astype(BF16)


def _blockdiag_ones(width):
    g = np.arange(width) // HEAD_DIM
    return jnp.asarray((g[:, None] == g[None, :]).astype(np.float32), dtype=BF16)


def _inproj(x2, mod, set_rows, g_attn, w_in, gq_t, gk_t, rope_tabs, emit_cache):
    t, d = x2.shape
    tm = ROW_TILE
    assert t % tm == 0 and (tm % set_rows == 0 or set_rows % tm == 0)
    tiles_per_set = max(1, set_rows // tm)
    sets_per_tile = max(1, tm // set_rows)
    rope = rope_tabs is not None
    assert not (rope and emit_cache), "cached keys are the position-free ones"
    nq = N_QKV
    row = lambda i: (i, 0)
    const = lambda i: (0, 0)
    in_specs = [pl.BlockSpec((tm, d), row),
                pl.BlockSpec((1, 8, d), _mod_index(mod, tiles_per_set)),
                pl.BlockSpec((1, d), const),
                pl.BlockSpec((d, nq), const, pipeline_mode=pl.Buffered(1)),
                pl.BlockSpec((1, 512), const),
                pl.BlockSpec((1, 128), const),
                pl.BlockSpec((512, 512), const),
                pl.BlockSpec((128, 128), const)]
    args = [x2, mod, g_attn, w_in, gq_t, gk_t, _blockdiag_ones(512), _blockdiag_ones(128)]
    if rope:
        in_specs += [pl.BlockSpec((tm, LANES), lambda i: (i % tiles_per_set, 0))] * 3
        args += list(rope_tabs)
    n_sets = t // set_rows
    wkv, wb = KV_HEADS_A * HEAD_DIM, HEADS_B * 2 * HEAD_DIM
    outs = [("tok", 512, BF16), ("feat", wkv, BF16), ("feat", wkv, BF16),
            ("tok", 512, BF16), ("feat", wb, BF16), ("tok", wb, BF16)]
    if emit_cache:
        outs += [("feat", wkv, F32), ("feat", wkv, F32), ("feat", wb, F32), ("tokhead", wb, F32)]
    feat = lambda i: (i // tiles_per_set, 0, i % tiles_per_set)
    feat_rows = min(tm, set_rows)

    def out_block(kind, w):
        if kind == "tok":
            return pl.BlockSpec((tm, w), row)
        if kind == "tokhead":
            return pl.BlockSpec((tm * HEADS_B, w // HEADS_B), row)
        return pl.BlockSpec((sets_per_tile, w, feat_rows), feat)

    def out_array(kind, w, dt):
        shape = {"tok": (t, w), "tokhead": (t * HEADS_B, w // HEADS_B), "feat": (n_sets, w, set_rows)}[kind]
        return jax.ShapeDtypeStruct(shape, dt)

    return pl.pallas_call(
        functools.partial(_inproj_kernel, rope=rope, emit_cache=emit_cache),
        grid=(t // tm,),
        in_specs=in_specs,
        out_specs=[out_block(kind, w) for kind, w, _ in outs],
        out_shape=[out_array(kind, w, dt) for kind, w, dt in outs],
        name="inproj_rope" if rope else "inproj",
        compiler_params=_cparams(1),
    )(*args)


def _cache_prep_kernel(*refs):
    half = len(refs) // 2
    for src, out in zip(refs[:half - 1], refs[half:-1]):
        out[...] = src[...].astype(BF16)
    v_src, v_out = refs[half - 1], refs[-1]
    keys = v_out.shape[1]
    for hd in range(HEADS_B):
        v_out[0, :, hd * 128:(hd + 1) * 128] = v_src[0, pl.ds(hd, keys, stride=HEADS_B), :].astype(BF16)


def _cache_prep(arrays, v_by_head):
    blk = lambda b: (b, 0, 0)
    n_sets, rows, width = v_by_head.shape
    v_shape = (n_sets, rows // HEADS_B, width * HEADS_B)
    return pl.pallas_call(
        _cache_prep_kernel,
        grid=(n_sets,),
        in_specs=[pl.BlockSpec((1,) + a.shape[1:], blk) for a in arrays + [v_by_head]],
        out_specs=[pl.BlockSpec((1,) + a.shape[1:], blk) for a in arrays] + [pl.BlockSpec((1,) + v_shape[1:], blk)],
        out_shape=[jax.ShapeDtypeStruct(a.shape, BF16) for a in arrays] + [jax.ShapeDtypeStruct(v_shape, BF16)],
        name="cache_prep",
        compiler_params=_cparams(1),
    )(*arrays, v_by_head)


def _exp_parts(scores):
    m = scores[0].max(axis=-1, keepdims=True)
    for s in scores[1:]:
        m = jnp.maximum(m, s.max(axis=-1, keepdims=True))
    return [jnp.exp2(s - m) for s in scores]


def _row_sum(parts):
    l = parts[0].sum(axis=-1, keepdims=True)
    for e in parts[1:]:
        l = l + e.sum(axis=-1, keepdims=True)
    return l


def _pad_rows(x, first, ones_row=False):
    if ones_row:
        z = jnp.where(lax.broadcasted_iota(jnp.int32, x.shape, 0) == 0, 1.0, 0.0).astype(x.dtype)
    else:
        z = jnp.zeros_like(x)
    return jnp.concatenate([x, z] if first else [z, x], axis=0)


def _gqa_kernel(*refs, n_src, mxu_sums):
    q_ref = refs[0]
    k_refs = refs[1:1 + n_src]
    v_refs = refs[1 + n_src:1 + 2 * n_src]
    o_ref = refs[1 + 2 * n_src]
    sets_here = k_refs[0].shape[0]
    tq = q_ref.shape[0] // sets_here
    lane = lax.broadcasted_iota(jnp.int32, (2 * tq, LANES), 1)
    lo_half = lane < HEAD_DIM
    for s in range(sets_here):
        rows = slice(s * tq, (s + 1) * tq)
        for g in range(KV_HEADS_A):
            c0 = g * 256
            f0 = g * HEAD_DIM
            q = jnp.concatenate([q_ref[rows, c0:c0 + 128], q_ref[rows, c0 + 128:c0 + 256]], axis=0)
            kts = [k[s, f0:f0 + HEAD_DIM, :] for k in k_refs]
            vts = [v[s, f0:f0 + HEAD_DIM, :] for v in v_refs]
            e_lo = _exp_parts([_dot(q, _pad_rows(kt, True)) for kt in kts])
            e_hi = _exp_parts([_dot(q, _pad_rows(kt, False)) for kt in kts])
            o_lo = o_hi = None
            for e_part, vt in zip(e_lo, vts):
                t = _dot_nt(e_part.astype(BF16), _pad_rows(vt, True, ones_row=mxu_sums))
                o_lo = t if o_lo is None else o_lo + t
            for e_part, vt in zip(e_hi, vts):
                t = _dot_nt(e_part.astype(BF16), _pad_rows(vt, False, ones_row=mxu_sums))
                o_hi = t if o_hi is None else o_hi + t
            if mxu_sums:
                l_lo, l_hi = o_lo[:, HEAD_DIM:HEAD_DIM + 1], o_hi[:, 0:1]
                o = jnp.where(lo_half, o_lo * (1.0 / l_lo), o_hi * (1.0 / l_hi))
            else:
                o = (o_lo + o_hi) * jnp.where(lo_half, 1.0 / _row_sum(e_lo), 1.0 / _row_sum(e_hi))
            o_ref[rows, c0:c0 + 128] = o[0:tq].astype(BF16)
            o_ref[rows, c0 + 128:c0 + 256] = o[tq:2 * tq].astype(BF16)


def _diff_kernel(*refs, n_src, lam_init, stack_maps):
    q_ref = refs[0]
    k_refs = refs[1:1 + n_src]
    v_refs = refs[1 + n_src:1 + 2 * n_src]
    lq1, lk1, lq2, lk2, gd_ref, o_ref = refs[1 + 2 * n_src:]
    lam_all = (jnp.exp(jnp.sum(lq1[...] * lk1[...], axis=-1, keepdims=True))
               - jnp.exp(jnp.sum(lq2[...] * lk2[...], axis=-1, keepdims=True)) + lam_init)
    sets_here = k_refs[0].shape[0]
    tq = q_ref.shape[0] // sets_here
    for s in range(sets_here):
        rows = slice(s * tq, (s + 1) * tq)
        for j in range(HEADS_B):
            lam = lam_all[j:j + 1, :]
            q = q_ref[rows, j * 128:(j + 1) * 128]
            k0s = [_pad_rows(k[s, j * 128:j * 128 + HEAD_DIM, :], True) for k in k_refs]
            k1s = [_pad_rows(k[s, j * 128 + HEAD_DIM:(j + 1) * 128, :], False) for k in k_refs]
            e0 = _exp_parts([_dot(q, k0) for k0 in k0s])
            e1 = _exp_parts([_dot(q, k1) for k1 in k1s])
            o0 = o1 = None
            for a0, a1, v in zip(e0, e1, v_refs):
                val = v[s, :, j * 128:(j + 1) * 128]
                if stack_maps:
                    t = _dot(jnp.concatenate([a0.astype(BF16), a1.astype(BF16)], axis=0), val)
                    t0, t1 = t[0:tq], t[tq:2 * tq]
                else:
                    t0, t1 = _dot(a0.astype(BF16), val), _dot(a1.astype(BF16), val)
                o0 = t0 if o0 is None else o0 + t0
                o1 = t1 if o1 is None else o1 + t1
            o = o0 * (1.0 / _row_sum(e0)) - o1 * (lam / _row_sum(e1))
            o = _rms(o, gd_ref[...]) * (1.0 - lam_init)
            o_ref[rows, j * 128:(j + 1) * 128] = o.astype(BF16)


def _attention(kind, q, ks, vs, set_rows, tq, sets_per_step, extra, extra_kw):
    t, qw = q.shape
    n_sets = t // set_rows
    q_tiles = set_rows // tq
    assert sets_per_step == 1 or q_tiles == 1
    in_specs = [pl.BlockSpec((sets_per_step * tq, qw), lambda b, i: (b * q_tiles + i, 0))]
    for a in list(ks) + list(vs):
        in_specs.append(pl.BlockSpec((sets_per_step,) + a.shape[1:], lambda b, i: (b, 0, 0)))
    for e in extra:
        in_specs.append(pl.BlockSpec(e.shape, lambda b, i: (0, 0)))
    body = functools.partial(_gqa_kernel if kind == "gqa" else _diff_kernel, n_src=len(ks), **extra_kw)
    return pl.pallas_call(
        body,
        grid=(n_sets // sets_per_step, q_tiles),
        in_specs=in_specs,
        out_specs=pl.BlockSpec((sets_per_step * tq, qw), lambda b, i: (b * q_tiles + i, 0)),
        out_shape=jax.ShapeDtypeStruct((t, qw), BF16),
        name=f"{kind}_attn_{len(ks)}src",
        compiler_params=_cparams(2),
    )(q, *ks, *vs, *extra)


def _post_kernel(x_ref, ya_ref, yb_ref, mod_ref, g1_ref, g2_ref, win_ref, wba_ref, wbb_ref, wo_ref, wr2_ref,
                 x1_o, h2_o, aff_o):
    d = x_ref.shape[1]
    g0 = win_ref.shape[1] - 2 * d
    mod = mod_ref[0]
    for r0 in range(0, x_ref.shape[0], CHAIN_ROWS):
        rows = slice(r0, r0 + CHAIN_ROWS)
        x = x_ref[rows, :]
        h = (_rms(x, g1_ref[...]) * (1.0 + mod[1:2]) + mod[0:1]).astype(BF16)
        ga = jax.nn.sigmoid(_dot(h, win_ref[:, g0:g0 + d]))
        merged = ga * _dot(ya_ref[rows, :], wba_ref[...])
        gb = jax.nn.sigmoid(_dot(h, win_ref[:, g0 + d:g0 + 2 * d]))
        merged = merged + gb * _dot(yb_ref[rows, :], wbb_ref[...])
        m = _dot(merged.astype(BF16), wo_ref[...])
        x1 = x + mod[2:3] * m
        x1_o[rows, :] = x1
        h2 = _rms(x1, g2_ref[...]) * (1.0 + mod[4:5]) + mod[3:4]
        h2_o[rows, :] = h2.astype(BF16)
        h2_hi, h2_lo = _split(h2)
        both = _dot(h2_hi, wr2_ref[...])
        logits = both[:, 0:LANES] + both[:, LANES:2 * LANES] + _dot(h2_lo, wr2_ref[:, 0:LANES])
        lane = lax.broadcasted_iota(jnp.int32, logits.shape, 1)
        logits = jnp.where(lane < N_EXPERTS, logits, NEG_BIG)
        e = jnp.exp(logits - logits.max(axis=-1, keepdims=True))
        aff_o[rows, :] = e / e.sum(axis=-1, keepdims=True)


def _post(x2, ya, yb, mod, set_rows, g1, g2, w_in, w_ba, w_bb, w_out, w_router2):
    t, d = x2.shape
    tm = POST_ROWS
    tiles_per_set = max(1, set_rows // tm)
    assert t % tm == 0 and tm % CHAIN_ROWS == 0 and (mod.shape[0] == 1 or set_rows % tm == 0)
    row = lambda i: (i, 0)
    const = lambda i: (0, 0)
    once = pl.Buffered(1)
    half = w_in.shape[1] // 2
    assert w_in.shape[1] == 2 * half and half % LANES == 0 and half >= 2 * d
    return pl.pallas_call(
        _post_kernel,
        grid=(t // tm,),
        in_specs=[pl.BlockSpec((tm, d), row),
                  pl.BlockSpec((tm, 512), row),
                  pl.BlockSpec((tm, 512), row),
                  pl.BlockSpec((1, 8, d), _mod_index(mod, tiles_per_set)),
                  pl.BlockSpec((1, d), const),
                  pl.BlockSpec((1, d), const),
                  pl.BlockSpec((d, half), lambda i: (0, 1), pipeline_mode=once),
                  pl.BlockSpec(w_ba.shape, const, pipeline_mode=once),
                  pl.BlockSpec(w_bb.shape, const, pipeline_mode=once),
                  pl.BlockSpec(w_out.shape, const, pipeline_mode=once),
                  pl.BlockSpec(w_router2.shape, const, pipeline_mode=once)],
        out_specs=[pl.BlockSpec((tm, d), row), pl.BlockSpec((tm, d), row), pl.BlockSpec((tm, LANES), row)],
        out_shape=[jax.ShapeDtypeStruct((t, d), F32), jax.ShapeDtypeStruct((t, d), BF16),
                   jax.ShapeDtypeStruct((t, LANES), F32)],
        name="post_attn",
        compiler_params=_cparams(1),
    )(x2, ya, yb, mod, g1, g2, w_in, w_ba, w_bb, w_out, w_router2)


def _select_kernel(aff_ref, slot_o, slot_t_o, affb_o, *, cap):
    aff = aff_ref[0]
    n = aff.shape[0]
    capf = float(cap)

    def enough(cand):
        return jnp.sum(jnp.where(aff >= cand, 1.0, 0.0), axis=0, keepdims=True) >= capf

    pw = jnp.ones((1, LANES), F32)
    for k in (64, 32, 16, 8, 4, 2, 1):
        pw = jnp.where(enough(pw * 2.0 ** -(k - 1)), pw, pw * 2.0 ** -k)

    def mantissa_step(_, carry):
        thr, step = carry
        cand = thr + step
        return jnp.where(enough(cand), cand, thr), step * 0.5

    thr, _ = lax.fori_loop(0, 23, mantissa_step, (pw, pw * 0.5))
    above = aff > thr
    tied = aff == thr
    need = capf - jnp.sum(jnp.where(above, 1.0, 0.0), axis=0, keepdims=True)
    r_i = lax.broadcasted_iota(jnp.int32, (n, n), 0)
    c_i = lax.broadcasted_iota(jnp.int32, (n, n), 1)
    before = jnp.where(c_i < r_i, 1.0, 0.0).astype(BF16)
    tie_rank = _dot(before, jnp.where(tied, 1.0, 0.0).astype(BF16))
    sel = above | (tied & (tie_rank < need))
    slot = _dot(before, jnp.where(sel, 1.0, 0.0).astype(BF16))
    slot = jnp.where(sel, slot, NOT_SELECTED)
    slot_o[0] = slot.astype(BF16)
    slot_t_o[0] = slot.T
    affb_o[0] = aff.astype(BF16)


def _dispatch_kernel(slot_ref, affb_ref, slot_t_ref, h2_ref, xe_o, gt_o, *, cap):
    n = slot_ref.shape[1]
    g = h2_ref.shape[0] // n
    per = min(N_EXPERTS, DISPATCH_ROWS // cap)
    shift = cap.bit_length() - 1
    slot_iota = lax.broadcasted_iota(jnp.int32, (cap, n), 0).astype(F32)
    for k in range(g):
        h2 = h2_ref[k * n:(k + 1) * n, :]
        slot_t = slot_t_ref[0, k * N_EXPERTS:(k + 1) * N_EXPERTS, :]

        for e0 in range(0, N_EXPERTS, per):
            onehot = jnp.concatenate(
                [jnp.where(slot_t[e:e + 1, :] == slot_iota, 1.0, 0.0) for e in range(e0, e0 + per)],
                axis=0).astype(BF16)
            rows = _dot(onehot, h2).astype(BF16)
            for j in range(per):
                xe_o[e0 + j, k * cap:(k + 1) * cap, :] = rows[j * cap:(j + 1) * cap]

        first_lane = ((pl.program_id(0) * g + k) % SETS_PER_PACK) * N_EXPERTS
        both = jnp.concatenate([slot_ref[0], affb_ref[0]], axis=1)
        for c0 in range(0, N_EXPERTS * cap, DISPATCH_ROWS):
            src = lax.broadcasted_iota(jnp.int32, (2 * LANES, DISPATCH_ROWS), 0)
            col = lax.broadcasted_iota(jnp.int32, (2 * LANES, DISPATCH_ROWS), 1) + c0
            lane_of_col = first_lane + lax.shift_right_logical(col, shift)
            spread = jnp.where(src == lane_of_col, SLOT_RADIX,
                               jnp.where(src == lane_of_col + LANES, 1.0, 0.0)).astype(BF16)
            want = (lax.broadcasted_iota(jnp.int32, (n, DISPATCH_ROWS), 1) & (cap - 1)).astype(F32)
            rest = _dot(both, spread) - want * SLOT_RADIX
            gate = jnp.where(rest >= 0.0, jnp.where(rest <= 1.0, rest, 0.0), 0.0)
            gt_o[k * n:(k + 1) * n, c0:c0 + DISPATCH_ROWS] = gate.astype(BF16)


def _route(aff, h2, n_sets, set_rows):
    t, d = h2.shape
    n = set_rows
    cap = CAPACITY_FACTOR * n // N_EXPERTS
    assert cap & (cap - 1) == 0 and cap % 16 == 0 and DISPATCH_ROWS % cap == 0
    n_packs = -(-n_sets // SETS_PER_PACK)
    aff16 = aff[:, :N_EXPERTS].reshape(n_sets, n, N_EXPERTS)
    aff16 = jnp.pad(aff16, ((0, n_packs * SETS_PER_PACK - n_sets), (0, 0), (0, 0)))
    packed = aff16.reshape(n_packs, SETS_PER_PACK, n, N_EXPERTS).transpose(0, 2, 1, 3).reshape(n_packs, n, LANES)
    pack_blk = lambda p: (p, 0, 0)
    slot, slot_t, affb = pl.pallas_call(
        functools.partial(_select_kernel, cap=cap),
        grid=(n_packs,),
        in_specs=[pl.BlockSpec((1, n, LANES), pack_blk)],
        out_specs=[pl.BlockSpec((1, n, LANES), pack_blk), pl.BlockSpec((1, LANES, n), pack_blk),
                   pl.BlockSpec((1, n, LANES), pack_blk)],
        out_shape=[jax.ShapeDtypeStruct((n_packs, n, LANES), BF16),
                   jax.ShapeDtypeStruct((n_packs, LANES, n), F32),
                   jax.ShapeDtypeStruct((n_packs, n, LANES), BF16)],
        name=f"select_cap{cap}",
        compiler_params=_cparams(1),
    )(packed)
    g = _sets_per_step(n_sets, n)
    per_pack = SETS_PER_PACK // g
    of_step = lambda s: (s // per_pack, 0, 0)
    return pl.pallas_call(
        functools.partial(_dispatch_kernel, cap=cap),
        grid=(n_sets // g,),
        in_specs=[pl.BlockSpec((1, n, LANES), of_step),
                  pl.BlockSpec((1, n, LANES), of_step),
                  pl.BlockSpec((1, g * N_EXPERTS, n), lambda s: (s // per_pack, s % per_pack, 0)),
                  pl.BlockSpec((g * n, d), lambda s: (s, 0))],
        out_specs=[pl.BlockSpec((N_EXPERTS, g * cap, d), lambda s: (0, s, 0)),
                   pl.BlockSpec((g * n, N_EXPERTS * cap), lambda s: (s, 0))],
        out_shape=[jax.ShapeDtypeStruct((N_EXPERTS, n_sets * cap, d), BF16),
                   jax.ShapeDtypeStruct((t, N_EXPERTS * cap), BF16)],
        name=f"dispatch_cap{cap}",
        compiler_params=_cparams(1),
    )(slot, affb, slot_t, h2)


def _expert_kernel(*refs, n_groups):
    x_refs = refs[:n_groups]
    wg_ref, wu_ref, wd_ref = refs[n_groups:n_groups + 3]
    o_refs = refs[n_groups + 3:2 * n_groups + 3]
    wg_s, wu_s, wd_s = refs[2 * n_groups + 3:]
    wg_s[...] = wg_ref[0].astype(BF16)
    wu_s[...] = wu_ref[0].astype(BF16)
    wd_s[...] = wd_ref[0].astype(BF16)
    sub = EXPERT_ROWS
    for x_ref, o_ref in zip(x_refs, o_refs):
        for r0 in range(0, x_ref.shape[1], sub):
            x = x_ref[0, r0:r0 + sub, :]
            a = _dot(x, wg_s[...])
            u = _dot(x, wu_s[...])
            hmid = (a * jax.nn.sigmoid(a) * u).astype(BF16)
            o_ref[0, r0:r0 + sub, :] = _dot(hmid, wd_s[...]).astype(BF16)


def _experts(xes, w_gate, w_up, w_down):
    e, d, f = w_gate.shape
    blk = lambda i: (i, 0, 0)
    return pl.pallas_call(
        functools.partial(_expert_kernel, n_groups=len(xes)),
        grid=(e,),
        in_specs=[pl.BlockSpec((1, x.shape[1], d), blk) for x in xes]
        + [pl.BlockSpec((1, d, f), blk), pl.BlockSpec((1, d, f), blk), pl.BlockSpec((1, f, d), blk)],
        out_specs=[pl.BlockSpec((1, x.shape[1], d), blk) for x in xes],
        out_shape=[jax.ShapeDtypeStruct(x.shape, BF16) for x in xes],
        scratch_shapes=[pltpu.VMEM((d, f), BF16), pltpu.VMEM((d, f), BF16), pltpu.VMEM((f, d), BF16)],
        name="experts",
        compiler_params=_cparams(1),
    )(*xes, w_gate, w_up, w_down)


def _combine_kernel(gt_ref, ye_ref, x1_ref, mod_ref, gf_ref, y_o, *, cap):
    e, gcap, d = ye_ref.shape
    g = gcap // cap
    n = x1_ref.shape[0] // g
    for k in range(g):
        rows = slice(k * n, (k + 1) * n)
        ye = ye_ref[:, k * cap:(k + 1) * cap, :].reshape(e * cap, d)
        moe = _dot(gt_ref[rows, :], ye)
        x = x1_ref[rows, :] + mod_ref[0][5:6] * moe
        y_o[rows, :] = _rms(x, gf_ref[...])


def _sets_per_step(n_sets, set_rows):
    g = SETS_PER_PACK if set_rows * SETS_PER_PACK <= SHARED_STEP_ROWS else 1
    return g if n_sets % g == 0 else 1


def _combine(gt, ye, x1, mod, n_sets, set_rows, g_final):
    t, d = x1.shape
    cap = ye.shape[1] // n_sets
    g = _sets_per_step(n_sets, set_rows) if mod.shape[0] == 1 else 1
    tm = min(ROW_TILE, set_rows) if g == 1 else g * set_rows
    tiles = max(1, set_rows // tm)
    return pl.pallas_call(
        functools.partial(_combine_kernel, cap=cap),
        grid=(n_sets // g, tiles),
        in_specs=[pl.BlockSpec((tm, gt.shape[1]), lambda s, i: (s * tiles + i, 0)),
                  pl.BlockSpec((N_EXPERTS, g * cap, d), lambda s, i: (0, s, 0)),
                  pl.BlockSpec((tm, d), lambda s, i: (s * tiles + i, 0)),
                  pl.BlockSpec((1, 8, d), (lambda s, i: (s, 0, 0)) if mod.shape[0] > 1 else (lambda s, i: (0, 0, 0))),
                  pl.BlockSpec((1, d), lambda s, i: (0, 0))],
        out_specs=pl.BlockSpec((tm, d), lambda s, i: (s * tiles + i, 0)),
        out_shape=jax.ShapeDtypeStruct((t, d), F32),
        name=f"combine_cap{cap}",
        compiler_params=_cparams(2),
    )(gt, ye, x1, mod, g_final)


def _rope_tables(n_tokens):
    n_rows = n_tokens // GRID_W
    rowp = jnp.repeat(jnp.arange(n_rows), GRID_W).astype(F32)
    colp = jnp.tile(jnp.arange(GRID_W), n_rows).astype(F32)
    quarter = HEAD_DIM // 4
    freqs = ROPE_THETA ** (-jnp.arange(quarter, dtype=F32) / quarter)
    ang = jnp.stack([rowp[:, None] * freqs, colp[:, None] * freqs], axis=1)
    cos, sin = jnp.cos(ang), jnp.sin(ang)
    zero = jnp.zeros_like(sin)
    c = jnp.stack([cos, cos], axis=2).reshape(n_tokens, HEAD_DIM)
    s_up = jnp.stack([-sin, zero], axis=2).reshape(n_tokens, HEAD_DIM)
    s_dn = jnp.stack([zero, sin], axis=2).reshape(n_tokens, HEAD_DIM)
    return tuple(jnp.tile(t, (1, LANES // HEAD_DIM)) for t in (c, s_up, s_dn))


def _mix_and_route(x2, n_sets, set_rows, mod, lw, rope_tabs, cache, emit_cache, lam_init, tq):
    (g_attn, g_ffn, w_in, gq_t, gk_t, lq1, lk1, lq2, lk2, g_diff, w_ba, w_bb, w_out, w_router2) = lw
    outs = _inproj(x2, mod, set_rows, g_attn, w_in, gq_t, gk_t, rope_tabs, emit_cache)
    qa, kta, vta, qb, ktb, vb16 = outs[:6]
    vb16 = vb16.reshape(n_sets, set_rows, vb16.shape[1])
    ks_a, vs_a, ks_b, vs_b = [kta], [vta], [ktb], [vb16]
    if cache is not None:
        ckta, cvta, cktb, cvb = cache
        ks_a, vs_a, ks_b, vs_b = [ckta, kta], [cvta, vta], [cktb, ktb], [cvb, vb16]
    per_step = CONTEXT_SETS_PER_STEP if (tq == set_rows and n_sets % CONTEXT_SETS_PER_STEP == 0) else 1
    ya = _attention("gqa", qa, ks_a, vs_a, set_rows, tq, per_step, (), {"mxu_sums": cache is not None})
    yb = _attention("diff", qb, ks_b, vs_b, set_rows, tq, per_step, (lq1, lk1, lq2, lk2, g_diff),
                    {"lam_init": lam_init, "stack_maps": cache is None})
    x1, h2, aff = _post(x2, ya, yb, mod, set_rows, g_attn, g_ffn, w_in, w_ba, w_bb, w_out, w_router2)
    xe, gt = _route(aff, h2, n_sets, set_rows)
    return x1, xe, gt, outs[6:]


def kernel(x_prompt, x_sample, cache_attn_k, cache_attn_v, cache_diff_k, cache_diff_v, c, c_ctx, w_mod, b_mod,
           g_attn_norm, g_ffn_norm, w_in, g_q_norm, g_k_norm, lambda_q1, lambda_k1, lambda_q2, lambda_k2,
           g_diff_norm, w_branch_a, w_branch_b, w_out, w_router, w_exp_gate, w_exp_up, w_exp_down, g_final):
    batch, seq, d = x_prompt.shape
    dec_batch, dec_seq, _ = x_sample.shape
    depth = w_in.shape[0]
    assert depth == 1, "the final norm is fused into the layer's combine step"
    past = cache_attn_k.shape[2]
    assert w_in.shape[2] == N_QKV + 2 * d

    xp = x_prompt.reshape(batch * seq, d)
    xs = x_sample.reshape(dec_batch * dec_seq, d)
    rope_tabs = _rope_tables(dec_seq)
    c_rows = jnp.concatenate([c, c_ctx[None, :], jnp.zeros((16 - dec_batch - 1, d), F32)], axis=0)
    yp = ys = None
    caches = []
    for l in range(depth):
        lam_init = 0.8 - 0.6 * math.exp(-0.3 * l)
        mod6 = _modulation(c_rows, w_mod[l], b_mod[l])
        mod = jnp.pad(jnp.transpose(mod6, (1, 0, 2)), ((0, 0), (0, 2), (0, 0)))
        mod_lat, mod_ctx = mod[:dec_batch], mod[dec_batch:dec_batch + 1]
        w_in16, w_ba16, w_bb16, w_out16, w_router2 = _round_weights(
            [w_in[l], w_branch_a[l], w_branch_b[l], w_out[l]],
            jnp.pad(w_router[l], ((0, 0), (0, LANES - N_EXPERTS))))
        lw = (g_attn_norm[l][None, :], g_ffn_norm[l][None, :], w_in16,
              jnp.tile(g_q_norm[l], HEADS_A)[None, :], jnp.tile(g_k_norm[l], KV_HEADS_A)[None, :],
              lambda_q1[l], lambda_k1[l], lambda_q2[l], lambda_k2[l], g_diff_norm[l][None, :],
              w_ba16, w_bb16, w_out16, w_router2)
        x1p, xe_p, gt_p, cache_out = _mix_and_route(xp, batch, seq, mod_ctx, lw, None, None, True, lam_init, seq)
        caches.append(cache_out)
        feat_major = lambda a: jnp.moveaxis(a[:, l], 1, -1).reshape(dec_batch, -1, past)
        cache_l = _cache_prep([feat_major(cache_attn_k), feat_major(cache_attn_v), feat_major(cache_diff_k)],
                              cache_diff_v[:, l].reshape(dec_batch, past * HEADS_B, -1))
        x1s, xe_s, gt_s, _ = _mix_and_route(xs, dec_batch, dec_seq, mod_lat, lw, rope_tabs, cache_l, False,
                                            lam_init, LATENT_Q_TILE)
        ye_p, ye_s = _experts([xe_p, xe_s], w_exp_gate[l], w_exp_up[l], w_exp_down[l])
        yp = _combine(gt_p, ye_p, x1p, mod_ctx, batch, seq, g_final[None, :])
        ys = _combine(gt_s, ye_s, x1s, mod_lat, dec_batch, dec_seq, g_final[None, :])
    y_prompt = yp.reshape(batch, seq, d)
    y_sample = ys.reshape(dec_batch, dec_seq, d)
    tok_major = lambda a, dims: jnp.moveaxis(a.reshape((batch,) + dims + (seq,)), -1, 1)
    new_attn_k = jnp.stack([tok_major(cc[0], (KV_HEADS_A, HEAD_DIM)) for cc in caches], axis=1)
    new_attn_v = jnp.stack([tok_major(cc[1], (KV_HEADS_A, HEAD_DIM)) for cc in caches], axis=1)
    new_diff_k = jnp.stack([tok_major(cc[2], (HEADS_B, 2, HEAD_DIM)) for cc in caches], axis=1)
    new_diff_v = jnp.stack([cc[3].reshape(batch, seq, HEADS_B, 2 * HEAD_DIM) for cc in caches], axis=1)
    return (y_prompt, y_sample, new_attn_k, new_attn_v, new_diff_k, new_diff_v)
```

```python
import functools
import math

import numpy as np
import jax
import jax.numpy as jnp
from jax import lax
from jax.experimental import pallas as pl
from jax.experimental.pallas import tpu as pltpu

F32 = jnp.float32
BF16 = jnp.bfloat16

HEAD_DIM = 64
HEADS_A = 8
KV_HEADS_A = 2
HEADS_B = 4
N_QKV = HEADS_A * HEAD_DIM + 2 * KV_HEADS_A * HEAD_DIM + 3 * HEADS_B * 2 * HEAD_DIM
N_EXPERTS = 16
CAPACITY_FACTOR = 2
GRID_W = 64
ROPE_THETA = 10000.0
EPS = 1e-6
LANES = 128
ROW_TILE = 1024
POST_ROWS = 1024
CHAIN_ROWS = 512
EXPERT_ROWS = 256
LATENT_Q_TILE = 512
CONTEXT_SETS_PER_STEP = 4
SETS_PER_PACK = LANES // N_EXPERTS
SHARED_STEP_ROWS = 2048
DISPATCH_ROWS = 512
NEG_BIG = -1e30
NOT_SELECTED = -1.0
SLOT_RADIX = 256.0
VMEM_LIMIT = 56 * 1024 * 1024
ROUND_STEPS = 8


def _cparams(n_axes):
    return pltpu.CompilerParams(dimension_semantics=("arbitrary",) * n_axes,
                                vmem_limit_bytes=VMEM_LIMIT)


def _dot(a, b):
    return jnp.dot(a, b, preferred_element_type=F32)


def _dot_nt(a, b):
    return lax.dot_general(a, b, (((1,), (1,)), ((), ())), preferred_element_type=F32)


def _split(a):
    hi = a.astype(BF16)
    lo = (a - hi.astype(F32)).astype(BF16)
    return hi, lo


def _dot3(a, b):
    a_hi, a_lo = _split(a)
    b_hi, b_lo = _split(b)
    return _dot(a_hi, b_hi) + _dot(a_lo, b_hi) + _dot(a_hi, b_lo)


def _rms(x, g):
    return x * lax.rsqrt(jnp.mean(x * x, axis=-1, keepdims=True) + EPS) * g


def _mod_index(mod, tiles_per_set):
    if mod.shape[0] == 1:
        return lambda i: (0, 0, 0)
    return lambda i: (i // tiles_per_set, 0, 0)


def _mod_kernel(c_ref, w_ref, b_ref, o_ref):
    c = c_ref[...]
    a = c * jax.nn.sigmoid(c)
    o_ref[0] = _dot3(a, w_ref[...]) + b_ref[0]


def _modulation(c_rows, w_mod, b_mod):
    r, d = c_rows.shape
    return pl.pallas_call(
        _mod_kernel,
        grid=(6,),
        in_specs=[pl.BlockSpec((r, d), lambda j: (0, 0)),
                  pl.BlockSpec((d, d), lambda j: (0, j)),
                  pl.BlockSpec((1, 1, d), lambda j: (j, 0, 0))],
        out_specs=pl.BlockSpec((1, r, d), lambda j: (j, 0, 0)),
        out_shape=jax.ShapeDtypeStruct((6, r, d), F32),
        name="mod",
        compiler_params=_cparams(1),
    )(c_rows, w_mod, b_mod.reshape(6, 1, d))


def _round_weights_kernel(*refs):
    n = len(refs) // 2
    for src, out in zip(refs[:n - 1], refs[n:-1]):
        out[...] = src[...].astype(BF16)
    hi, lo = _split(refs[n - 1][...])
    refs[-1][...] = jnp.concatenate([hi, lo], axis=1)


def _round_weights(weights, w_router_p):
    steps = ROUND_STEPS
    arrays = list(weights) + [w_router_p]
    blk = lambda a, cols: pl.BlockSpec((a.shape[0] // steps, cols), lambda i: (i, 0))
    out_cols = [a.shape[1] for a in weights] + [2 * w_router_p.shape[1]]
    return pl.pallas_call(
        _round_weights_kernel,
        grid=(steps,),
        in_specs=[blk(a, a.shape[1]) for a in arrays],
        out_specs=[blk(a, c) for a, c in zip(arrays, out_cols)],
        out_shape=[jax.ShapeDtypeStruct((a.shape[0], c), BF16) for a, c in zip(arrays, out_cols)],
        name="round_weights",
        compiler_params=_cparams(1),
    )(*arrays)


def _seg_sumsq(x, ones_blockdiag):
    return _dot((x * x).astype(BF16), ones_blockdiag)


def _rope(x, c, s_up, s_dn):
    w = x.shape[1]
    reps = w // c.shape[1]
    if reps > 1:
        c, s_up, s_dn = (jnp.concatenate([t] * reps, axis=1) for t in (c, s_up, s_dn))
    return x * c + pltpu.roll(x, w - 16, 1) * s_up + pltpu.roll(x, 16, 1) * s_dn


def _inproj_kernel(*refs, rope, emit_cache):
    (x_ref, mod_ref, g_ref, w_ref, gq_ref, gk_ref, bd512_ref, bd128_ref), refs = refs[:8], refs[8:]
    if rope:
        (c_ref, su_ref, sd_ref), refs = refs[:3], refs[3:]
    qa_o, kta_o, vta_o, qb_o, ktb_o, vb_o = refs[:6]
    if emit_cache:
        ka_c, va_c, kb_c, vb_c = refs[6:10]

    mod = mod_ref[0]
    wa = HEADS_A * HEAD_DIM
    wkv = KV_HEADS_A * HEAD_DIM
    wb = HEADS_B * 2 * HEAD_DIM
    o_ka, o_va, o_qb = wa, wa + wkv, wa + 2 * wkv
    o_kb, o_vb = o_qb + wb, o_qb + 2 * wb
    scale = HEAD_DIM ** -0.5 * math.log2(math.e)

    x = x_ref[...]
    h = (_rms(x, g_ref[...]) * (1.0 + mod[1:2]) + mod[0:1]).astype(BF16)
    if rope:
        tabs = (c_ref[...], su_ref[...], sd_ref[...])

    qa = _dot(h, w_ref[:, 0:wa])
    qa = qa * lax.rsqrt(_seg_sumsq(qa, bd512_ref[...]) * (1.0 / HEAD_DIM) + EPS) * gq_ref[...]
    if rope:
        qa = _rope(qa, *tabs)
    qa_o[...] = (qa * scale).astype(BF16)

    kv = _dot(h, w_ref[:, o_ka:o_qb])
    ka, va = kv[:, 0:wkv], kv[:, wkv:2 * wkv]
    ka = ka * lax.rsqrt(_seg_sumsq(ka, bd128_ref[...]) * (1.0 / HEAD_DIM) + EPS) * gk_ref[...]

    def put_feat(val, out_bf16, out_f32):
        rows = out_bf16.shape[2]
        for s in range(out_bf16.shape[0]):
            t = val[s * rows:(s + 1) * rows].T
            if out_f32 is not None:
                out_f32[s] = t
            out_bf16[s] = t.astype(BF16)

    if rope:
        put_feat(_rope(ka, *tabs), kta_o, None)
    else:
        put_feat(ka, kta_o, ka_c if emit_cache else None)
    put_feat(va, vta_o, va_c if emit_cache else None)

    qb = _dot(h, w_ref[:, o_qb:o_kb])
    if rope:
        qb = _rope(qb, *tabs)
    qb_o[...] = (qb * scale).astype(BF16)

    kb = _dot(h, w_ref[:, o_kb:o_vb])
    if rope:
        put_feat(_rope(kb, *tabs), ktb_o, None)
    else:
        put_feat(kb, ktb_o, kb_c if emit_cache else None)

    vb = _dot(h, w_ref[:, o_vb:o_vb + wb])
    if emit_cache:
        for hd in range(HEADS_B):
            vb_c[pl.ds(hd, vb.shape[0], stride=HEADS_B), :] = vb[:, hd * 128:(hd + 1) * 128]
    vb_o[...] = vb.astype(BF16)


def _blockdiag_ones(width):
    g = np.arange(width) // HEAD_DIM
    return jnp.asarray((g[:, None] == g[None, :]).astype(np.float32), dtype=BF16)


def _inproj(x2, mod, set_rows, g_attn, w_in, gq_t, gk_t, rope_tabs, emit_cache):
    t, d = x2.shape
    tm = ROW_TILE
    assert t % tm == 0 and (tm % set_rows == 0 or set_rows % tm == 0)
    tiles_per_set = max(1, set_rows // tm)
    sets_per_tile = max(1, tm // set_rows)
    rope = rope_tabs is not None
    assert not (rope and emit_cache), "cached keys are the position-free ones"
    nq = N_QKV
    row = lambda i: (i, 0)
    const = lambda i: (0, 0)
    in_specs = [pl.BlockSpec((tm, d), row),
                pl.BlockSpec((1, 8, d), _mod_index(mod, tiles_per_set)),
                pl.BlockSpec((1, d), const),
                pl.BlockSpec((d, nq), const, pipeline_mode=pl.Buffered(1)),
                pl.BlockSpec((1, 512), const),
                pl.BlockSpec((1, 128), const),
                pl.BlockSpec((512, 512), const),
                pl.BlockSpec((128, 128), const)]
    args = [x2, mod, g_attn, w_in, gq_t, gk_t, _blockdiag_ones(512), _blockdiag_ones(128)]
    if rope:
        in_specs += [pl.BlockSpec((tm, LANES), lambda i: (i % tiles_per_set, 0))] * 3
        args += list(rope_tabs)
    n_sets = t // set_rows
    wkv, wb = KV_HEADS_A * HEAD_DIM, HEADS_B * 2 * HEAD_DIM
    outs = [("tok", 512, BF16), ("feat", wkv, BF16), ("feat", wkv, BF16),
            ("tok", 512, BF16), ("feat", wb, BF16), ("tok", wb, BF16)]
    if emit_cache:
        outs += [("feat", wkv, F32), ("feat", wkv, F32), ("feat", wb, F32), ("tokhead", wb, F32)]
    feat = lambda i: (i // tiles_per_set, 0, i % tiles_per_set)
    feat_rows = min(tm, set_rows)

    def out_block(kind, w):
        if kind == "tok":
            return pl.BlockSpec((tm, w), row)
        if kind == "tokhead":
            return pl.BlockSpec((tm * HEADS_B, w // HEADS_B), row)
        return pl.BlockSpec((sets_per_tile, w, feat_rows), feat)

    def out_array(kind, w, dt):
        shape = {"tok": (t, w), "tokhead": (t * HEADS_B, w // HEADS_B), "feat": (n_sets, w, set_rows)}[kind]
        return jax.ShapeDtypeStruct(shape, dt)

    return pl.pallas_call(
        functools.partial(_inproj_kernel, rope=rope, emit_cache=emit_cache),
        grid=(t // tm,),
        in_specs=in_specs,
        out_specs=[out_block(kind, w) for kind, w, _ in outs],
        out_shape=[out_array(kind, w, dt) for kind, w, dt in outs],
        name="inproj_rope" if rope else "inproj",
        compiler_params=_cparams(1),
    )(*args)


def _cache_prep_kernel(*refs):
    half = len(refs) // 2
    for src, out in zip(refs[:half - 1], refs[half:-1]):
        out[...] = src[...].astype(BF16)
    v_src, v_out = refs[half - 1], refs[-1]
    keys = v_out.shape[1]
    for hd in range(HEADS_B):
        v_out[0, :, hd * 128:(hd + 1) * 128] = v_src[0, pl.ds(hd, keys, stride=HEADS_B), :].astype(BF16)


def _cache_prep(arrays, v_by_head):
    blk = lambda b: (b, 0, 0)
    n_sets, rows, width = v_by_head.shape
    v_shape = (n_sets, rows // HEADS_B, width * HEADS_B)
    return pl.pallas_call(
        _cache_prep_kernel,
        grid=(n_sets,),
        in_specs=[pl.BlockSpec((1,) + a.shape[1:], blk) for a in arrays + [v_by_head]],
        out_specs=[pl.BlockSpec((1,) + a.shape[1:], blk) for a in arrays] + [pl.BlockSpec((1,) + v_shape[1:], blk)],
        out_shape=[jax.ShapeDtypeStruct(a.shape, BF16) for a in arrays] + [jax.ShapeDtypeStruct(v_shape, BF16)],
        name="cache_prep",
        compiler_params=_cparams(1),
    )(*arrays, v_by_head)


def _exp_parts(scores):
    m = scores[0].max(axis=-1, keepdims=True)
    for s in scores[1:]:
        m = jnp.maximum(m, s.max(axis=-1, keepdims=True))
    return [jnp.exp2(s - m) for s in scores]


def _row_sum(parts):
    l = parts[0].sum(axis=-1, keepdims=True)
    for e in parts[1:]:
        l = l + e.sum(axis=-1, keepdims=True)
    return l


def _pad_rows(x, first, ones_row=False):
    if ones_row:
        z = jnp.where(lax.broadcasted_iota(jnp.int32, x.shape, 0) == 0, 1.0, 0.0).astype(x.dtype)
    else:
        z = jnp.zeros_like(x)
    return jnp.concatenate([x, z] if first else [z, x], axis=0)


def _gqa_kernel(*refs, n_src, mxu_sums):
    q_ref = refs[0]
    k_refs = refs[1:1 + n_src]
    v_refs = refs[1 + n_src:1 + 2 * n_src]
    o_ref = refs[1 + 2 * n_src]
    sets_here = k_refs[0].shape[0]
    tq = q_ref.shape[0] // sets_here
    lane = lax.broadcasted_iota(jnp.int32, (2 * tq, LANES), 1)
    lo_half = lane < HEAD_DIM
    for s in range(sets_here):
        rows = slice(s * tq, (s + 1) * tq)
        for g in range(KV_HEADS_A):
            c0 = g * 256
            f0 = g * HEAD_DIM
            q = jnp.concatenate([q_ref[rows, c0:c0 + 128], q_ref[rows, c0 + 128:c0 + 256]], axis=0)
            kts = [k[s, f0:f0 + HEAD_DIM, :] for k in k_refs]
            vts = [v[s, f0:f0 + HEAD_DIM, :] for v in v_refs]
            e_lo = _exp_parts([_dot(q, _pad_rows(kt, True)) for kt in kts])
            e_hi = _exp_parts([_dot(q, _pad_rows(kt, False)) for kt in kts])
            o_lo = o_hi = None
            for e_part, vt in zip(e_lo, vts):
                t = _dot_nt(e_part.astype(BF16), _pad_rows(vt, True, ones_row=mxu_sums))
                o_lo = t if o_lo is None else o_lo + t
            for e_part, vt in zip(e_hi, vts):
                t = _dot_nt(e_part.astype(BF16), _pad_rows(vt, False, ones_row=mxu_sums))
                o_hi = t if o_hi is None else o_hi + t
            if mxu_sums:
                l_lo, l_hi = o_lo[:, HEAD_DIM:HEAD_DIM + 1], o_hi[:, 0:1]
                o = jnp.where(lo_half, o_lo * (1.0 / l_lo), o_hi * (1.0 / l_hi))
            else:
                o = (o_lo + o_hi) * jnp.where(lo_half, 1.0 / _row_sum(e_lo), 1.0 / _row_sum(e_hi))
            o_ref[rows, c0:c0 + 128] = o[0:tq].astype(BF16)
            o_ref[rows, c0 + 128:c0 + 256] = o[tq:2 * tq].astype(BF16)


def _diff_kernel(*refs, n_src, lam_init, stack_maps):
    q_ref = refs[0]
    k_refs = refs[1:1 + n_src]
    v_refs = refs[1 + n_src:1 + 2 * n_src]
    lq1, lk1, lq2, lk2, gd_ref, o_ref = refs[1 + 2 * n_src:]
    lam_all = (jnp.exp(jnp.sum(lq1[...] * lk1[...], axis=-1, keepdims=True))
               - jnp.exp(jnp.sum(lq2[...] * lk2[...], axis=-1, keepdims=True)) + lam_init)
    sets_here = k_refs[0].shape[0]
    tq = q_ref.shape[0] // sets_here
    for s in range(sets_here):
        rows = slice(s * tq, (s + 1) * tq)
        for j in range(HEADS_B):
            lam = lam_all[j:j + 1, :]
            q = q_ref[rows, j * 128:(j + 1) * 128]
            k0s = [_pad_rows(k[s, j * 128:j * 128 + HEAD_DIM, :], True) for k in k_refs]
            k1s = [_pad_rows(k[s, j * 128 + HEAD_DIM:(j + 1) * 128, :], False) for k in k_refs]
            e0 = _exp_parts([_dot(q, k0) for k0 in k0s])
            e1 = _exp_parts([_dot(q, k1) for k1 in k1s])
            o0 = o1 = None
            for a0, a1, v in zip(e0, e1, v_refs):
                val = v[s, :, j * 128:(j + 1) * 128]
                if stack_maps:
                    t = _dot(jnp.concatenate([a0.astype(BF16), a1.astype(BF16)], axis=0), val)
                    t0, t1 = t[0:tq], t[tq:2 * tq]
                else:
                    t0, t1 = _dot(a0.astype(BF16), val), _dot(a1.astype(BF16), val)
                o0 = t0 if o0 is None else o0 + t0
                o1 = t1 if o1 is None else o1 + t1
            o = o0 * (1.0 / _row_sum(e0)) - o1 * (lam / _row_sum(e1))
            o = _rms(o, gd_ref[...]) * (1.0 - lam_init)
            o_ref[rows, j * 128:(j + 1) * 128] = o.astype(BF16)


def _attention(kind, q, ks, vs, set_rows, tq, sets_per_step, extra, extra_kw):
    t, qw = q.shape
    n_sets = t // set_rows
    q_tiles = set_rows // tq
    assert sets_per_step == 1 or q_tiles == 1
    in_specs = [pl.BlockSpec((sets_per_step * tq, qw), lambda b, i: (b * q_tiles + i, 0))]
    for a in list(ks) + list(vs):
        in_specs.append(pl.BlockSpec((sets_per_step,) + a.shape[1:], lambda b, i: (b, 0, 0)))
    for e in extra:
        in_specs.append(pl.BlockSpec(e.shape, lambda b, i: (0, 0)))
    body = functools.partial(_gqa_kernel if kind == "gqa" else _diff_kernel, n_src=len(ks), **extra_kw)
    return pl.pallas_call(
        body,
        grid=(n_sets // sets_per_step, q_tiles),
        in_specs=in_specs,
        out_specs=pl.BlockSpec((sets_per_step * tq, qw), lambda b, i: (b * q_tiles + i, 0)),
        out_shape=jax.ShapeDtypeStruct((t, qw), BF16),
        name=f"{kind}_attn_{len(ks)}src",
        compiler_params=_cparams(2),
    )(q, *ks, *vs, *extra)


def _post_kernel(x_ref, ya_ref, yb_ref, mod_ref, g1_ref, g2_ref, win_ref, wba_ref, wbb_ref, wo_ref, wr2_ref,
                 x1_o, h2_o, aff_o):
    d = x_ref.shape[1]
    g0 = win_ref.shape[1] - 2 * d
    mod = mod_ref[0]
    for r0 in range(0, x_ref.shape[0], CHAIN_ROWS):
        rows = slice(r0, r0 + CHAIN_ROWS)
        x = x_ref[rows, :]
        h = (_rms(x, g1_ref[...]) * (1.0 + mod[1:2]) + mod[0:1]).astype(BF16)
        ga = jax.nn.sigmoid(_dot(h, win_ref[:, g0:g0 + d]))
        merged = ga * _dot(ya_ref[rows, :], wba_ref[...])
        gb = jax.nn.sigmoid(_dot(h, win_ref[:, g0 + d:g0 + 2 * d]))
        merged = merged + gb * _dot(yb_ref[rows, :], wbb_ref[...])
        m = _dot(merged.astype(BF16), wo_ref[...])
        x1 = x + mod[2:3] * m
        x1_o[rows, :] = x1
        h2 = _rms(x1, g2_ref[...]) * (1.0 + mod[4:5]) + mod[3:4]
        h2_o[rows, :] = h2.astype(BF16)
        h2_hi, h2_lo = _split(h2)
        both = _dot(h2_hi, wr2_ref[...])
        logits = both[:, 0:LANES] + both[:, LANES:2 * LANES] + _dot(h2_lo, wr2_ref[:, 0:LANES])
        lane = lax.broadcasted_iota(jnp.int32, logits.shape, 1)
        logits = jnp.where(lane < N_EXPERTS, logits, NEG_BIG)
        e = jnp.exp(logits - logits.max(axis=-1, keepdims=True))
        aff_o[rows, :] = e / e.sum(axis=-1, keepdims=True)


def _post(x2, ya, yb, mod, set_rows, g1, g2, w_in, w_ba, w_bb, w_out, w_router2):
    t, d = x2.shape
    tm = POST_ROWS
    tiles_per_set = max(1, set_rows // tm)
    assert t % tm == 0 and tm % CHAIN_ROWS == 0 and (mod.shape[0] == 1 or set_rows % tm == 0)
    row = lambda i: (i, 0)
    const = lambda i: (0, 0)
    once = pl.Buffered(1)
    half = w_in.shape[1] // 2
    assert w_in.shape[1] == 2 * half and half % LANES == 0 and half >= 2 * d
    return pl.pallas_call(
        _post_kernel,
        grid=(t // tm,),
        in_specs=[pl.BlockSpec((tm, d), row),
                  pl.BlockSpec((tm, 512), row),
                  pl.BlockSpec((tm, 512), row),
                  pl.BlockSpec((1, 8, d), _mod_index(mod, tiles_per_set)),
                  pl.BlockSpec((1, d), const),
                  pl.BlockSpec((1, d), const),
                  pl.BlockSpec((d, half), lambda i: (0, 1), pipeline_mode=once),
                  pl.BlockSpec(w_ba.shape, const, pipeline_mode=once),
                  pl.BlockSpec(w_bb.shape, const, pipeline_mode=once),
                  pl.BlockSpec(w_out.shape, const, pipeline_mode=once),
                  pl.BlockSpec(w_router2.shape, const, pipeline_mode=once)],
        out_specs=[pl.BlockSpec((tm, d), row), pl.BlockSpec((tm, d), row), pl.BlockSpec((tm, LANES), row)],
        out_shape=[jax.ShapeDtypeStruct((t, d), F32), jax.ShapeDtypeStruct((t, d), BF16),
                   jax.ShapeDtypeStruct((t, LANES), F32)],
        name="post_attn",
        compiler_params=_cparams(1),
    )(x2, ya, yb, mod, g1, g2, w_in, w_ba, w_bb, w_out, w_router2)


def _select_kernel(aff_ref, slot_o, slot_t_o, affb_o, *, cap):
    aff = aff_ref[0]
    n = aff.shape[0]
    capf = float(cap)

    def enough(cand):
        return jnp.sum(jnp.where(aff >= cand, 1.0, 0.0), axis=0, keepdims=True) >= capf

    pw = jnp.ones((1, LANES), F32)
    for k in (64, 32, 16, 8, 4, 2, 1):
        pw = jnp.where(enough(pw * 2.0 ** -(k - 1)), pw, pw * 2.0 ** -k)

    def mantissa_step(_, carry):
        thr, step = carry
        cand = thr + step
        return jnp.where(enough(cand), cand, thr), step * 0.5

    thr, _ = lax.fori_loop(0, 23, mantissa_step, (pw, pw * 0.5))
    above = aff > thr
    tied = aff == thr
    need = capf - jnp.sum(jnp.where(above, 1.0, 0.0), axis=0, keepdims=True)
    r_i = lax.broadcasted_iota(jnp.int32, (n, n), 0)
    c_i = lax.broadcasted_iota(jnp.int32, (n, n), 1)
    before = jnp.where(c_i < r_i, 1.0, 0.0).astype(BF16)
    tie_rank = _dot(before, jnp.where(tied, 1.0, 0.0).astype(BF16))
    sel = above | (tied & (tie_rank < need))
    slot = _dot(before, jnp.where(sel, 1.0, 0.0).astype(BF16))
    slot = jnp.where(sel, slot, NOT_SELECTED)
    slot_o[0] = slot.astype(BF16)
    slot_t_o[0] = slot.T
    affb_o[0] = aff.astype(BF16)


def _dispatch_kernel(slot_t_ref, h2_ref, xe_o, *, cap):
    n = slot_t_ref.shape[2]
    g = h2_ref.shape[0] // n
    per = min(N_EXPERTS, DISPATCH_ROWS // cap)
    slot_iota = lax.broadcasted_iota(jnp.int32, (cap, n), 0).astype(F32)
    for k in range(g):
        h2 = h2_ref[k * n:(k + 1) * n, :]
        slot_t = slot_t_ref[0, k * N_EXPERTS:(k + 1) * N_EXPERTS, :]
        for e0 in range(0, N_EXPERTS, per):
            onehot = jnp.concatenate(
                [jnp.where(slot_t[e:e + 1, :] == slot_iota, 1.0, 0.0) for e in range(e0, e0 + per)],
                axis=0).astype(BF16)
            rows = _dot(onehot, h2).astype(BF16)
            for j in range(per):
                xe_o[e0 + j, k * cap:(k + 1) * cap, :] = rows[j * cap:(j + 1) * cap]


def _route(aff, h2, n_sets, set_rows):
    t, d = h2.shape
    n = set_rows
    cap = CAPACITY_FACTOR * n // N_EXPERTS
    assert cap & (cap - 1) == 0 and cap % 16 == 0 and DISPATCH_ROWS % cap == 0
    n_packs = -(-n_sets // SETS_PER_PACK)
    aff16 = aff[:, :N_EXPERTS].reshape(n_sets, n, N_EXPERTS)
    aff16 = jnp.pad(aff16, ((0, n_packs * SETS_PER_PACK - n_sets), (0, 0), (0, 0)))
    packed = aff16.reshape(n_packs, SETS_PER_PACK, n, N_EXPERTS).transpose(0, 2, 1, 3).reshape(n_packs, n, LANES)
    pack_blk = lambda p: (p, 0, 0)
    slot, slot_t, affb = pl.pallas_call(
        functools.partial(_select_kernel, cap=cap),
        grid=(n_packs,),
        in_specs=[pl.BlockSpec((1, n, LANES), pack_blk)],
        out_specs=[pl.BlockSpec((1, n, LANES), pack_blk), pl.BlockSpec((1, LANES, n), pack_blk),
                   pl.BlockSpec((1, n, LANES), pack_blk)],
        out_shape=[jax.ShapeDtypeStruct((n_packs, n, LANES), BF16),
                   jax.ShapeDtypeStruct((n_packs, LANES, n), F32),
                   jax.ShapeDtypeStruct((n_packs, n, LANES), BF16)],
        name=f"select_cap{cap}",
        compiler_params=_cparams(1),
    )(packed)
    g = _sets_per_step(n_sets, n)
    per_pack = SETS_PER_PACK // g
    xe = pl.pallas_call(
        functools.partial(_dispatch_kernel, cap=cap),
        grid=(n_sets // g,),
        in_specs=[pl.BlockSpec((1, g * N_EXPERTS, n), lambda s: (s // per_pack, s % per_pack, 0)),
                  pl.BlockSpec((g * n, d), lambda s: (s, 0))],
        out_specs=pl.BlockSpec((N_EXPERTS, g * cap, d), lambda s: (0, s, 0)),
        out_shape=jax.ShapeDtypeStruct((N_EXPERTS, n_sets * cap, d), BF16),
        name=f"dispatch_cap{cap}",
        compiler_params=_cparams(1),
    )(slot_t, h2)
    return xe, (slot, affb)


def _expert_kernel(*refs, n_groups):
    x_refs = refs[:n_groups]
    wg_ref, wu_ref, wd_ref = refs[n_groups:n_groups + 3]
    o_refs = refs[n_groups + 3:2 * n_groups + 3]
    wg_s, wu_s, wd_s = refs[2 * n_groups + 3:]
    wg_s[...] = wg_ref[0].astype(BF16)
    wu_s[...] = wu_ref[0].astype(BF16)
    wd_s[...] = wd_ref[0].astype(BF16)
    sub = EXPERT_ROWS
    for x_ref, o_ref in zip(x_refs, o_refs):
        for r0 in range(0, x_ref.shape[1], sub):
            x = x_ref[0, r0:r0 + sub, :]
            a = _dot(x, wg_s[...])
            u = _dot(x, wu_s[...])
            hmid = (a * jax.nn.sigmoid(a) * u).astype(BF16)
            o_ref[0, r0:r0 + sub, :] = _dot(hmid, wd_s[...]).astype(BF16)


def _experts(xes, w_gate, w_up, w_down):
    e, d, f = w_gate.shape
    blk = lambda i: (i, 0, 0)
    return pl.pallas_call(
        functools.partial(_expert_kernel, n_groups=len(xes)),
        grid=(e,),
        in_specs=[pl.BlockSpec((1, x.shape[1], d), blk) for x in xes]
        + [pl.BlockSpec((1, d, f), blk), pl.BlockSpec((1, d, f), blk), pl.BlockSpec((1, f, d), blk)],
        out_specs=[pl.BlockSpec((1, x.shape[1], d), blk) for x in xes],
        out_shape=[jax.ShapeDtypeStruct(x.shape, BF16) for x in xes],
        scratch_shapes=[pltpu.VMEM((d, f), BF16), pltpu.VMEM((d, f), BF16), pltpu.VMEM((f, d), BF16)],
        name="experts",
        compiler_params=_cparams(1),
    )(*xes, w_gate, w_up, w_down)


def _combine_kernel(slot_ref, affb_ref, ye_ref, x1_ref, mod_ref, gf_ref, y_o, *, cap):
    e, gcap, d = ye_ref.shape
    g = gcap // cap
    n = x1_ref.shape[0] // g
    shift = cap.bit_length() - 1
    both = jnp.concatenate([slot_ref[0], affb_ref[0]], axis=1)
    want = (lax.broadcasted_iota(jnp.int32, (n, DISPATCH_ROWS), 1) & (cap - 1)).astype(F32) * SLOT_RADIX
    for k in range(g):
        rows = slice(k * n, (k + 1) * n)
        first_lane = ((pl.program_id(0) * g + k) % SETS_PER_PACK) * N_EXPERTS
        pieces = []
        for c0 in range(0, e * cap, DISPATCH_ROWS):
            src = lax.broadcasted_iota(jnp.int32, (2 * LANES, DISPATCH_ROWS), 0)
            col = lax.broadcasted_iota(jnp.int32, (2 * LANES, DISPATCH_ROWS), 1) + c0
            lane_of_col = first_lane + lax.shift_right_logical(col, shift)
            spread = jnp.where(src == lane_of_col, SLOT_RADIX,
                               jnp.where(src == lane_of_col + LANES, 1.0, 0.0)).astype(BF16)
            rest = _dot(both, spread) - want
            pieces.append(jnp.where(rest >= 0.0, jnp.where(rest <= 1.0, rest, 0.0), 0.0).astype(BF16))
        gt = pieces[0] if len(pieces) == 1 else jnp.concatenate(pieces, axis=1)
        ye = ye_ref[:, k * cap:(k + 1) * cap, :].reshape(e * cap, d)
        moe = _dot(gt, ye)
        x = x1_ref[rows, :] + mod_ref[0][5:6] * moe
        y_o[rows, :] = _rms(x, gf_ref[...])


def _sets_per_step(n_sets, set_rows):
    g = SETS_PER_PACK if set_rows * SETS_PER_PACK <= SHARED_STEP_ROWS else 1
    return g if n_sets % g == 0 else 1


def _combine(packs, ye, x1, mod, n_sets, set_rows, g_final):
    slot, affb = packs
    t, d = x1.shape
    cap = ye.shape[1] // n_sets
    g = _sets_per_step(n_sets, set_rows) if mod.shape[0] == 1 else 1
    tm = min(ROW_TILE, set_rows) if g == 1 else g * set_rows
    tiles = max(1, set_rows // tm)
    per_pack = SETS_PER_PACK // g
    pack_rows = pl.BlockSpec((1, tm // g, LANES), lambda s, i: (s // per_pack, i, 0))
    return pl.pallas_call(
        functools.partial(_combine_kernel, cap=cap),
        grid=(n_sets // g, tiles),
        in_specs=[pack_rows, pack_rows,
                  pl.BlockSpec((N_EXPERTS, g * cap, d), lambda s, i: (0, s, 0)),
                  pl.BlockSpec((tm, d), lambda s, i: (s * tiles + i, 0)),
                  pl.BlockSpec((1, 8, d), (lambda s, i: (s, 0, 0)) if mod.shape[0] > 1 else (lambda s, i: (0, 0, 0))),
                  pl.BlockSpec((1, d), lambda s, i: (0, 0))],
        out_specs=pl.BlockSpec((tm, d), lambda s, i: (s * tiles + i, 0)),
        out_shape=jax.ShapeDtypeStruct((t, d), F32),
        name=f"combine_cap{cap}",
        compiler_params=_cparams(2),
    )(slot, affb, ye, x1, mod, g_final)


def _rope_tables(n_tokens):
    n_rows = n_tokens // GRID_W
    rowp = jnp.repeat(jnp.arange(n_rows), GRID_W).astype(F32)
    colp = jnp.tile(jnp.arange(GRID_W), n_rows).astype(F32)
    quarter = HEAD_DIM // 4
    freqs = ROPE_THETA ** (-jnp.arange(quarter, dtype=F32) / quarter)
    ang = jnp.stack([rowp[:, None] * freqs, colp[:, None] * freqs], axis=1)
    cos, sin = jnp.cos(ang), jnp.sin(ang)
    zero = jnp.zeros_like(sin)
    c = jnp.stack([cos, cos], axis=2).reshape(n_tokens, HEAD_DIM)
    s_up = jnp.stack([-sin, zero], axis=2).reshape(n_tokens, HEAD_DIM)
    s_dn = jnp.stack([zero, sin], axis=2).reshape(n_tokens, HEAD_DIM)
    return tuple(jnp.tile(t, (1, LANES // HEAD_DIM)) for t in (c, s_up, s_dn))


def _mix_and_route(x2, n_sets, set_rows, mod, lw, rope_tabs, cache, emit_cache, lam_init, tq):
    (g_attn, g_ffn, w_in, gq_t, gk_t, lq1, lk1, lq2, lk2, g_diff, w_ba, w_bb, w_out, w_router2) = lw
    outs = _inproj(x2, mod, set_rows, g_attn, w_in, gq_t, gk_t, rope_tabs, emit_cache)
    qa, kta, vta, qb, ktb, vb16 = outs[:6]
    vb16 = vb16.reshape(n_sets, set_rows, vb16.shape[1])
    ks_a, vs_a, ks_b, vs_b = [kta], [vta], [ktb], [vb16]
    if cache is not None:
        ckta, cvta, cktb, cvb = cache
        ks_a, vs_a, ks_b, vs_b = [ckta, kta], [cvta, vta], [cktb, ktb], [cvb, vb16]
    per_step = CONTEXT_SETS_PER_STEP if (tq == set_rows and n_sets % CONTEXT_SETS_PER_STEP == 0) else 1
    ya = _attention("gqa", qa, ks_a, vs_a, set_rows, tq, per_step, (), {"mxu_sums": cache is not None})
    yb = _attention("diff", qb, ks_b, vs_b, set_rows, tq, per_step, (lq1, lk1, lq2, lk2, g_diff),
                    {"lam_init": lam_init, "stack_maps": cache is None})
    x1, h2, aff = _post(x2, ya, yb, mod, set_rows, g_attn, g_ffn, w_in, w_ba, w_bb, w_out, w_router2)
    xe, gt = _route(aff, h2, n_sets, set_rows)
    return x1, xe, gt, outs[6:]


def kernel(x_prompt, x_sample, cache_attn_k, cache_attn_v, cache_diff_k, cache_diff_v, c, c_ctx, w_mod, b_mod,
           g_attn_norm, g_ffn_norm, w_in, g_q_norm, g_k_norm, lambda_q1, lambda_k1, lambda_q2, lambda_k2,
           g_diff_norm, w_branch_a, w_branch_b, w_out, w_router, w_exp_gate, w_exp_up, w_exp_down, g_final):
    batch, seq, d = x_prompt.shape
    dec_batch, dec_seq, _ = x_sample.shape
    depth = w_in.shape[0]
    assert depth == 1, "the final norm is fused into the layer's combine step"
    past = cache_attn_k.shape[2]
    assert w_in.shape[2] == N_QKV + 2 * d

    xp = x_prompt.reshape(batch * seq, d)
    xs = x_sample.reshape(dec_batch * dec_seq, d)
    rope_tabs = _rope_tables(dec_seq)
    c_rows = jnp.concatenate([c, c_ctx[None, :], jnp.zeros((16 - dec_batch - 1, d), F32)], axis=0)
    yp = ys = None
    caches = []
    for l in range(depth):
        lam_init = 0.8 - 0.6 * math.exp(-0.3 * l)
        mod6 = _modulation(c_rows, w_mod[l], b_mod[l])
        mod = jnp.pad(jnp.transpose(mod6, (1, 0, 2)), ((0, 0), (0, 2), (0, 0)))
        mod_lat, mod_ctx = mod[:dec_batch], mod[dec_batch:dec_batch + 1]
        w_in16, w_ba16, w_bb16, w_out16, w_router2 = _round_weights(
            [w_in[l], w_branch_a[l], w_branch_b[l], w_out[l]],
            jnp.pad(w_router[l], ((0, 0), (0, LANES - N_EXPERTS))))
        lw = (g_attn_norm[l][None, :], g_ffn_norm[l][None, :], w_in16,
              jnp.tile(g_q_norm[l], HEADS_A)[None, :], jnp.tile(g_k_norm[l], KV_HEADS_A)[None, :],
              lambda_q1[l], lambda_k1[l], lambda_q2[l], lambda_k2[l], g_diff_norm[l][None, :],
              w_ba16, w_bb16, w_out16, w_router2)
        x1p, xe_p, gt_p, cache_out = _mix_and_route(xp, batch, seq, mod_ctx, lw, None, None, True, lam_init, seq)
        caches.append(cache_out)
        feat_major = lambda a: jnp.moveaxis(a[:, l], 1, -1).reshape(dec_batch, -1, past)
        cache_l = _cache_prep([feat_major(cache_attn_k), feat_major(cache_attn_v), feat_major(cache_diff_k)],
                              cache_diff_v[:, l].reshape(dec_batch, past * HEADS_B, -1))
        x1s, xe_s, gt_s, _ = _mix_and_route(xs, dec_batch, dec_seq, mod_lat, lw, rope_tabs, cache_l, False,
                                            lam_init, LATENT_Q_TILE)
        ye_p, ye_s = _experts([xe_p, xe_s], w_exp_gate[l], w_exp_up[l], w_exp_down[l])
        yp = _combine(gt_p, ye_p, x1p, mod_ctx, batch, seq, g_final[None, :])
        ys = _combine(gt_s, ye_s, x1s, mod_lat, dec_batch, dec_seq, g_final[None, :])
    y_prompt = yp.reshape(batch, seq, d)
    y_sample = ys.reshape(dec_batch, dec_seq, d)
    tok_major = lambda a, dims: jnp.moveaxis(a.reshape((batch,) + dims + (seq,)), -1, 1)
    new_attn_k = jnp.stack([tok_major(cc[0], (KV_HEADS_A, HEAD_DIM)) for cc in caches], axis=1)
    new_attn_v = jnp.stack([tok_major(cc[1], (KV_HEADS_A, HEAD_DIM)) for cc in caches], axis=1)
    new_diff_k = jnp.stack([tok_major(cc[2], (HEADS_B, 2, HEAD_DIM)) for cc in caches], axis=1)
    new_diff_v = jnp.stack([cc[3].reshape(batch, seq, HEADS_B, 2 * HEAD_DIM) for cc in caches], axis=1)
    return (y_prompt, y_sample, new_attn_k, new_attn_v, new_diff_k, new_diff_v)
```

```python
import functools
import math

import numpy as np
import jax
import jax.numpy as jnp
from jax import lax
from jax.experimental import pallas as pl
from jax.experimental.pallas import tpu as pltpu

F32 = jnp.float32
BF16 = jnp.bfloat16

HEAD_DIM = 64
HEADS_A = 8
KV_HEADS_A = 2
HEADS_B = 4
N_QKV = HEADS_A * HEAD_DIM + 2 * KV_HEADS_A * HEAD_DIM + 3 * HEADS_B * 2 * HEAD_DIM
N_EXPERTS = 16
CAPACITY_FACTOR = 2
GRID_W = 64
ROPE_THETA = 10000.0
EPS = 1e-6
LANES = 128
ROW_TILE = 1024
POST_ROWS = 1024
CHAIN_ROWS = 512
EXPERT_ROWS = 256
LATENT_Q_TILE = 512
CONTEXT_SETS_PER_STEP = 4
SETS_PER_PACK = LANES // N_EXPERTS
SHARED_STEP_ROWS = 2048
DISPATCH_ROWS = 512
NEG_BIG = -1e30
NOT_SELECTED = -1.0
SLOT_RADIX = 256.0
VMEM_LIMIT = 56 * 1024 * 1024
ROUND_STEPS = 8


def _cparams(n_axes):
    return pltpu.CompilerParams(dimension_semantics=("arbitrary",) * n_axes,
                                vmem_limit_bytes=VMEM_LIMIT)


def _dot(a, b):
    return jnp.dot(a, b, preferred_element_type=F32)


def _dot_nt(a, b):
    return lax.dot_general(a, b, (((1,), (1,)), ((), ())), preferred_element_type=F32)


def _split(a):
    hi = a.astype(BF16)
    lo = (a - hi.astype(F32)).astype(BF16)
    return hi, lo


def _dot3(a, b):
    a_hi, a_lo = _split(a)
    b_hi, b_lo = _split(b)
    return _dot(a_hi, b_hi) + _dot(a_lo, b_hi) + _dot(a_hi, b_lo)


def _rms(x, g):
    return x * lax.rsqrt(jnp.mean(x * x, axis=-1, keepdims=True) + EPS) * g


def _mod_index(mod, tiles_per_set):
    if mod.shape[0] == 1:
        return lambda i: (0, 0, 0)
    return lambda i: (i // tiles_per_set, 0, 0)


def _mod_kernel(c_ref, w_ref, b_ref, o_ref):
    c = c_ref[...]
    a = c * jax.nn.sigmoid(c)
    o_ref[0] = _dot3(a, w_ref[...]) + b_ref[0]


def _modulation(c_rows, w_mod, b_mod):
    r, d = c_rows.shape
    return pl.pallas_call(
        _mod_kernel,
        grid=(6,),
        in_specs=[pl.BlockSpec((r, d), lambda j: (0, 0)),
                  pl.BlockSpec((d, d), lambda j: (0, j)),
                  pl.BlockSpec((1, 1, d), lambda j: (j, 0, 0))],
        out_specs=pl.BlockSpec((1, r, d), lambda j: (j, 0, 0)),
        out_shape=jax.ShapeDtypeStruct((6, r, d), F32),
        name="mod",
        compiler_params=_cparams(1),
    )(c_rows, w_mod, b_mod.reshape(6, 1, d))


def _round_weights_kernel(*refs):
    n = len(refs) // 2
    for src, out in zip(refs[:n - 1], refs[n:-1]):
        out[...] = src[...].astype(BF16)
    hi, lo = _split(refs[n - 1][...])
    refs[-1][...] = jnp.concatenate([hi, lo], axis=1)


def _round_weights(weights, w_router_p):
    steps = ROUND_STEPS
    arrays = list(weights) + [w_router_p]
    blk = lambda a, cols: pl.BlockSpec((a.shape[0] // steps, cols), lambda i: (i, 0))
    out_cols = [a.shape[1] for a in weights] + [2 * w_router_p.shape[1]]
    return pl.pallas_call(
        _round_weights_kernel,
        grid=(steps,),
        in_specs=[blk(a, a.shape[1]) for a in arrays],
        out_specs=[blk(a, c) for a, c in zip(arrays, out_cols)],
        out_shape=[jax.ShapeDtypeStruct((a.shape[0], c), BF16) for a, c in zip(arrays, out_cols)],
        name="round_weights",
        compiler_params=_cparams(1),
    )(*arrays)


def _seg_sumsq(x, ones_blockdiag):
    return _dot((x * x).astype(BF16), ones_blockdiag)


def _rope(x, c, s_up, s_dn):
    w = x.shape[1]
    reps = w // c.shape[1]
    if reps > 1:
        c, s_up, s_dn = (jnp.concatenate([t] * reps, axis=1) for t in (c, s_up, s_dn))
    return x * c + pltpu.roll(x, w - 16, 1) * s_up + pltpu.roll(x, 16, 1) * s_dn


def _inproj_kernel(*refs, rope, emit_cache):
    (x_ref, mod_ref, g_ref, w_ref, gq_ref, gk_ref, bd512_ref, bd128_ref), refs = refs[:8], refs[8:]
    if rope:
        (c_ref, su_ref, sd_ref), refs = refs[:3], refs[3:]
    qa_o, kta_o, vta_o, qb_o, ktb_o, vb_o = refs[:6]
    if emit_cache:
        ka_c, va_c, kb_c, vb_c = refs[6:10]

    mod = mod_ref[0]
    wa = HEADS_A * HEAD_DIM
    wkv = KV_HEADS_A * HEAD_DIM
    wb = HEADS_B * 2 * HEAD_DIM
    o_ka, o_va, o_qb = wa, wa + wkv, wa + 2 * wkv
    o_kb, o_vb = o_qb + wb, o_qb + 2 * wb
    scale = HEAD_DIM ** -0.5 * math.log2(math.e)

    x = x_ref[...]
    h = (_rms(x, g_ref[...]) * (1.0 + mod[1:2]) + mod[0:1]).astype(BF16)
    if rope:
        tabs = (c_ref[...], su_ref[...], sd_ref[...])

    qa = _dot(h, w_ref[:, 0:wa])
    qa = qa * lax.rsqrt(_seg_sumsq(qa, bd512_ref[...]) * (1.0 / HEAD_DIM) + EPS) * gq_ref[...]
    if rope:
        qa = _rope(qa, *tabs)
    qa_o[...] = (qa * scale).astype(BF16)

    kv = _dot(h, w_ref[:, o_ka:o_qb])
    ka, va = kv[:, 0:wkv], kv[:, wkv:2 * wkv]
    ka = ka * lax.rsqrt(_seg_sumsq(ka, bd128_ref[...]) * (1.0 / HEAD_DIM) + EPS) * gk_ref[...]

    def put_feat(val, out_bf16, out_f32):
        rows = out_bf16.shape[2]
        for s in range(out_bf16.shape[0]):
            t = val[s * rows:(s + 1) * rows].T
            if out_f32 is not None:
                out_f32[s] = t
            out_bf16[s] = t.astype(BF16)

    if rope:
        put_feat(_rope(ka, *tabs), kta_o, None)
    else:
        put_feat(ka, kta_o, ka_c if emit_cache else None)
    put_feat(va, vta_o, va_c if emit_cache else None)

    qb = _dot(h, w_ref[:, o_qb:o_kb])
    if rope:
        qb = _rope(qb, *tabs)
    qb_o[...] = (qb * scale).astype(BF16)

    kb = _dot(h, w_ref[:, o_kb:o_vb])
    if rope:
        put_feat(_rope(kb, *tabs), ktb_o, None)
    else:
        put_feat(kb, ktb_o, kb_c if emit_cache else None)

    vb = _dot(h, w_ref[:, o_vb:o_vb + wb])
    if emit_cache:
        for hd in range(HEADS_B):
            vb_c[pl.ds(hd, vb.shape[0], stride=HEADS_B), :] = vb[:, hd * 128:(hd + 1) * 128]
    vb_o[...] = vb.astype(BF16)


def _blockdiag_ones(width):
    g = np.arange(width) // HEAD_DIM
    return jnp.asarray((g[:, None] == g[None, :]).astype(np.float32), dtype=BF16)


def _inproj(x2, mod, set_rows, g_attn, w_in, gq_t, gk_t, rope_tabs, emit_cache):
    t, d = x2.shape
    tm = ROW_TILE
    assert t % tm == 0 and (tm % set_rows == 0 or set_rows % tm == 0)
    tiles_per_set = max(1, set_rows // tm)
    sets_per_tile = max(1, tm // set_rows)
    rope = rope_tabs is not None
    assert not (rope and emit_cache), "cached keys are the position-free ones"
    nq = N_QKV
    row = lambda i: (i, 0)
    const = lambda i: (0, 0)
    in_specs = [pl.BlockSpec((tm, d), row),
                pl.BlockSpec((1, 8, d), _mod_index(mod, tiles_per_set)),
                pl.BlockSpec((1, d), const),
                pl.BlockSpec((d, nq), const, pipeline_mode=pl.Buffered(1)),
                pl.BlockSpec((1, 512), const),
                pl.BlockSpec((1, 128), const),
                pl.BlockSpec((512, 512), const),
                pl.BlockSpec((128, 128), const)]
    args = [x2, mod, g_attn, w_in, gq_t, gk_t, _blockdiag_ones(512), _blockdiag_ones(128)]
    if rope:
        in_specs += [pl.BlockSpec((tm, LANES), lambda i: (i % tiles_per_set, 0))] * 3
        args += list(rope_tabs)
    n_sets = t // set_rows
    wkv, wb = KV_HEADS_A * HEAD_DIM, HEADS_B * 2 * HEAD_DIM
    outs = [("tok", 512, BF16), ("feat", wkv, BF16), ("feat", wkv, BF16),
            ("tok", 512, BF16), ("feat", wb, BF16), ("tok", wb, BF16)]
    if emit_cache:
        outs += [("feat", wkv, F32), ("feat", wkv, F32), ("feat", wb, F32), ("tokhead", wb, F32)]
    feat = lambda i: (i // tiles_per_set, 0, i % tiles_per_set)
    feat_rows = min(tm, set_rows)

    def out_block(kind, w):
        if kind == "tok":
            return pl.BlockSpec((tm, w), row)
        if kind == "tokhead":
            return pl.BlockSpec((tm * HEADS_B, w // HEADS_B), row)
        return pl.BlockSpec((sets_per_tile, w, feat_rows), feat)

    def out_array(kind, w, dt):
        shape = {"tok": (t, w), "tokhead": (t * HEADS_B, w // HEADS_B), "feat": (n_sets, w, set_rows)}[kind]
        return jax.ShapeDtypeStruct(shape, dt)

    return pl.pallas_call(
        functools.partial(_inproj_kernel, rope=rope, emit_cache=emit_cache),
        grid=(t // tm,),
        in_specs=in_specs,
        out_specs=[out_block(kind, w) for kind, w, _ in outs],
        out_shape=[out_array(kind, w, dt) for kind, w, dt in outs],
        name="inproj_rope" if rope else "inproj",
        compiler_params=_cparams(1),
    )(*args)


def _cache_prep_kernel(*refs):
    half = len(refs) // 2
    for src, out in zip(refs[:half - 1], refs[half:-1]):
        out[...] = src[...].astype(BF16)
    v_src, v_out = refs[half - 1], refs[-1]
    keys = v_out.shape[1]
    for hd in range(HEADS_B):
        v_out[0, :, hd * 128:(hd + 1) * 128] = v_src[0, pl.ds(hd, keys, stride=HEADS_B), :].astype(BF16)


def _cache_prep(arrays, v_by_head):
    blk = lambda b: (b, 0, 0)
    n_sets, rows, width = v_by_head.shape
    v_shape = (n_sets, rows // HEADS_B, width * HEADS_B)
    return pl.pallas_call(
        _cache_prep_kernel,
        grid=(n_sets,),
        in_specs=[pl.BlockSpec((1,) + a.shape[1:], blk) for a in arrays + [v_by_head]],
        out_specs=[pl.BlockSpec((1,) + a.shape[1:], blk) for a in arrays] + [pl.BlockSpec((1,) + v_shape[1:], blk)],
        out_shape=[jax.ShapeDtypeStruct(a.shape, BF16) for a in arrays] + [jax.ShapeDtypeStruct(v_shape, BF16)],
        name="cache_prep",
        compiler_params=_cparams(1),
    )(*arrays, v_by_head)


def _exp_parts(scores):
    m = scores[0].max(axis=-1, keepdims=True)
    for s in scores[1:]:
        m = jnp.maximum(m, s.max(axis=-1, keepdims=True))
    return [jnp.exp2(s - m) for s in scores]


def _row_sum(parts):
    l = parts[0].sum(axis=-1, keepdims=True)
    for e in parts[1:]:
        l = l + e.sum(axis=-1, keepdims=True)
    return l


def _pad_rows(x, first, ones_row=False):
    if ones_row:
        z = jnp.where(lax.broadcasted_iota(jnp.int32, x.shape, 0) == 0, 1.0, 0.0).astype(x.dtype)
    else:
        z = jnp.zeros_like(x)
    return jnp.concatenate([x, z] if first else [z, x], axis=0)


def _gqa_kernel(*refs, n_src, mxu_sums):
    q_ref = refs[0]
    k_refs = refs[1:1 + n_src]
    v_refs = refs[1 + n_src:1 + 2 * n_src]
    o_ref = refs[1 + 2 * n_src]
    sets_here = k_refs[0].shape[0]
    tq = q_ref.shape[0] // sets_here
    lane = lax.broadcasted_iota(jnp.int32, (2 * tq, LANES), 1)
    lo_half = lane < HEAD_DIM
    for s in range(sets_here):
        rows = slice(s * tq, (s + 1) * tq)
        for g in range(KV_HEADS_A):
            c0 = g * 256
            f0 = g * HEAD_DIM
            q = jnp.concatenate([q_ref[rows, c0:c0 + 128], q_ref[rows, c0 + 128:c0 + 256]], axis=0)
            kts = [k[s, f0:f0 + HEAD_DIM, :] for k in k_refs]
            vts = [v[s, f0:f0 + HEAD_DIM, :] for v in v_refs]
            e_lo = _exp_parts([_dot(q, _pad_rows(kt, True)) for kt in kts])
            e_hi = _exp_parts([_dot(q, _pad_rows(kt, False)) for kt in kts])
            o_lo = o_hi = None
            for e_part, vt in zip(e_lo, vts):
                t = _dot_nt(e_part.astype(BF16), _pad_rows(vt, True, ones_row=mxu_sums))
                o_lo = t if o_lo is None else o_lo + t
            for e_part, vt in zip(e_hi, vts):
                t = _dot_nt(e_part.astype(BF16), _pad_rows(vt, False, ones_row=mxu_sums))
                o_hi = t if o_hi is None else o_hi + t
            if mxu_sums:
                l_lo, l_hi = o_lo[:, HEAD_DIM:HEAD_DIM + 1], o_hi[:, 0:1]
                o = jnp.where(lo_half, o_lo * (1.0 / l_lo), o_hi * (1.0 / l_hi))
            else:
                o = (o_lo + o_hi) * jnp.where(lo_half, 1.0 / _row_sum(e_lo), 1.0 / _row_sum(e_hi))
            o_ref[rows, c0:c0 + 128] = o[0:tq].astype(BF16)
            o_ref[rows, c0 + 128:c0 + 256] = o[tq:2 * tq].astype(BF16)


def _diff_kernel(*refs, n_src, lam_init, stack_maps):
    q_ref = refs[0]
    k_refs = refs[1:1 + n_src]
    v_refs = refs[1 + n_src:1 + 2 * n_src]
    lq1, lk1, lq2, lk2, gd_ref, o_ref = refs[1 + 2 * n_src:]
    lam_all = (jnp.exp(jnp.sum(lq1[...] * lk1[...], axis=-1, keepdims=True))
               - jnp.exp(jnp.sum(lq2[...] * lk2[...], axis=-1, keepdims=True)) + lam_init)
    sets_here = k_refs[0].shape[0]
    tq = q_ref.shape[0] // sets_here
    for s in range(sets_here):
        rows = slice(s * tq, (s + 1) * tq)
        for j in range(HEADS_B):
            lam = lam_all[j:j + 1, :]
            q = q_ref[rows, j * 128:(j + 1) * 128]
            k0s = [_pad_rows(k[s, j * 128:j * 128 + HEAD_DIM, :], True) for k in k_refs]
            k1s = [_pad_rows(k[s, j * 128 + HEAD_DIM:(j + 1) * 128, :], False) for k in k_refs]
            e0 = _exp_parts([_dot(q, k0) for k0 in k0s])
            e1 = _exp_parts([_dot(q, k1) for k1 in k1s])
            o0 = o1 = None
            for a0, a1, v in zip(e0, e1, v_refs):
                val = v[s, :, j * 128:(j + 1) * 128]
                if stack_maps:
                    t = _dot(jnp.concatenate([a0.astype(BF16), a1.astype(BF16)], axis=0), val)
                    t0, t1 = t[0:tq], t[tq:2 * tq]
                else:
                    t0, t1 = _dot(a0.astype(BF16), val), _dot(a1.astype(BF16), val)
                o0 = t0 if o0 is None else o0 + t0
                o1 = t1 if o1 is None else o1 + t1
            o = o0 * (1.0 / _row_sum(e0)) - o1 * (lam / _row_sum(e1))
            o = _rms(o, gd_ref[...]) * (1.0 - lam_init)
            o_ref[rows, j * 128:(j + 1) * 128] = o.astype(BF16)


def _attention(kind, q, ks, vs, set_rows, tq, sets_per_step, extra, extra_kw):
    t, qw = q.shape
    n_sets = t // set_rows
    q_tiles = set_rows // tq
    assert sets_per_step == 1 or q_tiles == 1
    in_specs = [pl.BlockSpec((sets_per_step * tq, qw), lambda b, i: (b * q_tiles + i, 0))]
    for a in list(ks) + list(vs):
        in_specs.append(pl.BlockSpec((sets_per_step,) + a.shape[1:], lambda b, i: (b, 0, 0)))
    for e in extra:
        in_specs.append(pl.BlockSpec(e.shape, lambda b, i: (0, 0)))
    body = functools.partial(_gqa_kernel if kind == "gqa" else _diff_kernel, n_src=len(ks), **extra_kw)
    return pl.pallas_call(
        body,
        grid=(n_sets // sets_per_step, q_tiles),
        in_specs=in_specs,
        out_specs=pl.BlockSpec((sets_per_step * tq, qw), lambda b, i: (b * q_tiles + i, 0)),
        out_shape=jax.ShapeDtypeStruct((t, qw), BF16),
        name=f"{kind}_attn_{len(ks)}src",
        compiler_params=_cparams(2),
    )(q, *ks, *vs, *extra)


def _post_kernel(x_ref, ya_ref, yb_ref, mod_ref, g1_ref, g2_ref, win_ref, wba_ref, wbb_ref, wo_ref, wr2_ref,
                 x1_o, h2_o, aff_o):
    d = x_ref.shape[1]
    g0 = win_ref.shape[1] - 2 * d
    mod = mod_ref[0]
    for r0 in range(0, x_ref.shape[0], CHAIN_ROWS):
        rows = slice(r0, r0 + CHAIN_ROWS)
        x = x_ref[rows, :]
        h = (_rms(x, g1_ref[...]) * (1.0 + mod[1:2]) + mod[0:1]).astype(BF16)
        ga = jax.nn.sigmoid(_dot(h, win_ref[:, g0:g0 + d]))
        merged = ga * _dot(ya_ref[rows, :], wba_ref[...])
        gb = jax.nn.sigmoid(_dot(h, win_ref[:, g0 + d:g0 + 2 * d]))
        merged = merged + gb * _dot(yb_ref[rows, :], wbb_ref[...])
        m = _dot(merged.astype(BF16), wo_ref[...])
        x1 = x + mod[2:3] * m
        x1_o[rows, :] = x1
        h2 = _rms(x1, g2_ref[...]) * (1.0 + mod[4:5]) + mod[3:4]
        h2_o[rows, :] = h2.astype(BF16)
        h2_hi, h2_lo = _split(h2)
        both = _dot(h2_hi, wr2_ref[...])
        logits = both[:, 0:LANES] + both[:, LANES:2 * LANES] + _dot(h2_lo, wr2_ref[:, 0:LANES])
        lane = lax.broadcasted_iota(jnp.int32, logits.shape, 1)
        logits = jnp.where(lane < N_EXPERTS, logits, NEG_BIG)
        e = jnp.exp(logits - logits.max(axis=-1, keepdims=True))
        aff_o[rows, :] = e / e.sum(axis=-1, keepdims=True)


def _post(x2, ya, yb, mod, set_rows, g1, g2, w_in, w_ba, w_bb, w_out, w_router2):
    t, d = x2.shape
    tm = POST_ROWS
    tiles_per_set = max(1, set_rows // tm)
    assert t % tm == 0 and tm % CHAIN_ROWS == 0 and (mod.shape[0] == 1 or set_rows % tm == 0)
    row = lambda i: (i, 0)
    const = lambda i: (0, 0)
    once = pl.Buffered(1)
    half = w_in.shape[1] // 2
    assert w_in.shape[1] == 2 * half and half % LANES == 0 and half >= 2 * d
    return pl.pallas_call(
        _post_kernel,
        grid=(t // tm,),
        in_specs=[pl.BlockSpec((tm, d), row),
                  pl.BlockSpec((tm, 512), row),
                  pl.BlockSpec((tm, 512), row),
                  pl.BlockSpec((1, 8, d), _mod_index(mod, tiles_per_set)),
                  pl.BlockSpec((1, d), const),
                  pl.BlockSpec((1, d), const),
                  pl.BlockSpec((d, half), lambda i: (0, 1), pipeline_mode=once),
                  pl.BlockSpec(w_ba.shape, const, pipeline_mode=once),
                  pl.BlockSpec(w_bb.shape, const, pipeline_mode=once),
                  pl.BlockSpec(w_out.shape, const, pipeline_mode=once),
                  pl.BlockSpec(w_router2.shape, const, pipeline_mode=once)],
        out_specs=[pl.BlockSpec((tm, d), row), pl.BlockSpec((tm, d), row), pl.BlockSpec((tm, LANES), row)],
        out_shape=[jax.ShapeDtypeStruct((t, d), F32), jax.ShapeDtypeStruct((t, d), BF16),
                   jax.ShapeDtypeStruct((t, LANES), F32)],
        name="post_attn",
        compiler_params=_cparams(1),
    )(x2, ya, yb, mod, g1, g2, w_in, w_ba, w_bb, w_out, w_router2)


def _select_kernel(aff_ref, slot_o, slot_t_o, affb_o, *, cap):
    aff = aff_ref[0]
    n = aff.shape[0]
    capf = float(cap)

    def enough(cand):
        return jnp.sum(jnp.where(aff >= cand, 1.0, 0.0), axis=0, keepdims=True) >= capf

    pw = jnp.ones((1, LANES), F32)
    for k in (64, 32, 16, 8, 4, 2, 1):
        pw = jnp.where(enough(pw * 2.0 ** -(k - 1)), pw, pw * 2.0 ** -k)

    def mantissa_step(_, carry):
        thr, step = carry
        cand = thr + step
        return jnp.where(enough(cand), cand, thr), step * 0.5

    thr, _ = lax.fori_loop(0, 23, mantissa_step, (pw, pw * 0.5))
    above = aff > thr
    tied = aff == thr
    need = capf - jnp.sum(jnp.where(above, 1.0, 0.0), axis=0, keepdims=True)
    r_i = lax.broadcasted_iota(jnp.int32, (n, n), 0)
    c_i = lax.broadcasted_iota(jnp.int32, (n, n), 1)
    before = jnp.where(c_i < r_i, 1.0, 0.0).astype(BF16)
    tie_rank = _dot(before, jnp.where(tied, 1.0, 0.0).astype(BF16))
    sel = above | (tied & (tie_rank < need))
    slot = _dot(before, jnp.where(sel, 1.0, 0.0).astype(BF16))
    slot = jnp.where(sel, slot, NOT_SELECTED)
    slot_o[0] = slot.astype(BF16)
    slot_t_o[0] = slot.T
    affb_o[0] = aff.astype(BF16)


def _dispatch_kernel(slot_t_ref, h2_ref, xe_o, *, cap):
    n = slot_t_ref.shape[2]
    g = h2_ref.shape[0] // n
    per = min(N_EXPERTS, DISPATCH_ROWS // cap)
    slot_iota = lax.broadcasted_iota(jnp.int32, (cap, n), 0).astype(F32)
    for k in range(g):
        h2 = h2_ref[k * n:(k + 1) * n, :]
        slot_t = slot_t_ref[0, k * N_EXPERTS:(k + 1) * N_EXPERTS, :]
        for e0 in range(0, N_EXPERTS, per):
            onehot = jnp.concatenate(
                [jnp.where(slot_t[e:e + 1, :] == slot_iota, 1.0, 0.0) for e in range(e0, e0 + per)],
                axis=0).astype(BF16)
            rows = _dot(onehot, h2).astype(BF16)
            for j in range(per):
                xe_o[e0 + j, k * cap:(k + 1) * cap, :] = rows[j * cap:(j + 1) * cap]


def _route(aff, h2, n_sets, set_rows):
    t, d = h2.shape
    n = set_rows
    cap = CAPACITY_FACTOR * n // N_EXPERTS
    assert cap & (cap - 1) == 0 and cap % 16 == 0 and DISPATCH_ROWS % cap == 0
    n_packs = -(-n_sets // SETS_PER_PACK)
    aff16 = aff[:, :N_EXPERTS].reshape(n_sets, n, N_EXPERTS)
    aff16 = jnp.pad(aff16, ((0, n_packs * SETS_PER_PACK - n_sets), (0, 0), (0, 0)))
    packed = aff16.reshape(n_packs, SETS_PER_PACK, n, N_EXPERTS).transpose(0, 2, 1, 3).reshape(n_packs, n, LANES)
    pack_blk = lambda p: (p, 0, 0)
    slot, slot_t, affb = pl.pallas_call(
        functools.partial(_select_kernel, cap=cap),
        grid=(n_packs,),
        in_specs=[pl.BlockSpec((1, n, LANES), pack_blk)],
        out_specs=[pl.BlockSpec((1, n, LANES), pack_blk), pl.BlockSpec((1, LANES, n), pack_blk),
                   pl.BlockSpec((1, n, LANES), pack_blk)],
        out_shape=[jax.ShapeDtypeStruct((n_packs, n, LANES), BF16),
                   jax.ShapeDtypeStruct((n_packs, LANES, n), F32),
                   jax.ShapeDtypeStruct((n_packs, n, LANES), BF16)],
        name=f"select_cap{cap}",
        compiler_params=_cparams(1),
    )(packed)
    g = _sets_per_step(n_sets, n)
    per_pack = SETS_PER_PACK // g
    xe = pl.pallas_call(
        functools.partial(_dispatch_kernel, cap=cap),
        grid=(n_sets // g,),
        in_specs=[pl.BlockSpec((1, g * N_EXPERTS, n), lambda s: (s // per_pack, s % per_pack, 0)),
                  pl.BlockSpec((g * n, d), lambda s: (s, 0))],
        out_specs=pl.BlockSpec((N_EXPERTS, g * cap, d), lambda s: (0, s, 0)),
        out_shape=jax.ShapeDtypeStruct((N_EXPERTS, n_sets * cap, d), BF16),
        name=f"dispatch_cap{cap}",
        compiler_params=_cparams(1),
    )(slot_t, h2)
    return xe, (slot, affb)


def _expert_kernel(*refs, n_groups):
    x_refs = refs[:n_groups]
    wg_ref, wu_ref, wd_ref = refs[n_groups:n_groups + 3]
    o_refs = refs[n_groups + 3:2 * n_groups + 3]
    wg_s, wu_s, wd_s = refs[2 * n_groups + 3:]
    wg_s[...] = wg_ref[0].astype(BF16)
    wu_s[...] = wu_ref[0].astype(BF16)
    wd_s[...] = wd_ref[0].astype(BF16)
    sub = EXPERT_ROWS
    for x_ref, o_ref in zip(x_refs, o_refs):
        for r0 in range(0, x_ref.shape[1], sub):
            x = x_ref[0, r0:r0 + sub, :]
            a = _dot(x, wg_s[...])
            u = _dot(x, wu_s[...])
            hmid = (a * jax.nn.sigmoid(a) * u).astype(BF16)
            o_ref[0, r0:r0 + sub, :] = _dot(hmid, wd_s[...]).astype(BF16)


def _experts(xes, w_gate, w_up, w_down):
    e, d, f = w_gate.shape
    blk = lambda i: (i, 0, 0)
    return pl.pallas_call(
        functools.partial(_expert_kernel, n_groups=len(xes)),
        grid=(e,),
        in_specs=[pl.BlockSpec((1, x.shape[1], d), blk) for x in xes]
        + [pl.BlockSpec((1, d, f), blk), pl.BlockSpec((1, d, f), blk), pl.BlockSpec((1, f, d), blk)],
        out_specs=[pl.BlockSpec((1, x.shape[1], d), blk) for x in xes],
        out_shape=[jax.ShapeDtypeStruct(x.shape, BF16) for x in xes],
        scratch_shapes=[pltpu.VMEM((d, f), BF16), pltpu.VMEM((d, f), BF16), pltpu.VMEM((f, d), BF16)],
        name="experts",
        compiler_params=_cparams(1),
    )(*xes, w_gate, w_up, w_down)


def _combine_kernel(slot_ref, affb_ref, ye_ref, x1_ref, mod_ref, gf_ref, y_o, *, cap):
    e, gcap, d = ye_ref.shape
    g = gcap // cap
    n = x1_ref.shape[0] // g
    shift = cap.bit_length() - 1
    both = jnp.concatenate([slot_ref[0], affb_ref[0]], axis=1)
    want = (lax.broadcasted_iota(jnp.int32, (n, DISPATCH_ROWS), 1) & (cap - 1)).astype(F32) * SLOT_RADIX
    for k in range(g):
        rows = slice(k * n, (k + 1) * n)
        first_lane = ((pl.program_id(0) * g + k) % SETS_PER_PACK) * N_EXPERTS
        pieces = []
        for c0 in range(0, e * cap, DISPATCH_ROWS):
            src = lax.broadcasted_iota(jnp.int32, (2 * LANES, DISPATCH_ROWS), 0)
            col = lax.broadcasted_iota(jnp.int32, (2 * LANES, DISPATCH_ROWS), 1) + c0
            lane_of_col = first_lane + lax.shift_right_logical(col, shift)
            spread = jnp.where(src == lane_of_col, SLOT_RADIX,
                               jnp.where(src == lane_of_col + LANES, 1.0, 0.0)).astype(BF16)
            rest = _dot(both, spread) - want
            pieces.append(jnp.where(rest >= 0.0, jnp.where(rest <= 1.0, rest, 0.0), 0.0).astype(BF16))
        gt = pieces[0] if len(pieces) == 1 else jnp.concatenate(pieces, axis=1)
        ye = ye_ref[:, k * cap:(k + 1) * cap, :].reshape(e * cap, d)
        moe = _dot(gt, ye)
        x = x1_ref[rows, :] + mod_ref[0][5:6] * moe
        y_o[rows, :] = _rms(x, gf_ref[...])


def _sets_per_step(n_sets, set_rows):
    g = SETS_PER_PACK if set_rows * SETS_PER_PACK <= SHARED_STEP_ROWS else 1
    return g if n_sets % g == 0 else 1


def _combine(packs, ye, x1, mod, n_sets, set_rows, g_final):
    slot, affb = packs
    t, d = x1.shape
    cap = ye.shape[1] // n_sets
    g = _sets_per_step(n_sets, set_rows) if mod.shape[0] == 1 else 1
    tm = min(ROW_TILE, set_rows) if g == 1 else g * set_rows
    tiles = max(1, set_rows // tm)
    per_pack = SETS_PER_PACK // g
    pack_rows = pl.BlockSpec((1, tm // g, LANES), lambda s, i: (s // per_pack, i, 0))
    return pl.pallas_call(
        functools.partial(_combine_kernel, cap=cap),
        grid=(n_sets // g, tiles),
        in_specs=[pack_rows, pack_rows,
                  pl.BlockSpec((N_EXPERTS, g * cap, d), lambda s, i: (0, s, 0)),
                  pl.BlockSpec((tm, d), lambda s, i: (s * tiles + i, 0)),
                  pl.BlockSpec((1, 8, d), (lambda s, i: (s, 0, 0)) if mod.shape[0] > 1 else (lambda s, i: (0, 0, 0))),
                  pl.BlockSpec((1, d), lambda s, i: (0, 0))],
        out_specs=pl.BlockSpec((tm, d), lambda s, i: (s * tiles + i, 0)),
        out_shape=jax.ShapeDtypeStruct((t, d), F32),
        name=f"combine_cap{cap}",
        compiler_params=_cparams(2),
    )(slot, affb, ye, x1, mod, g_final)


def _rope_tables(n_tokens):
    n_rows = n_tokens // GRID_W
    rowp = jnp.repeat(jnp.arange(n_rows), GRID_W).astype(F32)
    colp = jnp.tile(jnp.arange(GRID_W), n_rows).astype(F32)
    quarter = HEAD_DIM // 4
    freqs = ROPE_THETA ** (-jnp.arange(quarter, dtype=F32) / quarter)
    ang = jnp.stack([rowp[:, None] * freqs, colp[:, None] * freqs], axis=1)
    cos, sin = jnp.cos(ang), jnp.sin(ang)
    zero = jnp.zeros_like(sin)
    c = jnp.stack([cos, cos], axis=2).reshape(n_tokens, HEAD_DIM)
    s_up = jnp.stack([-sin, zero], axis=2).reshape(n_tokens, HEAD_DIM)
    s_dn = jnp.stack([zero, sin], axis=2).reshape(n_tokens, HEAD_DIM)
    return tuple(jnp.tile(t, (1, LANES // HEAD_DIM)) for t in (c, s_up, s_dn))


def _mix_and_route(x2, n_sets, set_rows, mod, lw, rope_tabs, cache, emit_cache, lam_init, tq):
    (g_attn, g_ffn, w_in, gq_t, gk_t, lq1, lk1, lq2, lk2, g_diff, w_ba, w_bb, w_out, w_router2) = lw
    outs = _inproj(x2, mod, set_rows, g_attn, w_in, gq_t, gk_t, rope_tabs, emit_cache)
    qa, kta, vta, qb, ktb, vb16 = outs[:6]
    vb16 = vb16.reshape(n_sets, set_rows, vb16.shape[1])
    ks_a, vs_a, ks_b, vs_b = [kta], [vta], [ktb], [vb16]
    if cache is not None:
        ckta, cvta, cktb, cvb = cache
        ks_a, vs_a, ks_b, vs_b = [ckta, kta], [cvta, vta], [cktb, ktb], [cvb, vb16]
    per_step = CONTEXT_SETS_PER_STEP if (tq == set_rows and n_sets % CONTEXT_SETS_PER_STEP == 0) else 1
    ya = _attention("gqa", qa, ks_a, vs_a, set_rows, tq, per_step, (), {"mxu_sums": cache is not None})
    yb = _attention("diff", qb, ks_b, vs_b, set_rows, tq, per_step, (lq1, lk1, lq2, lk2, g_diff),
                    {"lam_init": lam_init, "stack_maps": cache is None})
    x1, h2, aff = _post(x2, ya, yb, mod, set_rows, g_attn, g_ffn, w_in, w_ba, w_bb, w_out, w_router2)
    xe, packs = _route(aff, h2, n_sets, set_rows)
    return x1, xe, packs, outs[6:]


def kernel(x_prompt, x_sample, cache_attn_k, cache_attn_v, cache_diff_k, cache_diff_v, c, c_ctx, w_mod, b_mod,
           g_attn_norm, g_ffn_norm, w_in, g_q_norm, g_k_norm, lambda_q1, lambda_k1, lambda_q2, lambda_k2,
           g_diff_norm, w_branch_a, w_branch_b, w_out, w_router, w_exp_gate, w_exp_up, w_exp_down, g_final):
    batch, seq, d = x_prompt.shape
    dec_batch, dec_seq, _ = x_sample.shape
    depth = w_in.shape[0]
    assert depth == 1, "the final norm is fused into the layer's combine step"
    past = cache_attn_k.shape[2]
    assert w_in.shape[2] == N_QKV + 2 * d

    xp = x_prompt.reshape(batch * seq, d)
    xs = x_sample.reshape(dec_batch * dec_seq, d)
    rope_tabs = _rope_tables(dec_seq)
    c_rows = jnp.concatenate([c, c_ctx[None, :], jnp.zeros((16 - dec_batch - 1, d), F32)], axis=0)
    yp = ys = None
    caches = []
    for l in range(depth):
        lam_init = 0.8 - 0.6 * math.exp(-0.3 * l)
        mod6 = _modulation(c_rows, w_mod[l], b_mod[l])
        mod = jnp.pad(jnp.transpose(mod6, (1, 0, 2)), ((0, 0), (0, 2), (0, 0)))
        mod_lat, mod_ctx = mod[:dec_batch], mod[dec_batch:dec_batch + 1]
        w_in16, w_ba16, w_bb16, w_out16, w_router2 = _round_weights(
            [w_in[l], w_branch_a[l], w_branch_b[l], w_out[l]],
            jnp.pad(w_router[l], ((0, 0), (0, LANES - N_EXPERTS))))
        lw = (g_attn_norm[l][None, :], g_ffn_norm[l][None, :], w_in16,
              jnp.tile(g_q_norm[l], HEADS_A)[None, :], jnp.tile(g_k_norm[l], KV_HEADS_A)[None, :],
              lambda_q1[l], lambda_k1[l], lambda_q2[l], lambda_k2[l], g_diff_norm[l][None, :],
              w_ba16, w_bb16, w_out16, w_router2)
        x1p, xe_p, packs_p, cache_out = _mix_and_route(xp, batch, seq, mod_ctx, lw, None, None, True, lam_init, seq)
        caches.append(cache_out)
        feat_major = lambda a: jnp.moveaxis(a[:, l], 1, -1).reshape(dec_batch, -1, past)
        cache_l = _cache_prep([feat_major(cache_attn_k), feat_major(cache_attn_v), feat_major(cache_diff_k)],
                              cache_diff_v[:, l].reshape(dec_batch, past * HEADS_B, -1))
        x1s, xe_s, packs_s, _ = _mix_and_route(xs, dec_batch, dec_seq, mod_lat, lw, rope_tabs, cache_l, False,
                                               lam_init, LATENT_Q_TILE)
        ye_p, ye_s = _experts([xe_p, xe_s], w_exp_gate[l], w_exp_up[l], w_exp_down[l])
        yp = _combine(packs_p, ye_p, x1p, mod_ctx, batch, seq, g_final[None, :])
        ys = _combine(packs_s, ye_s, x1s, mod_lat, dec_batch, dec_seq, g_final[None, :])
    y_prompt = yp.reshape(batch, seq, d)
    y_sample = ys.reshape(dec_batch, dec_seq, d)
    tok_major = lambda a, dims: jnp.moveaxis(a.reshape((batch,) + dims + (seq,)), -1, 1)
    new_attn_k = jnp.stack([tok_major(cc[0], (KV_HEADS_A, HEAD_DIM)) for cc in caches], axis=1)
    new_attn_v = jnp.stack([tok_major(cc[1], (KV_HEADS_A, HEAD_DIM)) for cc in caches], axis=1)
    new_diff_k = jnp.stack([tok_major(cc[2], (HEADS_B, 2, HEAD_DIM)) for cc in caches], axis=1)
    new_diff_v = jnp.stack([cc[3].reshape(batch, seq, HEADS_B, 2 * HEAD_DIM) for cc in caches], axis=1)
    return (y_prompt, y_sample, new_attn_k, new_attn_v, new_diff_k, new_diff_v)
```

```python
import functools
import math

import numpy as np
import jax
import jax.numpy as jnp
from jax import lax
from jax.experimental import pallas as pl
from jax.experimental.pallas import tpu as pltpu

F32 = jnp.float32
BF16 = jnp.bfloat16

HEAD_DIM = 64
HEADS_A = 8
KV_HEADS_A = 2
HEADS_B = 4
N_QKV = HEADS_A * HEAD_DIM + 2 * KV_HEADS_A * HEAD_DIM + 3 * HEADS_B * 2 * HEAD_DIM
N_EXPERTS = 16
CAPACITY_FACTOR = 2
GRID_W = 64
ROPE_THETA = 10000.0
EPS = 1e-6
LANES = 128
ROW_TILE = 1024
POST_ROWS = 1024
CHAIN_ROWS = 512
EXPERT_ROWS = 256
LATENT_Q_TILE = 512
CONTEXT_SETS_PER_STEP = 4
SETS_PER_PACK = LANES // N_EXPERTS
SHARED_STEP_ROWS = 2048
DISPATCH_ROWS = 512
NEG_BIG = -1e30
NOT_SELECTED = -1.0
SLOT_RADIX = 256.0
VMEM_LIMIT = 56 * 1024 * 1024
ROUND_STEPS = 8


def _cparams(n_axes):
    return pltpu.CompilerParams(dimension_semantics=("arbitrary",) * n_axes,
                                vmem_limit_bytes=VMEM_LIMIT)


def _dot(a, b):
    return jnp.dot(a, b, preferred_element_type=F32)


def _dot_nt(a, b):
    return lax.dot_general(a, b, (((1,), (1,)), ((), ())), preferred_element_type=F32)


def _split(a):
    hi = a.astype(BF16)
    lo = (a - hi.astype(F32)).astype(BF16)
    return hi, lo


def _dot3(a, b):
    a_hi, a_lo = _split(a)
    b_hi, b_lo = _split(b)
    return _dot(a_hi, b_hi) + _dot(a_lo, b_hi) + _dot(a_hi, b_lo)


def _rms(x, g):
    return x * lax.rsqrt(jnp.mean(x * x, axis=-1, keepdims=True) + EPS) * g


def _mod_index(mod, tiles_per_set):
    if mod.shape[0] == 1:
        return lambda i: (0, 0, 0)
    return lambda i: (i // tiles_per_set, 0, 0)


def _mod_kernel(c_ref, w_ref, b_ref, o_ref):
    c = c_ref[...]
    a = c * jax.nn.sigmoid(c)
    o_ref[0] = _dot3(a, w_ref[...]) + b_ref[0]


def _modulation(c_rows, w_mod, b_mod):
    r, d = c_rows.shape
    return pl.pallas_call(
        _mod_kernel,
        grid=(6,),
        in_specs=[pl.BlockSpec((r, d), lambda j: (0, 0)),
                  pl.BlockSpec((d, d), lambda j: (0, j)),
                  pl.BlockSpec((1, 1, d), lambda j: (j, 0, 0))],
        out_specs=pl.BlockSpec((1, r, d), lambda j: (j, 0, 0)),
        out_shape=jax.ShapeDtypeStruct((6, r, d), F32),
        name="mod",
        compiler_params=_cparams(1),
    )(c_rows, w_mod, b_mod.reshape(6, 1, d))


def _round_weights_kernel(*refs):
    n = len(refs) // 2
    for src, out in zip(refs[:n - 1], refs[n:-1]):
        out[...] = src[...].astype(BF16)
    hi, lo = _split(refs[n - 1][...])
    refs[-1][...] = jnp.concatenate([hi, lo], axis=1)


def _round_weights(weights, w_router_p):
    steps = ROUND_STEPS
    arrays = list(weights) + [w_router_p]
    blk = lambda a, cols: pl.BlockSpec((a.shape[0] // steps, cols), lambda i: (i, 0))
    out_cols = [a.shape[1] for a in weights] + [2 * w_router_p.shape[1]]
    return pl.pallas_call(
        _round_weights_kernel,
        grid=(steps,),
        in_specs=[blk(a, a.shape[1]) for a in arrays],
        out_specs=[blk(a, c) for a, c in zip(arrays, out_cols)],
        out_shape=[jax.ShapeDtypeStruct((a.shape[0], c), BF16) for a, c in zip(arrays, out_cols)],
        name="round_weights",
        compiler_params=_cparams(1),
    )(*arrays)


def _seg_sumsq(x, ones_blockdiag):
    return _dot((x * x).astype(BF16), ones_blockdiag)


def _rope(x, c, s_up, s_dn):
    w = x.shape[1]
    reps = w // c.shape[1]
    if reps > 1:
        c, s_up, s_dn = (jnp.concatenate([t] * reps, axis=1) for t in (c, s_up, s_dn))
    return x * c + pltpu.roll(x, w - 16, 1) * s_up + pltpu.roll(x, 16, 1) * s_dn


def _inproj_kernel(*refs, rope, emit_cache):
    (x_ref, mod_ref, g_ref, w_ref, gq_ref, gk_ref, bd512_ref, bd128_ref), refs = refs[:8], refs[8:]
    if rope:
        (c_ref, su_ref, sd_ref), refs = refs[:3], refs[3:]
    qa_o, kta_o, vta_o, qb_o, ktb_o, vb_o = refs[:6]
    if emit_cache:
        ka_c, va_c, kb_c, vb_c = refs[6:10]

    mod = mod_ref[0]
    wa = HEADS_A * HEAD_DIM
    wkv = KV_HEADS_A * HEAD_DIM
    wb = HEADS_B * 2 * HEAD_DIM
    o_ka, o_va, o_qb = wa, wa + wkv, wa + 2 * wkv
    o_kb, o_vb = o_qb + wb, o_qb + 2 * wb
    scale = HEAD_DIM ** -0.5 * math.log2(math.e)

    x = x_ref[...]
    h = (_rms(x, g_ref[...]) * (1.0 + mod[1:2]) + mod[0:1]).astype(BF16)
    if rope:
        tabs = (c_ref[...], su_ref[...], sd_ref[...])

    qa = _dot(h, w_ref[:, 0:wa])
    qa = qa * lax.rsqrt(_seg_sumsq(qa, bd512_ref[...]) * (1.0 / HEAD_DIM) + EPS) * gq_ref[...]
    if rope:
        qa = _rope(qa, *tabs)
    qa_o[...] = (qa * scale).astype(BF16)

    kv = _dot(h, w_ref[:, o_ka:o_qb])
    ka, va = kv[:, 0:wkv], kv[:, wkv:2 * wkv]
    ka = ka * lax.rsqrt(_seg_sumsq(ka, bd128_ref[...]) * (1.0 / HEAD_DIM) + EPS) * gk_ref[...]

    def put_feat(val, out_bf16, out_f32):
        rows = out_bf16.shape[2]
        for s in range(out_bf16.shape[0]):
            t = val[s * rows:(s + 1) * rows].T
            if out_f32 is not None:
                out_f32[s] = t
            out_bf16[s] = t.astype(BF16)

    if rope:
        put_feat(_rope(ka, *tabs), kta_o, None)
    else:
        put_feat(ka, kta_o, ka_c if emit_cache else None)
    put_feat(va, vta_o, va_c if emit_cache else None)

    qb = _dot(h, w_ref[:, o_qb:o_kb])
    if rope:
        qb = _rope(qb, *tabs)
    qb_o[...] = (qb * scale).astype(BF16)

    kb = _dot(h, w_ref[:, o_kb:o_vb])
    if rope:
        put_feat(_rope(kb, *tabs), ktb_o, None)
    else:
        put_feat(kb, ktb_o, kb_c if emit_cache else None)

    vb = _dot(h, w_ref[:, o_vb:o_vb + wb])
    if emit_cache:
        for hd in range(HEADS_B):
            vb_c[pl.ds(hd, vb.shape[0], stride=HEADS_B), :] = vb[:, hd * 128:(hd + 1) * 128]
    vb_o[...] = vb.astype(BF16)


def _blockdiag_ones(width):
    g = np.arange(width) // HEAD_DIM
    return jnp.asarray((g[:, None] == g[None, :]).astype(np.float32), dtype=BF16)


def _inproj(x2, mod, set_rows, g_attn, w_in, gq_t, gk_t, rope_tabs, emit_cache):
    t, d = x2.shape
    tm = ROW_TILE
    assert t % tm == 0 and (tm % set_rows == 0 or set_rows % tm == 0)
    tiles_per_set = max(1, set_rows // tm)
    sets_per_tile = max(1, tm // set_rows)
    rope = rope_tabs is not None
    assert not (rope and emit_cache), "cached keys are the position-free ones"
    nq = N_QKV
    row = lambda i: (i, 0)
    const = lambda i: (0, 0)
    in_specs = [pl.BlockSpec((tm, d), row),
                pl.BlockSpec((1, 8, d), _mod_index(mod, tiles_per_set)),
                pl.BlockSpec((1, d), const),
                pl.BlockSpec((d, nq), const, pipeline_mode=pl.Buffered(1)),
                pl.BlockSpec((1, 512), const),
                pl.BlockSpec((1, 128), const),
                pl.BlockSpec((512, 512), const),
                pl.BlockSpec((128, 128), const)]
    args = [x2, mod, g_attn, w_in, gq_t, gk_t, _blockdiag_ones(512), _blockdiag_ones(128)]
    if rope:
        in_specs += [pl.BlockSpec((tm, LANES), lambda i: (i % tiles_per_set, 0))] * 3
        args += list(rope_tabs)
    n_sets = t // set_rows
    wkv, wb = KV_HEADS_A * HEAD_DIM, HEADS_B * 2 * HEAD_DIM
    outs = [("tok", 512, BF16), ("feat", wkv, BF16), ("feat", wkv, BF16),
            ("tok", 512, BF16), ("feat", wb, BF16), ("tok", wb, BF16)]
    if emit_cache:
        outs += [("feat", wkv, F32), ("feat", wkv, F32), ("feat", wb, F32), ("tokhead", wb, F32)]
    feat = lambda i: (i // tiles_per_set, 0, i % tiles_per_set)
    feat_rows = min(tm, set_rows)

    def out_block(kind, w):
        if kind == "tok":
            return pl.BlockSpec((tm, w), row)
        if kind == "tokhead":
            return pl.BlockSpec((tm * HEADS_B, w // HEADS_B), row)
        return pl.BlockSpec((sets_per_tile, w, feat_rows), feat)

    def out_array(kind, w, dt):
        shape = {"tok": (t, w), "tokhead": (t * HEADS_B, w // HEADS_B), "feat": (n_sets, w, set_rows)}[kind]
        return jax.ShapeDtypeStruct(shape, dt)

    return pl.pallas_call(
        functools.partial(_inproj_kernel, rope=rope, emit_cache=emit_cache),
        grid=(t // tm,),
        in_specs=in_specs,
        out_specs=[out_block(kind, w) for kind, w, _ in outs],
        out_shape=[out_array(kind, w, dt) for kind, w, dt in outs],
        name="inproj_rope" if rope else "inproj",
        compiler_params=_cparams(1),
    )(*args)


def _cache_prep_kernel(*refs):
    half = len(refs) // 2
    for src, out in zip(refs[:half - 1], refs[half:-1]):
        out[...] = src[...].astype(BF16)
    v_src, v_out = refs[half - 1], refs[-1]
    keys = v_out.shape[1]
    for hd in range(HEADS_B):
        v_out[0, :, hd * 128:(hd + 1) * 128] = v_src[0, pl.ds(hd, keys, stride=HEADS_B), :].astype(BF16)


def _cache_prep(arrays, v_by_head):
    blk = lambda b: (b, 0, 0)
    n_sets, rows, width = v_by_head.shape
    v_shape = (n_sets, rows // HEADS_B, width * HEADS_B)
    return pl.pallas_call(
        _cache_prep_kernel,
        grid=(n_sets,),
        in_specs=[pl.BlockSpec((1,) + a.shape[1:], blk) for a in arrays + [v_by_head]],
        out_specs=[pl.BlockSpec((1,) + a.shape[1:], blk) for a in arrays] + [pl.BlockSpec((1,) + v_shape[1:], blk)],
        out_shape=[jax.ShapeDtypeStruct(a.shape, BF16) for a in arrays] + [jax.ShapeDtypeStruct(v_shape, BF16)],
        name="cache_prep",
        compiler_params=_cparams(1),
    )(*arrays, v_by_head)


def _exp_parts(scores):
    m = scores[0].max(axis=-1, keepdims=True)
    for s in scores[1:]:
        m = jnp.maximum(m, s.max(axis=-1, keepdims=True))
    return [jnp.exp2(s - m) for s in scores]


def _row_sum(parts):
    l = parts[0].sum(axis=-1, keepdims=True)
    for e in parts[1:]:
        l = l + e.sum(axis=-1, keepdims=True)
    return l


def _pad_rows(x, first, ones_row=False):
    if ones_row:
        z = jnp.where(lax.broadcasted_iota(jnp.int32, x.shape, 0) == 0, 1.0, 0.0).astype(x.dtype)
    else:
        z = jnp.zeros_like(x)
    return jnp.concatenate([x, z] if first else [z, x], axis=0)


def _gqa_kernel(*refs, n_src, mxu_sums):
    q_ref = refs[0]
    k_refs = refs[1:1 + n_src]
    v_refs = refs[1 + n_src:1 + 2 * n_src]
    o_ref = refs[1 + 2 * n_src]
    sets_here = k_refs[0].shape[0]
    tq = q_ref.shape[0] // sets_here
    lane = lax.broadcasted_iota(jnp.int32, (2 * tq, LANES), 1)
    lo_half = lane < HEAD_DIM
    for s in range(sets_here):
        rows = slice(s * tq, (s + 1) * tq)
        for g in range(KV_HEADS_A):
            c0 = g * 256
            f0 = g * HEAD_DIM
            q = jnp.concatenate([q_ref[rows, c0:c0 + 128], q_ref[rows, c0 + 128:c0 + 256]], axis=0)
            kts = [k[s, f0:f0 + HEAD_DIM, :] for k in k_refs]
            vts = [v[s, f0:f0 + HEAD_DIM, :] for v in v_refs]
            e_lo = _exp_parts([_dot(q, _pad_rows(kt, True)) for kt in kts])
            e_hi = _exp_parts([_dot(q, _pad_rows(kt, False)) for kt in kts])
            o_lo = o_hi = None
            for e_part, vt in zip(e_lo, vts):
                t = _dot_nt(e_part.astype(BF16), _pad_rows(vt, True, ones_row=mxu_sums))
                o_lo = t if o_lo is None else o_lo + t
            for e_part, vt in zip(e_hi, vts):
                t = _dot_nt(e_part.astype(BF16), _pad_rows(vt, False, ones_row=mxu_sums))
                o_hi = t if o_hi is None else o_hi + t
            if mxu_sums:
                l_lo, l_hi = o_lo[:, HEAD_DIM:HEAD_DIM + 1], o_hi[:, 0:1]
                o = jnp.where(lo_half, o_lo * (1.0 / l_lo), o_hi * (1.0 / l_hi))
            else:
                o = (o_lo + o_hi) * jnp.where(lo_half, 1.0 / _row_sum(e_lo), 1.0 / _row_sum(e_hi))
            o_ref[rows, c0:c0 + 128] = o[0:tq].astype(BF16)
            o_ref[rows, c0 + 128:c0 + 256] = o[tq:2 * tq].astype(BF16)


def _diff_kernel(*refs, n_src, lam_init, stack_maps):
    q_ref = refs[0]
    k_refs = refs[1:1 + n_src]
    v_refs = refs[1 + n_src:1 + 2 * n_src]
    lq1, lk1, lq2, lk2, gd_ref, o_ref = refs[1 + 2 * n_src:]
    lam_all = (jnp.exp(jnp.sum(lq1[...] * lk1[...], axis=-1, keepdims=True))
               - jnp.exp(jnp.sum(lq2[...] * lk2[...], axis=-1, keepdims=True)) + lam_init)
    sets_here = k_refs[0].shape[0]
    tq = q_ref.shape[0] // sets_here
    for s in range(sets_here):
        rows = slice(s * tq, (s + 1) * tq)
        for j in range(HEADS_B):
            lam = lam_all[j:j + 1, :]
            q = q_ref[rows, j * 128:(j + 1) * 128]
            k0s = [_pad_rows(k[s, j * 128:j * 128 + HEAD_DIM, :], True) for k in k_refs]
            k1s = [_pad_rows(k[s, j * 128 + HEAD_DIM:(j + 1) * 128, :], False) for k in k_refs]
            e0 = _exp_parts([_dot(q, k0) for k0 in k0s])
            e1 = _exp_parts([_dot(q, k1) for k1 in k1s])
            o0 = o1 = None
            for a0, a1, v in zip(e0, e1, v_refs):
                val = v[s, :, j * 128:(j + 1) * 128]
                if stack_maps:
                    t = _dot(jnp.concatenate([a0.astype(BF16), a1.astype(BF16)], axis=0), val)
                    t0, t1 = t[0:tq], t[tq:2 * tq]
                else:
                    t0, t1 = _dot(a0.astype(BF16), val), _dot(a1.astype(BF16), val)
                o0 = t0 if o0 is None else o0 + t0
                o1 = t1 if o1 is None else o1 + t1
            o = o0 * (1.0 / _row_sum(e0)) - o1 * (lam / _row_sum(e1))
            o = _rms(o, gd_ref[...]) * (1.0 - lam_init)
            o_ref[rows, j * 128:(j + 1) * 128] = o.astype(BF16)


def _both_mixers_kernel(*refs, n_src, lam_init, stack_maps, mxu_sums):
    per = 1 + 2 * n_src
    ya_o, yb_o = refs[-2:]
    _gqa_kernel(*refs[:per], ya_o, n_src=n_src, mxu_sums=mxu_sums)
    _diff_kernel(*refs[per:2 * per + 5], yb_o, n_src=n_src, lam_init=lam_init, stack_maps=stack_maps)


def _both_mixers(qa, ks_a, vs_a, qb, ks_b, vs_b, lam_refs, set_rows, tq, sets_per_step, lam_init):
    t = qa.shape[0]
    n_sets = t // set_rows
    q_tiles = set_rows // tq
    assert sets_per_step == 1 or q_tiles == 1
    rows = sets_per_step * tq
    tile = lambda b, i: (b * q_tiles + i, 0)
    kv = lambda a: pl.BlockSpec((sets_per_step,) + a.shape[1:], lambda b, i: (b, 0, 0))
    in_specs = ([pl.BlockSpec((rows, qa.shape[1]), tile)] + [kv(a) for a in list(ks_a) + list(vs_a)]
                + [pl.BlockSpec((rows, qb.shape[1]), tile)] + [kv(a) for a in list(ks_b) + list(vs_b)]
                + [pl.BlockSpec(e.shape, lambda b, i: (0, 0)) for e in lam_refs])
    return pl.pallas_call(
        functools.partial(_both_mixers_kernel, n_src=len(ks_a), lam_init=lam_init,
                          stack_maps=len(ks_a) == 1, mxu_sums=len(ks_a) > 1),
        grid=(n_sets // sets_per_step, q_tiles),
        in_specs=in_specs,
        out_specs=[pl.BlockSpec((rows, qa.shape[1]), tile), pl.BlockSpec((rows, qb.shape[1]), tile)],
        out_shape=[jax.ShapeDtypeStruct(qa.shape, BF16), jax.ShapeDtypeStruct(qb.shape, BF16)],
        name=f"mixers_{len(ks_a)}src",
        compiler_params=_cparams(2),
    )(qa, *ks_a, *vs_a, qb, *ks_b, *vs_b, *lam_refs)


def _post_kernel(x_ref, ya_ref, yb_ref, mod_ref, g1_ref, g2_ref, win_ref, wba_ref, wbb_ref, wo_ref, wr2_ref,
                 x1_o, h2_o, aff_o):
    d = x_ref.shape[1]
    g0 = win_ref.shape[1] - 2 * d
    mod = mod_ref[0]
    for r0 in range(0, x_ref.shape[0], CHAIN_ROWS):
        rows = slice(r0, r0 + CHAIN_ROWS)
        x = x_ref[rows, :]
        h = (_rms(x, g1_ref[...]) * (1.0 + mod[1:2]) + mod[0:1]).astype(BF16)
        ga = jax.nn.sigmoid(_dot(h, win_ref[:, g0:g0 + d]))
        merged = ga * _dot(ya_ref[rows, :], wba_ref[...])
        gb = jax.nn.sigmoid(_dot(h, win_ref[:, g0 + d:g0 + 2 * d]))
        merged = merged + gb * _dot(yb_ref[rows, :], wbb_ref[...])
        m = _dot(merged.astype(BF16), wo_ref[...])
        x1 = x + mod[2:3] * m
        x1_o[rows, :] = x1
        h2 = _rms(x1, g2_ref[...]) * (1.0 + mod[4:5]) + mod[3:4]
        h2_o[rows, :] = h2.astype(BF16)
        h2_hi, h2_lo = _split(h2)
        both = _dot(h2_hi, wr2_ref[...])
        logits = both[:, 0:LANES] + both[:, LANES:2 * LANES] + _dot(h2_lo, wr2_ref[:, 0:LANES])
        lane = lax.broadcasted_iota(jnp.int32, logits.shape, 1)
        logits = jnp.where(lane < N_EXPERTS, logits, NEG_BIG)
        e = jnp.exp(logits - logits.max(axis=-1, keepdims=True))
        aff_o[rows, :] = e / e.sum(axis=-1, keepdims=True)


def _post(x2, ya, yb, mod, set_rows, g1, g2, w_in, w_ba, w_bb, w_out, w_router2):
    t, d = x2.shape
    tm = POST_ROWS
    tiles_per_set = max(1, set_rows // tm)
    assert t % tm == 0 and tm % CHAIN_ROWS == 0 and (mod.shape[0] == 1 or set_rows % tm == 0)
    row = lambda i: (i, 0)
    const = lambda i: (0, 0)
    once = pl.Buffered(1)
    half = w_in.shape[1] // 2
    assert w_in.shape[1] == 2 * half and half % LANES == 0 and half >= 2 * d
    return pl.pallas_call(
        _post_kernel,
        grid=(t // tm,),
        in_specs=[pl.BlockSpec((tm, d), row),
                  pl.BlockSpec((tm, 512), row),
                  pl.BlockSpec((tm, 512), row),
                  pl.BlockSpec((1, 8, d), _mod_index(mod, tiles_per_set)),
                  pl.BlockSpec((1, d), const),
                  pl.BlockSpec((1, d), const),
                  pl.BlockSpec((d, half), lambda i: (0, 1), pipeline_mode=once),
                  pl.BlockSpec(w_ba.shape, const, pipeline_mode=once),
                  pl.BlockSpec(w_bb.shape, const, pipeline_mode=once),
                  pl.BlockSpec(w_out.shape, const, pipeline_mode=once),
                  pl.BlockSpec(w_router2.shape, const, pipeline_mode=once)],
        out_specs=[pl.BlockSpec((tm, d), row), pl.BlockSpec((tm, d), row), pl.BlockSpec((tm, LANES), row)],
        out_shape=[jax.ShapeDtypeStruct((t, d), F32), jax.ShapeDtypeStruct((t, d), BF16),
                   jax.ShapeDtypeStruct((t, LANES), F32)],
        name="post_attn",
        compiler_params=_cparams(1),
    )(x2, ya, yb, mod, g1, g2, w_in, w_ba, w_bb, w_out, w_router2)


def _select_kernel(aff_ref, slot_o, slot_t_o, affb_o, *, cap):
    aff = aff_ref[0]
    n = aff.shape[0]
    capf = float(cap)

    def enough(cand):
        return jnp.sum(jnp.where(aff >= cand, 1.0, 0.0), axis=0, keepdims=True) >= capf

    pw = jnp.ones((1, LANES), F32)
    for k in (64, 32, 16, 8, 4, 2, 1):
        pw = jnp.where(enough(pw * 2.0 ** -(k - 1)), pw, pw * 2.0 ** -k)

    def mantissa_step(_, carry):
        thr, step = carry
        cand = thr + step
        return jnp.where(enough(cand), cand, thr), step * 0.5

    thr, _ = lax.fori_loop(0, 23, mantissa_step, (pw, pw * 0.5))
    above = aff > thr
    tied = aff == thr
    need = capf - jnp.sum(jnp.where(above, 1.0, 0.0), axis=0, keepdims=True)
    r_i = lax.broadcasted_iota(jnp.int32, (n, n), 0)
    c_i = lax.broadcasted_iota(jnp.int32, (n, n), 1)
    before = jnp.where(c_i < r_i, 1.0, 0.0).astype(BF16)
    tie_rank = _dot(before, jnp.where(tied, 1.0, 0.0).astype(BF16))
    sel = above | (tied & (tie_rank < need))
    slot = _dot(before, jnp.where(sel, 1.0, 0.0).astype(BF16))
    slot = jnp.where(sel, slot, NOT_SELECTED)
    slot_o[0] = slot.astype(BF16)
    slot_t_o[0] = slot.T
    affb_o[0] = aff.astype(BF16)


def _dispatch_kernel(slot_t_ref, h2_ref, xe_o, *, cap):
    n = slot_t_ref.shape[2]
    g = h2_ref.shape[0] // n
    per = min(N_EXPERTS, DISPATCH_ROWS // cap)
    slot_iota = lax.broadcasted_iota(jnp.int32, (cap, n), 0).astype(F32)
    for k in range(g):
        h2 = h2_ref[k * n:(k + 1) * n, :]
        slot_t = slot_t_ref[0, k * N_EXPERTS:(k + 1) * N_EXPERTS, :]
        for e0 in range(0, N_EXPERTS, per):
            onehot = jnp.concatenate(
                [jnp.where(slot_t[e:e + 1, :] == slot_iota, 1.0, 0.0) for e in range(e0, e0 + per)],
                axis=0).astype(BF16)
            rows = _dot(onehot, h2).astype(BF16)
            for j in range(per):
                xe_o[e0 + j, k * cap:(k + 1) * cap, :] = rows[j * cap:(j + 1) * cap]


def _route(aff, h2, n_sets, set_rows):
    t, d = h2.shape
    n = set_rows
    cap = CAPACITY_FACTOR * n // N_EXPERTS
    assert cap & (cap - 1) == 0 and cap % 16 == 0 and DISPATCH_ROWS % cap == 0
    n_packs = -(-n_sets // SETS_PER_PACK)
    aff16 = aff[:, :N_EXPERTS].reshape(n_sets, n, N_EXPERTS)
    aff16 = jnp.pad(aff16, ((0, n_packs * SETS_PER_PACK - n_sets), (0, 0), (0, 0)))
    packed = aff16.reshape(n_packs, SETS_PER_PACK, n, N_EXPERTS).transpose(0, 2, 1, 3).reshape(n_packs, n, LANES)
    pack_blk = lambda p: (p, 0, 0)
    slot, slot_t, affb = pl.pallas_call(
        functools.partial(_select_kernel, cap=cap),
        grid=(n_packs,),
        in_specs=[pl.BlockSpec((1, n, LANES), pack_blk)],
        out_specs=[pl.BlockSpec((1, n, LANES), pack_blk), pl.BlockSpec((1, LANES, n), pack_blk),
                   pl.BlockSpec((1, n, LANES), pack_blk)],
        out_shape=[jax.ShapeDtypeStruct((n_packs, n, LANES), BF16),
                   jax.ShapeDtypeStruct((n_packs, LANES, n), F32),
                   jax.ShapeDtypeStruct((n_packs, n, LANES), BF16)],
        name=f"select_cap{cap}",
        compiler_params=_cparams(1),
    )(packed)
    g = _sets_per_step(n_sets, n)
    per_pack = SETS_PER_PACK // g
    xe = pl.pallas_call(
        functools.partial(_dispatch_kernel, cap=cap),
        grid=(n_sets // g,),
        in_specs=[pl.BlockSpec((1, g * N_EXPERTS, n), lambda s: (s // per_pack, s % per_pack, 0)),
                  pl.BlockSpec((g * n, d), lambda s: (s, 0))],
        out_specs=pl.BlockSpec((N_EXPERTS, g * cap, d), lambda s: (0, s, 0)),
        out_shape=jax.ShapeDtypeStruct((N_EXPERTS, n_sets * cap, d), BF16),
        name=f"dispatch_cap{cap}",
        compiler_params=_cparams(1),
    )(slot_t, h2)
    return xe, (slot, affb)


def _expert_kernel(*refs, n_groups):
    x_refs = refs[:n_groups]
    wg_ref, wu_ref, wd_ref = refs[n_groups:n_groups + 3]
    o_refs = refs[n_groups + 3:2 * n_groups + 3]
    wg_s, wu_s, wd_s = refs[2 * n_groups + 3:]
    wg_s[...] = wg_ref[0].astype(BF16)
    wu_s[...] = wu_ref[0].astype(BF16)
    wd_s[...] = wd_ref[0].astype(BF16)
    sub = EXPERT_ROWS
    for x_ref, o_ref in zip(x_refs, o_refs):
        for r0 in range(0, x_ref.shape[1], sub):
            x = x_ref[0, r0:r0 + sub, :]
            a = _dot(x, wg_s[...])
            u = _dot(x, wu_s[...])
            hmid = (a * jax.nn.sigmoid(a) * u).astype(BF16)
            o_ref[0, r0:r0 + sub, :] = _dot(hmid, wd_s[...]).astype(BF16)


def _experts(xes, w_gate, w_up, w_down):
    e, d, f = w_gate.shape
    blk = lambda i: (i, 0, 0)
    return pl.pallas_call(
        functools.partial(_expert_kernel, n_groups=len(xes)),
        grid=(e,),
        in_specs=[pl.BlockSpec((1, x.shape[1], d), blk) for x in xes]
        + [pl.BlockSpec((1, d, f), blk), pl.BlockSpec((1, d, f), blk), pl.BlockSpec((1, f, d), blk)],
        out_specs=[pl.BlockSpec((1, x.shape[1], d), blk) for x in xes],
        out_shape=[jax.ShapeDtypeStruct(x.shape, BF16) for x in xes],
        scratch_shapes=[pltpu.VMEM((d, f), BF16), pltpu.VMEM((d, f), BF16), pltpu.VMEM((f, d), BF16)],
        name="experts",
        compiler_params=_cparams(1),
    )(*xes, w_gate, w_up, w_down)


def _combine_kernel(slot_ref, affb_ref, ye_ref, x1_ref, mod_ref, gf_ref, y_o, *, cap):
    e, gcap, d = ye_ref.shape
    g = gcap // cap
    n = x1_ref.shape[0] // g
    shift = cap.bit_length() - 1
    both = jnp.concatenate([slot_ref[0], affb_ref[0]], axis=1)
    want = (lax.broadcasted_iota(jnp.int32, (n, DISPATCH_ROWS), 1) & (cap - 1)).astype(F32) * SLOT_RADIX
    for k in range(g):
        rows = slice(k * n, (k + 1) * n)
        first_lane = ((pl.program_id(0) * g + k) % SETS_PER_PACK) * N_EXPERTS
        pieces = []
        for c0 in range(0, e * cap, DISPATCH_ROWS):
            src = lax.broadcasted_iota(jnp.int32, (2 * LANES, DISPATCH_ROWS), 0)
            col = lax.broadcasted_iota(jnp.int32, (2 * LANES, DISPATCH_ROWS), 1) + c0
            lane_of_col = first_lane + lax.shift_right_logical(col, shift)
            spread = jnp.where(src == lane_of_col, SLOT_RADIX,
                               jnp.where(src == lane_of_col + LANES, 1.0, 0.0)).astype(BF16)
            rest = _dot(both, spread) - want
            pieces.append(jnp.where(rest >= 0.0, jnp.where(rest <= 1.0, rest, 0.0), 0.0).astype(BF16))
        gt = pieces[0] if len(pieces) == 1 else jnp.concatenate(pieces, axis=1)
        ye = ye_ref[:, k * cap:(k + 1) * cap, :].reshape(e * cap, d)
        moe = _dot(gt, ye)
        x = x1_ref[rows, :] + mod_ref[0][5:6] * moe
        y_o[rows, :] = _rms(x, gf_ref[...])


def _sets_per_step(n_sets, set_rows):
    g = SETS_PER_PACK if set_rows * SETS_PER_PACK <= SHARED_STEP_ROWS else 1
    return g if n_sets % g == 0 else 1


def _combine(packs, ye, x1, mod, n_sets, set_rows, g_final):
    slot, affb = packs
    t, d = x1.shape
    cap = ye.shape[1] // n_sets
    g = _sets_per_step(n_sets, set_rows) if mod.shape[0] == 1 else 1
    tm = min(ROW_TILE, set_rows) if g == 1 else g * set_rows
    tiles = max(1, set_rows // tm)
    per_pack = SETS_PER_PACK // g
    pack_rows = pl.BlockSpec((1, tm // g, LANES), lambda s, i: (s // per_pack, i, 0))
    return pl.pallas_call(
        functools.partial(_combine_kernel, cap=cap),
        grid=(n_sets // g, tiles),
        in_specs=[pack_rows, pack_rows,
                  pl.BlockSpec((N_EXPERTS, g * cap, d), lambda s, i: (0, s, 0)),
                  pl.BlockSpec((tm, d), lambda s, i: (s * tiles + i, 0)),
                  pl.BlockSpec((1, 8, d), (lambda s, i: (s, 0, 0)) if mod.shape[0] > 1 else (lambda s, i: (0, 0, 0))),
                  pl.BlockSpec((1, d), lambda s, i: (0, 0))],
        out_specs=pl.BlockSpec((tm, d), lambda s, i: (s * tiles + i, 0)),
        out_shape=jax.ShapeDtypeStruct((t, d), F32),
        name=f"combine_cap{cap}",
        compiler_params=_cparams(2),
    )(slot, affb, ye, x1, mod, g_final)


def _rope_tables(n_tokens):
    n_rows = n_tokens // GRID_W
    rowp = jnp.repeat(jnp.arange(n_rows), GRID_W).astype(F32)
    colp = jnp.tile(jnp.arange(GRID_W), n_rows).astype(F32)
    quarter = HEAD_DIM // 4
    freqs = ROPE_THETA ** (-jnp.arange(quarter, dtype=F32) / quarter)
    ang = jnp.stack([rowp[:, None] * freqs, colp[:, None] * freqs], axis=1)
    cos, sin = jnp.cos(ang), jnp.sin(ang)
    zero = jnp.zeros_like(sin)
    c = jnp.stack([cos, cos], axis=2).reshape(n_tokens, HEAD_DIM)
    s_up = jnp.stack([-sin, zero], axis=2).reshape(n_tokens, HEAD_DIM)
    s_dn = jnp.stack([zero, sin], axis=2).reshape(n_tokens, HEAD_DIM)
    return tuple(jnp.tile(t, (1, LANES // HEAD_DIM)) for t in (c, s_up, s_dn))


def _mix_and_route(x2, n_sets, set_rows, mod, lw, rope_tabs, cache, emit_cache, lam_init, tq):
    (g_attn, g_ffn, w_in, gq_t, gk_t, lq1, lk1, lq2, lk2, g_diff, w_ba, w_bb, w_out, w_router2) = lw
    outs = _inproj(x2, mod, set_rows, g_attn, w_in, gq_t, gk_t, rope_tabs, emit_cache)
    qa, kta, vta, qb, ktb, vb16 = outs[:6]
    vb16 = vb16.reshape(n_sets, set_rows, vb16.shape[1])
    ks_a, vs_a, ks_b, vs_b = [kta], [vta], [ktb], [vb16]
    if cache is not None:
        ckta, cvta, cktb, cvb = cache
        ks_a, vs_a, ks_b, vs_b = [ckta, kta], [cvta, vta], [cktb, ktb], [cvb, vb16]
    per_step = CONTEXT_SETS_PER_STEP if (tq == set_rows and n_sets % CONTEXT_SETS_PER_STEP == 0) else 1
    ya, yb = _both_mixers(qa, ks_a, vs_a, qb, ks_b, vs_b, (lq1, lk1, lq2, lk2, g_diff), set_rows, tq, per_step,
                          lam_init)
    x1, h2, aff = _post(x2, ya, yb, mod, set_rows, g_attn, g_ffn, w_in, w_ba, w_bb, w_out, w_router2)
    xe, packs = _route(aff, h2, n_sets, set_rows)
    return x1, xe, packs, outs[6:]


def kernel(x_prompt, x_sample, cache_attn_k, cache_attn_v, cache_diff_k, cache_diff_v, c, c_ctx, w_mod, b_mod,
           g_attn_norm, g_ffn_norm, w_in, g_q_norm, g_k_norm, lambda_q1, lambda_k1, lambda_q2, lambda_k2,
           g_diff_norm, w_branch_a, w_branch_b, w_out, w_router, w_exp_gate, w_exp_up, w_exp_down, g_final):
    batch, seq, d = x_prompt.shape
    dec_batch, dec_seq, _ = x_sample.shape
    depth = w_in.shape[0]
    assert depth == 1, "the final norm is fused into the layer's combine step"
    past = cache_attn_k.shape[2]
    assert w_in.shape[2] == N_QKV + 2 * d

    xp = x_prompt.reshape(batch * seq, d)
    xs = x_sample.reshape(dec_batch * dec_seq, d)
    rope_tabs = _rope_tables(dec_seq)
    c_rows = jnp.concatenate([c, c_ctx[None, :], jnp.zeros((16 - dec_batch - 1, d), F32)], axis=0)
    yp = ys = None
    caches = []
    for l in range(depth):
        lam_init = 0.8 - 0.6 * math.exp(-0.3 * l)
        mod6 = _modulation(c_rows, w_mod[l], b_mod[l])
        mod = jnp.pad(jnp.transpose(mod6, (1, 0, 2)), ((0, 0), (0, 2), (0, 0)))
        mod_lat, mod_ctx = mod[:dec_batch], mod[dec_batch:dec_batch + 1]
        w_in16, w_ba16, w_bb16, w_out16, w_router2 = _round_weights(
            [w_in[l], w_branch_a[l], w_branch_b[l], w_out[l]],
            jnp.pad(w_router[l], ((0, 0), (0, LANES - N_EXPERTS))))
        lw = (g_attn_norm[l][None, :], g_ffn_norm[l][None, :], w_in16,
              jnp.tile(g_q_norm[l], HEADS_A)[None, :], jnp.tile(g_k_norm[l], KV_HEADS_A)[None, :],
              lambda_q1[l], lambda_k1[l], lambda_q2[l], lambda_k2[l], g_diff_norm[l][None, :],
              w_ba16, w_bb16, w_out16, w_router2)
        x1p, xe_p, packs_p, cache_out = _mix_and_route(xp, batch, seq, mod_ctx, lw, None, None, True, lam_init, seq)
        caches.append(cache_out)
        feat_major = lambda a: jnp.moveaxis(a[:, l], 1, -1).reshape(dec_batch, -1, past)
        cache_l = _cache_prep([feat_major(cache_attn_k), feat_major(cache_attn_v), feat_major(cache_diff_k)],
                              cache_diff_v[:, l].reshape(dec_batch, past * HEADS_B, -1))
        x1s, xe_s, packs_s, _ = _mix_and_route(xs, dec_batch, dec_seq, mod_lat, lw, rope_tabs, cache_l, False,
                                               lam_init, LATENT_Q_TILE)
        ye_p, ye_s = _experts([xe_p, xe_s], w_exp_gate[l], w_exp_up[l], w_exp_down[l])
        yp = _combine(packs_p, ye_p, x1p, mod_ctx, batch, seq, g_final[None, :])
        ys = _combine(packs_s, ye_s, x1s, mod_lat, dec_batch, dec_seq, g_final[None, :])
    y_prompt = yp.reshape(batch, seq, d)
    y_sample = ys.reshape(dec_batch, dec_seq, d)
    tok_major = lambda a, dims: jnp.moveaxis(a.reshape((batch,) + dims + (seq,)), -1, 1)
    new_attn_k = jnp.stack([tok_major(cc[0], (KV_HEADS_A, HEAD_DIM)) for cc in caches], axis=1)
    new_attn_v = jnp.stack([tok_major(cc[1], (KV_HEADS_A, HEAD_DIM)) for cc in caches], axis=1)
    new_diff_k = jnp.stack([tok_major(cc[2], (HEADS_B, 2, HEAD_DIM)) for cc in caches], axis=1)
    new_diff_v = jnp.stack([cc[3].reshape(batch, seq, HEADS_B, 2 * HEAD_DIM) for cc in caches], axis=1)
    return (y_prompt, y_sample, new_attn_k, new_attn_v, new_diff_k, new_diff_v)
```

```python
import functools
import math

import numpy as np
import jax
import jax.numpy as jnp
from jax import lax
from jax.experimental import pallas as pl
from jax.experimental.pallas import tpu as pltpu

F32 = jnp.float32
BF16 = jnp.bfloat16

HEAD_DIM = 64
HEADS_A = 8
KV_HEADS_A = 2
HEADS_B = 4
N_QKV = HEADS_A * HEAD_DIM + 2 * KV_HEADS_A * HEAD_DIM + 3 * HEADS_B * 2 * HEAD_DIM
N_EXPERTS = 16
CAPACITY_FACTOR = 2
GRID_W = 64
ROPE_THETA = 10000.0
EPS = 1e-6
LANES = 128
ROW_TILE = 1024
POST_ROWS = 1024
CHAIN_ROWS = 512
EXPERT_ROWS = 256
LATENT_Q_TILE = 512
CONTEXT_SETS_PER_STEP = 4
SETS_PER_PACK = LANES // N_EXPERTS
SHARED_STEP_ROWS = 2048
DISPATCH_ROWS = 512
NEG_BIG = -1e30
NOT_SELECTED = -1.0
SLOT_RADIX = 256.0
VMEM_LIMIT = 56 * 1024 * 1024
ROUND_STEPS = 8


def _cparams(n_axes):
    return pltpu.CompilerParams(dimension_semantics=("arbitrary",) * n_axes,
                                vmem_limit_bytes=VMEM_LIMIT)


def _dot(a, b):
    return jnp.dot(a, b, preferred_element_type=F32)


def _dot_nt(a, b):
    return lax.dot_general(a, b, (((1,), (1,)), ((), ())), preferred_element_type=F32)


def _split(a):
    hi = a.astype(BF16)
    lo = (a - hi.astype(F32)).astype(BF16)
    return hi, lo


def _dot3(a, b):
    m = a.shape[0]
    a_hi, a_lo = _split(a)
    b_hi, b_lo = _split(b)
    both = _dot(jnp.concatenate([a_hi, a_lo], axis=0), b_hi)
    return both[0:m] + both[m:2 * m] + _dot(a_hi, b_lo)


def _rms(x, g):
    return x * lax.rsqrt(jnp.mean(x * x, axis=-1, keepdims=True) + EPS) * g


def _mod_index(mod, tiles_per_set):
    if mod.shape[0] == 1:
        return lambda i: (0, 0, 0)
    return lambda i: (i // tiles_per_set, 0, 0)


def _mod_kernel(c_ref, w_ref, b_ref, o_ref):
    c = c_ref[...]
    a = c * jax.nn.sigmoid(c)
    o_ref[0] = _dot3(a, w_ref[...]) + b_ref[0]


def _modulation(c_rows, w_mod, b_mod):
    r, d = c_rows.shape
    return pl.pallas_call(
        _mod_kernel,
        grid=(6,),
        in_specs=[pl.BlockSpec((r, d), lambda j: (0, 0)),
                  pl.BlockSpec((d, d), lambda j: (0, j)),
                  pl.BlockSpec((1, 1, d), lambda j: (j, 0, 0))],
        out_specs=pl.BlockSpec((1, r, d), lambda j: (j, 0, 0)),
        out_shape=jax.ShapeDtypeStruct((6, r, d), F32),
        name="mod",
        compiler_params=_cparams(1),
    )(c_rows, w_mod, b_mod.reshape(6, 1, d))


def _round_weights_kernel(*refs):
    n = len(refs) // 2
    for src, out in zip(refs[:n - 1], refs[n:-1]):
        out[...] = src[...].astype(BF16)
    hi, lo = _split(refs[n - 1][...])
    refs[-1][...] = jnp.concatenate([hi, lo], axis=1)


def _round_weights(weights, w_router_p):
    steps = ROUND_STEPS
    arrays = list(weights) + [w_router_p]
    blk = lambda a, cols: pl.BlockSpec((a.shape[0] // steps, cols), lambda i: (i, 0))
    out_cols = [a.shape[1] for a in weights] + [2 * w_router_p.shape[1]]
    return pl.pallas_call(
        _round_weights_kernel,
        grid=(steps,),
        in_specs=[blk(a, a.shape[1]) for a in arrays],
        out_specs=[blk(a, c) for a, c in zip(arrays, out_cols)],
        out_shape=[jax.ShapeDtypeStruct((a.shape[0], c), BF16) for a, c in zip(arrays, out_cols)],
        name="round_weights",
        compiler_params=_cparams(1),
    )(*arrays)


def _seg_sumsq(x, ones_blockdiag):
    return _dot((x * x).astype(BF16), ones_blockdiag)


def _rope(x, c, s_up, s_dn):
    w = x.shape[1]
    reps = w // c.shape[1]
    if reps > 1:
        c, s_up, s_dn = (jnp.concatenate([t] * reps, axis=1) for t in (c, s_up, s_dn))
    return x * c + pltpu.roll(x, w - 16, 1) * s_up + pltpu.roll(x, 16, 1) * s_dn


def _inproj_kernel(*refs, rope, emit_cache):
    (x_ref, mod_ref, g_ref, w_ref, gq_ref, gk_ref, bd512_ref, bd128_ref), refs = refs[:8], refs[8:]
    if rope:
        (c_ref, su_ref, sd_ref), refs = refs[:3], refs[3:]
    qa_o, kta_o, vta_o, qb_o, ktb_o, vb_o = refs[:6]
    if emit_cache:
        ka_c, va_c, kb_c, vb_c = refs[6:10]

    mod = mod_ref[0]
    wa = HEADS_A * HEAD_DIM
    wkv = KV_HEADS_A * HEAD_DIM
    wb = HEADS_B * 2 * HEAD_DIM
    o_ka, o_va, o_qb = wa, wa + wkv, wa + 2 * wkv
    o_kb, o_vb = o_qb + wb, o_qb + 2 * wb
    scale = HEAD_DIM ** -0.5 * math.log2(math.e)

    x = x_ref[...]
    h = (_rms(x, g_ref[...]) * (1.0 + mod[1:2]) + mod[0:1]).astype(BF16)
    if rope:
        tabs = (c_ref[...], su_ref[...], sd_ref[...])

    qa = _dot(h, w_ref[:, 0:wa])
    qa = qa * lax.rsqrt(_seg_sumsq(qa, bd512_ref[...]) * (1.0 / HEAD_DIM) + EPS) * gq_ref[...]
    if rope:
        qa = _rope(qa, *tabs)
    qa_o[...] = (qa * scale).astype(BF16)

    kv = _dot(h, w_ref[:, o_ka:o_qb])
    ka, va = kv[:, 0:wkv], kv[:, wkv:2 * wkv]
    ka = ka * lax.rsqrt(_seg_sumsq(ka, bd128_ref[...]) * (1.0 / HEAD_DIM) + EPS) * gk_ref[...]

    def put_feat(val, out_bf16, out_f32):
        rows = out_bf16.shape[2]
        for s in range(out_bf16.shape[0]):
            t = val[s * rows:(s + 1) * rows].T
            if out_f32 is not None:
                out_f32[s] = t
            out_bf16[s] = t.astype(BF16)

    if rope:
        put_feat(_rope(ka, *tabs), kta_o, None)
    else:
        put_feat(ka, kta_o, ka_c if emit_cache else None)
    put_feat(va, vta_o, va_c if emit_cache else None)

    qb = _dot(h, w_ref[:, o_qb:o_kb])
    if rope:
        qb = _rope(qb, *tabs)
    qb_o[...] = (qb * scale).astype(BF16)

    kb = _dot(h, w_ref[:, o_kb:o_vb])
    if rope:
        put_feat(_rope(kb, *tabs), ktb_o, None)
    else:
        put_feat(kb, ktb_o, kb_c if emit_cache else None)

    vb = _dot(h, w_ref[:, o_vb:o_vb + wb])
    if emit_cache:
        for hd in range(HEADS_B):
            vb_c[pl.ds(hd, vb.shape[0], stride=HEADS_B), :] = vb[:, hd * 128:(hd + 1) * 128]
    vb_o[...] = vb.astype(BF16)


def _blockdiag_ones(width):
    g = np.arange(width) // HEAD_DIM
    return jnp.asarray((g[:, None] == g[None, :]).astype(np.float32), dtype=BF16)


def _inproj(x2, mod, set_rows, g_attn, w_in, gq_t, gk_t, rope_tabs, emit_cache):
    t, d = x2.shape
    tm = ROW_TILE
    assert t % tm == 0 and (tm % set_rows == 0 or set_rows % tm == 0)
    tiles_per_set = max(1, set_rows // tm)
    sets_per_tile = max(1, tm // set_rows)
    rope = rope_tabs is not None
    assert not (rope and emit_cache), "cached keys are the position-free ones"
    nq = N_QKV
    row = lambda i: (i, 0)
    const = lambda i: (0, 0)
    in_specs = [pl.BlockSpec((tm, d), row),
                pl.BlockSpec((1, 8, d), _mod_index(mod, tiles_per_set)),
                pl.BlockSpec((1, d), const),
                pl.BlockSpec((d, nq), const, pipeline_mode=pl.Buffered(1)),
                pl.BlockSpec((1, 512), const),
                pl.BlockSpec((1, 128), const),
                pl.BlockSpec((512, 512), const),
                pl.BlockSpec((128, 128), const)]
    args = [x2, mod, g_attn, w_in, gq_t, gk_t, _blockdiag_ones(512), _blockdiag_ones(128)]
    if rope:
        in_specs += [pl.BlockSpec((tm, LANES), lambda i: (i % tiles_per_set, 0))] * 3
        args += list(rope_tabs)
    n_sets = t // set_rows
    wkv, wb = KV_HEADS_A * HEAD_DIM, HEADS_B * 2 * HEAD_DIM
    outs = [("tok", 512, BF16), ("feat", wkv, BF16), ("feat", wkv, BF16),
            ("tok", 512, BF16), ("feat", wb, BF16), ("tok", wb, BF16)]
    if emit_cache:
        outs += [("feat", wkv, F32), ("feat", wkv, F32), ("feat", wb, F32), ("tokhead", wb, F32)]
    feat = lambda i: (i // tiles_per_set, 0, i % tiles_per_set)
    feat_rows = min(tm, set_rows)

    def out_block(kind, w):
        if kind == "tok":
            return pl.BlockSpec((tm, w), row)
        if kind == "tokhead":
            return pl.BlockSpec((tm * HEADS_B, w // HEADS_B), row)
        return pl.BlockSpec((sets_per_tile, w, feat_rows), feat)

    def out_array(kind, w, dt):
        shape = {"tok": (t, w), "tokhead": (t * HEADS_B, w // HEADS_B), "feat": (n_sets, w, set_rows)}[kind]
        return jax.ShapeDtypeStruct(shape, dt)

    return pl.pallas_call(
        functools.partial(_inproj_kernel, rope=rope, emit_cache=emit_cache),
        grid=(t // tm,),
        in_specs=in_specs,
        out_specs=[out_block(kind, w) for kind, w, _ in outs],
        out_shape=[out_array(kind, w, dt) for kind, w, dt in outs],
        name="inproj_rope" if rope else "inproj",
        compiler_params=_cparams(1),
    )(*args)


def _cache_prep_kernel(*refs):
    half = len(refs) // 2
    for src, out in zip(refs[:half - 1], refs[half:-1]):
        out[...] = src[...].astype(BF16)
    v_src, v_out = refs[half - 1], refs[-1]
    keys = v_out.shape[1]
    for hd in range(HEADS_B):
        v_out[0, :, hd * 128:(hd + 1) * 128] = v_src[0, pl.ds(hd, keys, stride=HEADS_B), :].astype(BF16)


def _cache_prep(arrays, v_by_head):
    blk = lambda b: (b, 0, 0)
    n_sets, rows, width = v_by_head.shape
    v_shape = (n_sets, rows // HEADS_B, width * HEADS_B)
    return pl.pallas_call(
        _cache_prep_kernel,
        grid=(n_sets,),
        in_specs=[pl.BlockSpec((1,) + a.shape[1:], blk) for a in arrays + [v_by_head]],
        out_specs=[pl.BlockSpec((1,) + a.shape[1:], blk) for a in arrays] + [pl.BlockSpec((1,) + v_shape[1:], blk)],
        out_shape=[jax.ShapeDtypeStruct(a.shape, BF16) for a in arrays] + [jax.ShapeDtypeStruct(v_shape, BF16)],
        name="cache_prep",
        compiler_params=_cparams(1),
    )(*arrays, v_by_head)


def _exp_parts(scores):
    m = scores[0].max(axis=-1, keepdims=True)
    for s in scores[1:]:
        m = jnp.maximum(m, s.max(axis=-1, keepdims=True))
    return [jnp.exp2(s - m) for s in scores]


def _row_sum(parts):
    l = parts[0].sum(axis=-1, keepdims=True)
    for e in parts[1:]:
        l = l + e.sum(axis=-1, keepdims=True)
    return l


def _pad_rows(x, first, ones_row=False):
    if ones_row:
        z = jnp.where(lax.broadcasted_iota(jnp.int32, x.shape, 0) == 0, 1.0, 0.0).astype(x.dtype)
    else:
        z = jnp.zeros_like(x)
    return jnp.concatenate([x, z] if first else [z, x], axis=0)


def _gqa_kernel(*refs, n_src, mxu_sums):
    q_ref = refs[0]
    k_refs = refs[1:1 + n_src]
    v_refs = refs[1 + n_src:1 + 2 * n_src]
    o_ref = refs[1 + 2 * n_src]
    sets_here = k_refs[0].shape[0]
    tq = q_ref.shape[0] // sets_here
    lane = lax.broadcasted_iota(jnp.int32, (2 * tq, LANES), 1)
    lo_half = lane < HEAD_DIM
    for s in range(sets_here):
        rows = slice(s * tq, (s + 1) * tq)
        for g in range(KV_HEADS_A):
            c0 = g * 256
            f0 = g * HEAD_DIM
            q = jnp.concatenate([q_ref[rows, c0:c0 + 128], q_ref[rows, c0 + 128:c0 + 256]], axis=0)
            kts = [k[s, f0:f0 + HEAD_DIM, :] for k in k_refs]
            vts = [v[s, f0:f0 + HEAD_DIM, :] for v in v_refs]
            e_lo = _exp_parts([_dot(q, _pad_rows(kt, True)) for kt in kts])
            e_hi = _exp_parts([_dot(q, _pad_rows(kt, False)) for kt in kts])
            o_lo = o_hi = None
            for e_part, vt in zip(e_lo, vts):
                t = _dot_nt(e_part.astype(BF16), _pad_rows(vt, True, ones_row=mxu_sums))
                o_lo = t if o_lo is None else o_lo + t
            for e_part, vt in zip(e_hi, vts):
                t = _dot_nt(e_part.astype(BF16), _pad_rows(vt, False, ones_row=mxu_sums))
                o_hi = t if o_hi is None else o_hi + t
            if mxu_sums:
                l_lo, l_hi = o_lo[:, HEAD_DIM:HEAD_DIM + 1], o_hi[:, 0:1]
                o = jnp.where(lo_half, o_lo * (1.0 / l_lo), o_hi * (1.0 / l_hi))
            else:
                o = (o_lo + o_hi) * jnp.where(lo_half, 1.0 / _row_sum(e_lo), 1.0 / _row_sum(e_hi))
            o_ref[rows, c0:c0 + 128] = o[0:tq].astype(BF16)
            o_ref[rows, c0 + 128:c0 + 256] = o[tq:2 * tq].astype(BF16)


def _diff_kernel(*refs, n_src, lam_init, stack_maps):
    q_ref = refs[0]
    k_refs = refs[1:1 + n_src]
    v_refs = refs[1 + n_src:1 + 2 * n_src]
    lq1, lk1, lq2, lk2, gd_ref, o_ref = refs[1 + 2 * n_src:]
    lam_all = (jnp.exp(jnp.sum(lq1[...] * lk1[...], axis=-1, keepdims=True))
               - jnp.exp(jnp.sum(lq2[...] * lk2[...], axis=-1, keepdims=True)) + lam_init)
    sets_here = k_refs[0].shape[0]
    tq = q_ref.shape[0] // sets_here
    for s in range(sets_here):
        rows = slice(s * tq, (s + 1) * tq)
        for j in range(HEADS_B):
            lam = lam_all[j:j + 1, :]
            q = q_ref[rows, j * 128:(j + 1) * 128]
            k0s = [_pad_rows(k[s, j * 128:j * 128 + HEAD_DIM, :], True) for k in k_refs]
            k1s = [_pad_rows(k[s, j * 128 + HEAD_DIM:(j + 1) * 128, :], False) for k in k_refs]
            e0 = _exp_parts([_dot(q, k0) for k0 in k0s])
            e1 = _exp_parts([_dot(q, k1) for k1 in k1s])
            o0 = o1 = None
            for a0, a1, v in zip(e0, e1, v_refs):
                val = v[s, :, j * 128:(j + 1) * 128]
                if stack_maps:
                    t = _dot(jnp.concatenate([a0.astype(BF16), a1.astype(BF16)], axis=0), val)
                    t0, t1 = t[0:tq], t[tq:2 * tq]
                else:
                    t0, t1 = _dot(a0.astype(BF16), val), _dot(a1.astype(BF16), val)
                o0 = t0 if o0 is None else o0 + t0
                o1 = t1 if o1 is None else o1 + t1
            o = o0 * (1.0 / _row_sum(e0)) - o1 * (lam / _row_sum(e1))
            o = _rms(o, gd_ref[...]) * (1.0 - lam_init)
            o_ref[rows, j * 128:(j + 1) * 128] = o.astype(BF16)


def _both_mixers_kernel(*refs, n_src, lam_init, stack_maps, mxu_sums):
    per = 1 + 2 * n_src
    ya_o, yb_o = refs[-2:]
    _gqa_kernel(*refs[:per], ya_o, n_src=n_src, mxu_sums=mxu_sums)
    _diff_kernel(*refs[per:2 * per + 5], yb_o, n_src=n_src, lam_init=lam_init, stack_maps=stack_maps)


def _both_mixers(qa, ks_a, vs_a, qb, ks_b, vs_b, lam_refs, set_rows, tq, sets_per_step, lam_init):
    t = qa.shape[0]
    n_sets = t // set_rows
    q_tiles = set_rows // tq
    assert sets_per_step == 1 or q_tiles == 1
    rows = sets_per_step * tq
    tile = lambda b, i: (b * q_tiles + i, 0)
    kv = lambda a: pl.BlockSpec((sets_per_step,) + a.shape[1:], lambda b, i: (b, 0, 0))
    in_specs = ([pl.BlockSpec((rows, qa.shape[1]), tile)] + [kv(a) for a in list(ks_a) + list(vs_a)]
                + [pl.BlockSpec((rows, qb.shape[1]), tile)] + [kv(a) for a in list(ks_b) + list(vs_b)]
                + [pl.BlockSpec(e.shape, lambda b, i: (0, 0)) for e in lam_refs])
    return pl.pallas_call(
        functools.partial(_both_mixers_kernel, n_src=len(ks_a), lam_init=lam_init,
                          stack_maps=len(ks_a) == 1, mxu_sums=len(ks_a) > 1),
        grid=(n_sets // sets_per_step, q_tiles),
        in_specs=in_specs,
        out_specs=[pl.BlockSpec((rows, qa.shape[1]), tile), pl.BlockSpec((rows, qb.shape[1]), tile)],
        out_shape=[jax.ShapeDtypeStruct(qa.shape, BF16), jax.ShapeDtypeStruct(qb.shape, BF16)],
        name=f"mixers_{len(ks_a)}src",
        compiler_params=_cparams(2),
    )(qa, *ks_a, *vs_a, qb, *ks_b, *vs_b, *lam_refs)


def _post_kernel(x_ref, ya_ref, yb_ref, mod_ref, g1_ref, g2_ref, win_ref, wba_ref, wbb_ref, wo_ref, wr2_ref,
                 x1_o, h2_o, aff_o):
    d = x_ref.shape[1]
    g0 = win_ref.shape[1] - 2 * d
    mod = mod_ref[0]
    for r0 in range(0, x_ref.shape[0], CHAIN_ROWS):
        rows = slice(r0, r0 + CHAIN_ROWS)
        x = x_ref[rows, :]
        h = (_rms(x, g1_ref[...]) * (1.0 + mod[1:2]) + mod[0:1]).astype(BF16)
        ga = jax.nn.sigmoid(_dot(h, win_ref[:, g0:g0 + d]))
        merged = ga * _dot(ya_ref[rows, :], wba_ref[...])
        gb = jax.nn.sigmoid(_dot(h, win_ref[:, g0 + d:g0 + 2 * d]))
        merged = merged + gb * _dot(yb_ref[rows, :], wbb_ref[...])
        m = _dot(merged.astype(BF16), wo_ref[...])
        x1 = x + mod[2:3] * m
        x1_o[rows, :] = x1
        h2 = _rms(x1, g2_ref[...]) * (1.0 + mod[4:5]) + mod[3:4]
        h2_o[rows, :] = h2.astype(BF16)
        h2_hi, h2_lo = _split(h2)
        both = _dot(h2_hi, wr2_ref[...])
        logits = both[:, 0:LANES] + both[:, LANES:2 * LANES] + _dot(h2_lo, wr2_ref[:, 0:LANES])
        lane = lax.broadcasted_iota(jnp.int32, logits.shape, 1)
        logits = jnp.where(lane < N_EXPERTS, logits, NEG_BIG)
        e = jnp.exp(logits - logits.max(axis=-1, keepdims=True))
        aff_o[rows, :] = e / e.sum(axis=-1, keepdims=True)


def _post(x2, ya, yb, mod, set_rows, g1, g2, w_in, w_ba, w_bb, w_out, w_router2):
    t, d = x2.shape
    tm = POST_ROWS
    tiles_per_set = max(1, set_rows // tm)
    assert t % tm == 0 and tm % CHAIN_ROWS == 0 and (mod.shape[0] == 1 or set_rows % tm == 0)
    row = lambda i: (i, 0)
    const = lambda i: (0, 0)
    once = pl.Buffered(1)
    half = w_in.shape[1] // 2
    assert w_in.shape[1] == 2 * half and half % LANES == 0 and half >= 2 * d
    return pl.pallas_call(
        _post_kernel,
        grid=(t // tm,),
        in_specs=[pl.BlockSpec((tm, d), row),
                  pl.BlockSpec((tm, 512), row),
                  pl.BlockSpec((tm, 512), row),
                  pl.BlockSpec((1, 8, d), _mod_index(mod, tiles_per_set)),
                  pl.BlockSpec((1, d), const),
                  pl.BlockSpec((1, d), const),
                  pl.BlockSpec((d, half), lambda i: (0, 1), pipeline_mode=once),
                  pl.BlockSpec(w_ba.shape, const, pipeline_mode=once),
                  pl.BlockSpec(w_bb.shape, const, pipeline_mode=once),
                  pl.BlockSpec(w_out.shape, const, pipeline_mode=once),
                  pl.BlockSpec(w_router2.shape, const, pipeline_mode=once)],
        out_specs=[pl.BlockSpec((tm, d), row), pl.BlockSpec((tm, d), row), pl.BlockSpec((tm, LANES), row)],
        out_shape=[jax.ShapeDtypeStruct((t, d), F32), jax.ShapeDtypeStruct((t, d), BF16),
                   jax.ShapeDtypeStruct((t, LANES), F32)],
        name="post_attn",
        compiler_params=_cparams(1),
    )(x2, ya, yb, mod, g1, g2, w_in, w_ba, w_bb, w_out, w_router2)


def _select_kernel(aff_ref, slot_o, slot_t_o, affb_o, *, cap):
    aff = aff_ref[0]
    n = aff.shape[0]
    capf = float(cap)

    def enough(cand):
        return jnp.sum(jnp.where(aff >= cand, 1.0, 0.0), axis=0, keepdims=True) >= capf

    pw = jnp.ones((1, LANES), F32)
    for k in (64, 32, 16, 8, 4, 2, 1):
        pw = jnp.where(enough(pw * 2.0 ** -(k - 1)), pw, pw * 2.0 ** -k)

    def mantissa_step(_, carry):
        thr, step = carry
        cand = thr + step
        return jnp.where(enough(cand), cand, thr), step * 0.5

    thr, _ = lax.fori_loop(0, 23, mantissa_step, (pw, pw * 0.5))
    above = aff > thr
    tied = aff == thr
    need = capf - jnp.sum(jnp.where(above, 1.0, 0.0), axis=0, keepdims=True)
    r_i = lax.broadcasted_iota(jnp.int32, (n, n), 0)
    c_i = lax.broadcasted_iota(jnp.int32, (n, n), 1)
    before = jnp.where(c_i < r_i, 1.0, 0.0).astype(BF16)
    tie_rank = _dot(before, jnp.where(tied, 1.0, 0.0).astype(BF16))
    sel = above | (tied & (tie_rank < need))
    slot = _dot(before, jnp.where(sel, 1.0, 0.0).astype(BF16))
    slot = jnp.where(sel, slot, NOT_SELECTED)
    slot_o[0] = slot.astype(BF16)
    slot_t_o[0] = slot.T
    affb_o[0] = aff.astype(BF16)


def _dispatch_kernel(slot_t_ref, h2_ref, xe_o, *, cap):
    n = slot_t_ref.shape[2]
    g = h2_ref.shape[0] // n
    per = min(N_EXPERTS, DISPATCH_ROWS // cap)
    slot_iota = lax.broadcasted_iota(jnp.int32, (cap, n), 0).astype(F32)
    for k in range(g):
        h2 = h2_ref[k * n:(k + 1) * n, :]
        slot_t = slot_t_ref[0, k * N_EXPERTS:(k + 1) * N_EXPERTS, :]
        for e0 in range(0, N_EXPERTS, per):
            onehot = jnp.concatenate(
                [jnp.where(slot_t[e:e + 1, :] == slot_iota, 1.0, 0.0) for e in range(e0, e0 + per)],
                axis=0).astype(BF16)
            rows = _dot(onehot, h2).astype(BF16)
            for j in range(per):
                xe_o[e0 + j, k * cap:(k + 1) * cap, :] = rows[j * cap:(j + 1) * cap]


def _route(aff, h2, n_sets, set_rows):
    t, d = h2.shape
    n = set_rows
    cap = CAPACITY_FACTOR * n // N_EXPERTS
    assert cap & (cap - 1) == 0 and cap % 16 == 0 and DISPATCH_ROWS % cap == 0
    assert cap <= SLOT_RADIX, "slot ids must stay exact in bf16 and below the radix of the slot/gate packing"
    n_packs = -(-n_sets // SETS_PER_PACK)
    aff16 = aff[:, :N_EXPERTS].reshape(n_sets, n, N_EXPERTS)
    aff16 = jnp.pad(aff16, ((0, n_packs * SETS_PER_PACK - n_sets), (0, 0), (0, 0)))
    packed = aff16.reshape(n_packs, SETS_PER_PACK, n, N_EXPERTS).transpose(0, 2, 1, 3).reshape(n_packs, n, LANES)
    pack_blk = lambda p: (p, 0, 0)
    slot, slot_t, affb = pl.pallas_call(
        functools.partial(_select_kernel, cap=cap),
        grid=(n_packs,),
        in_specs=[pl.BlockSpec((1, n, LANES), pack_blk)],
        out_specs=[pl.BlockSpec((1, n, LANES), pack_blk), pl.BlockSpec((1, LANES, n), pack_blk),
                   pl.BlockSpec((1, n, LANES), pack_blk)],
        out_shape=[jax.ShapeDtypeStruct((n_packs, n, LANES), BF16),
                   jax.ShapeDtypeStruct((n_packs, LANES, n), F32),
                   jax.ShapeDtypeStruct((n_packs, n, LANES), BF16)],
        name=f"select_cap{cap}",
        compiler_params=_cparams(1),
    )(packed)
    g = _sets_per_step(n_sets, n)
    per_pack = SETS_PER_PACK // g
    xe = pl.pallas_call(
        functools.partial(_dispatch_kernel, cap=cap),
        grid=(n_sets // g,),
        in_specs=[pl.BlockSpec((1, g * N_EXPERTS, n), lambda s: (s // per_pack, s % per_pack, 0)),
                  pl.BlockSpec((g * n, d), lambda s: (s, 0))],
        out_specs=pl.BlockSpec((N_EXPERTS, g * cap, d), lambda s: (0, s, 0)),
        out_shape=jax.ShapeDtypeStruct((N_EXPERTS, n_sets * cap, d), BF16),
        name=f"dispatch_cap{cap}",
        compiler_params=_cparams(1),
    )(slot_t, h2)
    return xe, (slot, affb)


def _expert_kernel(*refs, n_groups):
    x_refs = refs[:n_groups]
    wg_ref, wu_ref, wd_ref = refs[n_groups:n_groups + 3]
    o_refs = refs[n_groups + 3:2 * n_groups + 3]
    wg_s, wu_s, wd_s = refs[2 * n_groups + 3:]
    wg_s[...] = wg_ref[0].astype(BF16)
    wu_s[...] = wu_ref[0].astype(BF16)
    wd_s[...] = wd_ref[0].astype(BF16)
    sub = EXPERT_ROWS
    for x_ref, o_ref in zip(x_refs, o_refs):
        for r0 in range(0, x_ref.shape[1], sub):
            x = x_ref[0, r0:r0 + sub, :]
            a = _dot(x, wg_s[...])
            u = _dot(x, wu_s[...])
            hmid = (a * jax.nn.sigmoid(a) * u).astype(BF16)
            o_ref[0, r0:r0 + sub, :] = _dot(hmid, wd_s[...]).astype(BF16)


def _experts(xes, w_gate, w_up, w_down):
    e, d, f = w_gate.shape
    blk = lambda i: (i, 0, 0)
    return pl.pallas_call(
        functools.partial(_expert_kernel, n_groups=len(xes)),
        grid=(e,),
        in_specs=[pl.BlockSpec((1, x.shape[1], d), blk) for x in xes]
        + [pl.BlockSpec((1, d, f), blk), pl.BlockSpec((1, d, f), blk), pl.BlockSpec((1, f, d), blk)],
        out_specs=[pl.BlockSpec((1, x.shape[1], d), blk) for x in xes],
        out_shape=[jax.ShapeDtypeStruct(x.shape, BF16) for x in xes],
        scratch_shapes=[pltpu.VMEM((d, f), BF16), pltpu.VMEM((d, f), BF16), pltpu.VMEM((f, d), BF16)],
        name="experts",
        compiler_params=_cparams(1),
    )(*xes, w_gate, w_up, w_down)


def _combine_kernel(slot_ref, affb_ref, ye_ref, x1_ref, mod_ref, gf_ref, y_o, *, cap):
    e, gcap, d = ye_ref.shape
    g = gcap // cap
    n = x1_ref.shape[0] // g
    shift = cap.bit_length() - 1
    both = jnp.concatenate([slot_ref[0], affb_ref[0]], axis=1)
    want = (lax.broadcasted_iota(jnp.int32, (n, DISPATCH_ROWS), 1) & (cap - 1)).astype(F32) * SLOT_RADIX
    for k in range(g):
        rows = slice(k * n, (k + 1) * n)
        first_lane = ((pl.program_id(0) * g + k) % SETS_PER_PACK) * N_EXPERTS
        pieces = []
        for c0 in range(0, e * cap, DISPATCH_ROWS):
            src = lax.broadcasted_iota(jnp.int32, (2 * LANES, DISPATCH_ROWS), 0)
            col = lax.broadcasted_iota(jnp.int32, (2 * LANES, DISPATCH_ROWS), 1) + c0
            lane_of_col = first_lane + lax.shift_right_logical(col, shift)
            spread = jnp.where(src == lane_of_col, SLOT_RADIX,
                               jnp.where(src == lane_of_col + LANES, 1.0, 0.0)).astype(BF16)
            rest = _dot(both, spread) - want
            pieces.append(jnp.where(rest >= 0.0, jnp.where(rest <= 1.0, rest, 0.0), 0.0).astype(BF16))
        gt = pieces[0] if len(pieces) == 1 else jnp.concatenate(pieces, axis=1)
        ye = ye_ref[:, k * cap:(k + 1) * cap, :].reshape(e * cap, d)
        moe = _dot(gt, ye)
        x = x1_ref[rows, :] + mod_ref[0][5:6] * moe
        y_o[rows, :] = _rms(x, gf_ref[...])


def _sets_per_step(n_sets, set_rows):
    g = SETS_PER_PACK if set_rows * SETS_PER_PACK <= SHARED_STEP_ROWS else 1
    return g if n_sets % g == 0 else 1


def _combine(packs, ye, x1, mod, n_sets, set_rows, g_final):
    slot, affb = packs
    t, d = x1.shape
    cap = ye.shape[1] // n_sets
    g = _sets_per_step(n_sets, set_rows) if mod.shape[0] == 1 else 1
    tm = min(ROW_TILE, set_rows) if g == 1 else g * set_rows
    tiles = max(1, set_rows // tm)
    per_pack = SETS_PER_PACK // g
    pack_rows = pl.BlockSpec((1, tm // g, LANES), lambda s, i: (s // per_pack, i, 0))
    return pl.pallas_call(
        functools.partial(_combine_kernel, cap=cap),
        grid=(n_sets // g, tiles),
        in_specs=[pack_rows, pack_rows,
                  pl.BlockSpec((N_EXPERTS, g * cap, d), lambda s, i: (0, s, 0)),
                  pl.BlockSpec((tm, d), lambda s, i: (s * tiles + i, 0)),
                  pl.BlockSpec((1, 8, d), (lambda s, i: (s, 0, 0)) if mod.shape[0] > 1 else (lambda s, i: (0, 0, 0))),
                  pl.BlockSpec((1, d), lambda s, i: (0, 0))],
        out_specs=pl.BlockSpec((tm, d), lambda s, i: (s * tiles + i, 0)),
        out_shape=jax.ShapeDtypeStruct((t, d), F32),
        name=f"combine_cap{cap}",
        compiler_params=_cparams(2),
    )(slot, affb, ye, x1, mod, g_final)


def _rope_tables(n_tokens):
    n_rows = n_tokens // GRID_W
    rowp = jnp.repeat(jnp.arange(n_rows), GRID_W).astype(F32)
    colp = jnp.tile(jnp.arange(GRID_W), n_rows).astype(F32)
    quarter = HEAD_DIM // 4
    freqs = ROPE_THETA ** (-jnp.arange(quarter, dtype=F32) / quarter)
    ang = jnp.stack([rowp[:, None] * freqs, colp[:, None] * freqs], axis=1)
    cos, sin = jnp.cos(ang), jnp.sin(ang)
    zero = jnp.zeros_like(sin)
    c = jnp.stack([cos, cos], axis=2).reshape(n_tokens, HEAD_DIM)
    s_up = jnp.stack([-sin, zero], axis=2).reshape(n_tokens, HEAD_DIM)
    s_dn = jnp.stack([zero, sin], axis=2).reshape(n_tokens, HEAD_DIM)
    return tuple(jnp.tile(t, (1, LANES // HEAD_DIM)) for t in (c, s_up, s_dn))


def _mix_and_route(x2, n_sets, set_rows, mod, lw, rope_tabs, cache, emit_cache, lam_init, tq):
    (g_attn, g_ffn, w_in, gq_t, gk_t, lq1, lk1, lq2, lk2, g_diff, w_ba, w_bb, w_out, w_router2) = lw
    outs = _inproj(x2, mod, set_rows, g_attn, w_in, gq_t, gk_t, rope_tabs, emit_cache)
    qa, kta, vta, qb, ktb, vb16 = outs[:6]
    vb16 = vb16.reshape(n_sets, set_rows, vb16.shape[1])
    ks_a, vs_a, ks_b, vs_b = [kta], [vta], [ktb], [vb16]
    if cache is not None:
        ckta, cvta, cktb, cvb = cache
        ks_a, vs_a, ks_b, vs_b = [ckta, kta], [cvta, vta], [cktb, ktb], [cvb, vb16]
    per_step = CONTEXT_SETS_PER_STEP if (tq == set_rows and n_sets % CONTEXT_SETS_PER_STEP == 0) else 1
    ya, yb = _both_mixers(qa, ks_a, vs_a, qb, ks_b, vs_b, (lq1, lk1, lq2, lk2, g_diff), set_rows, tq, per_step,
                          lam_init)
    x1, h2, aff = _post(x2, ya, yb, mod, set_rows, g_attn, g_ffn, w_in, w_ba, w_bb, w_out, w_router2)
    xe, packs = _route(aff, h2, n_sets, set_rows)
    return x1, xe, packs, outs[6:]


def kernel(x_prompt, x_sample, cache_attn_k, cache_attn_v, cache_diff_k, cache_diff_v, c, c_ctx, w_mod, b_mod,
           g_attn_norm, g_ffn_norm, w_in, g_q_norm, g_k_norm, lambda_q1, lambda_k1, lambda_q2, lambda_k2,
           g_diff_norm, w_branch_a, w_branch_b, w_out, w_router, w_exp_gate, w_exp_up, w_exp_down, g_final):
    batch, seq, d = x_prompt.shape
    dec_batch, dec_seq, _ = x_sample.shape
    depth = w_in.shape[0]
    assert depth == 1, "the final norm is fused into the layer's combine step"
    past = cache_attn_k.shape[2]
    assert w_in.shape[2] == N_QKV + 2 * d

    xp = x_prompt.reshape(batch * seq, d)
    xs = x_sample.reshape(dec_batch * dec_seq, d)
    rope_tabs = _rope_tables(dec_seq)
    c_rows = jnp.concatenate([c, c_ctx[None, :], jnp.zeros((16 - dec_batch - 1, d), F32)], axis=0)
    yp = ys = None
    caches = []
    for l in range(depth):
        lam_init = 0.8 - 0.6 * math.exp(-0.3 * l)
        mod6 = _modulation(c_rows, w_mod[l], b_mod[l])
        mod = jnp.pad(jnp.transpose(mod6, (1, 0, 2)), ((0, 0), (0, 2), (0, 0)))
        mod_lat, mod_ctx = mod[:dec_batch], mod[dec_batch:dec_batch + 1]
        w_in16, w_ba16, w_bb16, w_out16, w_router2 = _round_weights(
            [w_in[l], w_branch_a[l], w_branch_b[l], w_out[l]],
            jnp.pad(w_router[l], ((0, 0), (0, LANES - N_EXPERTS))))
        lw = (g_attn_norm[l][None, :], g_ffn_norm[l][None, :], w_in16,
              jnp.tile(g_q_norm[l], HEADS_A)[None, :], jnp.tile(g_k_norm[l], KV_HEADS_A)[None, :],
              lambda_q1[l], lambda_k1[l], lambda_q2[l], lambda_k2[l], g_diff_norm[l][None, :],
              w_ba16, w_bb16, w_out16, w_router2)
        x1p, xe_p, packs_p, cache_out = _mix_and_route(xp, batch, seq, mod_ctx, lw, None, None, True, lam_init, seq)
        caches.append(cache_out)
        feat_major = lambda a: jnp.moveaxis(a[:, l], 1, -1).reshape(dec_batch, -1, past)
        cache_l = _cache_prep([feat_major(cache_attn_k), feat_major(cache_attn_v), feat_major(cache_diff_k)],
                              cache_diff_v[:, l].reshape(dec_batch, past * HEADS_B, -1))
        x1s, xe_s, packs_s, _ = _mix_and_route(xs, dec_batch, dec_seq, mod_lat, lw, rope_tabs, cache_l, False,
                                               lam_init, LATENT_Q_TILE)
        ye_p, ye_s = _experts([xe_p, xe_s], w_exp_gate[l], w_exp_up[l], w_exp_down[l])
        yp = _combine(packs_p, ye_p, x1p, mod_ctx, batch, seq, g_final[None, :])
        ys = _combine(packs_s, ye_s, x1s, mod_lat, dec_batch, dec_seq, g_final[None, :])
    y_prompt = yp.reshape(batch, seq, d)
    y_sample = ys.reshape(dec_batch, dec_seq, d)
    tok_major = lambda a, dims: jnp.moveaxis(a.reshape((batch,) + dims + (seq,)), -1, 1)
    new_attn_k = jnp.stack([tok_major(cc[0], (KV_HEADS_A, HEAD_DIM)) for cc in caches], axis=1)
    new_attn_v = jnp.stack([tok_major(cc[1], (KV_HEADS_A, HEAD_DIM)) for cc in caches], axis=1)
    new_diff_k = jnp.stack([tok_major(cc[2], (HEADS_B, 2, HEAD_DIM)) for cc in caches], axis=1)
    new_diff_v = jnp.stack([cc[3].reshape(batch, seq, HEADS_B, 2 * HEAD_DIM) for cc in caches], axis=1)
    return (y_prompt, y_sample, new_attn_k, new_attn_v, new_diff_k, new_diff_v)
```

```python
import functools
import math

import numpy as np
import jax
import jax.numpy as jnp
from jax import lax
from jax.experimental import pallas as pl
from jax.experimental.pallas import tpu as pltpu

F32 = jnp.float32
BF16 = jnp.bfloat16

HEAD_DIM = 64
HEADS_A = 8
KV_HEADS_A = 2
HEADS_B = 4
N_QKV = HEADS_A * HEAD_DIM + 2 * KV_HEADS_A * HEAD_DIM + 3 * HEADS_B * 2 * HEAD_DIM
N_EXPERTS = 16
CAPACITY_FACTOR = 2
GRID_W = 64
ROPE_THETA = 10000.0
EPS = 1e-6
LANES = 128
ROW_TILE = 1024
POST_ROWS = 1024
CHAIN_ROWS = 512
EXPERT_ROWS = 256
LATENT_Q_TILE = 512
CONTEXT_SETS_PER_STEP = 4
SETS_PER_PACK = LANES // N_EXPERTS
SHARED_STEP_ROWS = 2048
DISPATCH_ROWS = 512
NEG_BIG = -1e30
NOT_SELECTED = -1.0
SLOT_RADIX = 256.0
VMEM_LIMIT = 56 * 1024 * 1024
ROUND_STEPS = 8


def _cparams(n_axes):
    return pltpu.CompilerParams(dimension_semantics=("arbitrary",) * n_axes,
                                vmem_limit_bytes=VMEM_LIMIT)


def _dot(a, b):
    return jnp.dot(a, b, preferred_element_type=F32)


def _dot_nt(a, b):
    return lax.dot_general(a, b, (((1,), (1,)), ((), ())), preferred_element_type=F32)


def _split(a):
    hi = a.astype(BF16)
    lo = (a - hi.astype(F32)).astype(BF16)
    return hi, lo


def _dot3(a, b):
    m = a.shape[0]
    a_hi, a_lo = _split(a)
    b_hi, b_lo = _split(b)
    both = _dot(jnp.concatenate([a_hi, a_lo], axis=0), b_hi)
    return both[0:m] + both[m:2 * m] + _dot(a_hi, b_lo)


def _rms(x, g):
    return x * lax.rsqrt(jnp.mean(x * x, axis=-1, keepdims=True) + EPS) * g


def _mod_index(mod, tiles_per_set):
    if mod.shape[0] == 1:
        return lambda i: (0, 0, 0)
    return lambda i: (i // tiles_per_set, 0, 0)


def _mod_kernel(c_ref, w_ref, b_ref, o_ref):
    c = c_ref[...]
    a = c * jax.nn.sigmoid(c)
    o_ref[0] = _dot3(a, w_ref[...]) + b_ref[0]


def _modulation(c_rows, w_mod, b_mod):
    r, d = c_rows.shape
    return pl.pallas_call(
        _mod_kernel,
        grid=(6,),
        in_specs=[pl.BlockSpec((r, d), lambda j: (0, 0)),
                  pl.BlockSpec((d, d), lambda j: (0, j)),
                  pl.BlockSpec((1, 1, d), lambda j: (j, 0, 0))],
        out_specs=pl.BlockSpec((1, r, d), lambda j: (j, 0, 0)),
        out_shape=jax.ShapeDtypeStruct((6, r, d), F32),
        name="mod",
        compiler_params=_cparams(1),
    )(c_rows, w_mod, b_mod.reshape(6, 1, d))


def _round_weights_kernel(*refs):
    n = len(refs) // 2
    for src, out in zip(refs[:n - 1], refs[n:-1]):
        out[...] = src[...].astype(BF16)
    hi, lo = _split(refs[n - 1][...])
    refs[-1][...] = jnp.concatenate([hi, lo], axis=1)


def _round_weights(weights, w_router_p):
    steps = ROUND_STEPS
    arrays = list(weights) + [w_router_p]
    blk = lambda a, cols: pl.BlockSpec((a.shape[0] // steps, cols), lambda i: (i, 0))
    out_cols = [a.shape[1] for a in weights] + [2 * w_router_p.shape[1]]
    return pl.pallas_call(
        _round_weights_kernel,
        grid=(steps,),
        in_specs=[blk(a, a.shape[1]) for a in arrays],
        out_specs=[blk(a, c) for a, c in zip(arrays, out_cols)],
        out_shape=[jax.ShapeDtypeStruct((a.shape[0], c), BF16) for a, c in zip(arrays, out_cols)],
        name="round_weights",
        compiler_params=_cparams(1),
    )(*arrays)


def _seg_sumsq(x, ones_blockdiag):
    return _dot((x * x).astype(BF16), ones_blockdiag)


def _rope(x, c, s_up, s_dn):
    w = x.shape[1]
    reps = w // c.shape[1]
    if reps > 1:
        c, s_up, s_dn = (jnp.concatenate([t] * reps, axis=1) for t in (c, s_up, s_dn))
    return x * c + pltpu.roll(x, w - 16, 1) * s_up + pltpu.roll(x, 16, 1) * s_dn


def _inproj_kernel(*refs, rope, emit_cache):
    (x_ref, mod_ref, g_ref, w_ref, gq_ref, gk_ref, bd512_ref, bd128_ref), refs = refs[:8], refs[8:]
    if rope:
        (c_ref, su_ref, sd_ref), refs = refs[:3], refs[3:]
    qa_o, kta_o, vta_o, qb_o, ktb_o, vb_o = refs[:6]
    if emit_cache:
        ka_c, va_c, kb_c, vb_c = refs[6:10]

    mod = mod_ref[0]
    wa = HEADS_A * HEAD_DIM
    wkv = KV_HEADS_A * HEAD_DIM
    wb = HEADS_B * 2 * HEAD_DIM
    o_ka, o_va, o_qb = wa, wa + wkv, wa + 2 * wkv
    o_kb, o_vb = o_qb + wb, o_qb + 2 * wb
    scale = HEAD_DIM ** -0.5 * math.log2(math.e)

    x = x_ref[...]
    h = (_rms(x, g_ref[...]) * (1.0 + mod[1:2]) + mod[0:1]).astype(BF16)
    if rope:
        tabs = (c_ref[...], su_ref[...], sd_ref[...])

    qa = _dot(h, w_ref[:, 0:wa])
    qa = qa * lax.rsqrt(_seg_sumsq(qa, bd512_ref[...]) * (1.0 / HEAD_DIM) + EPS) * gq_ref[...]
    if rope:
        qa = _rope(qa, *tabs)
    qa_o[...] = (qa * scale).astype(BF16)

    kv = _dot(h, w_ref[:, o_ka:o_qb])
    ka, va = kv[:, 0:wkv], kv[:, wkv:2 * wkv]
    ka = ka * lax.rsqrt(_seg_sumsq(ka, bd128_ref[...]) * (1.0 / HEAD_DIM) + EPS) * gk_ref[...]

    def put_feat(val, out_bf16, out_f32):
        rows = out_bf16.shape[2]
        for s in range(out_bf16.shape[0]):
            t = val[s * rows:(s + 1) * rows].T
            if out_f32 is not None:
                out_f32[s] = t
            out_bf16[s] = t.astype(BF16)

    if rope:
        put_feat(_rope(ka, *tabs), kta_o, None)
    else:
        put_feat(ka, kta_o, ka_c if emit_cache else None)
    put_feat(va, vta_o, va_c if emit_cache else None)

    qb = _dot(h, w_ref[:, o_qb:o_kb])
    if rope:
        qb = _rope(qb, *tabs)
    qb_o[...] = (qb * scale).astype(BF16)

    kb = _dot(h, w_ref[:, o_kb:o_vb])
    if rope:
        put_feat(_rope(kb, *tabs), ktb_o, None)
    else:
        put_feat(kb, ktb_o, kb_c if emit_cache else None)

    vb = _dot(h, w_ref[:, o_vb:o_vb + wb])
    if emit_cache:
        for hd in range(HEADS_B):
            vb_c[pl.ds(hd, vb.shape[0], stride=HEADS_B), :] = vb[:, hd * 128:(hd + 1) * 128]
    vb_o[...] = vb.astype(BF16)


def _blockdiag_ones(width):
    g = np.arange(width) // HEAD_DIM
    return jnp.asarray((g[:, None] == g[None, :]).astype(np.float32), dtype=BF16)


def _inproj(x2, mod, set_rows, g_attn, w_in, gq_t, gk_t, rope_tabs, emit_cache):
    t, d = x2.shape
    tm = ROW_TILE
    assert t % tm == 0 and (tm % set_rows == 0 or set_rows % tm == 0)
    tiles_per_set = max(1, set_rows // tm)
    sets_per_tile = max(1, tm // set_rows)
    rope = rope_tabs is not None
    assert not (rope and emit_cache), "cached keys are the position-free ones"
    nq = N_QKV
    row = lambda i: (i, 0)
    const = lambda i: (0, 0)
    in_specs = [pl.BlockSpec((tm, d), row),
                pl.BlockSpec((1, 8, d), _mod_index(mod, tiles_per_set)),
                pl.BlockSpec((1, d), const),
                pl.BlockSpec((d, nq), const, pipeline_mode=pl.Buffered(1)),
                pl.BlockSpec((1, 512), const),
                pl.BlockSpec((1, 128), const),
                pl.BlockSpec((512, 512), const),
                pl.BlockSpec((128, 128), const)]
    args = [x2, mod, g_attn, w_in, gq_t, gk_t, _blockdiag_ones(512), _blockdiag_ones(128)]
    if rope:
        in_specs += [pl.BlockSpec((tm, LANES), lambda i: (i % tiles_per_set, 0))] * 3
        args += list(rope_tabs)
    n_sets = t // set_rows
    wkv, wb = KV_HEADS_A * HEAD_DIM, HEADS_B * 2 * HEAD_DIM
    outs = [("tok", 512, BF16), ("feat", wkv, BF16), ("feat", wkv, BF16),
            ("tok", 512, BF16), ("feat", wb, BF16), ("tok", wb, BF16)]
    if emit_cache:
        outs += [("feat", wkv, F32), ("feat", wkv, F32), ("feat", wb, F32), ("tokhead", wb, F32)]
    feat = lambda i: (i // tiles_per_set, 0, i % tiles_per_set)
    feat_rows = min(tm, set_rows)

    def out_block(kind, w):
        if kind == "tok":
            return pl.BlockSpec((tm, w), row)
        if kind == "tokhead":
            return pl.BlockSpec((tm * HEADS_B, w // HEADS_B), row)
        return pl.BlockSpec((sets_per_tile, w, feat_rows), feat)

    def out_array(kind, w, dt):
        shape = {"tok": (t, w), "tokhead": (t * HEADS_B, w // HEADS_B), "feat": (n_sets, w, set_rows)}[kind]
        return jax.ShapeDtypeStruct(shape, dt)

    return pl.pallas_call(
        functools.partial(_inproj_kernel, rope=rope, emit_cache=emit_cache),
        grid=(t // tm,),
        in_specs=in_specs,
        out_specs=[out_block(kind, w) for kind, w, _ in outs],
        out_shape=[out_array(kind, w, dt) for kind, w, dt in outs],
        name="inproj_rope" if rope else "inproj",
        compiler_params=_cparams(1),
    )(*args)


def _cache_prep_kernel(*refs):
    half = len(refs) // 2
    for src, out in zip(refs[:half - 1], refs[half:-1]):
        out[...] = src[...].astype(BF16)
    v_src, v_out = refs[half - 1], refs[-1]
    keys = v_out.shape[1]
    for hd in range(HEADS_B):
        v_out[0, :, hd * 128:(hd + 1) * 128] = v_src[0, pl.ds(hd, keys, stride=HEADS_B), :].astype(BF16)


def _cache_prep(arrays, v_by_head):
    blk = lambda b: (b, 0, 0)
    n_sets, rows, width = v_by_head.shape
    v_shape = (n_sets, rows // HEADS_B, width * HEADS_B)
    return pl.pallas_call(
        _cache_prep_kernel,
        grid=(n_sets,),
        in_specs=[pl.BlockSpec((1,) + a.shape[1:], blk) for a in arrays + [v_by_head]],
        out_specs=[pl.BlockSpec((1,) + a.shape[1:], blk) for a in arrays] + [pl.BlockSpec((1,) + v_shape[1:], blk)],
        out_shape=[jax.ShapeDtypeStruct(a.shape, BF16) for a in arrays] + [jax.ShapeDtypeStruct(v_shape, BF16)],
        name="cache_prep",
        compiler_params=_cparams(1),
    )(*arrays, v_by_head)


def _exp_parts(scores):
    m = scores[0].max(axis=-1, keepdims=True)
    for s in scores[1:]:
        m = jnp.maximum(m, s.max(axis=-1, keepdims=True))
    return [jnp.exp2(s - m) for s in scores]


def _row_sum(parts):
    l = parts[0].sum(axis=-1, keepdims=True)
    for e in parts[1:]:
        l = l + e.sum(axis=-1, keepdims=True)
    return l


def _pad_rows(x, first, ones_row=False):
    if ones_row:
        z = jnp.where(lax.broadcasted_iota(jnp.int32, x.shape, 0) == 0, 1.0, 0.0).astype(x.dtype)
    else:
        z = jnp.zeros_like(x)
    return jnp.concatenate([x, z] if first else [z, x], axis=0)


def _gqa_kernel(*refs, n_src, mxu_sums):
    q_ref = refs[0]
    k_refs = refs[1:1 + n_src]
    v_refs = refs[1 + n_src:1 + 2 * n_src]
    o_ref = refs[1 + 2 * n_src]
    sets_here = k_refs[0].shape[0]
    tq = q_ref.shape[0] // sets_here
    lane = lax.broadcasted_iota(jnp.int32, (2 * tq, LANES), 1)
    lo_half = lane < HEAD_DIM
    for s in range(sets_here):
        rows = slice(s * tq, (s + 1) * tq)
        for g in range(KV_HEADS_A):
            c0 = g * 256
            f0 = g * HEAD_DIM
            q = jnp.concatenate([q_ref[rows, c0:c0 + 128], q_ref[rows, c0 + 128:c0 + 256]], axis=0)
            kts = [k[s, f0:f0 + HEAD_DIM, :] for k in k_refs]
            vts = [v[s, f0:f0 + HEAD_DIM, :] for v in v_refs]
            e_lo = _exp_parts([_dot(q, _pad_rows(kt, True)) for kt in kts])
            e_hi = _exp_parts([_dot(q, _pad_rows(kt, False)) for kt in kts])
            o_lo = o_hi = None
            for e_part, vt in zip(e_lo, vts):
                t = _dot_nt(e_part.astype(BF16), _pad_rows(vt, True, ones_row=mxu_sums))
                o_lo = t if o_lo is None else o_lo + t
            for e_part, vt in zip(e_hi, vts):
                t = _dot_nt(e_part.astype(BF16), _pad_rows(vt, False, ones_row=mxu_sums))
                o_hi = t if o_hi is None else o_hi + t
            if mxu_sums:
                l_lo, l_hi = o_lo[:, HEAD_DIM:HEAD_DIM + 1], o_hi[:, 0:1]
                o = jnp.where(lo_half, o_lo * (1.0 / l_lo), o_hi * (1.0 / l_hi))
            else:
                o = (o_lo + o_hi) * jnp.where(lo_half, 1.0 / _row_sum(e_lo), 1.0 / _row_sum(e_hi))
            o_ref[rows, c0:c0 + 128] = o[0:tq].astype(BF16)
            o_ref[rows, c0 + 128:c0 + 256] = o[tq:2 * tq].astype(BF16)


def _diff_kernel(*refs, n_src, lam_init, stack_maps):
    q_ref = refs[0]
    k_refs = refs[1:1 + n_src]
    v_refs = refs[1 + n_src:1 + 2 * n_src]
    lq1, lk1, lq2, lk2, gd_ref, o_ref = refs[1 + 2 * n_src:]
    lam_all = (jnp.exp(jnp.sum(lq1[...] * lk1[...], axis=-1, keepdims=True))
               - jnp.exp(jnp.sum(lq2[...] * lk2[...], axis=-1, keepdims=True)) + lam_init)
    sets_here = k_refs[0].shape[0]
    tq = q_ref.shape[0] // sets_here
    for s in range(sets_here):
        rows = slice(s * tq, (s + 1) * tq)
        for j in range(HEADS_B):
            lam = lam_all[j:j + 1, :]
            q = q_ref[rows, j * 128:(j + 1) * 128]
            k0s = [_pad_rows(k[s, j * 128:j * 128 + HEAD_DIM, :], True) for k in k_refs]
            k1s = [_pad_rows(k[s, j * 128 + HEAD_DIM:(j + 1) * 128, :], False) for k in k_refs]
            e0 = _exp_parts([_dot(q, k0) for k0 in k0s])
            e1 = _exp_parts([_dot(q, k1) for k1 in k1s])
            o0 = o1 = None
            for a0, a1, v in zip(e0, e1, v_refs):
                val = v[s, :, j * 128:(j + 1) * 128]
                if stack_maps:
                    t = _dot(jnp.concatenate([a0.astype(BF16), a1.astype(BF16)], axis=0), val)
                    t0, t1 = t[0:tq], t[tq:2 * tq]
                else:
                    t0, t1 = _dot(a0.astype(BF16), val), _dot(a1.astype(BF16), val)
                o0 = t0 if o0 is None else o0 + t0
                o1 = t1 if o1 is None else o1 + t1
            o = o0 * (1.0 / _row_sum(e0)) - o1 * (lam / _row_sum(e1))
            o = _rms(o, gd_ref[...]) * (1.0 - lam_init)
            o_ref[rows, j * 128:(j + 1) * 128] = o.astype(BF16)


def _both_mixers_kernel(*refs, n_src, lam_init, stack_maps, mxu_sums):
    per = 1 + 2 * n_src
    ya_o, yb_o = refs[-2:]
    _gqa_kernel(*refs[:per], ya_o, n_src=n_src, mxu_sums=mxu_sums)
    _diff_kernel(*refs[per:2 * per + 5], yb_o, n_src=n_src, lam_init=lam_init, stack_maps=stack_maps)


def _both_mixers(qa, ks_a, vs_a, qb, ks_b, vs_b, lam_refs, set_rows, tq, sets_per_step, lam_init):
    t = qa.shape[0]
    n_sets = t // set_rows
    q_tiles = set_rows // tq
    assert sets_per_step == 1 or q_tiles == 1
    rows = sets_per_step * tq
    tile = lambda b, i: (b * q_tiles + i, 0)
    kv = lambda a: pl.BlockSpec((sets_per_step,) + a.shape[1:], lambda b, i: (b, 0, 0))
    in_specs = ([pl.BlockSpec((rows, qa.shape[1]), tile)] + [kv(a) for a in list(ks_a) + list(vs_a)]
                + [pl.BlockSpec((rows, qb.shape[1]), tile)] + [kv(a) for a in list(ks_b) + list(vs_b)]
                + [pl.BlockSpec(e.shape, lambda b, i: (0, 0)) for e in lam_refs])
    return pl.pallas_call(
        functools.partial(_both_mixers_kernel, n_src=len(ks_a), lam_init=lam_init,
                          stack_maps=len(ks_a) == 1, mxu_sums=len(ks_a) > 1),
        grid=(n_sets // sets_per_step, q_tiles),
        in_specs=in_specs,
        out_specs=[pl.BlockSpec((rows, qa.shape[1]), tile), pl.BlockSpec((rows, qb.shape[1]), tile)],
        out_shape=[jax.ShapeDtypeStruct(qa.shape, BF16), jax.ShapeDtypeStruct(qb.shape, BF16)],
        name=f"mixers_{len(ks_a)}src",
        compiler_params=_cparams(2),
    )(qa, *ks_a, *vs_a, qb, *ks_b, *vs_b, *lam_refs)


def _post_kernel(x_ref, ya_ref, yb_ref, mod_ref, g1_ref, g2_ref, win_ref, wba_ref, wbb_ref, wo_ref, wr2_ref,
                 x1_o, h2_o, aff_o):
    d = x_ref.shape[1]
    g0 = win_ref.shape[1] - 2 * d
    mod = mod_ref[0]
    for r0 in range(0, x_ref.shape[0], CHAIN_ROWS):
        rows = slice(r0, r0 + CHAIN_ROWS)
        x = x_ref[rows, :]
        h = (_rms(x, g1_ref[...]) * (1.0 + mod[1:2]) + mod[0:1]).astype(BF16)
        ga = jax.nn.sigmoid(_dot(h, win_ref[:, g0:g0 + d]))
        merged = ga * _dot(ya_ref[rows, :], wba_ref[...])
        gb = jax.nn.sigmoid(_dot(h, win_ref[:, g0 + d:g0 + 2 * d]))
        merged = merged + gb * _dot(yb_ref[rows, :], wbb_ref[...])
        m = _dot(merged.astype(BF16), wo_ref[...])
        x1 = x + mod[2:3] * m
        x1_o[rows, :] = x1
        h2 = _rms(x1, g2_ref[...]) * (1.0 + mod[4:5]) + mod[3:4]
        h2_o[rows, :] = h2.astype(BF16)
        h2_hi, h2_lo = _split(h2)
        both = _dot(h2_hi, wr2_ref[...])
        logits = both[:, 0:LANES] + both[:, LANES:2 * LANES] + _dot(h2_lo, wr2_ref[:, 0:LANES])
        lane = lax.broadcasted_iota(jnp.int32, logits.shape, 1)
        logits = jnp.where(lane < N_EXPERTS, logits, NEG_BIG)
        e = jnp.exp(logits - logits.max(axis=-1, keepdims=True))
        aff_o[rows, :] = e / e.sum(axis=-1, keepdims=True)


def _post(x2, ya, yb, mod, set_rows, g1, g2, w_in, w_ba, w_bb, w_out, w_router2):
    t, d = x2.shape
    tm = POST_ROWS
    tiles_per_set = max(1, set_rows // tm)
    assert t % tm == 0 and tm % CHAIN_ROWS == 0 and (mod.shape[0] == 1 or set_rows % tm == 0)
    row = lambda i: (i, 0)
    const = lambda i: (0, 0)
    once = pl.Buffered(1)
    half = w_in.shape[1] // 2
    assert w_in.shape[1] == 2 * half and half % LANES == 0 and half >= 2 * d
    return pl.pallas_call(
        _post_kernel,
        grid=(t // tm,),
        in_specs=[pl.BlockSpec((tm, d), row),
                  pl.BlockSpec((tm, 512), row),
                  pl.BlockSpec((tm, 512), row),
                  pl.BlockSpec((1, 8, d), _mod_index(mod, tiles_per_set)),
                  pl.BlockSpec((1, d), const),
                  pl.BlockSpec((1, d), const),
                  pl.BlockSpec((d, half), lambda i: (0, 1), pipeline_mode=once),
                  pl.BlockSpec(w_ba.shape, const, pipeline_mode=once),
                  pl.BlockSpec(w_bb.shape, const, pipeline_mode=once),
                  pl.BlockSpec(w_out.shape, const, pipeline_mode=once),
                  pl.BlockSpec(w_router2.shape, const, pipeline_mode=once)],
        out_specs=[pl.BlockSpec((tm, d), row), pl.BlockSpec((tm, d), row), pl.BlockSpec((tm, LANES), row)],
        out_shape=[jax.ShapeDtypeStruct((t, d), F32), jax.ShapeDtypeStruct((t, d), BF16),
                   jax.ShapeDtypeStruct((t, LANES), F32)],
        name="post_attn",
        compiler_params=_cparams(1),
    )(x2, ya, yb, mod, g1, g2, w_in, w_ba, w_bb, w_out, w_router2)


def _select_kernel(aff_ref, slot_o, slot_t_o, affb_o, *, cap):
    aff = aff_ref[0]
    n = aff.shape[0]
    capf = float(cap)

    def enough(cand):
        return jnp.sum(jnp.where(aff >= cand, 1.0, 0.0), axis=0, keepdims=True) >= capf

    pw = jnp.ones((1, LANES), F32)
    for k in (64, 32, 16, 8, 4, 2, 1):
        pw = jnp.where(enough(pw * 2.0 ** -(k - 1)), pw, pw * 2.0 ** -k)

    def mantissa_step(_, carry):
        thr, step = carry
        cand = thr + step
        return jnp.where(enough(cand), cand, thr), step * 0.5

    thr, _ = lax.fori_loop(0, 23, mantissa_step, (pw, pw * 0.5))
    above = aff > thr
    tied = aff == thr
    need = capf - jnp.sum(jnp.where(above, 1.0, 0.0), axis=0, keepdims=True)
    r_i = lax.broadcasted_iota(jnp.int32, (n, n), 0)
    c_i = lax.broadcasted_iota(jnp.int32, (n, n), 1)
    before = jnp.where(c_i < r_i, 1.0, 0.0).astype(BF16)
    tie_rank = _dot(before, jnp.where(tied, 1.0, 0.0).astype(BF16))
    sel = above | (tied & (tie_rank < need))
    slot = _dot(before, jnp.where(sel, 1.0, 0.0).astype(BF16))
    slot = jnp.where(sel, slot, NOT_SELECTED)
    slot_o[0] = slot.astype(BF16)
    slot_t_o[0] = slot.T
    affb_o[0] = aff.astype(BF16)


def _dispatch_kernel(slot_t_ref, h2_ref, xe_o, *, cap):
    n = slot_t_ref.shape[2]
    g = h2_ref.shape[0] // n
    per = min(N_EXPERTS, DISPATCH_ROWS // cap)
    slot_iota = lax.broadcasted_iota(jnp.int32, (cap, n), 0).astype(F32)
    for k in range(g):
        h2 = h2_ref[k * n:(k + 1) * n, :]
        slot_t = slot_t_ref[0, k * N_EXPERTS:(k + 1) * N_EXPERTS, :]
        for e0 in range(0, N_EXPERTS, per):
            onehot = jnp.concatenate(
                [jnp.where(slot_t[e:e + 1, :] == slot_iota, 1.0, 0.0) for e in range(e0, e0 + per)],
                axis=0).astype(BF16)
            rows = _dot(onehot, h2).astype(BF16)
            for j in range(per):
                xe_o[e0 + j, k * cap:(k + 1) * cap, :] = rows[j * cap:(j + 1) * cap]


def _route(aff, h2, n_sets, set_rows):
    t, d = h2.shape
    n = set_rows
    cap = CAPACITY_FACTOR * n // N_EXPERTS
    assert cap & (cap - 1) == 0 and cap % 16 == 0 and DISPATCH_ROWS % cap == 0
    assert cap <= SLOT_RADIX, "slot ids must stay exact in bf16 and below the radix of the slot/gate packing"
    n_packs = -(-n_sets // SETS_PER_PACK)
    aff16 = aff[:, :N_EXPERTS].reshape(n_sets, n, N_EXPERTS)
    aff16 = jnp.pad(aff16, ((0, n_packs * SETS_PER_PACK - n_sets), (0, 0), (0, 0)))
    packed = aff16.reshape(n_packs, SETS_PER_PACK, n, N_EXPERTS).transpose(0, 2, 1, 3).reshape(n_packs, n, LANES)
    pack_blk = lambda p: (p, 0, 0)
    slot, slot_t, affb = pl.pallas_call(
        functools.partial(_select_kernel, cap=cap),
        grid=(n_packs,),
        in_specs=[pl.BlockSpec((1, n, LANES), pack_blk)],
        out_specs=[pl.BlockSpec((1, n, LANES), pack_blk), pl.BlockSpec((1, LANES, n), pack_blk),
                   pl.BlockSpec((1, n, LANES), pack_blk)],
        out_shape=[jax.ShapeDtypeStruct((n_packs, n, LANES), BF16),
                   jax.ShapeDtypeStruct((n_packs, LANES, n), F32),
                   jax.ShapeDtypeStruct((n_packs, n, LANES), BF16)],
        name=f"select_cap{cap}",
        compiler_params=_cparams(1),
    )(packed)
    g = _sets_per_step(n_sets, n)
    per_pack = SETS_PER_PACK // g
    xe = pl.pallas_call(
        functools.partial(_dispatch_kernel, cap=cap),
        grid=(n_sets // g,),
        in_specs=[pl.BlockSpec((1, g * N_EXPERTS, n), lambda s: (s // per_pack, s % per_pack, 0)),
                  pl.BlockSpec((g * n, d), lambda s: (s, 0))],
        out_specs=pl.BlockSpec((N_EXPERTS, g * cap, d), lambda s: (0, s, 0)),
        out_shape=jax.ShapeDtypeStruct((N_EXPERTS, n_sets * cap, d), BF16),
        name=f"dispatch_cap{cap}",
        compiler_params=_cparams(1),
    )(slot_t, h2)
    return xe, (slot, affb)


def _expert_kernel(*refs, n_groups):
    x_refs = refs[:n_groups]
    wg_ref, wu_ref, wd_ref = refs[n_groups:n_groups + 3]
    o_refs = refs[n_groups + 3:2 * n_groups + 3]
    wg_s, wu_s, wd_s = refs[2 * n_groups + 3:]
    wg_s[...] = wg_ref[0].astype(BF16)
    wu_s[...] = wu_ref[0].astype(BF16)
    wd_s[...] = wd_ref[0].astype(BF16)
    sub = EXPERT_ROWS
    for x_ref, o_ref in zip(x_refs, o_refs):
        for r0 in range(0, x_ref.shape[1], sub):
            x = x_ref[0, r0:r0 + sub, :]
            a = _dot(x, wg_s[...])
            u = _dot(x, wu_s[...])
            hmid = (a * jax.nn.sigmoid(a) * u).astype(BF16)
            o_ref[0, r0:r0 + sub, :] = _dot(hmid, wd_s[...]).astype(BF16)


def _experts(xes, w_gate, w_up, w_down):
    e, d, f = w_gate.shape
    blk = lambda i: (i, 0, 0)
    return pl.pallas_call(
        functools.partial(_expert_kernel, n_groups=len(xes)),
        grid=(e,),
        in_specs=[pl.BlockSpec((1, x.shape[1], d), blk) for x in xes]
        + [pl.BlockSpec((1, d, f), blk), pl.BlockSpec((1, d, f), blk), pl.BlockSpec((1, f, d), blk)],
        out_specs=[pl.BlockSpec((1, x.shape[1], d), blk) for x in xes],
        out_shape=[jax.ShapeDtypeStruct(x.shape, BF16) for x in xes],
        scratch_shapes=[pltpu.VMEM((d, f), BF16), pltpu.VMEM((d, f), BF16), pltpu.VMEM((f, d), BF16)],
        name="experts",
        compiler_params=_cparams(1),
    )(*xes, w_gate, w_up, w_down)


def _combine_kernel(slot_ref, affb_ref, ye_ref, x1_ref, mod_ref, gf_ref, y_o, *, cap):
    e, gcap, d = ye_ref.shape
    g = gcap // cap
    n = x1_ref.shape[0] // g
    shift = cap.bit_length() - 1
    both = jnp.concatenate([slot_ref[0], affb_ref[0]], axis=1)
    want = (lax.broadcasted_iota(jnp.int32, (n, DISPATCH_ROWS), 1) & (cap - 1)).astype(F32) * SLOT_RADIX
    for k in range(g):
        rows = slice(k * n, (k + 1) * n)
        first_lane = ((pl.program_id(0) * g + k) % SETS_PER_PACK) * N_EXPERTS
        pieces = []
        for c0 in range(0, e * cap, DISPATCH_ROWS):
            src = lax.broadcasted_iota(jnp.int32, (2 * LANES, DISPATCH_ROWS), 0)
            col = lax.broadcasted_iota(jnp.int32, (2 * LANES, DISPATCH_ROWS), 1) + c0
            lane_of_col = first_lane + lax.shift_right_logical(col, shift)
            spread = jnp.where(src == lane_of_col, SLOT_RADIX,
                               jnp.where(src == lane_of_col + LANES, 1.0, 0.0)).astype(BF16)
            rest = _dot(both, spread) - want
            pieces.append(jnp.where(rest >= 0.0, jnp.where(rest <= 1.0, rest, 0.0), 0.0).astype(BF16))
        gt = pieces[0] if len(pieces) == 1 else jnp.concatenate(pieces, axis=1)
        ye = ye_ref[:, k * cap:(k + 1) * cap, :].reshape(e * cap, d)
        moe = _dot(gt, ye)
        x = x1_ref[rows, :] + mod_ref[0][5:6] * moe
        y_o[rows, :] = _rms(x, gf_ref[...])


def _sets_per_step(n_sets, set_rows):
    g = SETS_PER_PACK if set_rows * SETS_PER_PACK <= SHARED_STEP_ROWS else 1
    return g if n_sets % g == 0 else 1


def _combine(packs, ye, x1, mod, n_sets, set_rows, g_final):
    slot, affb = packs
    t, d = x1.shape
    cap = ye.shape[1] // n_sets
    g = _sets_per_step(n_sets, set_rows) if mod.shape[0] == 1 else 1
    tm = min(ROW_TILE, set_rows) if g == 1 else g * set_rows
    tiles = max(1, set_rows // tm)
    per_pack = SETS_PER_PACK // g
    pack_rows = pl.BlockSpec((1, tm // g, LANES), lambda s, i: (s // per_pack, i, 0))
    return pl.pallas_call(
        functools.partial(_combine_kernel, cap=cap),
        grid=(n_sets // g, tiles),
        in_specs=[pack_rows, pack_rows,
                  pl.BlockSpec((N_EXPERTS, g * cap, d), lambda s, i: (0, s, 0)),
                  pl.BlockSpec((tm, d), lambda s, i: (s * tiles + i, 0)),
                  pl.BlockSpec((1, 8, d), (lambda s, i: (s, 0, 0)) if mod.shape[0] > 1 else (lambda s, i: (0, 0, 0))),
                  pl.BlockSpec((1, d), lambda s, i: (0, 0))],
        out_specs=pl.BlockSpec((tm, d), lambda s, i: (s * tiles + i, 0)),
        out_shape=jax.ShapeDtypeStruct((t, d), F32),
        name=f"combine_cap{cap}",
        compiler_params=_cparams(2),
    )(slot, affb, ye, x1, mod, g_final)


def _rope_tables(n_tokens):
    n_rows = n_tokens // GRID_W
    rowp = np.repeat(np.arange(n_rows), GRID_W).astype(np.float32)
    colp = np.tile(np.arange(GRID_W), n_rows).astype(np.float32)
    quarter = HEAD_DIM // 4
    freqs = (np.float32(ROPE_THETA) ** (-np.arange(quarter, dtype=np.float32) / np.float32(quarter))).astype(np.float32)
    ang = np.stack([rowp[:, None] * freqs, colp[:, None] * freqs], axis=1)
    cos = np.cos(ang.astype(np.float64)).astype(np.float32)
    sin = np.sin(ang.astype(np.float64)).astype(np.float32)
    zero = np.zeros_like(sin)
    c = np.stack([cos, cos], axis=2).reshape(n_tokens, HEAD_DIM)
    s_up = np.stack([-sin, zero], axis=2).reshape(n_tokens, HEAD_DIM)
    s_dn = np.stack([zero, sin], axis=2).reshape(n_tokens, HEAD_DIM)
    return tuple(jnp.asarray(np.tile(t, (1, LANES // HEAD_DIM))) for t in (c, s_up, s_dn))


def _mix_and_route(x2, n_sets, set_rows, mod, lw, rope_tabs, cache, emit_cache, lam_init, tq):
    (g_attn, g_ffn, w_in, gq_t, gk_t, lq1, lk1, lq2, lk2, g_diff, w_ba, w_bb, w_out, w_router2) = lw
    outs = _inproj(x2, mod, set_rows, g_attn, w_in, gq_t, gk_t, rope_tabs, emit_cache)
    qa, kta, vta, qb, ktb, vb16 = outs[:6]
    vb16 = vb16.reshape(n_sets, set_rows, vb16.shape[1])
    ks_a, vs_a, ks_b, vs_b = [kta], [vta], [ktb], [vb16]
    if cache is not None:
        ckta, cvta, cktb, cvb = cache
        ks_a, vs_a, ks_b, vs_b = [ckta, kta], [cvta, vta], [cktb, ktb], [cvb, vb16]
    per_step = CONTEXT_SETS_PER_STEP if (tq == set_rows and n_sets % CONTEXT_SETS_PER_STEP == 0) else 1
    ya, yb = _both_mixers(qa, ks_a, vs_a, qb, ks_b, vs_b, (lq1, lk1, lq2, lk2, g_diff), set_rows, tq, per_step,
                          lam_init)
    x1, h2, aff = _post(x2, ya, yb, mod, set_rows, g_attn, g_ffn, w_in, w_ba, w_bb, w_out, w_router2)
    xe, packs = _route(aff, h2, n_sets, set_rows)
    return x1, xe, packs, outs[6:]


def kernel(x_prompt, x_sample, cache_attn_k, cache_attn_v, cache_diff_k, cache_diff_v, c, c_ctx, w_mod, b_mod,
           g_attn_norm, g_ffn_norm, w_in, g_q_norm, g_k_norm, lambda_q1, lambda_k1, lambda_q2, lambda_k2,
           g_diff_norm, w_branch_a, w_branch_b, w_out, w_router, w_exp_gate, w_exp_up, w_exp_down, g_final):
    batch, seq, d = x_prompt.shape
    dec_batch, dec_seq, _ = x_sample.shape
    depth = w_in.shape[0]
    assert depth == 1, "the final norm is fused into the layer's combine step"
    past = cache_attn_k.shape[2]
    assert w_in.shape[2] == N_QKV + 2 * d

    xp = x_prompt.reshape(batch * seq, d)
    xs = x_sample.reshape(dec_batch * dec_seq, d)
    rope_tabs = _rope_tables(dec_seq)
    c_rows = jnp.concatenate([c, c_ctx[None, :], jnp.zeros((16 - dec_batch - 1, d), F32)], axis=0)
    yp = ys = None
    caches = []
    for l in range(depth):
        lam_init = 0.8 - 0.6 * math.exp(-0.3 * l)
        mod6 = _modulation(c_rows, w_mod[l], b_mod[l])
        mod = jnp.pad(jnp.transpose(mod6, (1, 0, 2)), ((0, 0), (0, 2), (0, 0)))
        mod_lat, mod_ctx = mod[:dec_batch], mod[dec_batch:dec_batch + 1]
        w_in16, w_ba16, w_bb16, w_out16, w_router2 = _round_weights(
            [w_in[l], w_branch_a[l], w_branch_b[l], w_out[l]],
            jnp.pad(w_router[l], ((0, 0), (0, LANES - N_EXPERTS))))
        lw = (g_attn_norm[l][None, :], g_ffn_norm[l][None, :], w_in16,
              jnp.tile(g_q_norm[l], HEADS_A)[None, :], jnp.tile(g_k_norm[l], KV_HEADS_A)[None, :],
              lambda_q1[l], lambda_k1[l], lambda_q2[l], lambda_k2[l], g_diff_norm[l][None, :],
              w_ba16, w_bb16, w_out16, w_router2)
        x1p, xe_p, packs_p, cache_out = _mix_and_route(xp, batch, seq, mod_ctx, lw, None, None, True, lam_init, seq)
        caches.append(cache_out)
        feat_major = lambda a: jnp.moveaxis(a[:, l], 1, -1).reshape(dec_batch, -1, past)
        cache_l = _cache_prep([feat_major(cache_attn_k), feat_major(cache_attn_v), feat_major(cache_diff_k)],
                              cache_diff_v[:, l].reshape(dec_batch, past * HEADS_B, -1))
        x1s, xe_s, packs_s, _ = _mix_and_route(xs, dec_batch, dec_seq, mod_lat, lw, rope_tabs, cache_l, False,
                                               lam_init, LATENT_Q_TILE)
        ye_p, ye_s = _experts([xe_p, xe_s], w_exp_gate[l], w_exp_up[l], w_exp_down[l])
        yp = _combine(packs_p, ye_p, x1p, mod_ctx, batch, seq, g_final[None, :])
        ys = _combine(packs_s, ye_s, x1s, mod_lat, dec_batch, dec_seq, g_final[None, :])
    y_prompt = yp.reshape(batch, seq, d)
    y_sample = ys.reshape(dec_batch, dec_seq, d)
    tok_major = lambda a, dims: jnp.moveaxis(a.reshape((batch,) + dims + (seq,)), -1, 1)
    new_attn_k = jnp.stack([tok_major(cc[0], (KV_HEADS_A, HEAD_DIM)) for cc in caches], axis=1)
    new_attn_v = jnp.stack([tok_major(cc[1], (KV_HEADS_A, HEAD_DIM)) for cc in caches], axis=1)
    new_diff_k = jnp.stack([tok_major(cc[2], (HEADS_B, 2, HEAD_DIM)) for cc in caches], axis=1)
    new_diff_v = jnp.stack([cc[3].reshape(batch, seq, HEADS_B, 2 * HEAD_DIM) for cc in caches], axis=1)
    return (y_prompt, y_sample, new_attn_k, new_attn_v, new_diff_k, new_diff_v)
```

```python
import functools
import math

import numpy as np
import jax
import jax.numpy as jnp
from jax import lax
from jax.experimental import pallas as pl
from jax.experimental.pallas import tpu as pltpu

F32 = jnp.float32
BF16 = jnp.bfloat16

HEAD_DIM = 64
HEADS_A = 8
KV_HEADS_A = 2
HEADS_B = 4
N_QKV = HEADS_A * HEAD_DIM + 2 * KV_HEADS_A * HEAD_DIM + 3 * HEADS_B * 2 * HEAD_DIM
N_EXPERTS = 16
CAPACITY_FACTOR = 2
GRID_W = 64
ROPE_THETA = 10000.0
EPS = 1e-6
LANES = 128
ROW_TILE = 1024
POST_ROWS = 1024
CHAIN_ROWS = 512
EXPERT_ROWS = 256
LATENT_Q_TILE = 512
CONTEXT_SETS_PER_STEP = 4
SETS_PER_PACK = LANES // N_EXPERTS
SHARED_STEP_ROWS = 2048
DISPATCH_ROWS = 512
NEG_BIG = -1e30
NOT_SELECTED = -1.0
SLOT_RADIX = 256.0
VMEM_LIMIT = 56 * 1024 * 1024
ROUND_STEPS = 8


def _cparams(n_axes):
    return pltpu.CompilerParams(dimension_semantics=("arbitrary",) * n_axes,
                                vmem_limit_bytes=VMEM_LIMIT)


def _dot(a, b):
    return jnp.dot(a, b, preferred_element_type=F32)


def _dot_nt(a, b):
    return lax.dot_general(a, b, (((1,), (1,)), ((), ())), preferred_element_type=F32)


def _split(a):
    hi = a.astype(BF16)
    lo = (a - hi.astype(F32)).astype(BF16)
    return hi, lo


def _dot3(a, b):
    m = a.shape[0]
    a_hi, a_lo = _split(a)
    b_hi, b_lo = _split(b)
    both = _dot(jnp.concatenate([a_hi, a_lo], axis=0), b_hi)
    return both[0:m] + both[m:2 * m] + _dot(a_hi, b_lo)


def _rms(x, g):
    return x * lax.rsqrt(jnp.mean(x * x, axis=-1, keepdims=True) + EPS) * g


def _mod_index(mod, tiles_per_set):
    if mod.shape[0] == 1:
        return lambda i: (0, 0, 0)
    return lambda i: (i // tiles_per_set, 0, 0)


def _mod_kernel(c_ref, w_ref, b_ref, o_ref):
    c = c_ref[...]
    a = c * jax.nn.sigmoid(c)
    o_ref[0] = _dot3(a, w_ref[...]) + b_ref[0]


def _modulation(c_rows, w_mod, b_mod):
    r, d = c_rows.shape
    return pl.pallas_call(
        _mod_kernel,
        grid=(6,),
        in_specs=[pl.BlockSpec((r, d), lambda j: (0, 0)),
                  pl.BlockSpec((d, d), lambda j: (0, j)),
                  pl.BlockSpec((1, 1, d), lambda j: (j, 0, 0))],
        out_specs=pl.BlockSpec((1, r, d), lambda j: (j, 0, 0)),
        out_shape=jax.ShapeDtypeStruct((6, r, d), F32),
        name="mod",
        compiler_params=_cparams(1),
    )(c_rows, w_mod, b_mod.reshape(6, 1, d))


def _round_weights_kernel(*refs):
    n = len(refs) // 2
    for src, out in zip(refs[:n - 1], refs[n:-1]):
        out[...] = src[...].astype(BF16)
    hi, lo = _split(refs[n - 1][...])
    refs[-1][...] = jnp.concatenate([hi, lo], axis=1)


def _round_weights(weights, w_router_p):
    steps = ROUND_STEPS
    arrays = list(weights) + [w_router_p]
    blk = lambda a, cols: pl.BlockSpec((a.shape[0] // steps, cols), lambda i: (i, 0))
    out_cols = [a.shape[1] for a in weights] + [2 * w_router_p.shape[1]]
    return pl.pallas_call(
        _round_weights_kernel,
        grid=(steps,),
        in_specs=[blk(a, a.shape[1]) for a in arrays],
        out_specs=[blk(a, c) for a, c in zip(arrays, out_cols)],
        out_shape=[jax.ShapeDtypeStruct((a.shape[0], c), BF16) for a, c in zip(arrays, out_cols)],
        name="round_weights",
        compiler_params=_cparams(1),
    )(*arrays)


def _seg_sumsq(x, ones_blockdiag):
    return _dot((x * x).astype(BF16), ones_blockdiag)


def _rope(x, c, s_up, s_dn):
    w = x.shape[1]
    reps = w // c.shape[1]
    if reps > 1:
        c, s_up, s_dn = (jnp.concatenate([t] * reps, axis=1) for t in (c, s_up, s_dn))
    return x * c + pltpu.roll(x, w - 16, 1) * s_up + pltpu.roll(x, 16, 1) * s_dn


def _inproj_kernel(*refs, rope, emit_cache):
    (x_ref, mod_ref, g_ref, w_ref, gq_ref, gk_ref, bd512_ref, bd128_ref), refs = refs[:8], refs[8:]
    if rope:
        (c_ref, su_ref, sd_ref), refs = refs[:3], refs[3:]
    qa_o, kta_o, vta_o, qb_o, ktb_o, vb_o = refs[:6]
    if emit_cache:
        ka_c, va_c, kb_c, vb_c = refs[6:10]

    mod = mod_ref[0]
    wa = HEADS_A * HEAD_DIM
    wkv = KV_HEADS_A * HEAD_DIM
    wb = HEADS_B * 2 * HEAD_DIM
    o_ka, o_va, o_qb = wa, wa + wkv, wa + 2 * wkv
    o_kb, o_vb = o_qb + wb, o_qb + 2 * wb
    scale = HEAD_DIM ** -0.5 * math.log2(math.e)

    x = x_ref[...]
    h = (_rms(x, g_ref[...]) * (1.0 + mod[1:2]) + mod[0:1]).astype(BF16)
    if rope:
        tabs = (c_ref[...], su_ref[...], sd_ref[...])

    qa = _dot(h, w_ref[:, 0:wa])
    qa = qa * lax.rsqrt(_seg_sumsq(qa, bd512_ref[...]) * (1.0 / HEAD_DIM) + EPS) * gq_ref[...]
    if rope:
        qa = _rope(qa, *tabs)
    qa_o[...] = (qa * scale).astype(BF16)

    kv = _dot(h, w_ref[:, o_ka:o_qb])
    ka, va = kv[:, 0:wkv], kv[:, wkv:2 * wkv]
    ka = ka * lax.rsqrt(_seg_sumsq(ka, bd128_ref[...]) * (1.0 / HEAD_DIM) + EPS) * gk_ref[...]

    def put_feat(val, out_bf16, out_f32):
        rows = out_bf16.shape[2]
        for s in range(out_bf16.shape[0]):
            t = val[s * rows:(s + 1) * rows].T
            if out_f32 is not None:
                out_f32[s] = t
            out_bf16[s] = t.astype(BF16)

    if rope:
        put_feat(_rope(ka, *tabs), kta_o, None)
    else:
        put_feat(ka, kta_o, ka_c if emit_cache else None)
    put_feat(va, vta_o, va_c if emit_cache else None)

    qb = _dot(h, w_ref[:, o_qb:o_kb])
    if rope:
        qb = _rope(qb, *tabs)
    qb_o[...] = (qb * scale).astype(BF16)

    kb = _dot(h, w_ref[:, o_kb:o_vb])
    if rope:
        put_feat(_rope(kb, *tabs), ktb_o, None)
    else:
        put_feat(kb, ktb_o, kb_c if emit_cache else None)

    vb = _dot(h, w_ref[:, o_vb:o_vb + wb])
    if emit_cache:
        for hd in range(HEADS_B):
            vb_c[pl.ds(hd, vb.shape[0], stride=HEADS_B), :] = vb[:, hd * 128:(hd + 1) * 128]
    vb_o[...] = vb.astype(BF16)


def _blockdiag_ones(width):
    g = np.arange(width) // HEAD_DIM
    return jnp.asarray((g[:, None] == g[None, :]).astype(np.float32), dtype=BF16)


def _inproj(x2, mod, set_rows, g_attn, w_in, gq_t, gk_t, rope_tabs, emit_cache):
    t, d = x2.shape
    tm = ROW_TILE
    assert t % tm == 0 and (tm % set_rows == 0 or set_rows % tm == 0)
    tiles_per_set = max(1, set_rows // tm)
    sets_per_tile = max(1, tm // set_rows)
    rope = rope_tabs is not None
    assert not (rope and emit_cache), "cached keys are the position-free ones"
    nq = N_QKV
    row = lambda i: (i, 0)
    const = lambda i: (0, 0)
    in_specs = [pl.BlockSpec((tm, d), row),
                pl.BlockSpec((1, 8, d), _mod_index(mod, tiles_per_set)),
                pl.BlockSpec((1, d), const),
                pl.BlockSpec((d, nq), const, pipeline_mode=pl.Buffered(1)),
                pl.BlockSpec((1, 512), const),
                pl.BlockSpec((1, 128), const),
                pl.BlockSpec((512, 512), const),
                pl.BlockSpec((128, 128), const)]
    args = [x2, mod, g_attn, w_in, gq_t, gk_t, _blockdiag_ones(512), _blockdiag_ones(128)]
    if rope:
        in_specs += [pl.BlockSpec((tm, LANES), lambda i: (i % tiles_per_set, 0))] * 3
        args += list(rope_tabs)
    n_sets = t // set_rows
    wkv, wb = KV_HEADS_A * HEAD_DIM, HEADS_B * 2 * HEAD_DIM
    outs = [("tok", 512, BF16), ("feat", wkv, BF16), ("feat", wkv, BF16),
            ("tok", 512, BF16), ("feat", wb, BF16), ("tok", wb, BF16)]
    if emit_cache:
        outs += [("feat", wkv, F32), ("feat", wkv, F32), ("feat", wb, F32), ("tokhead", wb, F32)]
    feat = lambda i: (i // tiles_per_set, 0, i % tiles_per_set)
    feat_rows = min(tm, set_rows)

    def out_block(kind, w):
        if kind == "tok":
            return pl.BlockSpec((tm, w), row)
        if kind == "tokhead":
            return pl.BlockSpec((tm * HEADS_B, w // HEADS_B), row)
        return pl.BlockSpec((sets_per_tile, w, feat_rows), feat)

    def out_array(kind, w, dt):
        shape = {"tok": (t, w), "tokhead": (t * HEADS_B, w // HEADS_B), "feat": (n_sets, w, set_rows)}[kind]
        return jax.ShapeDtypeStruct(shape, dt)

    return pl.pallas_call(
        functools.partial(_inproj_kernel, rope=rope, emit_cache=emit_cache),
        grid=(t // tm,),
        in_specs=in_specs,
        out_specs=[out_block(kind, w) for kind, w, _ in outs],
        out_shape=[out_array(kind, w, dt) for kind, w, dt in outs],
        name="inproj_rope" if rope else "inproj",
        compiler_params=_cparams(1),
    )(*args)


def _exp_parts(scores):
    m = scores[0].max(axis=-1, keepdims=True)
    for s in scores[1:]:
        m = jnp.maximum(m, s.max(axis=-1, keepdims=True))
    return [jnp.exp2(s - m) for s in scores]


def _row_sum(parts):
    l = parts[0].sum(axis=-1, keepdims=True)
    for e in parts[1:]:
        l = l + e.sum(axis=-1, keepdims=True)
    return l


def _pad_rows(x, first, ones_row=False):
    if ones_row:
        z = jnp.where(lax.broadcasted_iota(jnp.int32, x.shape, 0) == 0, 1.0, 0.0).astype(x.dtype)
    else:
        z = jnp.zeros_like(x)
    return jnp.concatenate([x, z] if first else [z, x], axis=0)


def _gqa_kernel(*refs, n_src, mxu_sums):
    q_ref = refs[0]
    k_refs = refs[1:1 + n_src]
    v_refs = refs[1 + n_src:1 + 2 * n_src]
    o_ref = refs[1 + 2 * n_src]
    sets_here = k_refs[0].shape[0]
    tq = q_ref.shape[0] // sets_here
    lane = lax.broadcasted_iota(jnp.int32, (2 * tq, LANES), 1)
    lo_half = lane < HEAD_DIM
    for s in range(sets_here):
        rows = slice(s * tq, (s + 1) * tq)
        for g in range(KV_HEADS_A):
            c0 = g * 256
            f0 = g * HEAD_DIM
            q = jnp.concatenate([q_ref[rows, c0:c0 + 128], q_ref[rows, c0 + 128:c0 + 256]], axis=0)
            kts = [k[s, f0:f0 + HEAD_DIM, :].astype(BF16) for k in k_refs]
            vts = [v[s, f0:f0 + HEAD_DIM, :].astype(BF16) for v in v_refs]
            e_lo = _exp_parts([_dot(q, _pad_rows(kt, True)) for kt in kts])
            e_hi = _exp_parts([_dot(q, _pad_rows(kt, False)) for kt in kts])
            o_lo = o_hi = None
            for e_part, vt in zip(e_lo, vts):
                t = _dot_nt(e_part.astype(BF16), _pad_rows(vt, True, ones_row=mxu_sums))
                o_lo = t if o_lo is None else o_lo + t
            for e_part, vt in zip(e_hi, vts):
                t = _dot_nt(e_part.astype(BF16), _pad_rows(vt, False, ones_row=mxu_sums))
                o_hi = t if o_hi is None else o_hi + t
            if mxu_sums:
                l_lo, l_hi = o_lo[:, HEAD_DIM:HEAD_DIM + 1], o_hi[:, 0:1]
                o = jnp.where(lo_half, o_lo * (1.0 / l_lo), o_hi * (1.0 / l_hi))
            else:
                o = (o_lo + o_hi) * jnp.where(lo_half, 1.0 / _row_sum(e_lo), 1.0 / _row_sum(e_hi))
            o_ref[rows, c0:c0 + 128] = o[0:tq].astype(BF16)
            o_ref[rows, c0 + 128:c0 + 256] = o[tq:2 * tq].astype(BF16)


def _diff_kernel(*refs, n_src, lam_init, stack_maps):
    q_ref = refs[0]
    k_refs = refs[1:1 + n_src]
    v_refs = refs[1 + n_src:1 + 2 * n_src]
    lq1, lk1, lq2, lk2, gd_ref, o_ref = refs[1 + 2 * n_src:]
    lam_all = (jnp.exp(jnp.sum(lq1[...] * lk1[...], axis=-1, keepdims=True))
               - jnp.exp(jnp.sum(lq2[...] * lk2[...], axis=-1, keepdims=True)) + lam_init)
    sets_here = k_refs[0].shape[0]
    tq = q_ref.shape[0] // sets_here
    for s in range(sets_here):
        rows = slice(s * tq, (s + 1) * tq)
        for j in range(HEADS_B):
            lam = lam_all[j:j + 1, :]
            q = q_ref[rows, j * 128:(j + 1) * 128]
            k0s = [_pad_rows(k[s, j * 128:j * 128 + HEAD_DIM, :].astype(BF16), True) for k in k_refs]
            k1s = [_pad_rows(k[s, j * 128 + HEAD_DIM:(j + 1) * 128, :].astype(BF16), False) for k in k_refs]
            e0 = _exp_parts([_dot(q, k0) for k0 in k0s])
            e1 = _exp_parts([_dot(q, k1) for k1 in k1s])
            o0 = o1 = None
            for a0, a1, v in zip(e0, e1, v_refs):
                if v.shape[2] == LANES:
                    val = v[s, pl.ds(j, v.shape[1] // HEADS_B, stride=HEADS_B), :].astype(BF16)
                else:
                    val = v[s, :, j * 128:(j + 1) * 128]
                if stack_maps:
                    t = _dot(jnp.concatenate([a0.astype(BF16), a1.astype(BF16)], axis=0), val)
                    t0, t1 = t[0:tq], t[tq:2 * tq]
                else:
                    t0, t1 = _dot(a0.astype(BF16), val), _dot(a1.astype(BF16), val)
                o0 = t0 if o0 is None else o0 + t0
                o1 = t1 if o1 is None else o1 + t1
            o = o0 * (1.0 / _row_sum(e0)) - o1 * (lam / _row_sum(e1))
            o = _rms(o, gd_ref[...]) * (1.0 - lam_init)
            o_ref[rows, j * 128:(j + 1) * 128] = o.astype(BF16)


def _both_mixers_kernel(*refs, n_src, lam_init, stack_maps, mxu_sums):
    per = 1 + 2 * n_src
    ya_o, yb_o = refs[-2:]
    _gqa_kernel(*refs[:per], ya_o, n_src=n_src, mxu_sums=mxu_sums)
    _diff_kernel(*refs[per:2 * per + 5], yb_o, n_src=n_src, lam_init=lam_init, stack_maps=stack_maps)


def _both_mixers(qa, ks_a, vs_a, qb, ks_b, vs_b, lam_refs, set_rows, tq, sets_per_step, lam_init):
    t = qa.shape[0]
    n_sets = t // set_rows
    q_tiles = set_rows // tq
    assert sets_per_step == 1 or q_tiles == 1
    rows = sets_per_step * tq
    tile = lambda b, i: (b * q_tiles + i, 0)
    kv = lambda a: pl.BlockSpec((sets_per_step,) + a.shape[1:], lambda b, i: (b, 0, 0))
    in_specs = ([pl.BlockSpec((rows, qa.shape[1]), tile)] + [kv(a) for a in list(ks_a) + list(vs_a)]
                + [pl.BlockSpec((rows, qb.shape[1]), tile)] + [kv(a) for a in list(ks_b) + list(vs_b)]
                + [pl.BlockSpec(e.shape, lambda b, i: (0, 0)) for e in lam_refs])
    return pl.pallas_call(
        functools.partial(_both_mixers_kernel, n_src=len(ks_a), lam_init=lam_init,
                          stack_maps=len(ks_a) == 1, mxu_sums=len(ks_a) > 1),
        grid=(n_sets // sets_per_step, q_tiles),
        in_specs=in_specs,
        out_specs=[pl.BlockSpec((rows, qa.shape[1]), tile), pl.BlockSpec((rows, qb.shape[1]), tile)],
        out_shape=[jax.ShapeDtypeStruct(qa.shape, BF16), jax.ShapeDtypeStruct(qb.shape, BF16)],
        name=f"mixers_{len(ks_a)}src",
        compiler_params=_cparams(2),
    )(qa, *ks_a, *vs_a, qb, *ks_b, *vs_b, *lam_refs)


def _post_kernel(x_ref, ya_ref, yb_ref, mod_ref, g1_ref, g2_ref, win_ref, wba_ref, wbb_ref, wo_ref, wr2_ref,
                 x1_o, h2_o, aff_o):
    d = x_ref.shape[1]
    g0 = win_ref.shape[1] - 2 * d
    mod = mod_ref[0]
    for r0 in range(0, x_ref.shape[0], CHAIN_ROWS):
        rows = slice(r0, r0 + CHAIN_ROWS)
        x = x_ref[rows, :]
        h = (_rms(x, g1_ref[...]) * (1.0 + mod[1:2]) + mod[0:1]).astype(BF16)
        ga = jax.nn.sigmoid(_dot(h, win_ref[:, g0:g0 + d]))
        merged = ga * _dot(ya_ref[rows, :], wba_ref[...])
        gb = jax.nn.sigmoid(_dot(h, win_ref[:, g0 + d:g0 + 2 * d]))
        merged = merged + gb * _dot(yb_ref[rows, :], wbb_ref[...])
        m = _dot(merged.astype(BF16), wo_ref[...])
        x1 = x + mod[2:3] * m
        x1_o[rows, :] = x1
        h2 = _rms(x1, g2_ref[...]) * (1.0 + mod[4:5]) + mod[3:4]
        h2_o[rows, :] = h2.astype(BF16)
        h2_hi, h2_lo = _split(h2)
        both = _dot(h2_hi, wr2_ref[...])
        logits = both[:, 0:LANES] + both[:, LANES:2 * LANES] + _dot(h2_lo, wr2_ref[:, 0:LANES])
        lane = lax.broadcasted_iota(jnp.int32, logits.shape, 1)
        logits = jnp.where(lane < N_EXPERTS, logits, NEG_BIG)
        e = jnp.exp(logits - logits.max(axis=-1, keepdims=True))
        aff_o[rows, :] = e / e.sum(axis=-1, keepdims=True)


def _post(x2, ya, yb, mod, set_rows, g1, g2, w_in, w_ba, w_bb, w_out, w_router2):
    t, d = x2.shape
    tm = POST_ROWS
    tiles_per_set = max(1, set_rows // tm)
    assert t % tm == 0 and tm % CHAIN_ROWS == 0 and (mod.shape[0] == 1 or set_rows % tm == 0)
    row = lambda i: (i, 0)
    const = lambda i: (0, 0)
    once = pl.Buffered(1)
    half = w_in.shape[1] // 2
    assert w_in.shape[1] == 2 * half and half % LANES == 0 and half >= 2 * d
    return pl.pallas_call(
        _post_kernel,
        grid=(t // tm,),
        in_specs=[pl.BlockSpec((tm, d), row),
                  pl.BlockSpec((tm, 512), row),
                  pl.BlockSpec((tm, 512), row),
                  pl.BlockSpec((1, 8, d), _mod_index(mod, tiles_per_set)),
                  pl.BlockSpec((1, d), const),
                  pl.BlockSpec((1, d), const),
                  pl.BlockSpec((d, half), lambda i: (0, 1), pipeline_mode=once),
                  pl.BlockSpec(w_ba.shape, const, pipeline_mode=once),
                  pl.BlockSpec(w_bb.shape, const, pipeline_mode=once),
                  pl.BlockSpec(w_out.shape, const, pipeline_mode=once),
                  pl.BlockSpec(w_router2.shape, const, pipeline_mode=once)],
        out_specs=[pl.BlockSpec((tm, d), row), pl.BlockSpec((tm, d), row), pl.BlockSpec((tm, LANES), row)],
        out_shape=[jax.ShapeDtypeStruct((t, d), F32), jax.ShapeDtypeStruct((t, d), BF16),
                   jax.ShapeDtypeStruct((t, LANES), F32)],
        name="post_attn",
        compiler_params=_cparams(1),
    )(x2, ya, yb, mod, g1, g2, w_in, w_ba, w_bb, w_out, w_router2)


def _select_kernel(aff_ref, slot_o, slot_t_o, affb_o, *, cap):
    aff = aff_ref[0]
    n = aff.shape[0]
    capf = float(cap)

    def enough(cand):
        return jnp.sum(jnp.where(aff >= cand, 1.0, 0.0), axis=0, keepdims=True) >= capf

    pw = jnp.ones((1, LANES), F32)
    for k in (64, 32, 16, 8, 4, 2, 1):
        pw = jnp.where(enough(pw * 2.0 ** -(k - 1)), pw, pw * 2.0 ** -k)

    def mantissa_step(_, carry):
        thr, step = carry
        cand = thr + step
        return jnp.where(enough(cand), cand, thr), step * 0.5

    thr, _ = lax.fori_loop(0, 23, mantissa_step, (pw, pw * 0.5))
    above = aff > thr
    tied = aff == thr
    need = capf - jnp.sum(jnp.where(above, 1.0, 0.0), axis=0, keepdims=True)
    r_i = lax.broadcasted_iota(jnp.int32, (n, n), 0)
    c_i = lax.broadcasted_iota(jnp.int32, (n, n), 1)
    before = jnp.where(c_i < r_i, 1.0, 0.0).astype(BF16)
    tie_rank = _dot(before, jnp.where(tied, 1.0, 0.0).astype(BF16))
    sel = above | (tied & (tie_rank < need))
    slot = _dot(before, jnp.where(sel, 1.0, 0.0).astype(BF16))
    slot = jnp.where(sel, slot, NOT_SELECTED)
    slot_o[0] = slot.astype(BF16)
    slot_t_o[0] = slot.T
    affb_o[0] = aff.astype(BF16)


def _dispatch_kernel(slot_t_ref, h2_ref, xe_o, *, cap):
    n = slot_t_ref.shape[2]
    g = h2_ref.shape[0] // n
    per = min(N_EXPERTS, DISPATCH_ROWS // cap)
    slot_iota = lax.broadcasted_iota(jnp.int32, (cap, n), 0).astype(F32)
    for k in range(g):
        h2 = h2_ref[k * n:(k + 1) * n, :]
        slot_t = slot_t_ref[0, k * N_EXPERTS:(k + 1) * N_EXPERTS, :]
        for e0 in range(0, N_EXPERTS, per):
            onehot = jnp.concatenate(
                [jnp.where(slot_t[e:e + 1, :] == slot_iota, 1.0, 0.0) for e in range(e0, e0 + per)],
                axis=0).astype(BF16)
            rows = _dot(onehot, h2).astype(BF16)
            for j in range(per):
                xe_o[e0 + j, k * cap:(k + 1) * cap, :] = rows[j * cap:(j + 1) * cap]


def _route(aff, h2, n_sets, set_rows):
    t, d = h2.shape
    n = set_rows
    cap = CAPACITY_FACTOR * n // N_EXPERTS
    assert cap & (cap - 1) == 0 and cap % 16 == 0 and DISPATCH_ROWS % cap == 0
    assert cap <= SLOT_RADIX, "slot ids must stay exact in bf16 and below the radix of the slot/gate packing"
    n_packs = -(-n_sets // SETS_PER_PACK)
    aff16 = aff[:, :N_EXPERTS].reshape(n_sets, n, N_EXPERTS)
    aff16 = jnp.pad(aff16, ((0, n_packs * SETS_PER_PACK - n_sets), (0, 0), (0, 0)))
    packed = aff16.reshape(n_packs, SETS_PER_PACK, n, N_EXPERTS).transpose(0, 2, 1, 3).reshape(n_packs, n, LANES)
    pack_blk = lambda p: (p, 0, 0)
    slot, slot_t, affb = pl.pallas_call(
        functools.partial(_select_kernel, cap=cap),
        grid=(n_packs,),
        in_specs=[pl.BlockSpec((1, n, LANES), pack_blk)],
        out_specs=[pl.BlockSpec((1, n, LANES), pack_blk), pl.BlockSpec((1, LANES, n), pack_blk),
                   pl.BlockSpec((1, n, LANES), pack_blk)],
        out_shape=[jax.ShapeDtypeStruct((n_packs, n, LANES), BF16),
                   jax.ShapeDtypeStruct((n_packs, LANES, n), F32),
                   jax.ShapeDtypeStruct((n_packs, n, LANES), BF16)],
        name=f"select_cap{cap}",
        compiler_params=_cparams(1),
    )(packed)
    g = _sets_per_step(n_sets, n)
    per_pack = SETS_PER_PACK // g
    xe = pl.pallas_call(
        functools.partial(_dispatch_kernel, cap=cap),
        grid=(n_sets // g,),
        in_specs=[pl.BlockSpec((1, g * N_EXPERTS, n), lambda s: (s // per_pack, s % per_pack, 0)),
                  pl.BlockSpec((g * n, d), lambda s: (s, 0))],
        out_specs=pl.BlockSpec((N_EXPERTS, g * cap, d), lambda s: (0, s, 0)),
        out_shape=jax.ShapeDtypeStruct((N_EXPERTS, n_sets * cap, d), BF16),
        name=f"dispatch_cap{cap}",
        compiler_params=_cparams(1),
    )(slot_t, h2)
    return xe, (slot, affb)


def _expert_kernel(*refs, n_groups):
    x_refs = refs[:n_groups]
    wg_ref, wu_ref, wd_ref = refs[n_groups:n_groups + 3]
    o_refs = refs[n_groups + 3:2 * n_groups + 3]
    wg_s, wu_s, wd_s = refs[2 * n_groups + 3:]
    wg_s[...] = wg_ref[0].astype(BF16)
    wu_s[...] = wu_ref[0].astype(BF16)
    wd_s[...] = wd_ref[0].astype(BF16)
    sub = EXPERT_ROWS
    for x_ref, o_ref in zip(x_refs, o_refs):
        for r0 in range(0, x_ref.shape[1], sub):
            x = x_ref[0, r0:r0 + sub, :]
            a = _dot(x, wg_s[...])
            u = _dot(x, wu_s[...])
            hmid = (a * jax.nn.sigmoid(a) * u).astype(BF16)
            o_ref[0, r0:r0 + sub, :] = _dot(hmid, wd_s[...]).astype(BF16)


def _experts(xes, w_gate, w_up, w_down):
    e, d, f = w_gate.shape
    blk = lambda i: (i, 0, 0)
    return pl.pallas_call(
        functools.partial(_expert_kernel, n_groups=len(xes)),
        grid=(e,),
        in_specs=[pl.BlockSpec((1, x.shape[1], d), blk) for x in xes]
        + [pl.BlockSpec((1, d, f), blk), pl.BlockSpec((1, d, f), blk), pl.BlockSpec((1, f, d), blk)],
        out_specs=[pl.BlockSpec((1, x.shape[1], d), blk) for x in xes],
        out_shape=[jax.ShapeDtypeStruct(x.shape, BF16) for x in xes],
        scratch_shapes=[pltpu.VMEM((d, f), BF16), pltpu.VMEM((d, f), BF16), pltpu.VMEM((f, d), BF16)],
        name="experts",
        compiler_params=_cparams(1),
    )(*xes, w_gate, w_up, w_down)


def _combine_kernel(slot_ref, affb_ref, ye_ref, x1_ref, mod_ref, gf_ref, y_o, *, cap):
    e, gcap, d = ye_ref.shape
    g = gcap // cap
    n = x1_ref.shape[0] // g
    shift = cap.bit_length() - 1
    both = jnp.concatenate([slot_ref[0], affb_ref[0]], axis=1)
    want = (lax.broadcasted_iota(jnp.int32, (n, DISPATCH_ROWS), 1) & (cap - 1)).astype(F32) * SLOT_RADIX
    for k in range(g):
        rows = slice(k * n, (k + 1) * n)
        first_lane = ((pl.program_id(0) * g + k) % SETS_PER_PACK) * N_EXPERTS
        pieces = []
        for c0 in range(0, e * cap, DISPATCH_ROWS):
            src = lax.broadcasted_iota(jnp.int32, (2 * LANES, DISPATCH_ROWS), 0)
            col = lax.broadcasted_iota(jnp.int32, (2 * LANES, DISPATCH_ROWS), 1) + c0
            lane_of_col = first_lane + lax.shift_right_logical(col, shift)
            spread = jnp.where(src == lane_of_col, SLOT_RADIX,
                               jnp.where(src == lane_of_col + LANES, 1.0, 0.0)).astype(BF16)
            rest = _dot(both, spread) - want
            pieces.append(jnp.where(rest >= 0.0, jnp.where(rest <= 1.0, rest, 0.0), 0.0).astype(BF16))
        gt = pieces[0] if len(pieces) == 1 else jnp.concatenate(pieces, axis=1)
        ye = ye_ref[:, k * cap:(k + 1) * cap, :].reshape(e * cap, d)
        moe = _dot(gt, ye)
        x = x1_ref[rows, :] + mod_ref[0][5:6] * moe
        y_o[rows, :] = _rms(x, gf_ref[...])


def _sets_per_step(n_sets, set_rows):
    g = SETS_PER_PACK if set_rows * SETS_PER_PACK <= SHARED_STEP_ROWS else 1
    return g if n_sets % g == 0 else 1


def _combine(packs, ye, x1, mod, n_sets, set_rows, g_final):
    slot, affb = packs
    t, d = x1.shape
    cap = ye.shape[1] // n_sets
    g = _sets_per_step(n_sets, set_rows) if mod.shape[0] == 1 else 1
    tm = min(ROW_TILE, set_rows) if g == 1 else g * set_rows
    tiles = max(1, set_rows // tm)
    per_pack = SETS_PER_PACK // g
    pack_rows = pl.BlockSpec((1, tm // g, LANES), lambda s, i: (s // per_pack, i, 0))
    return pl.pallas_call(
        functools.partial(_combine_kernel, cap=cap),
        grid=(n_sets // g, tiles),
        in_specs=[pack_rows, pack_rows,
                  pl.BlockSpec((N_EXPERTS, g * cap, d), lambda s, i: (0, s, 0)),
                  pl.BlockSpec((tm, d), lambda s, i: (s * tiles + i, 0)),
                  pl.BlockSpec((1, 8, d), (lambda s, i: (s, 0, 0)) if mod.shape[0] > 1 else (lambda s, i: (0, 0, 0))),
                  pl.BlockSpec((1, d), lambda s, i: (0, 0))],
        out_specs=pl.BlockSpec((tm, d), lambda s, i: (s * tiles + i, 0)),
        out_shape=jax.ShapeDtypeStruct((t, d), F32),
        name=f"combine_cap{cap}",
        compiler_params=_cparams(2),
    )(slot, affb, ye, x1, mod, g_final)


def _rope_tables(n_tokens):
    n_rows = n_tokens // GRID_W
    rowp = np.repeat(np.arange(n_rows), GRID_W).astype(np.float32)
    colp = np.tile(np.arange(GRID_W), n_rows).astype(np.float32)
    quarter = HEAD_DIM // 4
    freqs = (np.float32(ROPE_THETA) ** (-np.arange(quarter, dtype=np.float32) / np.float32(quarter))).astype(np.float32)
    ang = np.stack([rowp[:, None] * freqs, colp[:, None] * freqs], axis=1)
    cos = np.cos(ang.astype(np.float64)).astype(np.float32)
    sin = np.sin(ang.astype(np.float64)).astype(np.float32)
    zero = np.zeros_like(sin)
    c = np.stack([cos, cos], axis=2).reshape(n_tokens, HEAD_DIM)
    s_up = np.stack([-sin, zero], axis=2).reshape(n_tokens, HEAD_DIM)
    s_dn = np.stack([zero, sin], axis=2).reshape(n_tokens, HEAD_DIM)
    return tuple(jnp.asarray(np.tile(t, (1, LANES // HEAD_DIM))) for t in (c, s_up, s_dn))


def _mix_and_route(x2, n_sets, set_rows, mod, lw, rope_tabs, cache, emit_cache, lam_init, tq):
    (g_attn, g_ffn, w_in, gq_t, gk_t, lq1, lk1, lq2, lk2, g_diff, w_ba, w_bb, w_out, w_router2) = lw
    outs = _inproj(x2, mod, set_rows, g_attn, w_in, gq_t, gk_t, rope_tabs, emit_cache)
    qa, kta, vta, qb, ktb, vb16 = outs[:6]
    vb16 = vb16.reshape(n_sets, set_rows, vb16.shape[1])
    ks_a, vs_a, ks_b, vs_b = [kta], [vta], [ktb], [vb16]
    if cache is not None:
        ckta, cvta, cktb, cvb = cache
        ks_a, vs_a, ks_b, vs_b = [ckta, kta], [cvta, vta], [cktb, ktb], [cvb, vb16]
    per_step = CONTEXT_SETS_PER_STEP if (tq == set_rows and n_sets % CONTEXT_SETS_PER_STEP == 0) else 1
    ya, yb = _both_mixers(qa, ks_a, vs_a, qb, ks_b, vs_b, (lq1, lk1, lq2, lk2, g_diff), set_rows, tq, per_step,
                          lam_init)
    x1, h2, aff = _post(x2, ya, yb, mod, set_rows, g_attn, g_ffn, w_in, w_ba, w_bb, w_out, w_router2)
    xe, packs = _route(aff, h2, n_sets, set_rows)
    return x1, xe, packs, outs[6:]


def kernel(x_prompt, x_sample, cache_attn_k, cache_attn_v, cache_diff_k, cache_diff_v, c, c_ctx, w_mod, b_mod,
           g_attn_norm, g_ffn_norm, w_in, g_q_norm, g_k_norm, lambda_q1, lambda_k1, lambda_q2, lambda_k2,
           g_diff_norm, w_branch_a, w_branch_b, w_out, w_router, w_exp_gate, w_exp_up, w_exp_down, g_final):
    batch, seq, d = x_prompt.shape
    dec_batch, dec_seq, _ = x_sample.shape
    depth = w_in.shape[0]
    assert depth == 1, "the final norm is fused into the layer's combine step"
    past = cache_attn_k.shape[2]
    assert w_in.shape[2] == N_QKV + 2 * d

    xp = x_prompt.reshape(batch * seq, d)
    xs = x_sample.reshape(dec_batch * dec_seq, d)
    rope_tabs = _rope_tables(dec_seq)
    c_rows = jnp.concatenate([c, c_ctx[None, :], jnp.zeros((16 - dec_batch - 1, d), F32)], axis=0)
    yp = ys = None
    caches = []
    for l in range(depth):
        lam_init = 0.8 - 0.6 * math.exp(-0.3 * l)
        mod6 = _modulation(c_rows, w_mod[l], b_mod[l])
        mod = jnp.pad(jnp.transpose(mod6, (1, 0, 2)), ((0, 0), (0, 2), (0, 0)))
        mod_lat, mod_ctx = mod[:dec_batch], mod[dec_batch:dec_batch + 1]
        w_in16, w_ba16, w_bb16, w_out16, w_router2 = _round_weights(
            [w_in[l], w_branch_a[l], w_branch_b[l], w_out[l]],
            jnp.pad(w_router[l], ((0, 0), (0, LANES - N_EXPERTS))))
        lw = (g_attn_norm[l][None, :], g_ffn_norm[l][None, :], w_in16,
              jnp.tile(g_q_norm[l], HEADS_A)[None, :], jnp.tile(g_k_norm[l], KV_HEADS_A)[None, :],
              lambda_q1[l], lambda_k1[l], lambda_q2[l], lambda_k2[l], g_diff_norm[l][None, :],
              w_ba16, w_bb16, w_out16, w_router2)
        x1p, xe_p, packs_p, cache_out = _mix_and_route(xp, batch, seq, mod_ctx, lw, None, None, True, lam_init, seq)
        caches.append(cache_out)
        feat_major = lambda a: jnp.moveaxis(a[:, l], 1, -1).reshape(dec_batch, -1, past)
        cache_l = (feat_major(cache_attn_k), feat_major(cache_attn_v), feat_major(cache_diff_k),
                   cache_diff_v[:, l].reshape(dec_batch, past * HEADS_B, -1))
        x1s, xe_s, packs_s, _ = _mix_and_route(xs, dec_batch, dec_seq, mod_lat, lw, rope_tabs, cache_l, False,
                                               lam_init, LATENT_Q_TILE)
        ye_p, ye_s = _experts([xe_p, xe_s], w_exp_gate[l], w_exp_up[l], w_exp_down[l])
        yp = _combine(packs_p, ye_p, x1p, mod_ctx, batch, seq, g_final[None, :])
        ys = _combine(packs_s, ye_s, x1s, mod_lat, dec_batch, dec_seq, g_final[None, :])
    y_prompt = yp.reshape(batch, seq, d)
    y_sample = ys.reshape(dec_batch, dec_seq, d)
    tok_major = lambda a, dims: jnp.moveaxis(a.reshape((batch,) + dims + (seq,)), -1, 1)
    new_attn_k = jnp.stack([tok_major(cc[0], (KV_HEADS_A, HEAD_DIM)) for cc in caches], axis=1)
    new_attn_v = jnp.stack([tok_major(cc[1], (KV_HEADS_A, HEAD_DIM)) for cc in caches], axis=1)
    new_diff_k = jnp.stack([tok_major(cc[2], (HEADS_B, 2, HEAD_DIM)) for cc in caches], axis=1)
    new_diff_v = jnp.stack([cc[3].reshape(batch, seq, HEADS_B, 2 * HEAD_DIM) for cc in caches], axis=1)
    return (y_prompt, y_sample, new_attn_k, new_attn_v, new_diff_k, new_diff_v)
```

```python
import functools
import math

import numpy as np
import jax
import jax.numpy as jnp
from jax import lax
from jax.experimental import pallas as pl
from jax.experimental.pallas import tpu as pltpu

F32 = jnp.float32
BF16 = jnp.bfloat16

HEAD_DIM = 64
HEADS_A = 8
KV_HEADS_A = 2
HEADS_B = 4
N_QKV = HEADS_A * HEAD_DIM + 2 * KV_HEADS_A * HEAD_DIM + 3 * HEADS_B * 2 * HEAD_DIM
N_EXPERTS = 16
CAPACITY_FACTOR = 2
GRID_W = 64
ROPE_THETA = 10000.0
EPS = 1e-6
LANES = 128
ROW_TILE = 1024
POST_ROWS = 1024
CHAIN_ROWS = 512
EXPERT_ROWS = 256
LATENT_Q_TILE = 512
CONTEXT_SETS_PER_STEP = 4
SETS_PER_PACK = LANES // N_EXPERTS
SHARED_STEP_ROWS = 2048
DISPATCH_ROWS = 512
NEG_BIG = -1e30
NOT_SELECTED = -1.0
SLOT_RADIX = 256.0
VMEM_LIMIT = 56 * 1024 * 1024
ROUND_STEPS = 8


def _cparams(n_axes):
    return pltpu.CompilerParams(dimension_semantics=("arbitrary",) * n_axes,
                                vmem_limit_bytes=VMEM_LIMIT)


def _dot(a, b):
    return jnp.dot(a, b, preferred_element_type=F32)


def _dot_nt(a, b):
    return lax.dot_general(a, b, (((1,), (1,)), ((), ())), preferred_element_type=F32)


def _split(a):
    hi = a.astype(BF16)
    lo = (a - hi.astype(F32)).astype(BF16)
    return hi, lo


def _dot3(a, b):
    m = a.shape[0]
    a_hi, a_lo = _split(a)
    b_hi, b_lo = _split(b)
    both = _dot(jnp.concatenate([a_hi, a_lo], axis=0), b_hi)
    return both[0:m] + both[m:2 * m] + _dot(a_hi, b_lo)


def _rms(x, g):
    return x * lax.rsqrt(jnp.mean(x * x, axis=-1, keepdims=True) + EPS) * g


def _mod_index(mod, tiles_per_set):
    if mod.shape[0] == 1:
        return lambda i: (0, 0, 0)
    return lambda i: (i // tiles_per_set, 0, 0)


def _mod_kernel(c_ref, w_ref, b_ref, o_ref):
    c = c_ref[...]
    a = c * jax.nn.sigmoid(c)
    o_ref[0] = _dot3(a, w_ref[...]) + b_ref[0]


def _modulation(c_rows, w_mod, b_mod):
    r, d = c_rows.shape
    return pl.pallas_call(
        _mod_kernel,
        grid=(6,),
        in_specs=[pl.BlockSpec((r, d), lambda j: (0, 0)),
                  pl.BlockSpec((d, d), lambda j: (0, j)),
                  pl.BlockSpec((1, 1, d), lambda j: (j, 0, 0))],
        out_specs=pl.BlockSpec((1, r, d), lambda j: (j, 0, 0)),
        out_shape=jax.ShapeDtypeStruct((6, r, d), F32),
        name="mod",
        compiler_params=_cparams(1),
    )(c_rows, w_mod, b_mod.reshape(6, 1, d))


def _round_weights_kernel(*refs):
    n = len(refs) // 2
    for src, out in zip(refs[:n - 1], refs[n:-1]):
        out[...] = src[...].astype(BF16)
    hi, lo = _split(refs[n - 1][...])
    refs[-1][...] = jnp.concatenate([hi, lo], axis=1)


def _round_weights(weights, w_router_p):
    steps = ROUND_STEPS
    arrays = list(weights) + [w_router_p]
    blk = lambda a, cols: pl.BlockSpec((a.shape[0] // steps, cols), lambda i: (i, 0))
    out_cols = [a.shape[1] for a in weights] + [2 * w_router_p.shape[1]]
    return pl.pallas_call(
        _round_weights_kernel,
        grid=(steps,),
        in_specs=[blk(a, a.shape[1]) for a in arrays],
        out_specs=[blk(a, c) for a, c in zip(arrays, out_cols)],
        out_shape=[jax.ShapeDtypeStruct((a.shape[0], c), BF16) for a, c in zip(arrays, out_cols)],
        name="round_weights",
        compiler_params=_cparams(1),
    )(*arrays)


def _seg_sumsq(x, ones_blockdiag):
    return _dot((x * x).astype(BF16), ones_blockdiag)


def _rope(x, c, s_up, s_dn):
    w = x.shape[1]
    reps = w // c.shape[1]
    if reps > 1:
        c, s_up, s_dn = (jnp.concatenate([t] * reps, axis=1) for t in (c, s_up, s_dn))
    return x * c + pltpu.roll(x, w - 16, 1) * s_up + pltpu.roll(x, 16, 1) * s_dn


def _inproj_kernel(*refs, rope, emit_cache):
    (x_ref, mod_ref, g_ref, w_ref, gq_ref, gk_ref, bd512_ref, bd128_ref), refs = refs[:8], refs[8:]
    if rope:
        (c_ref, su_ref, sd_ref), refs = refs[:3], refs[3:]
    qa_o, kta_o, vta_o, qb_o, ktb_o, vb_o = refs[:6]
    if emit_cache:
        ka_c, va_c, kb_c, vb_c = refs[6:10]

    mod = mod_ref[0]
    wa = HEADS_A * HEAD_DIM
    wkv = KV_HEADS_A * HEAD_DIM
    wb = HEADS_B * 2 * HEAD_DIM
    o_ka, o_va, o_qb = wa, wa + wkv, wa + 2 * wkv
    o_kb, o_vb = o_qb + wb, o_qb + 2 * wb
    scale = HEAD_DIM ** -0.5 * math.log2(math.e)

    x = x_ref[...]
    h = (_rms(x, g_ref[...]) * (1.0 + mod[1:2]) + mod[0:1]).astype(BF16)
    if rope:
        tabs = (c_ref[...], su_ref[...], sd_ref[...])

    qa = _dot(h, w_ref[:, 0:wa])
    qa = qa * lax.rsqrt(_seg_sumsq(qa, bd512_ref[...]) * (1.0 / HEAD_DIM) + EPS) * gq_ref[...]
    if rope:
        qa = _rope(qa, *tabs)
    qa_o[...] = (qa * scale).astype(BF16)

    kv = _dot(h, w_ref[:, o_ka:o_qb])
    ka, va = kv[:, 0:wkv], kv[:, wkv:2 * wkv]
    ka = ka * lax.rsqrt(_seg_sumsq(ka, bd128_ref[...]) * (1.0 / HEAD_DIM) + EPS) * gk_ref[...]

    def put_feat(val, out_bf16, out_f32):
        rows = out_bf16.shape[2]
        for s in range(out_bf16.shape[0]):
            t = val[s * rows:(s + 1) * rows].T
            if out_f32 is not None:
                out_f32[s] = t
            out_bf16[s] = t.astype(BF16)

    if rope:
        put_feat(_rope(ka, *tabs), kta_o, None)
    else:
        put_feat(ka, kta_o, ka_c if emit_cache else None)
    put_feat(va, vta_o, va_c if emit_cache else None)

    qb = _dot(h, w_ref[:, o_qb:o_kb])
    if rope:
        qb = _rope(qb, *tabs)
    qb_o[...] = (qb * scale).astype(BF16)

    kb = _dot(h, w_ref[:, o_kb:o_vb])
    if rope:
        put_feat(_rope(kb, *tabs), ktb_o, None)
    else:
        put_feat(kb, ktb_o, kb_c if emit_cache else None)

    vb = _dot(h, w_ref[:, o_vb:o_vb + wb])
    if emit_cache:
        for hd in range(HEADS_B):
            vb_c[pl.ds(hd, vb.shape[0], stride=HEADS_B), :] = vb[:, hd * 128:(hd + 1) * 128]
    vb_o[...] = vb.astype(BF16)


def _blockdiag_ones(width):
    g = np.arange(width) // HEAD_DIM
    return jnp.asarray((g[:, None] == g[None, :]).astype(np.float32), dtype=BF16)


def _inproj(x2, mod, set_rows, g_attn, w_in, gq_t, gk_t, rope_tabs, emit_cache):
    t, d = x2.shape
    tm = ROW_TILE
    assert t % tm == 0 and (tm % set_rows == 0 or set_rows % tm == 0)
    tiles_per_set = max(1, set_rows // tm)
    sets_per_tile = max(1, tm // set_rows)
    rope = rope_tabs is not None
    assert not (rope and emit_cache), "cached keys are the position-free ones"
    nq = N_QKV
    row = lambda i: (i, 0)
    const = lambda i: (0, 0)
    in_specs = [pl.BlockSpec((tm, d), row),
                pl.BlockSpec((1, 8, d), _mod_index(mod, tiles_per_set)),
                pl.BlockSpec((1, d), const),
                pl.BlockSpec((d, nq), const, pipeline_mode=pl.Buffered(1)),
                pl.BlockSpec((1, 512), const),
                pl.BlockSpec((1, 128), const),
                pl.BlockSpec((512, 512), const),
                pl.BlockSpec((128, 128), const)]
    args = [x2, mod, g_attn, w_in, gq_t, gk_t, _blockdiag_ones(512), _blockdiag_ones(128)]
    if rope:
        in_specs += [pl.BlockSpec((tm, LANES), lambda i: (i % tiles_per_set, 0))] * 3
        args += list(rope_tabs)
    n_sets = t // set_rows
    wkv, wb = KV_HEADS_A * HEAD_DIM, HEADS_B * 2 * HEAD_DIM
    outs = [("tok", 512, BF16), ("feat", wkv, BF16), ("feat", wkv, BF16),
            ("tok", 512, BF16), ("feat", wb, BF16), ("tok", wb, BF16)]
    if emit_cache:
        outs += [("feat", wkv, F32), ("feat", wkv, F32), ("feat", wb, F32), ("tokhead", wb, F32)]
    feat = lambda i: (i // tiles_per_set, 0, i % tiles_per_set)
    feat_rows = min(tm, set_rows)

    def out_block(kind, w):
        if kind == "tok":
            return pl.BlockSpec((tm, w), row)
        if kind == "tokhead":
            return pl.BlockSpec((tm * HEADS_B, w // HEADS_B), row)
        return pl.BlockSpec((sets_per_tile, w, feat_rows), feat)

    def out_array(kind, w, dt):
        shape = {"tok": (t, w), "tokhead": (t * HEADS_B, w // HEADS_B), "feat": (n_sets, w, set_rows)}[kind]
        return jax.ShapeDtypeStruct(shape, dt)

    return pl.pallas_call(
        functools.partial(_inproj_kernel, rope=rope, emit_cache=emit_cache),
        grid=(t // tm,),
        in_specs=in_specs,
        out_specs=[out_block(kind, w) for kind, w, _ in outs],
        out_shape=[out_array(kind, w, dt) for kind, w, dt in outs],
        name="inproj_rope" if rope else "inproj",
        compiler_params=_cparams(1),
    )(*args)


def _exp_parts(scores):
    m = scores[0].max(axis=-1, keepdims=True)
    for s in scores[1:]:
        m = jnp.maximum(m, s.max(axis=-1, keepdims=True))
    return [jnp.exp2(s - m) for s in scores]


def _row_sum(parts):
    l = parts[0].sum(axis=-1, keepdims=True)
    for e in parts[1:]:
        l = l + e.sum(axis=-1, keepdims=True)
    return l


def _pad_rows(x, first, ones_row=False):
    if ones_row:
        z = jnp.where(lax.broadcasted_iota(jnp.int32, x.shape, 0) == 0, 1.0, 0.0).astype(x.dtype)
    else:
        z = jnp.zeros_like(x)
    return jnp.concatenate([x, z] if first else [z, x], axis=0)


def _gqa_kernel(*refs, n_src, mxu_sums):
    q_ref = refs[0]
    k_refs = refs[1:1 + n_src]
    v_refs = refs[1 + n_src:1 + 2 * n_src]
    o_ref = refs[1 + 2 * n_src]
    sets_here = k_refs[0].shape[0]
    tq = q_ref.shape[0] // sets_here
    lane = lax.broadcasted_iota(jnp.int32, (2 * tq, LANES), 1)
    lo_half = lane < HEAD_DIM
    for s in range(sets_here):
        rows = slice(s * tq, (s + 1) * tq)
        for g in range(KV_HEADS_A):
            c0 = g * 256
            f0 = g * HEAD_DIM
            q = jnp.concatenate([q_ref[rows, c0:c0 + 128], q_ref[rows, c0 + 128:c0 + 256]], axis=0)
            kts = [k[s, f0:f0 + HEAD_DIM, :].astype(BF16) for k in k_refs]
            vts = [v[s, f0:f0 + HEAD_DIM, :].astype(BF16) for v in v_refs]
            e_lo = _exp_parts([_dot(q, _pad_rows(kt, True)) for kt in kts])
            e_hi = _exp_parts([_dot(q, _pad_rows(kt, False)) for kt in kts])
            o_lo = o_hi = None
            for e_part, vt in zip(e_lo, vts):
                t = _dot_nt(e_part.astype(BF16), _pad_rows(vt, True, ones_row=mxu_sums))
                o_lo = t if o_lo is None else o_lo + t
            for e_part, vt in zip(e_hi, vts):
                t = _dot_nt(e_part.astype(BF16), _pad_rows(vt, False, ones_row=mxu_sums))
                o_hi = t if o_hi is None else o_hi + t
            if mxu_sums:
                l_lo, l_hi = o_lo[:, HEAD_DIM:HEAD_DIM + 1], o_hi[:, 0:1]
                o = jnp.where(lo_half, o_lo * (1.0 / l_lo), o_hi * (1.0 / l_hi))
            else:
                o = (o_lo + o_hi) * jnp.where(lo_half, 1.0 / _row_sum(e_lo), 1.0 / _row_sum(e_hi))
            o_ref[rows, c0:c0 + 128] = o[0:tq].astype(BF16)
            o_ref[rows, c0 + 128:c0 + 256] = o[tq:2 * tq].astype(BF16)


def _diff_kernel(*refs, n_src, lam_init, stack_maps):
    q_ref = refs[0]
    k_refs = refs[1:1 + n_src]
    v_refs = refs[1 + n_src:1 + 2 * n_src]
    lq1, lk1, lq2, lk2, gd_ref, o_ref = refs[1 + 2 * n_src:]
    lam_all = (jnp.exp(jnp.sum(lq1[...] * lk1[...], axis=-1, keepdims=True))
               - jnp.exp(jnp.sum(lq2[...] * lk2[...], axis=-1, keepdims=True)) + lam_init)
    sets_here = k_refs[0].shape[0]
    tq = q_ref.shape[0] // sets_here
    for s in range(sets_here):
        rows = slice(s * tq, (s + 1) * tq)
        for j in range(HEADS_B):
            lam = lam_all[j:j + 1, :]
            q = q_ref[rows, j * 128:(j + 1) * 128]
            k0s = [_pad_rows(k[s, j * 128:j * 128 + HEAD_DIM, :].astype(BF16), True) for k in k_refs]
            k1s = [_pad_rows(k[s, j * 128 + HEAD_DIM:(j + 1) * 128, :].astype(BF16), False) for k in k_refs]
            e0 = _exp_parts([_dot(q, k0) for k0 in k0s])
            e1 = _exp_parts([_dot(q, k1) for k1 in k1s])
            o0 = o1 = None
            for a0, a1, v in zip(e0, e1, v_refs):
                if v.shape[2] == LANES:
                    val = v[s, pl.ds(j, v.shape[1] // HEADS_B, stride=HEADS_B), :].astype(BF16)
                else:
                    val = v[s, :, j * 128:(j + 1) * 128]
                if stack_maps:
                    t = _dot(jnp.concatenate([a0.astype(BF16), a1.astype(BF16)], axis=0), val)
                    t0, t1 = t[0:tq], t[tq:2 * tq]
                else:
                    t0, t1 = _dot(a0.astype(BF16), val), _dot(a1.astype(BF16), val)
                o0 = t0 if o0 is None else o0 + t0
                o1 = t1 if o1 is None else o1 + t1
            o = o0 * (1.0 / _row_sum(e0)) - o1 * (lam / _row_sum(e1))
            o = _rms(o, gd_ref[...]) * (1.0 - lam_init)
            o_ref[rows, j * 128:(j + 1) * 128] = o.astype(BF16)


def _both_mixers_kernel(*refs, n_src, lam_init, stack_maps, mxu_sums):
    per = 1 + 2 * n_src
    ya_o, yb_o = refs[-2:]
    _gqa_kernel(*refs[:per], ya_o, n_src=n_src, mxu_sums=mxu_sums)
    _diff_kernel(*refs[per:2 * per + 5], yb_o, n_src=n_src, lam_init=lam_init, stack_maps=stack_maps)


def _both_mixers(qa, ks_a, vs_a, qb, ks_b, vs_b, lam_refs, set_rows, tq, sets_per_step, lam_init):
    t = qa.shape[0]
    n_sets = t // set_rows
    q_tiles = set_rows // tq
    assert sets_per_step == 1 or q_tiles == 1
    rows = sets_per_step * tq
    tile = lambda b, i: (b * q_tiles + i, 0)
    kv = lambda a: pl.BlockSpec((sets_per_step,) + a.shape[1:], lambda b, i: (b, 0, 0))
    in_specs = ([pl.BlockSpec((rows, qa.shape[1]), tile)] + [kv(a) for a in list(ks_a) + list(vs_a)]
                + [pl.BlockSpec((rows, qb.shape[1]), tile)] + [kv(a) for a in list(ks_b) + list(vs_b)]
                + [pl.BlockSpec(e.shape, lambda b, i: (0, 0)) for e in lam_refs])
    return pl.pallas_call(
        functools.partial(_both_mixers_kernel, n_src=len(ks_a), lam_init=lam_init,
                          stack_maps=len(ks_a) == 1, mxu_sums=len(ks_a) > 1),
        grid=(n_sets // sets_per_step, q_tiles),
        in_specs=in_specs,
        out_specs=[pl.BlockSpec((rows, qa.shape[1]), tile), pl.BlockSpec((rows, qb.shape[1]), tile)],
        out_shape=[jax.ShapeDtypeStruct(qa.shape, BF16), jax.ShapeDtypeStruct(qb.shape, BF16)],
        name=f"mixers_{len(ks_a)}src",
        compiler_params=_cparams(2),
    )(qa, *ks_a, *vs_a, qb, *ks_b, *vs_b, *lam_refs)


def _post_kernel(x_ref, ya_ref, yb_ref, mod_ref, g1_ref, g2_ref, win_ref, wba_ref, wbb_ref, wo_ref, wr2_ref,
                 x1_o, h2_o, aff_o):
    d = x_ref.shape[1]
    g0 = win_ref.shape[1] - 2 * d
    mod = mod_ref[0]
    for r0 in range(0, x_ref.shape[0], CHAIN_ROWS):
        rows = slice(r0, r0 + CHAIN_ROWS)
        x = x_ref[rows, :]
        h = (_rms(x, g1_ref[...]) * (1.0 + mod[1:2]) + mod[0:1]).astype(BF16)
        ga = jax.nn.sigmoid(_dot(h, win_ref[:, g0:g0 + d]))
        merged = ga * _dot(ya_ref[rows, :], wba_ref[...])
        gb = jax.nn.sigmoid(_dot(h, win_ref[:, g0 + d:g0 + 2 * d]))
        merged = merged + gb * _dot(yb_ref[rows, :], wbb_ref[...])
        m = _dot(merged.astype(BF16), wo_ref[...])
        x1 = x + mod[2:3] * m
        x1_o[rows, :] = x1
        h2 = _rms(x1, g2_ref[...]) * (1.0 + mod[4:5]) + mod[3:4]
        h2_o[rows, :] = h2.astype(BF16)
        h2_hi, h2_lo = _split(h2)
        both = _dot(h2_hi, wr2_ref[...])
        logits = both[:, 0:LANES] + both[:, LANES:2 * LANES] + _dot(h2_lo, wr2_ref[:, 0:LANES])
        lane = lax.broadcasted_iota(jnp.int32, logits.shape, 1)
        logits = jnp.where(lane < N_EXPERTS, logits, NEG_BIG)
        e = jnp.exp(logits - logits.max(axis=-1, keepdims=True))
        aff_o[rows, :] = e / e.sum(axis=-1, keepdims=True)


def _post(x2, ya, yb, mod, set_rows, g1, g2, w_in, w_ba, w_bb, w_out, w_router2):
    t, d = x2.shape
    tm = POST_ROWS
    tiles_per_set = max(1, set_rows // tm)
    assert t % tm == 0 and tm % CHAIN_ROWS == 0 and (mod.shape[0] == 1 or set_rows % tm == 0)
    row = lambda i: (i, 0)
    const = lambda i: (0, 0)
    once = pl.Buffered(1)
    half = w_in.shape[1] // 2
    assert w_in.shape[1] == 2 * half and half % LANES == 0 and half >= 2 * d
    return pl.pallas_call(
        _post_kernel,
        grid=(t // tm,),
        in_specs=[pl.BlockSpec((tm, d), row),
                  pl.BlockSpec((tm, 512), row),
                  pl.BlockSpec((tm, 512), row),
                  pl.BlockSpec((1, 8, d), _mod_index(mod, tiles_per_set)),
                  pl.BlockSpec((1, d), const),
                  pl.BlockSpec((1, d), const),
                  pl.BlockSpec((d, half), lambda i: (0, 1), pipeline_mode=once),
                  pl.BlockSpec(w_ba.shape, const, pipeline_mode=once),
                  pl.BlockSpec(w_bb.shape, const, pipeline_mode=once),
                  pl.BlockSpec(w_out.shape, const, pipeline_mode=once),
                  pl.BlockSpec(w_router2.shape, const, pipeline_mode=once)],
        out_specs=[pl.BlockSpec((tm, d), row), pl.BlockSpec((tm, d), row), pl.BlockSpec((tm, LANES), row)],
        out_shape=[jax.ShapeDtypeStruct((t, d), F32), jax.ShapeDtypeStruct((t, d), BF16),
                   jax.ShapeDtypeStruct((t, LANES), F32)],
        name="post_attn",
        compiler_params=_cparams(1),
    )(x2, ya, yb, mod, g1, g2, w_in, w_ba, w_bb, w_out, w_router2)


def _select_kernel(aff_ref, slot_o, slot_t_o, affb_o, *, cap):
    n = slot_o.shape[1]
    aff = aff_ref[0:n, :]
    for k in range(1, SETS_PER_PACK):
        aff = aff + pltpu.roll(aff_ref[k * n:(k + 1) * n, :], N_EXPERTS * k, 1)
    capf = float(cap)

    def enough(cand):
        return jnp.sum(jnp.where(aff >= cand, 1.0, 0.0), axis=0, keepdims=True) >= capf

    pw = jnp.ones((1, LANES), F32)
    for k in (64, 32, 16, 8, 4, 2, 1):
        pw = jnp.where(enough(pw * 2.0 ** -(k - 1)), pw, pw * 2.0 ** -k)

    def mantissa_step(_, carry):
        thr, step = carry
        cand = thr + step
        return jnp.where(enough(cand), cand, thr), step * 0.5

    thr, _ = lax.fori_loop(0, 23, mantissa_step, (pw, pw * 0.5))
    above = aff > thr
    tied = aff == thr
    need = capf - jnp.sum(jnp.where(above, 1.0, 0.0), axis=0, keepdims=True)
    r_i = lax.broadcasted_iota(jnp.int32, (n, n), 0)
    c_i = lax.broadcasted_iota(jnp.int32, (n, n), 1)
    before = jnp.where(c_i < r_i, 1.0, 0.0).astype(BF16)
    tie_rank = _dot(before, jnp.where(tied, 1.0, 0.0).astype(BF16))
    sel = above | (tied & (tie_rank < need))
    slot = _dot(before, jnp.where(sel, 1.0, 0.0).astype(BF16))
    slot = jnp.where(sel, slot, NOT_SELECTED)
    slot_o[0] = slot.astype(BF16)
    slot_t_o[0] = slot.T
    affb_o[0] = aff.astype(BF16)


def _dispatch_kernel(slot_t_ref, h2_ref, xe_o, *, cap):
    n = slot_t_ref.shape[2]
    g = h2_ref.shape[0] // n
    per = min(N_EXPERTS, DISPATCH_ROWS // cap)
    slot_iota = lax.broadcasted_iota(jnp.int32, (cap, n), 0).astype(F32)
    for k in range(g):
        h2 = h2_ref[k * n:(k + 1) * n, :]
        slot_t = slot_t_ref[0, k * N_EXPERTS:(k + 1) * N_EXPERTS, :]
        for e0 in range(0, N_EXPERTS, per):
            onehot = jnp.concatenate(
                [jnp.where(slot_t[e:e + 1, :] == slot_iota, 1.0, 0.0) for e in range(e0, e0 + per)],
                axis=0).astype(BF16)
            rows = _dot(onehot, h2).astype(BF16)
            for j in range(per):
                xe_o[e0 + j, k * cap:(k + 1) * cap, :] = rows[j * cap:(j + 1) * cap]


def _route(aff, h2, n_sets, set_rows):
    t, d = h2.shape
    n = set_rows
    cap = CAPACITY_FACTOR * n // N_EXPERTS
    assert cap & (cap - 1) == 0 and cap % 16 == 0 and DISPATCH_ROWS % cap == 0
    assert cap <= SLOT_RADIX, "slot ids must stay exact in bf16 and below the radix of the slot/gate packing"
    n_packs = -(-n_sets // SETS_PER_PACK)
    aff = jnp.pad(aff, ((0, (n_packs * SETS_PER_PACK - n_sets) * n), (0, 0)))
    pack_blk = lambda p: (p, 0, 0)
    slot, slot_t, affb = pl.pallas_call(
        functools.partial(_select_kernel, cap=cap),
        grid=(n_packs,),
        in_specs=[pl.BlockSpec((SETS_PER_PACK * n, LANES), lambda p: (p, 0))],
        out_specs=[pl.BlockSpec((1, n, LANES), pack_blk), pl.BlockSpec((1, LANES, n), pack_blk),
                   pl.BlockSpec((1, n, LANES), pack_blk)],
        out_shape=[jax.ShapeDtypeStruct((n_packs, n, LANES), BF16),
                   jax.ShapeDtypeStruct((n_packs, LANES, n), F32),
                   jax.ShapeDtypeStruct((n_packs, n, LANES), BF16)],
        name=f"select_cap{cap}",
        compiler_params=_cparams(1),
    )(aff)
    g = _sets_per_step(n_sets, n)
    per_pack = SETS_PER_PACK // g
    xe = pl.pallas_call(
        functools.partial(_dispatch_kernel, cap=cap),
        grid=(n_sets // g,),
        in_specs=[pl.BlockSpec((1, g * N_EXPERTS, n), lambda s: (s // per_pack, s % per_pack, 0)),
                  pl.BlockSpec((g * n, d), lambda s: (s, 0))],
        out_specs=pl.BlockSpec((N_EXPERTS, g * cap, d), lambda s: (0, s, 0)),
        out_shape=jax.ShapeDtypeStruct((N_EXPERTS, n_sets * cap, d), BF16),
        name=f"dispatch_cap{cap}",
        compiler_params=_cparams(1),
    )(slot_t, h2)
    return xe, (slot, affb)


def _expert_kernel(*refs, n_groups):
    x_refs = refs[:n_groups]
    wg_ref, wu_ref, wd_ref = refs[n_groups:n_groups + 3]
    o_refs = refs[n_groups + 3:2 * n_groups + 3]
    wg_s, wu_s, wd_s = refs[2 * n_groups + 3:]
    wg_s[...] = wg_ref[0].astype(BF16)
    wu_s[...] = wu_ref[0].astype(BF16)
    wd_s[...] = wd_ref[0].astype(BF16)
    sub = EXPERT_ROWS
    for x_ref, o_ref in zip(x_refs, o_refs):
        for r0 in range(0, x_ref.shape[1], sub):
            x = x_ref[0, r0:r0 + sub, :]
            a = _dot(x, wg_s[...])
            u = _dot(x, wu_s[...])
            hmid = (a * jax.nn.sigmoid(a) * u).astype(BF16)
            o_ref[0, r0:r0 + sub, :] = _dot(hmid, wd_s[...]).astype(BF16)


def _experts(xes, w_gate, w_up, w_down):
    e, d, f = w_gate.shape
    blk = lambda i: (i, 0, 0)
    return pl.pallas_call(
        functools.partial(_expert_kernel, n_groups=len(xes)),
        grid=(e,),
        in_specs=[pl.BlockSpec((1, x.shape[1], d), blk) for x in xes]
        + [pl.BlockSpec((1, d, f), blk), pl.BlockSpec((1, d, f), blk), pl.BlockSpec((1, f, d), blk)],
        out_specs=[pl.BlockSpec((1, x.shape[1], d), blk) for x in xes],
        out_shape=[jax.ShapeDtypeStruct(x.shape, BF16) for x in xes],
        scratch_shapes=[pltpu.VMEM((d, f), BF16), pltpu.VMEM((d, f), BF16), pltpu.VMEM((f, d), BF16)],
        name="experts",
        compiler_params=_cparams(1),
    )(*xes, w_gate, w_up, w_down)


def _combine_kernel(slot_ref, affb_ref, ye_ref, x1_ref, mod_ref, gf_ref, y_o, *, cap):
    e, gcap, d = ye_ref.shape
    g = gcap // cap
    n = x1_ref.shape[0] // g
    shift = cap.bit_length() - 1
    both = jnp.concatenate([slot_ref[0], affb_ref[0]], axis=1)
    want = (lax.broadcasted_iota(jnp.int32, (n, DISPATCH_ROWS), 1) & (cap - 1)).astype(F32) * SLOT_RADIX
    for k in range(g):
        rows = slice(k * n, (k + 1) * n)
        first_lane = ((pl.program_id(0) * g + k) % SETS_PER_PACK) * N_EXPERTS
        pieces = []
        for c0 in range(0, e * cap, DISPATCH_ROWS):
            src = lax.broadcasted_iota(jnp.int32, (2 * LANES, DISPATCH_ROWS), 0)
            col = lax.broadcasted_iota(jnp.int32, (2 * LANES, DISPATCH_ROWS), 1) + c0
            lane_of_col = first_lane + lax.shift_right_logical(col, shift)
            spread = jnp.where(src == lane_of_col, SLOT_RADIX,
                               jnp.where(src == lane_of_col + LANES, 1.0, 0.0)).astype(BF16)
            rest = _dot(both, spread) - want
            pieces.append(jnp.where(rest >= 0.0, jnp.where(rest <= 1.0, rest, 0.0), 0.0).astype(BF16))
        gt = pieces[0] if len(pieces) == 1 else jnp.concatenate(pieces, axis=1)
        ye = ye_ref[:, k * cap:(k + 1) * cap, :].reshape(e * cap, d)
        moe = _dot(gt, ye)
        x = x1_ref[rows, :] + mod_ref[0][5:6] * moe
        y_o[rows, :] = _rms(x, gf_ref[...])


def _sets_per_step(n_sets, set_rows):
    g = SETS_PER_PACK if set_rows * SETS_PER_PACK <= SHARED_STEP_ROWS else 1
    return g if n_sets % g == 0 else 1


def _combine(packs, ye, x1, mod, n_sets, set_rows, g_final):
    slot, affb = packs
    t, d = x1.shape
    cap = ye.shape[1] // n_sets
    g = _sets_per_step(n_sets, set_rows) if mod.shape[0] == 1 else 1
    tm = min(ROW_TILE, set_rows) if g == 1 else g * set_rows
    tiles = max(1, set_rows // tm)
    per_pack = SETS_PER_PACK // g
    pack_rows = pl.BlockSpec((1, tm // g, LANES), lambda s, i: (s // per_pack, i, 0))
    return pl.pallas_call(
        functools.partial(_combine_kernel, cap=cap),
        grid=(n_sets // g, tiles),
        in_specs=[pack_rows, pack_rows,
                  pl.BlockSpec((N_EXPERTS, g * cap, d), lambda s, i: (0, s, 0)),
                  pl.BlockSpec((tm, d), lambda s, i: (s * tiles + i, 0)),
                  pl.BlockSpec((1, 8, d), (lambda s, i: (s, 0, 0)) if mod.shape[0] > 1 else (lambda s, i: (0, 0, 0))),
                  pl.BlockSpec((1, d), lambda s, i: (0, 0))],
        out_specs=pl.BlockSpec((tm, d), lambda s, i: (s * tiles + i, 0)),
        out_shape=jax.ShapeDtypeStruct((t, d), F32),
        name=f"combine_cap{cap}",
        compiler_params=_cparams(2),
    )(slot, affb, ye, x1, mod, g_final)


def _rope_tables(n_tokens):
    n_rows = n_tokens // GRID_W
    rowp = np.repeat(np.arange(n_rows), GRID_W).astype(np.float32)
    colp = np.tile(np.arange(GRID_W), n_rows).astype(np.float32)
    quarter = HEAD_DIM // 4
    freqs = (np.float32(ROPE_THETA) ** (-np.arange(quarter, dtype=np.float32) / np.float32(quarter))).astype(np.float32)
    ang = np.stack([rowp[:, None] * freqs, colp[:, None] * freqs], axis=1)
    cos = np.cos(ang.astype(np.float64)).astype(np.float32)
    sin = np.sin(ang.astype(np.float64)).astype(np.float32)
    zero = np.zeros_like(sin)
    c = np.stack([cos, cos], axis=2).reshape(n_tokens, HEAD_DIM)
    s_up = np.stack([-sin, zero], axis=2).reshape(n_tokens, HEAD_DIM)
    s_dn = np.stack([zero, sin], axis=2).reshape(n_tokens, HEAD_DIM)
    return tuple(jnp.asarray(np.tile(t, (1, LANES // HEAD_DIM))) for t in (c, s_up, s_dn))


def _mix_and_route(x2, n_sets, set_rows, mod, lw, rope_tabs, cache, emit_cache, lam_init, tq):
    (g_attn, g_ffn, w_in, gq_t, gk_t, lq1, lk1, lq2, lk2, g_diff, w_ba, w_bb, w_out, w_router2) = lw
    outs = _inproj(x2, mod, set_rows, g_attn, w_in, gq_t, gk_t, rope_tabs, emit_cache)
    qa, kta, vta, qb, ktb, vb16 = outs[:6]
    vb16 = vb16.reshape(n_sets, set_rows, vb16.shape[1])
    ks_a, vs_a, ks_b, vs_b = [kta], [vta], [ktb], [vb16]
    if cache is not None:
        ckta, cvta, cktb, cvb = cache
        ks_a, vs_a, ks_b, vs_b = [ckta, kta], [cvta, vta], [cktb, ktb], [cvb, vb16]
    per_step = CONTEXT_SETS_PER_STEP if (tq == set_rows and n_sets % CONTEXT_SETS_PER_STEP == 0) else 1
    ya, yb = _both_mixers(qa, ks_a, vs_a, qb, ks_b, vs_b, (lq1, lk1, lq2, lk2, g_diff), set_rows, tq, per_step,
                          lam_init)
    x1, h2, aff = _post(x2, ya, yb, mod, set_rows, g_attn, g_ffn, w_in, w_ba, w_bb, w_out, w_router2)
    xe, packs = _route(aff, h2, n_sets, set_rows)
    return x1, xe, packs, outs[6:]


def kernel(x_prompt, x_sample, cache_attn_k, cache_attn_v, cache_diff_k, cache_diff_v, c, c_ctx, w_mod, b_mod,
           g_attn_norm, g_ffn_norm, w_in, g_q_norm, g_k_norm, lambda_q1, lambda_k1, lambda_q2, lambda_k2,
           g_diff_norm, w_branch_a, w_branch_b, w_out, w_router, w_exp_gate, w_exp_up, w_exp_down, g_final):
    batch, seq, d = x_prompt.shape
    dec_batch, dec_seq, _ = x_sample.shape
    depth = w_in.shape[0]
    assert depth == 1, "the final norm is fused into the layer's combine step"
    past = cache_attn_k.shape[2]
    assert w_in.shape[2] == N_QKV + 2 * d

    xp = x_prompt.reshape(batch * seq, d)
    xs = x_sample.reshape(dec_batch * dec_seq, d)
    rope_tabs = _rope_tables(dec_seq)
    c_rows = jnp.concatenate([c, c_ctx[None, :], jnp.zeros((16 - dec_batch - 1, d), F32)], axis=0)
    yp = ys = None
    caches = []
    for l in range(depth):
        lam_init = 0.8 - 0.6 * math.exp(-0.3 * l)
        mod6 = _modulation(c_rows, w_mod[l], b_mod[l])
        mod = jnp.pad(jnp.transpose(mod6, (1, 0, 2)), ((0, 0), (0, 2), (0, 0)))
        mod_lat, mod_ctx = mod[:dec_batch], mod[dec_batch:dec_batch + 1]
        w_in16, w_ba16, w_bb16, w_out16, w_router2 = _round_weights(
            [w_in[l], w_branch_a[l], w_branch_b[l], w_out[l]],
            jnp.pad(w_router[l], ((0, 0), (0, LANES - N_EXPERTS))))
        lw = (g_attn_norm[l][None, :], g_ffn_norm[l][None, :], w_in16,
              jnp.tile(g_q_norm[l], HEADS_A)[None, :], jnp.tile(g_k_norm[l], KV_HEADS_A)[None, :],
              lambda_q1[l], lambda_k1[l], lambda_q2[l], lambda_k2[l], g_diff_norm[l][None, :],
              w_ba16, w_bb16, w_out16, w_router2)
        x1p, xe_p, packs_p, cache_out = _mix_and_route(xp, batch, seq, mod_ctx, lw, None, None, True, lam_init, seq)
        caches.append(cache_out)
        feat_major = lambda a: jnp.moveaxis(a[:, l], 1, -1).reshape(dec_batch, -1, past)
        cache_l = (feat_major(cache_attn_k), feat_major(cache_attn_v), feat_major(cache_diff_k),
                   cache_diff_v[:, l].reshape(dec_batch, past * HEADS_B, -1))
        x1s, xe_s, packs_s, _ = _mix_and_route(xs, dec_batch, dec_seq, mod_lat, lw, rope_tabs, cache_l, False,
                                               lam_init, LATENT_Q_TILE)
        ye_p, ye_s = _experts([xe_p, xe_s], w_exp_gate[l], w_exp_up[l], w_exp_down[l])
        yp = _combine(packs_p, ye_p, x1p, mod_ctx, batch, seq, g_final[None, :])
        ys = _combine(packs_s, ye_s, x1s, mod_lat, dec_batch, dec_seq, g_final[None, :])
    y_prompt = yp.reshape(batch, seq, d)
    y_sample = ys.reshape(dec_batch, dec_seq, d)
    tok_major = lambda a, dims: jnp.moveaxis(a.reshape((batch,) + dims + (seq,)), -1, 1)
    new_attn_k = jnp.stack([tok_major(cc[0], (KV_HEADS_A, HEAD_DIM)) for cc in caches], axis=1)
    new_attn_v = jnp.stack([tok_major(cc[1], (KV_HEADS_A, HEAD_DIM)) for cc in caches], axis=1)
    new_diff_k = jnp.stack([tok_major(cc[2], (HEADS_B, 2, HEAD_DIM)) for cc in caches], axis=1)
    new_diff_v = jnp.stack([cc[3].reshape(batch, seq, HEADS_B, 2 * HEAD_DIM) for cc in caches], axis=1)
    return (y_prompt, y_sample, new_attn_k, new_attn_v, new_diff_k, new_diff_v)
```

```python
import functools
import math

import numpy as np
import jax
import jax.numpy as jnp
from jax import lax
from jax.experimental import pallas as pl
from jax.experimental.pallas import tpu as pltpu

F32 = jnp.float32
BF16 = jnp.bfloat16

HEAD_DIM = 64
HEADS_A = 8
KV_HEADS_A = 2
HEADS_B = 4
N_QKV = HEADS_A * HEAD_DIM + 2 * KV_HEADS_A * HEAD_DIM + 3 * HEADS_B * 2 * HEAD_DIM
N_EXPERTS = 16
CAPACITY_FACTOR = 2
GRID_W = 64
ROPE_THETA = 10000.0
EPS = 1e-6
LANES = 128
ROW_TILE = 1024
POST_ROWS = 1024
CHAIN_ROWS = 512
EXPERT_ROWS = 256
LATENT_Q_TILE = 512
CONTEXT_SETS_PER_STEP = 4
SETS_PER_PACK = LANES // N_EXPERTS
SHARED_STEP_ROWS = 2048
DISPATCH_ROWS = 512
NEG_BIG = -1e30
NOT_SELECTED = -1.0
SLOT_RADIX = 256.0
VMEM_LIMIT = 56 * 1024 * 1024
ROUND_STEPS = 8


def _cparams(n_axes):
    return pltpu.CompilerParams(dimension_semantics=("arbitrary",) * n_axes,
                                vmem_limit_bytes=VMEM_LIMIT)


def _dot(a, b):
    return jnp.dot(a, b, preferred_element_type=F32)


def _dot_nt(a, b):
    return lax.dot_general(a, b, (((1,), (1,)), ((), ())), preferred_element_type=F32)


def _split(a):
    hi = a.astype(BF16)
    lo = (a - hi.astype(F32)).astype(BF16)
    return hi, lo


def _dot3(a, b):
    m = a.shape[0]
    a_hi, a_lo = _split(a)
    b_hi, b_lo = _split(b)
    both = _dot(jnp.concatenate([a_hi, a_lo], axis=0), b_hi)
    return both[0:m] + both[m:2 * m] + _dot(a_hi, b_lo)


def _rms(x, g):
    return x * lax.rsqrt(jnp.mean(x * x, axis=-1, keepdims=True) + EPS) * g


def _mod_index(mod, tiles_per_set):
    if mod.shape[0] == 1:
        return lambda i: (0, 0, 0)
    return lambda i: (i // tiles_per_set, 0, 0)


def _mod_kernel(c_ref, w_ref, b_ref, o_ref):
    c = c_ref[...]
    a = c * jax.nn.sigmoid(c)
    o_ref[0] = _dot3(a, w_ref[...]) + b_ref[0]


def _modulation(c_rows, w_mod, b_mod):
    r, d = c_rows.shape
    return pl.pallas_call(
        _mod_kernel,
        grid=(6,),
        in_specs=[pl.BlockSpec((r, d), lambda j: (0, 0)),
                  pl.BlockSpec((d, d), lambda j: (0, j)),
                  pl.BlockSpec((1, 1, d), lambda j: (j, 0, 0))],
        out_specs=pl.BlockSpec((1, r, d), lambda j: (j, 0, 0)),
        out_shape=jax.ShapeDtypeStruct((6, r, d), F32),
        name="mod",
        compiler_params=_cparams(1),
    )(c_rows, w_mod, b_mod.reshape(6, 1, d))


def _round_weights_kernel(*refs):
    n = len(refs) // 2
    for src, out in zip(refs[:n - 1], refs[n:-1]):
        out[...] = src[...].astype(BF16)
    hi, lo = _split(refs[n - 1][...])
    refs[-1][...] = jnp.concatenate([hi, lo], axis=1)


def _round_weights(weights, w_router_p):
    steps = ROUND_STEPS
    arrays = list(weights) + [w_router_p]
    blk = lambda a, cols: pl.BlockSpec((a.shape[0] // steps, cols), lambda i: (i, 0))
    out_cols = [a.shape[1] for a in weights] + [2 * w_router_p.shape[1]]
    return pl.pallas_call(
        _round_weights_kernel,
        grid=(steps,),
        in_specs=[blk(a, a.shape[1]) for a in arrays],
        out_specs=[blk(a, c) for a, c in zip(arrays, out_cols)],
        out_shape=[jax.ShapeDtypeStruct((a.shape[0], c), BF16) for a, c in zip(arrays, out_cols)],
        name="round_weights",
        compiler_params=_cparams(1),
    )(*arrays)


def _seg_sumsq(x, ones_blockdiag):
    return _dot((x * x).astype(BF16), ones_blockdiag)


def _rope(x, c, s_up, s_dn):
    w = x.shape[1]
    reps = w // c.shape[1]
    if reps > 1:
        c, s_up, s_dn = (jnp.concatenate([t] * reps, axis=1) for t in (c, s_up, s_dn))
    return x * c + pltpu.roll(x, w - 16, 1) * s_up + pltpu.roll(x, 16, 1) * s_dn


def _inproj_kernel(*refs, rope, emit_cache):
    (x_ref, mod_ref, g_ref, w_ref, gq_ref, gk_ref, bd512_ref, bd128_ref), refs = refs[:8], refs[8:]
    if rope:
        (c_ref, su_ref, sd_ref), refs = refs[:3], refs[3:]
    qa_o, kta_o, vta_o, qb_o, ktb_o, vb_o = refs[:6]
    if emit_cache:
        ka_c, va_c, kb_c, vb_c = refs[6:10]

    mod = mod_ref[0]
    wa = HEADS_A * HEAD_DIM
    wkv = KV_HEADS_A * HEAD_DIM
    wb = HEADS_B * 2 * HEAD_DIM
    o_ka, o_va, o_qb = wa, wa + wkv, wa + 2 * wkv
    o_kb, o_vb = o_qb + wb, o_qb + 2 * wb
    scale = HEAD_DIM ** -0.5 * math.log2(math.e)

    x = x_ref[...]
    h = (_rms(x, g_ref[...]) * (1.0 + mod[1:2]) + mod[0:1]).astype(BF16)
    if rope:
        tabs = (c_ref[...], su_ref[...], sd_ref[...])

    qa = _dot(h, w_ref[:, 0:wa])
    qa = qa * lax.rsqrt(_seg_sumsq(qa, bd512_ref[...]) * (1.0 / HEAD_DIM) + EPS) * gq_ref[...]
    if rope:
        qa = _rope(qa, *tabs)
    qa_o[...] = (qa * scale).astype(BF16)

    kv = _dot(h, w_ref[:, o_ka:o_qb])
    ka, va = kv[:, 0:wkv], kv[:, wkv:2 * wkv]
    ka = ka * lax.rsqrt(_seg_sumsq(ka, bd128_ref[...]) * (1.0 / HEAD_DIM) + EPS) * gk_ref[...]

    def put_feat(val, out_bf16, out_f32):
        rows = out_bf16.shape[2]
        for s in range(out_bf16.shape[0]):
            t = val[s * rows:(s + 1) * rows].T
            if out_f32 is not None:
                out_f32[s] = t
            out_bf16[s] = t.astype(BF16)

    if rope:
        put_feat(_rope(ka, *tabs), kta_o, None)
    else:
        put_feat(ka, kta_o, ka_c if emit_cache else None)
    put_feat(va, vta_o, va_c if emit_cache else None)

    qb = _dot(h, w_ref[:, o_qb:o_kb])
    if rope:
        qb = _rope(qb, *tabs)
    qb_o[...] = (qb * scale).astype(BF16)

    kb = _dot(h, w_ref[:, o_kb:o_vb])
    if rope:
        put_feat(_rope(kb, *tabs), ktb_o, None)
    else:
        put_feat(kb, ktb_o, kb_c if emit_cache else None)

    vb = _dot(h, w_ref[:, o_vb:o_vb + wb])
    if emit_cache:
        for hd in range(HEADS_B):
            vb_c[pl.ds(hd, vb.shape[0], stride=HEADS_B), :] = vb[:, hd * 128:(hd + 1) * 128]
    vb_o[...] = vb.astype(BF16)


def _blockdiag_ones(width):
    g = np.arange(width) // HEAD_DIM
    return jnp.asarray((g[:, None] == g[None, :]).astype(np.float32), dtype=BF16)


def _inproj(x2, mod, set_rows, g_attn, w_in, gq_t, gk_t, rope_tabs, emit_cache):
    t, d = x2.shape
    tm = ROW_TILE
    assert t % tm == 0 and (tm % set_rows == 0 or set_rows % tm == 0)
    tiles_per_set = max(1, set_rows // tm)
    sets_per_tile = max(1, tm // set_rows)
    rope = rope_tabs is not None
    assert not (rope and emit_cache), "cached keys are the position-free ones"
    nq = N_QKV
    row = lambda i: (i, 0)
    const = lambda i: (0, 0)
    in_specs = [pl.BlockSpec((tm, d), row),
                pl.BlockSpec((1, 8, d), _mod_index(mod, tiles_per_set)),
                pl.BlockSpec((1, d), const),
                pl.BlockSpec((d, nq), const, pipeline_mode=pl.Buffered(1)),
                pl.BlockSpec((1, 512), const),
                pl.BlockSpec((1, 128), const),
                pl.BlockSpec((512, 512), const),
                pl.BlockSpec((128, 128), const)]
    args = [x2, mod, g_attn, w_in, gq_t, gk_t, _blockdiag_ones(512), _blockdiag_ones(128)]
    if rope:
        in_specs += [pl.BlockSpec((tm, LANES), lambda i: (i % tiles_per_set, 0))] * 3
        args += list(rope_tabs)
    n_sets = t // set_rows
    wkv, wb = KV_HEADS_A * HEAD_DIM, HEADS_B * 2 * HEAD_DIM
    outs = [("tok", 512, BF16), ("feat", wkv, BF16), ("feat", wkv, BF16),
            ("tok", 512, BF16), ("feat", wb, BF16), ("tok", wb, BF16)]
    if emit_cache:
        outs += [("feat", wkv, F32), ("feat", wkv, F32), ("feat", wb, F32), ("tokhead", wb, F32)]
    feat = lambda i: (i // tiles_per_set, 0, i % tiles_per_set)
    feat_rows = min(tm, set_rows)

    def out_block(kind, w):
        if kind == "tok":
            return pl.BlockSpec((tm, w), row)
        if kind == "tokhead":
            return pl.BlockSpec((tm * HEADS_B, w // HEADS_B), row)
        return pl.BlockSpec((sets_per_tile, w, feat_rows), feat)

    def out_array(kind, w, dt):
        shape = {"tok": (t, w), "tokhead": (t * HEADS_B, w // HEADS_B), "feat": (n_sets, w, set_rows)}[kind]
        return jax.ShapeDtypeStruct(shape, dt)

    return pl.pallas_call(
        functools.partial(_inproj_kernel, rope=rope, emit_cache=emit_cache),
        grid=(t // tm,),
        in_specs=in_specs,
        out_specs=[out_block(kind, w) for kind, w, _ in outs],
        out_shape=[out_array(kind, w, dt) for kind, w, dt in outs],
        name="inproj_rope" if rope else "inproj",
        compiler_params=_cparams(1),
    )(*args)


def _exp_parts(scores):
    m = scores[0].max(axis=-1, keepdims=True)
    for s in scores[1:]:
        m = jnp.maximum(m, s.max(axis=-1, keepdims=True))
    return [jnp.exp2(s - m) for s in scores]


def _row_sum(parts):
    l = parts[0].sum(axis=-1, keepdims=True)
    for e in parts[1:]:
        l = l + e.sum(axis=-1, keepdims=True)
    return l


def _pad_rows(x, first, ones_row=False):
    if ones_row:
        z = jnp.where(lax.broadcasted_iota(jnp.int32, x.shape, 0) == 0, 1.0, 0.0).astype(x.dtype)
    else:
        z = jnp.zeros_like(x)
    return jnp.concatenate([x, z] if first else [z, x], axis=0)


def _gqa_kernel(*refs, n_src, mxu_sums):
    q_ref = refs[0]
    k_refs = refs[1:1 + n_src]
    v_refs = refs[1 + n_src:1 + 2 * n_src]
    o_ref = refs[1 + 2 * n_src]
    sets_here = k_refs[0].shape[0]
    tq = q_ref.shape[0] // sets_here
    lane = lax.broadcasted_iota(jnp.int32, (2 * tq, LANES), 1)
    lo_half = lane < HEAD_DIM
    for s in range(sets_here):
        rows = slice(s * tq, (s + 1) * tq)
        for g in range(KV_HEADS_A):
            c0 = g * 256
            f0 = g * HEAD_DIM
            q = jnp.concatenate([q_ref[rows, c0:c0 + 128], q_ref[rows, c0 + 128:c0 + 256]], axis=0)
            kts = [k[s, f0:f0 + HEAD_DIM, :].astype(BF16) for k in k_refs]
            vts = [v[s, f0:f0 + HEAD_DIM, :].astype(BF16) for v in v_refs]
            e_lo = _exp_parts([_dot(q, _pad_rows(kt, True)) for kt in kts])
            e_hi = _exp_parts([_dot(q, _pad_rows(kt, False)) for kt in kts])
            o_lo = o_hi = None
            for e_part, vt in zip(e_lo, vts):
                t = _dot_nt(e_part.astype(BF16), _pad_rows(vt, True, ones_row=mxu_sums))
                o_lo = t if o_lo is None else o_lo + t
            for e_part, vt in zip(e_hi, vts):
                t = _dot_nt(e_part.astype(BF16), _pad_rows(vt, False, ones_row=mxu_sums))
                o_hi = t if o_hi is None else o_hi + t
            if mxu_sums:
                l_lo, l_hi = o_lo[:, HEAD_DIM:HEAD_DIM + 1], o_hi[:, 0:1]
                o = jnp.where(lo_half, o_lo * (1.0 / l_lo), o_hi * (1.0 / l_hi))
            else:
                o = (o_lo + o_hi) * jnp.where(lo_half, 1.0 / _row_sum(e_lo), 1.0 / _row_sum(e_hi))
            o_ref[rows, c0:c0 + 128] = o[0:tq].astype(BF16)
            o_ref[rows, c0 + 128:c0 + 256] = o[tq:2 * tq].astype(BF16)


def _diff_kernel(*refs, n_src, lam_init, stack_maps):
    q_ref = refs[0]
    k_refs = refs[1:1 + n_src]
    v_refs = refs[1 + n_src:1 + 2 * n_src]
    lq1, lk1, lq2, lk2, gd_ref, o_ref = refs[1 + 2 * n_src:]
    lam_all = (jnp.exp(jnp.sum(lq1[...] * lk1[...], axis=-1, keepdims=True))
               - jnp.exp(jnp.sum(lq2[...] * lk2[...], axis=-1, keepdims=True)) + lam_init)
    sets_here = k_refs[0].shape[0]
    tq = q_ref.shape[0] // sets_here
    for s in range(sets_here):
        rows = slice(s * tq, (s + 1) * tq)
        for j in range(HEADS_B):
            lam = lam_all[j:j + 1, :]
            q = q_ref[rows, j * 128:(j + 1) * 128]
            k0s = [_pad_rows(k[s, j * 128:j * 128 + HEAD_DIM, :].astype(BF16), True) for k in k_refs]
            k1s = [_pad_rows(k[s, j * 128 + HEAD_DIM:(j + 1) * 128, :].astype(BF16), False) for k in k_refs]
            e0 = _exp_parts([_dot(q, k0) for k0 in k0s])
            e1 = _exp_parts([_dot(q, k1) for k1 in k1s])
            o0 = o1 = None
            for a0, a1, v in zip(e0, e1, v_refs):
                if v.shape[2] == LANES:
                    val = v[s, pl.ds(j, v.shape[1] // HEADS_B, stride=HEADS_B), :].astype(BF16)
                else:
                    val = v[s, :, j * 128:(j + 1) * 128]
                if stack_maps:
                    t = _dot(jnp.concatenate([a0.astype(BF16), a1.astype(BF16)], axis=0), val)
                    t0, t1 = t[0:tq], t[tq:2 * tq]
                else:
                    t0, t1 = _dot(a0.astype(BF16), val), _dot(a1.astype(BF16), val)
                o0 = t0 if o0 is None else o0 + t0
                o1 = t1 if o1 is None else o1 + t1
            o = o0 * (1.0 / _row_sum(e0)) - o1 * (lam / _row_sum(e1))
            o = _rms(o, gd_ref[...]) * (1.0 - lam_init)
            o_ref[rows, j * 128:(j + 1) * 128] = o.astype(BF16)


def _both_mixers_kernel(*refs, n_src, lam_init, stack_maps, mxu_sums):
    per = 1 + 2 * n_src
    ya_o, yb_o = refs[-2:]
    _gqa_kernel(*refs[:per], ya_o, n_src=n_src, mxu_sums=mxu_sums)
    _diff_kernel(*refs[per:2 * per + 5], yb_o, n_src=n_src, lam_init=lam_init, stack_maps=stack_maps)


def _both_mixers(qa, ks_a, vs_a, qb, ks_b, vs_b, lam_refs, set_rows, tq, sets_per_step, lam_init):
    t = qa.shape[0]
    n_sets = t // set_rows
    q_tiles = set_rows // tq
    assert sets_per_step == 1 or q_tiles == 1
    rows = sets_per_step * tq
    tile = lambda b, i: (b * q_tiles + i, 0)
    kv = lambda a: pl.BlockSpec((sets_per_step,) + a.shape[1:], lambda b, i: (b, 0, 0))
    in_specs = ([pl.BlockSpec((rows, qa.shape[1]), tile)] + [kv(a) for a in list(ks_a) + list(vs_a)]
                + [pl.BlockSpec((rows, qb.shape[1]), tile)] + [kv(a) for a in list(ks_b) + list(vs_b)]
                + [pl.BlockSpec(e.shape, lambda b, i: (0, 0)) for e in lam_refs])
    return pl.pallas_call(
        functools.partial(_both_mixers_kernel, n_src=len(ks_a), lam_init=lam_init,
                          stack_maps=len(ks_a) == 1, mxu_sums=len(ks_a) > 1),
        grid=(n_sets // sets_per_step, q_tiles),
        in_specs=in_specs,
        out_specs=[pl.BlockSpec((rows, qa.shape[1]), tile), pl.BlockSpec((rows, qb.shape[1]), tile)],
        out_shape=[jax.ShapeDtypeStruct(qa.shape, BF16), jax.ShapeDtypeStruct(qb.shape, BF16)],
        name=f"mixers_{len(ks_a)}src",
        compiler_params=_cparams(2),
    )(qa, *ks_a, *vs_a, qb, *ks_b, *vs_b, *lam_refs)


def _post_kernel(x_ref, ya_ref, yb_ref, mod_ref, g1_ref, g2_ref, win_ref, wba_ref, wbb_ref, wo_ref, wr2_ref,
                 x1_o, h2_o, aff_o):
    d = x_ref.shape[1]
    g0 = win_ref.shape[1] - 2 * d
    mod = mod_ref[0]
    for r0 in range(0, x_ref.shape[0], CHAIN_ROWS):
        rows = slice(r0, r0 + CHAIN_ROWS)
        x = x_ref[rows, :]
        h = (_rms(x, g1_ref[...]) * (1.0 + mod[1:2]) + mod[0:1]).astype(BF16)
        ga = jax.nn.sigmoid(_dot(h, win_ref[:, g0:g0 + d]))
        merged = ga * _dot(ya_ref[rows, :], wba_ref[...])
        gb = jax.nn.sigmoid(_dot(h, win_ref[:, g0 + d:g0 + 2 * d]))
        merged = merged + gb * _dot(yb_ref[rows, :], wbb_ref[...])
        m = _dot(merged.astype(BF16), wo_ref[...])
        x1 = x + mod[2:3] * m
        x1_o[rows, :] = x1
        h2 = _rms(x1, g2_ref[...]) * (1.0 + mod[4:5]) + mod[3:4]
        h2_o[rows, :] = h2.astype(BF16)
        h2_hi, h2_lo = _split(h2)
        both = _dot(h2_hi, wr2_ref[...])
        logits = both[:, 0:LANES] + both[:, LANES:2 * LANES] + _dot(h2_lo, wr2_ref[:, 0:LANES])
        lane = lax.broadcasted_iota(jnp.int32, logits.shape, 1)
        logits = jnp.where(lane < N_EXPERTS, logits, NEG_BIG)
        e = jnp.exp(logits - logits.max(axis=-1, keepdims=True))
        aff_o[rows, :] = e / e.sum(axis=-1, keepdims=True)


def _post(x2, ya, yb, mod, set_rows, g1, g2, w_in, w_ba, w_bb, w_out, w_router2):
    t, d = x2.shape
    tm = POST_ROWS
    tiles_per_set = max(1, set_rows // tm)
    assert t % tm == 0 and tm % CHAIN_ROWS == 0 and (mod.shape[0] == 1 or set_rows % tm == 0)
    row = lambda i: (i, 0)
    const = lambda i: (0, 0)
    once = pl.Buffered(1)
    half = w_in.shape[1] // 2
    assert w_in.shape[1] == 2 * half and half % LANES == 0 and half >= 2 * d
    return pl.pallas_call(
        _post_kernel,
        grid=(t // tm,),
        in_specs=[pl.BlockSpec((tm, d), row),
                  pl.BlockSpec((tm, 512), row),
                  pl.BlockSpec((tm, 512), row),
                  pl.BlockSpec((1, 8, d), _mod_index(mod, tiles_per_set)),
                  pl.BlockSpec((1, d), const),
                  pl.BlockSpec((1, d), const),
                  pl.BlockSpec((d, half), lambda i: (0, 1), pipeline_mode=once),
                  pl.BlockSpec(w_ba.shape, const, pipeline_mode=once),
                  pl.BlockSpec(w_bb.shape, const, pipeline_mode=once),
                  pl.BlockSpec(w_out.shape, const, pipeline_mode=once),
                  pl.BlockSpec(w_router2.shape, const, pipeline_mode=once)],
        out_specs=[pl.BlockSpec((tm, d), row), pl.BlockSpec((tm, d), row), pl.BlockSpec((tm, LANES), row)],
        out_shape=[jax.ShapeDtypeStruct((t, d), F32), jax.ShapeDtypeStruct((t, d), BF16),
                   jax.ShapeDtypeStruct((t, LANES), F32)],
        name="post_attn",
        compiler_params=_cparams(1),
    )(x2, ya, yb, mod, g1, g2, w_in, w_ba, w_bb, w_out, w_router2)


def _select_kernel(aff_ref, slot_o, slot_t_o, affb_o, *, cap):
    n = slot_o.shape[1]
    aff = aff_ref[0:n, :]
    for k in range(1, SETS_PER_PACK):
        aff = aff + pltpu.roll(aff_ref[k * n:(k + 1) * n, :], N_EXPERTS * k, 1)
    capf = float(cap)

    def enough(cand):
        return jnp.sum(jnp.where(aff >= cand, 1.0, 0.0), axis=0, keepdims=True) >= capf

    pw = jnp.ones((1, LANES), F32)
    for k in (64, 32, 16, 8, 4, 2, 1):
        pw = jnp.where(enough(pw * 2.0 ** -(k - 1)), pw, pw * 2.0 ** -k)

    def mantissa_step(_, carry):
        thr, step = carry
        cand = thr + step
        return jnp.where(enough(cand), cand, thr), step * 0.5

    thr, _ = lax.fori_loop(0, 23, mantissa_step, (pw, pw * 0.5))
    above = aff > thr
    tied = aff == thr
    need = capf - jnp.sum(jnp.where(above, 1.0, 0.0), axis=0, keepdims=True)
    r_i = lax.broadcasted_iota(jnp.int32, (n, n), 0)
    c_i = lax.broadcasted_iota(jnp.int32, (n, n), 1)
    before = jnp.where(c_i < r_i, 1.0, 0.0).astype(BF16)
    tie_rank = _dot(before, jnp.where(tied, 1.0, 0.0).astype(BF16))
    sel = above | (tied & (tie_rank < need))
    slot = _dot(before, jnp.where(sel, 1.0, 0.0).astype(BF16))
    slot = jnp.where(sel, slot, NOT_SELECTED)
    slot_o[0] = slot.astype(BF16)
    slot_t_o[0] = slot.T
    affb_o[0] = aff.astype(BF16)


def _dispatch_kernel(slot_t_ref, h2_ref, xe_o, *, cap):
    n = slot_t_ref.shape[2]
    g = h2_ref.shape[0] // n
    per = min(N_EXPERTS, DISPATCH_ROWS // cap)
    slot_iota = lax.broadcasted_iota(jnp.int32, (cap, n), 0).astype(F32)
    for k in range(g):
        h2 = h2_ref[k * n:(k + 1) * n, :]
        slot_t = slot_t_ref[0, k * N_EXPERTS:(k + 1) * N_EXPERTS, :]
        for e0 in range(0, N_EXPERTS, per):
            onehot = jnp.concatenate(
                [jnp.where(slot_t[e:e + 1, :] == slot_iota, 1.0, 0.0) for e in range(e0, e0 + per)],
                axis=0).astype(BF16)
            rows = _dot(onehot, h2).astype(BF16)
            for j in range(per):
                xe_o[e0 + j, k * cap:(k + 1) * cap, :] = rows[j * cap:(j + 1) * cap]


def _select_dispatch_kernel(aff_ref, h2_ref, slot_o, affb_o, xe_o, slot_t_s, *, cap):
    _select_kernel(aff_ref, slot_o, slot_t_s, affb_o, cap=cap)
    _dispatch_kernel(slot_t_s, h2_ref, xe_o, cap=cap)


def _route(aff, h2, n_sets, set_rows):
    t, d = h2.shape
    n = set_rows
    cap = CAPACITY_FACTOR * n // N_EXPERTS
    assert cap & (cap - 1) == 0 and cap % 16 == 0 and DISPATCH_ROWS % cap == 0
    assert cap <= SLOT_RADIX, "slot ids must stay exact in bf16 and below the radix of the slot/gate packing"
    n_packs = -(-n_sets // SETS_PER_PACK)
    aff = jnp.pad(aff, ((0, (n_packs * SETS_PER_PACK - n_sets) * n), (0, 0)))
    pack_blk = lambda p: (p, 0, 0)
    g = _sets_per_step(n_sets, n)
    if g == SETS_PER_PACK and n_sets % SETS_PER_PACK == 0:
        slot, affb, xe = pl.pallas_call(
            functools.partial(_select_dispatch_kernel, cap=cap),
            grid=(n_packs,),
            in_specs=[pl.BlockSpec((SETS_PER_PACK * n, LANES), lambda p: (p, 0)),
                      pl.BlockSpec((SETS_PER_PACK * n, d), lambda p: (p, 0))],
            out_specs=[pl.BlockSpec((1, n, LANES), pack_blk), pl.BlockSpec((1, n, LANES), pack_blk),
                       pl.BlockSpec((N_EXPERTS, SETS_PER_PACK * cap, d), lambda p: (0, p, 0))],
            out_shape=[jax.ShapeDtypeStruct((n_packs, n, LANES), BF16),
                       jax.ShapeDtypeStruct((n_packs, n, LANES), BF16),
                       jax.ShapeDtypeStruct((N_EXPERTS, n_sets * cap, d), BF16)],
            scratch_shapes=[pltpu.VMEM((1, LANES, n), F32)],
            name=f"select_dispatch_cap{cap}",
            compiler_params=_cparams(1),
        )(aff, h2)
        return xe, (slot, affb)
    slot, slot_t, affb = pl.pallas_call(
        functools.partial(_select_kernel, cap=cap),
        grid=(n_packs,),
        in_specs=[pl.BlockSpec((SETS_PER_PACK * n, LANES), lambda p: (p, 0))],
        out_specs=[pl.BlockSpec((1, n, LANES), pack_blk), pl.BlockSpec((1, LANES, n), pack_blk),
                   pl.BlockSpec((1, n, LANES), pack_blk)],
        out_shape=[jax.ShapeDtypeStruct((n_packs, n, LANES), BF16),
                   jax.ShapeDtypeStruct((n_packs, LANES, n), F32),
                   jax.ShapeDtypeStruct((n_packs, n, LANES), BF16)],
        name=f"select_cap{cap}",
        compiler_params=_cparams(1),
    )(aff)
    per_pack = SETS_PER_PACK // g
    xe = pl.pallas_call(
        functools.partial(_dispatch_kernel, cap=cap),
        grid=(n_sets // g,),
        in_specs=[pl.BlockSpec((1, g * N_EXPERTS, n), lambda s: (s // per_pack, s % per_pack, 0)),
                  pl.BlockSpec((g * n, d), lambda s: (s, 0))],
        out_specs=pl.BlockSpec((N_EXPERTS, g * cap, d), lambda s: (0, s, 0)),
        out_shape=jax.ShapeDtypeStruct((N_EXPERTS, n_sets * cap, d), BF16),
        name=f"dispatch_cap{cap}",
        compiler_params=_cparams(1),
    )(slot_t, h2)
    return xe, (slot, affb)


def _expert_kernel(*refs, n_groups):
    x_refs = refs[:n_groups]
    wg_ref, wu_ref, wd_ref = refs[n_groups:n_groups + 3]
    o_refs = refs[n_groups + 3:2 * n_groups + 3]
    wg_s, wu_s, wd_s = refs[2 * n_groups + 3:]
    wg_s[...] = wg_ref[0].astype(BF16)
    wu_s[...] = wu_ref[0].astype(BF16)
    wd_s[...] = wd_ref[0].astype(BF16)
    sub = EXPERT_ROWS
    for x_ref, o_ref in zip(x_refs, o_refs):
        for r0 in range(0, x_ref.shape[1], sub):
            x = x_ref[0, r0:r0 + sub, :]
            a = _dot(x, wg_s[...])
            u = _dot(x, wu_s[...])
            hmid = (a * jax.nn.sigmoid(a) * u).astype(BF16)
            o_ref[0, r0:r0 + sub, :] = _dot(hmid, wd_s[...]).astype(BF16)


def _experts(xes, w_gate, w_up, w_down):
    e, d, f = w_gate.shape
    blk = lambda i: (i, 0, 0)
    return pl.pallas_call(
        functools.partial(_expert_kernel, n_groups=len(xes)),
        grid=(e,),
        in_specs=[pl.BlockSpec((1, x.shape[1], d), blk) for x in xes]
        + [pl.BlockSpec((1, d, f), blk), pl.BlockSpec((1, d, f), blk), pl.BlockSpec((1, f, d), blk)],
        out_specs=[pl.BlockSpec((1, x.shape[1], d), blk) for x in xes],
        out_shape=[jax.ShapeDtypeStruct(x.shape, BF16) for x in xes],
        scratch_shapes=[pltpu.VMEM((d, f), BF16), pltpu.VMEM((d, f), BF16), pltpu.VMEM((f, d), BF16)],
        name="experts",
        compiler_params=_cparams(1),
    )(*xes, w_gate, w_up, w_down)


def _combine_kernel(slot_ref, affb_ref, ye_ref, x1_ref, mod_ref, gf_ref, y_o, *, cap):
    e, gcap, d = ye_ref.shape
    g = gcap // cap
    n = x1_ref.shape[0] // g
    shift = cap.bit_length() - 1
    both = jnp.concatenate([slot_ref[0], affb_ref[0]], axis=1)
    want = (lax.broadcasted_iota(jnp.int32, (n, DISPATCH_ROWS), 1) & (cap - 1)).astype(F32) * SLOT_RADIX
    for k in range(g):
        rows = slice(k * n, (k + 1) * n)
        first_lane = ((pl.program_id(0) * g + k) % SETS_PER_PACK) * N_EXPERTS
        pieces = []
        for c0 in range(0, e * cap, DISPATCH_ROWS):
            src = lax.broadcasted_iota(jnp.int32, (2 * LANES, DISPATCH_ROWS), 0)
            col = lax.broadcasted_iota(jnp.int32, (2 * LANES, DISPATCH_ROWS), 1) + c0
            lane_of_col = first_lane + lax.shift_right_logical(col, shift)
            spread = jnp.where(src == lane_of_col, SLOT_RADIX,
                               jnp.where(src == lane_of_col + LANES, 1.0, 0.0)).astype(BF16)
            rest = _dot(both, spread) - want
            pieces.append(jnp.where(rest >= 0.0, jnp.where(rest <= 1.0, rest, 0.0), 0.0).astype(BF16))
        gt = pieces[0] if len(pieces) == 1 else jnp.concatenate(pieces, axis=1)
        ye = ye_ref[:, k * cap:(k + 1) * cap, :].reshape(e * cap, d)
        moe = _dot(gt, ye)
        x = x1_ref[rows, :] + mod_ref[0][5:6] * moe
        y_o[rows, :] = _rms(x, gf_ref[...])


def _sets_per_step(n_sets, set_rows):
    g = SETS_PER_PACK if set_rows * SETS_PER_PACK <= SHARED_STEP_ROWS else 1
    return g if n_sets % g == 0 else 1


def _combine(packs, ye, x1, mod, n_sets, set_rows, g_final):
    slot, affb = packs
    t, d = x1.shape
    cap = ye.shape[1] // n_sets
    g = _sets_per_step(n_sets, set_rows) if mod.shape[0] == 1 else 1
    tm = min(ROW_TILE, set_rows) if g == 1 else g * set_rows
    tiles = max(1, set_rows // tm)
    per_pack = SETS_PER_PACK // g
    pack_rows = pl.BlockSpec((1, tm // g, LANES), lambda s, i: (s // per_pack, i, 0))
    return pl.pallas_call(
        functools.partial(_combine_kernel, cap=cap),
        grid=(n_sets // g, tiles),
        in_specs=[pack_rows, pack_rows,
                  pl.BlockSpec((N_EXPERTS, g * cap, d), lambda s, i: (0, s, 0)),
                  pl.BlockSpec((tm, d), lambda s, i: (s * tiles + i, 0)),
                  pl.BlockSpec((1, 8, d), (lambda s, i: (s, 0, 0)) if mod.shape[0] > 1 else (lambda s, i: (0, 0, 0))),
                  pl.BlockSpec((1, d), lambda s, i: (0, 0))],
        out_specs=pl.BlockSpec((tm, d), lambda s, i: (s * tiles + i, 0)),
        out_shape=jax.ShapeDtypeStruct((t, d), F32),
        name=f"combine_cap{cap}",
        compiler_params=_cparams(2),
    )(slot, affb, ye, x1, mod, g_final)


def _rope_tables(n_tokens):
    n_rows = n_tokens // GRID_W
    rowp = np.repeat(np.arange(n_rows), GRID_W).astype(np.float32)
    colp = np.tile(np.arange(GRID_W), n_rows).astype(np.float32)
    quarter = HEAD_DIM // 4
    freqs = (np.float32(ROPE_THETA) ** (-np.arange(quarter, dtype=np.float32) / np.float32(quarter))).astype(np.float32)
    ang = np.stack([rowp[:, None] * freqs, colp[:, None] * freqs], axis=1)
    cos = np.cos(ang.astype(np.float64)).astype(np.float32)
    sin = np.sin(ang.astype(np.float64)).astype(np.float32)
    zero = np.zeros_like(sin)
    c = np.stack([cos, cos], axis=2).reshape(n_tokens, HEAD_DIM)
    s_up = np.stack([-sin, zero], axis=2).reshape(n_tokens, HEAD_DIM)
    s_dn = np.stack([zero, sin], axis=2).reshape(n_tokens, HEAD_DIM)
    return tuple(jnp.asarray(np.tile(t, (1, LANES // HEAD_DIM))) for t in (c, s_up, s_dn))


def _mix_and_route(x2, n_sets, set_rows, mod, lw, rope_tabs, cache, emit_cache, lam_init, tq):
    (g_attn, g_ffn, w_in, gq_t, gk_t, lq1, lk1, lq2, lk2, g_diff, w_ba, w_bb, w_out, w_router2) = lw
    outs = _inproj(x2, mod, set_rows, g_attn, w_in, gq_t, gk_t, rope_tabs, emit_cache)
    qa, kta, vta, qb, ktb, vb16 = outs[:6]
    vb16 = vb16.reshape(n_sets, set_rows, vb16.shape[1])
    ks_a, vs_a, ks_b, vs_b = [kta], [vta], [ktb], [vb16]
    if cache is not None:
        ckta, cvta, cktb, cvb = cache
        ks_a, vs_a, ks_b, vs_b = [ckta, kta], [cvta, vta], [cktb, ktb], [cvb, vb16]
    per_step = CONTEXT_SETS_PER_STEP if (tq == set_rows and n_sets % CONTEXT_SETS_PER_STEP == 0) else 1
    ya, yb = _both_mixers(qa, ks_a, vs_a, qb, ks_b, vs_b, (lq1, lk1, lq2, lk2, g_diff), set_rows, tq, per_step,
                          lam_init)
    x1, h2, aff = _post(x2, ya, yb, mod, set_rows, g_attn, g_ffn, w_in, w_ba, w_bb, w_out, w_router2)
    xe, packs = _route(aff, h2, n_sets, set_rows)
    return x1, xe, packs, outs[6:]


def kernel(x_prompt, x_sample, cache_attn_k, cache_attn_v, cache_diff_k, cache_diff_v, c, c_ctx, w_mod, b_mod,
           g_attn_norm, g_ffn_norm, w_in, g_q_norm, g_k_norm, lambda_q1, lambda_k1, lambda_q2, lambda_k2,
           g_diff_norm, w_branch_a, w_branch_b, w_out, w_router, w_exp_gate, w_exp_up, w_exp_down, g_final):
    batch, seq, d = x_prompt.shape
    dec_batch, dec_seq, _ = x_sample.shape
    depth = w_in.shape[0]
    assert depth == 1, "the final norm is fused into the layer's combine step"
    past = cache_attn_k.shape[2]
    assert w_in.shape[2] == N_QKV + 2 * d

    xp = x_prompt.reshape(batch * seq, d)
    xs = x_sample.reshape(dec_batch * dec_seq, d)
    rope_tabs = _rope_tables(dec_seq)
    c_rows = jnp.concatenate([c, c_ctx[None, :], jnp.zeros((16 - dec_batch - 1, d), F32)], axis=0)
    yp = ys = None
    caches = []
    for l in range(depth):
        lam_init = 0.8 - 0.6 * math.exp(-0.3 * l)
        mod6 = _modulation(c_rows, w_mod[l], b_mod[l])
        mod = jnp.pad(jnp.transpose(mod6, (1, 0, 2)), ((0, 0), (0, 2), (0, 0)))
        mod_lat, mod_ctx = mod[:dec_batch], mod[dec_batch:dec_batch + 1]
        w_in16, w_ba16, w_bb16, w_out16, w_router2 = _round_weights(
            [w_in[l], w_branch_a[l], w_branch_b[l], w_out[l]],
            jnp.pad(w_router[l], ((0, 0), (0, LANES - N_EXPERTS))))
        lw = (g_attn_norm[l][None, :], g_ffn_norm[l][None, :], w_in16,
              jnp.tile(g_q_norm[l], HEADS_A)[None, :], jnp.tile(g_k_norm[l], KV_HEADS_A)[None, :],
              lambda_q1[l], lambda_k1[l], lambda_q2[l], lambda_k2[l], g_diff_norm[l][None, :],
              w_ba16, w_bb16, w_out16, w_router2)
        x1p, xe_p, packs_p, cache_out = _mix_and_route(xp, batch, seq, mod_ctx, lw, None, None, True, lam_init, seq)
        caches.append(cache_out)
        feat_major = lambda a: jnp.moveaxis(a[:, l], 1, -1).reshape(dec_batch, -1, past)
        cache_l = (feat_major(cache_attn_k), feat_major(cache_attn_v), feat_major(cache_diff_k),
                   cache_diff_v[:, l].reshape(dec_batch, past * HEADS_B, -1))
        x1s, xe_s, packs_s, _ = _mix_and_route(xs, dec_batch, dec_seq, mod_lat, lw, rope_tabs, cache_l, False,
                                               lam_init, LATENT_Q_TILE)
        ye_p, ye_s = _experts([xe_p, xe_s], w_exp_gate[l], w_exp_up[l], w_exp_down[l])
        yp = _combine(packs_p, ye_p, x1p, mod_ctx, batch, seq, g_final[None, :])
        ys = _combine(packs_s, ye_s, x1s, mod_lat, dec_batch, dec_seq, g_final[None, :])
    y_prompt = yp.reshape(batch, seq, d)
    y_sample = ys.reshape(dec_batch, dec_seq, d)
    tok_major = lambda a, dims: jnp.moveaxis(a.reshape((batch,) + dims + (seq,)), -1, 1)
    new_attn_k = jnp.stack([tok_major(cc[0], (KV_HEADS_A, HEAD_DIM)) for cc in caches], axis=1)
    new_attn_v = jnp.stack([tok_major(cc[1], (KV_HEADS_A, HEAD_DIM)) for cc in caches], axis=1)
    new_diff_k = jnp.stack([tok_major(cc[2], (HEADS_B, 2, HEAD_DIM)) for cc in caches], axis=1)
    new_diff_v = jnp.stack([cc[3].reshape(batch, seq, HEADS_B, 2 * HEAD_DIM) for cc in caches], axis=1)
    return (y_prompt, y_sample, new_attn_k, new_attn_v, new_diff_k, new_diff_v)
```
